```python
import jax, jax.numpy as jnp
from jax import lax
import numpy as np

D_MODEL = 1024
BATCH = 2
SEQ = 8192
DEPTH = 1
DEC_BATCH = 128
DEC_SEQ = 8
PAST_LEN = 16384
PAGE_SIZE = 128

M_HEADS = 4
M_DK = 256
M_DV = 256
M_WIDTH = M_HEADS * M_DV
M_CHUNK = 128
A_Q_HEADS = 16
A_KV_HEADS = 4
A_GROUP = A_Q_HEADS // A_KV_HEADS
A_HD = 64
WINDOW = 128
A_BLOCK = 128
ROT_DIM = A_HD // 4
ROPE_THETA = 500000.0
N_EXPERTS = 32
TOP_K = 4
D_FF = D_MODEL
SWIGLU_LIMIT = 7.0
SWIGLU_ALPHA = 1.702
MOE_BLOCK = 128
NORM_EPS = 1e-6
COL_SIZES = (M_HEADS * M_DK, M_HEADS * M_DK, M_HEADS * M_DV, M_WIDTH, M_HEADS, M_HEADS,
             A_Q_HEADS * A_HD, A_KV_HEADS * A_HD, A_KV_HEADS * A_HD, D_MODEL, D_MODEL)

kernel_name = 'hybrid_mlstm_swa_moe_step'

F32 = jnp.float32


def rmsnorm(x, g):
    xf = x.astype(F32)
    y = xf * lax.rsqrt(jnp.mean(xf * xf, axis=-1, keepdims=True) + NORM_EPS)
    return (y * g.astype(F32)).astype(x.dtype)


def split_cols(p):
    out = []
    off = 0
    for n in COL_SIZES:
        out.append(p[..., off:off + n])
        off += n
    return out


def partial_rope(x, positions):
    half = ROT_DIM // 2
    inv = ROPE_THETA ** (-(jnp.arange(half, dtype=F32) * 2.0 / ROT_DIM))
    ang = positions.astype(F32)[:, None] * inv[None, :]
    cos = jnp.cos(ang)[None, :, None, :]
    sin = jnp.sin(ang)[None, :, None, :]
    xr = x[..., :ROT_DIM].astype(F32)
    x1, x2 = xr[..., :half], xr[..., half:]
    rot = jnp.concatenate([x1 * cos - x2 * sin, x2 * cos + x1 * sin], axis=-1)
    return jnp.concatenate([rot.astype(x.dtype), x[..., ROT_DIM:]], axis=-1)


def mlstm_chunkwise(q, k, v, logi, logf, c0, n0, m0):
    b, s, h, _ = q.shape
    L = min(M_CHUNK, s)
    nc = s // L

    def to_chunks(a):
        a = a.astype(F32).reshape((b, nc, L) + a.shape[2:])
        return jnp.moveaxis(a, 1, 0)

    xs = tuple(to_chunks(a) for a in (q, k, v, logi, logf))
    causal = jnp.tril(jnp.ones((L, L), dtype=bool))

    def step(carry, inp):
        c, n, m = carry
        qc, kc, vc, li, lf = inp
        li = jnp.transpose(li, (0, 2, 1))
        lf = jnp.transpose(lf, (0, 2, 1))
        fcum = jnp.cumsum(lf, axis=-1)
        a = li - fcum
        m_t = jnp.maximum(fcum + m[..., None], lax.cummax(a, axis=2) + fcum)
        log_d = fcum[..., :, None] + a[..., None, :] - m_t[..., :, None]
        d = jnp.exp(jnp.where(causal, log_d, -jnp.inf))
        inter = jnp.exp(fcum + m[..., None] - m_t)
        qk = jnp.einsum('bthd,bshd->bhts', qc, kc) * d
        num = (inter[..., None] * jnp.einsum('bhvd,bthd->bhtv', c, qc)
               + jnp.einsum('bhts,bshv->bhtv', qk, vc))
        den = inter * jnp.einsum('bhd,bthd->bht', n, qc) + qk.sum(-1)
        hc = num / jnp.maximum(jnp.abs(den), 1.0)[..., None]
        m_end = m_t[..., -1]
        f_end = fcum[..., -1]
        decay = jnp.exp(f_end + m - m_end)
        w_src = jnp.exp(a + (f_end - m_end)[..., None])
        c_new = decay[..., None, None] * c + jnp.einsum('bhs,bshv,bshd->bhvd', w_src, vc, kc)
        n_new = decay[..., None] * n + jnp.einsum('bhs,bshd->bhd', w_src, kc)
        return (c_new, n_new, m_end), hc

    carry0 = (c0.astype(F32), n0.astype(F32), m0.astype(F32))
    (c, n, m), hs = lax.scan(step, carry0, xs)
    hs = jnp.transpose(hs, (1, 0, 3, 2, 4)).reshape(b, s, h, -1)
    return hs, c, n, m


def sink_attention(q, k, v, mask, sinks):
    s = jnp.einsum('...qkgd,...skd->...kgqs', q.astype(F32), k.astype(F32)) * (A_HD ** -0.5)
    s = jnp.where(mask, s, -jnp.inf)
    sink = sinks.astype(F32).reshape(A_KV_HEADS, A_GROUP, 1, 1)
    mx = jnp.maximum(s.max(axis=-1, keepdims=True), sink)
    p = jnp.exp(s - mx)
    w = p / (p.sum(axis=-1, keepdims=True) + jnp.exp(sink - mx))
    return jnp.einsum('...kgqs,...skd->...qkgd', w, v.astype(F32))


def swa_prompt(q, k, v, sinks):
    b, s = q.shape[:2]
    nb = s // A_BLOCK
    qb = q.reshape(b, nb, A_BLOCK, A_KV_HEADS, A_GROUP, A_HD)

    def band(a):
        ab = a.reshape(b, nb, A_BLOCK, A_KV_HEADS, A_HD)
        prev = jnp.concatenate([jnp.zeros_like(ab[:, :1]), ab[:, :-1]], axis=1)
        return jnp.concatenate([prev, ab], axis=2)

    kb, vb = band(k), band(v)
    qi = jnp.arange(A_BLOCK)[:, None] + A_BLOCK
    ki = jnp.arange(2 * A_BLOCK)[None, :]
    diff = qi - ki
    in_win = (diff >= 0) & (diff <= WINDOW)
    has_prev = (jnp.arange(nb) > 0)[:, None, None] | (ki >= A_BLOCK)[None]
    mask = (in_win[None] & has_prev)[None, :, None, None]
    o = sink_attention(qb, kb, vb, mask, sinks)
    return o.reshape(b, s, A_Q_HEADS * A_HD)


def swa_sample(q, k_new, v_new, k_buf, v_buf, sinks):
    db, t = q.shape[:2]
    wc = k_buf.shape[1]
    kk = jnp.concatenate([k_buf.astype(k_new.dtype), k_new], axis=1)
    vv = jnp.concatenate([v_buf.astype(v_new.dtype), v_new], axis=1)
    diff = (jnp.arange(t)[:, None] + wc) - jnp.arange(wc + t)[None, :]
    mask = ((diff >= 0) & (diff <= WINDOW))[None, None, None]
    o = sink_attention(q, kk, vv, mask, sinks)
    return o.reshape(db, t, A_Q_HEADS * A_HD), kk[:, -wc:], vv[:, -wc:]


def moe_ffn(x, w_router, b_router, w_gate_up, b_gate_up, w_down, b_down):
    n_tok, d = x.shape
    logits = (x @ w_router).astype(F32) + b_router.astype(F32)
    top_val, top_idx = lax.top_k(logits, TOP_K)
    gates = jax.nn.softmax(top_val, axis=-1)
    nk = n_tok * TOP_K
    flat_e = top_idx.reshape(nk)
    flat_tok = jnp.repeat(jnp.arange(n_tok, dtype=jnp.int32), TOP_K)
    order = jnp.argsort(flat_e)
    e_sorted = flat_e[order]
    counts = jnp.bincount(flat_e, length=N_EXPERTS)
    padded = (counts + MOE_BLOCK - 1) // MOE_BLOCK * MOE_BLOCK
    padded_end = jnp.cumsum(padded)
    group_start = jnp.cumsum(counts) - counts
    dest = (padded_end - padded)[e_sorted] + jnp.arange(nk) - group_start[e_sorted]
    n_blocks = -(-nk // MOE_BLOCK) + N_EXPERTS
    n_slots = n_blocks * MOE_BLOCK
    slot_tok = jnp.zeros((n_slots,), jnp.int32).at[dest].set(flat_tok[order])
    slot_gate = jnp.zeros((n_slots,), F32).at[dest].set(gates.reshape(nk)[order])
    block_expert = jnp.minimum(
        jnp.searchsorted(padded_end, jnp.arange(n_blocks) * MOE_BLOCK, side='right'), N_EXPERTS - 1)
    xs = x[slot_tok].reshape(n_blocks, MOE_BLOCK, d)

    def expert_block(args):
        xb, e = args
        hgu = xb @ w_gate_up[e] + b_gate_up[e]
        gate = jnp.minimum(hgu[:, :D_FF], SWIGLU_LIMIT)
        up = jnp.clip(hgu[:, D_FF:], -SWIGLU_LIMIT, SWIGLU_LIMIT)
        act = gate * jax.nn.sigmoid(SWIGLU_ALPHA * gate) * (up + 1.0)
        return act @ w_down[e] + b_down[e]

    ys = lax.map(expert_block, (xs, block_expert))
    ys = ys.reshape(n_slots, d).astype(F32) * slot_gate[:, None]
    return jnp.zeros((n_tok, d), F32).at[slot_tok].add(ys).astype(x.dtype)


def hybrid_layer(x, pos0, c0, n0, m0, k_buf, v_buf, norm1, w_in, b_igate, b_fgate, mlstm_norm,
                 w_proj_m, w_proj_a, attn_sinks, w_out, norm2, w_router, b_router,
                 w_gate_up, b_gate_up, w_down, b_down):
    b, s, _ = x.shape
    xn = rmsnorm(x, norm1)
    q_m, k_m, v_m, o_m, i_m, f_m, q_a, k_a, v_a, g_m, g_a = split_cols(xn @ w_in)
    logi = (i_m + b_igate).astype(F32)
    logf = jax.nn.log_sigmoid((f_m + b_fgate).astype(F32))
    h, c, n, m = mlstm_chunkwise(
        q_m.reshape(b, s, M_HEADS, M_DK),
        k_m.reshape(b, s, M_HEADS, M_DK) * (M_DK ** -0.5),
        v_m.reshape(b, s, M_HEADS, M_DV), logi, logf, c0, n0, m0)
    h = h * lax.rsqrt(jnp.mean(h * h, axis=-1, keepdims=True) + NORM_EPS)
    h_m = (h.reshape(b, s, M_WIDTH) * mlstm_norm.astype(F32)
           * jax.nn.sigmoid(o_m.astype(F32))).astype(x.dtype)
    pos = pos0 + jnp.arange(s)
    qa = partial_rope(q_a.reshape(b, s, A_Q_HEADS, A_HD), pos).reshape(b, s, A_KV_HEADS, A_GROUP, A_HD)
    ka = partial_rope(k_a.reshape(b, s, A_KV_HEADS, A_HD), pos)
    va = v_a.reshape(b, s, A_KV_HEADS, A_HD)
    if k_buf is None:
        h_a = swa_prompt(qa, ka, va, attn_sinks)
        wc = min(WINDOW, s)
        k_keep, v_keep = ka[:, s - wc:], va[:, s - wc:]
    else:
        h_a, k_keep, v_keep = swa_sample(qa, ka, va, k_buf, v_buf, attn_sinks)
    mixed = (jax.nn.sigmoid(g_m) * (h_m @ w_proj_m)
             + jax.nn.sigmoid(g_a) * (h_a.astype(x.dtype) @ w_proj_a))
    x = x + mixed @ w_out
    xn2 = rmsnorm(x, norm2)
    x = x + moe_ffn(xn2.reshape(b * s, D_MODEL), w_router, b_router, w_gate_up, b_gate_up,
                    w_down, b_down).reshape(b, s, D_MODEL)
    return x, k_keep, v_keep, c, n, m


def setup_inputs(seed: int = 0) -> dict:
    key = jax.random.key(seed)
    ks = jax.random.split(key, 24)

    def nrm(k, shape, scale):
        return jax.random.normal(k, shape, F32) * scale

    wc = min(WINDOW, PAST_LEN)
    n_cols = sum(COL_SIZES)
    return {
        'x_prompt': nrm(ks[0], (BATCH, SEQ, D_MODEL), 1.0),
        'x_sample': nrm(ks[1], (DEC_BATCH, DEC_SEQ, D_MODEL), 1.0),
        'cache_k': nrm(ks[2], (DEPTH, DEC_BATCH, wc, A_KV_HEADS, A_HD), 1.0),
        'cache_v': nrm(ks[3], (DEPTH, DEC_BATCH, wc, A_KV_HEADS, A_HD), 1.0),
        'state_c': nrm(ks[4], (DEPTH, DEC_BATCH, M_HEADS, M_DV, M_DK), 0.1),
        'state_n': nrm(ks[5], (DEPTH, DEC_BATCH, M_HEADS, M_DK), 0.1),
        'state_m': nrm(ks[6], (DEPTH, DEC_BATCH, M_HEADS), 1.0),
        'norm1': 1.0 + nrm(ks[7], (DEPTH, D_MODEL), 0.02),
        'w_in': nrm(ks[8], (DEPTH, D_MODEL, n_cols), D_MODEL ** -0.5),
        'b_igate': nrm(ks[9], (DEPTH, M_HEADS), 0.1),
        'b_fgate': jnp.linspace(3.0, 6.0, M_HEADS, dtype=F32)[None] + nrm(ks[10], (DEPTH, M_HEADS), 0.1),
        'mlstm_norm': 1.0 + nrm(ks[11], (DEPTH, M_WIDTH), 0.02),
        'w_proj_m': nrm(ks[12], (DEPTH, M_WIDTH, D_MODEL), M_WIDTH ** -0.5),
        'w_proj_a': nrm(ks[13], (DEPTH, A_Q_HEADS * A_HD, D_MODEL), (A_Q_HEADS * A_HD) ** -0.5),
        'attn_sinks': nrm(ks[14], (DEPTH, A_Q_HEADS), 0.5),
        'w_out': nrm(ks[15], (DEPTH, D_MODEL, D_MODEL), D_MODEL ** -0.5),
        'norm2': 1.0 + nrm(ks[16], (DEPTH, D_MODEL), 0.02),
        'w_router': nrm(ks[17], (DEPTH, D_MODEL, N_EXPERTS), D_MODEL ** -0.5),
        'b_router': nrm(ks[18], (DEPTH, N_EXPERTS), 0.01),
        'w_gate_up': nrm(ks[19], (DEPTH, N_EXPERTS, D_MODEL, 2 * D_FF), D_MODEL ** -0.5),
        'b_gate_up': nrm(ks[20], (DEPTH, N_EXPERTS, 2 * D_FF), 0.02),
        'w_down': nrm(ks[21], (DEPTH, N_EXPERTS, D_FF, D_MODEL), D_FF ** -0.5),
        'b_down': nrm(ks[22], (DEPTH, N_EXPERTS, D_MODEL), 0.02),
        'norm_f': 1.0 + nrm(ks[23], (D_MODEL,), 0.02),
    }


def reference(x_prompt, x_sample, cache_k, cache_v, state_c, state_n, state_m, norm1, w_in,
              b_igate, b_fgate, mlstm_norm, w_proj_m, w_proj_a, attn_sinks, w_out, norm2,
              w_router, b_router, w_gate_up, b_gate_up, w_down, b_down, norm_f):
    hp, hs = x_prompt, x_sample
    pk, pv, pc, pn, pm = [], [], [], [], []
    sk, sv, sc, sn, sm = [], [], [], [], []
    bp = x_prompt.shape[0]
    for l in range(DEPTH):
        w = (norm1[l], w_in[l], b_igate[l], b_fgate[l], mlstm_norm[l], w_proj_m[l], w_proj_a[l],
             attn_sinks[l], w_out[l], norm2[l], w_router[l], b_router[l], w_gate_up[l],
             b_gate_up[l], w_down[l], b_down[l])
        c0 = jnp.zeros((bp, M_HEADS, M_DV, M_DK), F32)
        n0 = jnp.zeros((bp, M_HEADS, M_DK), F32)
        m0 = jnp.zeros((bp, M_HEADS), F32)
        hp, k1, v1, c1, n1, m1 = hybrid_layer(hp, 0, c0, n0, m0, None, None, *w)
        hs, k2, v2, c2, n2, m2 = hybrid_layer(hs, PAST_LEN, state_c[l], state_n[l], state_m[l],
                                              cache_k[l], cache_v[l], *w)
        pk.append(k1); pv.append(v1); pc.append(c1); pn.append(n1); pm.append(m1)
        sk.append(k2); sv.append(v2); sc.append(c2); sn.append(n2); sm.append(m2)
    y_prompt = rmsnorm(hp, norm_f)
    y_sample = rmsnorm(hs, norm_f)
    return (y_prompt, y_sample,
            jnp.stack(pk), jnp.stack(pv), jnp.stack(pc), jnp.stack(pn), jnp.stack(pm),
            jnp.stack(sk), jnp.stack(sv), jnp.stack(sc), jnp.stack(sn), jnp.stack(sm))
```

```python
import functools

import jax
import jax.numpy as jnp
from jax import lax
from jax.experimental import pallas as pl
from jax.experimental.pallas import tpu as pltpu

F32 = jnp.float32
BF16 = jnp.bfloat16
I32 = jnp.int32

D_MODEL = 1024
M_HEADS = 4
M_DK = 256
M_DV = 256
M_CHUNK = 128
A_Q_HEADS = 16
A_KV_HEADS = 4
A_GROUP = 4
A_HD = 64
A_KV = A_KV_HEADS * A_HD
WINDOW = 128
A_BLOCK = 128
ROT_DIM = 16
ROPE_THETA = 500000.0
N_EXPERTS = 32
TOP_K = 4
D_FF = 1024
SWIGLU_LIMIT = 7.0
SWIGLU_ALPHA = 1.702
NORM_EPS = 1e-6

LANES = 128
SUBLANES = 8
VMEM_LIMIT = 56 * 1024 * 1024

C_QKV = 0
C_OG = 3072
C_QA = 6144
C_KA = 7168
C_VA = 7424
C_GATE = 7680
N_COLS = 7808

TOK_TILE = 256
RUN_ALIGN = SUBLANES
STAGE_ROWS = 1280
MOE_BLOCK = 256


def _cparams(sem):
    return pltpu.CompilerParams(dimension_semantics=sem, vmem_limit_bytes=VMEM_LIMIT)


def _rms(x, g):
    return x * lax.rsqrt(jnp.mean(x * x, axis=-1, keepdims=True) + NORM_EPS) * g


def _rope(x, cos_t, sin_t, first_half):
    n = x.shape[1]
    fwd = pltpu.roll(x, n - ROT_DIM // 2, axis=1)
    bwd = pltpu.roll(x, ROT_DIM // 2, axis=1)
    outs = []
    for b in range(n // LANES):
        sl = slice(b * LANES, (b + 1) * LANES)
        partner = jnp.where(first_half, fwd[:, sl], bwd[:, sl])
        outs.append(x[:, sl] * cos_t + partner * sin_t)
    return jnp.concatenate(outs, axis=1)


def _inproj_kernel(x_ref, g1_ref, w_ref, cos_ref, sin_ref,
                   qkv_ref, og_ref, qa_ref, ka_ref, va_ref, gate_ref):
    x = x_ref[...]
    xn = _rms(x, g1_ref[...]).astype(BF16)
    act = qkv_ref.dtype

    def proj(c0, n):
        return jnp.dot(xn, w_ref[:, c0:c0 + n], preferred_element_type=F32)

    for j in range(3):
        qkv_ref[:, j * 1024:(j + 1) * 1024] = proj(C_QKV + j * 1024, 1024).astype(act)
    for j in range(3):
        og_ref[:, j * 1024:(j + 1) * 1024] = jax.nn.sigmoid(proj(C_OG + j * 1024, 1024)).astype(act)
    cos_t = cos_ref[...]
    sin_t = sin_ref[...]
    lane = lax.broadcasted_iota(I32, (1, LANES), 1)
    first_half = (lane & (ROT_DIM - 1)) < (ROT_DIM // 2)
    qa_ref[...] = _rope(proj(C_QA, 1024), cos_t, sin_t, first_half).astype(act)
    ka_ref[...] = _rope(proj(C_KA, A_KV), cos_t, sin_t, first_half)
    va_ref[...] = proj(C_VA, A_KV)
    gate_ref[...] = proj(C_GATE, LANES)


def _inproj(x, g1, w, cos_t, sin_t, tm, rope_blocks, act):
    t = x.shape[0]
    tok = lambda n, dt: jax.ShapeDtypeStruct((t, n), dt)
    row = lambda n: pl.BlockSpec((tm, n), lambda i: (i, 0))
    return pl.pallas_call(
        _inproj_kernel,
        grid=(t // tm,),
        in_specs=[row(D_MODEL),
                  pl.BlockSpec((1, D_MODEL), lambda i: (0, 0)),
                  pl.BlockSpec((D_MODEL, N_COLS), lambda i: (0, 0), pipeline_mode=pl.Buffered(1)),
                  pl.BlockSpec((tm, LANES), lambda i: (i % rope_blocks, 0)),
                  pl.BlockSpec((tm, LANES), lambda i: (i % rope_blocks, 0))],
        out_specs=[row(3072), row(3072), row(1024), row(A_KV), row(A_KV), row(LANES)],
        out_shape=[tok(3072, act), tok(3072, act), tok(1024, act), tok(A_KV, F32), tok(A_KV, F32),
                   tok(LANES, F32)],
        compiler_params=_cparams(("arbitrary",)),
        name="inproj",
    )(x, g1, w, cos_t, sin_t)


def _mlstm_kernel(L, qkv_ref, gate_ref, og_ref, c0_ref, n0_ref, m0_ref, bi_ref, bf_ref, mn_ref,
                  h_ref, c_ref, n_ref, m_ref, c_s, n_s, m_s):
    ci = pl.program_id(1)

    @pl.when(ci == 0)
    def _():
        c_s[...] = c0_ref[0]
        n_s[...] = n0_ref[0]
        m_s[...] = m0_ref[0]

    gc = gate_ref[...]
    if L < LANES:
        gc = jnp.concatenate([gc, jnp.zeros((LANES - L, LANES), F32)], axis=0)
    gr = gc.T
    li = gr[0:M_HEADS] + bi_ref[...]
    z = gr[M_HEADS:2 * M_HEADS] + bf_ref[...]
    lf = jnp.minimum(z, 0.0) - jnp.log1p(jnp.exp(-jnp.abs(z)))
    r_i = lax.broadcasted_iota(I32, (LANES, LANES), 0)
    c_i = lax.broadcasted_iota(I32, (LANES, LANES), 1)
    upper = (r_i <= c_i).astype(F32)
    fcum = jnp.dot(lf, upper, preferred_element_type=F32, precision=lax.Precision.HIGHEST)
    a = li - fcum
    lane = lax.broadcasted_iota(I32, (M_HEADS, LANES), 1)
    cmx = a
    sh = 1
    while sh < L:
        cmx = jnp.maximum(cmx, jnp.where(lane >= sh, pltpu.roll(cmx, sh, axis=1), -jnp.inf))
        sh *= 2
    m_prev = m_s[...]
    m_t = jnp.maximum(fcum + m_prev, cmx + fcum)
    g_row = fcum - m_t
    inter_row = jnp.exp(fcum + m_prev - m_t)
    f_end = fcum[:, L - 1:L]
    m_end = m_t[:, L - 1:L]
    decay = jnp.exp(f_end + m_prev - m_end)
    wsrc_row = jnp.exp(a + (f_end - m_end))
    stack = jnp.concatenate([g_row, inter_row, wsrc_row, jnp.zeros((LANES - 3 * M_HEADS, LANES), F32)], axis=0)
    cols = stack.T[0:L]

    t_i = lax.broadcasted_iota(I32, (L, L), 0)
    s_i = lax.broadcasted_iota(I32, (L, L), 1)
    causal = s_i <= t_i
    nt = (((1,), (1,)), ((), ()))
    tn = (((0,), (0,)), ((), ()))
    for h in range(M_HEADS):
        q = qkv_ref[:, h * M_DK:(h + 1) * M_DK].astype(BF16)
        k = qkv_ref[:, 1024 + h * M_DK:1024 + (h + 1) * M_DK].astype(BF16)
        v = qkv_ref[:, 2048 + h * M_DV:2048 + (h + 1) * M_DV].astype(BF16)
        g_col = cols[:, h:h + 1]
        inter_col = cols[:, M_HEADS + h:M_HEADS + h + 1]
        w_col = cols[:, 2 * M_HEADS + h:2 * M_HEADS + h + 1]
        d = jnp.exp(jnp.where(causal, g_col + a[h:h + 1, 0:L], -jnp.inf))
        s = lax.dot_general(q, k, nt, preferred_element_type=F32)
        qk = s * d
        c_h = c_s[h]
        n_h = n_s[h:h + 1, :]
        cq = lax.dot_general(q, c_h.astype(BF16), nt, preferred_element_type=F32)
        pv = jnp.dot(qk.astype(BF16), v, preferred_element_type=F32)
        num = inter_col * cq + pv
        den = inter_col * jnp.sum(q.astype(F32) * n_h, axis=1, keepdims=True) + jnp.sum(qk, axis=1, keepdims=True)
        hh = num / jnp.maximum(jnp.abs(den), 1.0)
        hn = hh * lax.rsqrt(jnp.mean(hh * hh, axis=1, keepdims=True) + NORM_EPS)
        sl = slice(h * M_DV, (h + 1) * M_DV)
        h_ref[:, sl] = (hn * mn_ref[:, sl] * og_ref[:, sl].astype(F32)).astype(h_ref.dtype)
        kw = w_col * k.astype(F32)
        dec = decay[h:h + 1, 0:1]
        c_s[h] = dec * c_h + lax.dot_general(v, kw.astype(BF16), tn, preferred_element_type=F32)
        n_s[h:h + 1, :] = dec * n_h + jnp.sum(kw, axis=0, keepdims=True)
    m_s[...] = m_end

    @pl.when(ci == pl.num_programs(1) - 1)
    def _():
        c_ref[0] = c_s[...]
        n_ref[0] = n_s[...]
        m_ref[0] = m_s[...]


def _mlstm(qkv, gates, og, c0, n0, m0, b_i, b_f, mnorm, nb, L):
    t = qkv.shape[0]
    nc = t // (nb * L)
    tokmap = lambda b, c: (b * nc + c, 0)
    const2 = lambda b, c: (0, 0)
    return pl.pallas_call(
        functools.partial(_mlstm_kernel, L),
        grid=(nb, nc),
        in_specs=[pl.BlockSpec((L, 3072), tokmap),
                  pl.BlockSpec((L, LANES), tokmap),
                  pl.BlockSpec((L, 1024), tokmap),
                  pl.BlockSpec((1, M_HEADS, M_DV, M_DK), lambda b, c: (b, 0, 0, 0)),
                  pl.BlockSpec((1, M_HEADS, M_DK), lambda b, c: (b, 0, 0)),
                  pl.BlockSpec((1, M_HEADS, 1), lambda b, c: (b, 0, 0)),
                  pl.BlockSpec((M_HEADS, 1), const2),
                  pl.BlockSpec((M_HEADS, 1), const2),
                  pl.BlockSpec((1, 1024), const2)],
        out_specs=[pl.BlockSpec((L, 1024), tokmap),
                   pl.BlockSpec((1, M_HEADS, M_DV, M_DK), lambda b, c: (b, 0, 0, 0)),
                   pl.BlockSpec((1, M_HEADS, M_DK), lambda b, c: (b, 0, 0)),
                   pl.BlockSpec((1, M_HEADS, 1), lambda b, c: (b, 0, 0))],
        out_shape=[jax.ShapeDtypeStruct((t, 1024), qkv.dtype),
                   jax.ShapeDtypeStruct((nb, M_HEADS, M_DV, M_DK), F32),
                   jax.ShapeDtypeStruct((nb, M_HEADS, M_DK), F32),
                   jax.ShapeDtypeStruct((nb, M_HEADS, 1), F32)],
        scratch_shapes=[pltpu.VMEM((M_HEADS, M_DV, M_DK), F32),
                        pltpu.VMEM((M_HEADS, M_DK), F32),
                        pltpu.VMEM((M_HEADS, 1), F32)],
        compiler_params=_cparams(("arbitrary", "arbitrary")),
        name="mlstm_L%d" % L,
    )(qkv, gates, og, c0, n0, m0, b_i, b_f, mnorm)


def _swa_core(q, k_prev, v_prev, k_cur, v_cur, has_prev, sink_ref):
    m = q.shape[0]
    q = q.astype(BF16)
    qst = jnp.concatenate([q[:, j * A_KV:(j + 1) * A_KV] for j in range(A_GROUP)], axis=0)
    k_all = jnp.concatenate([k_prev, k_cur], axis=0)
    v_all = jnp.concatenate([v_prev, v_cur], axis=0)
    lane = lax.broadcasted_iota(I32, (1, A_KV), 1)
    kbd, vbd = [], []
    for g in range(A_KV_HEADS):
        in_g = (lane >= g * A_HD) & (lane < (g + 1) * A_HD)
        kbd.append(jnp.where(in_g, k_all, 0.0).astype(BF16))
        vbd.append(jnp.where(in_g, v_all, 0.0).astype(BF16))
    kbd = jnp.concatenate(kbd, axis=0)
    vbd = jnp.concatenate(vbd, axis=0)
    s = lax.dot_general(qst, kbd, (((1,), (1,)), ((), ())), preferred_element_type=F32)
    nkeys = 2 * A_BLOCK
    t_i = lax.broadcasted_iota(I32, (A_GROUP * m, nkeys), 0) & (m - 1)
    s_i = lax.broadcasted_iota(I32, (A_GROUP * m, nkeys), 1)
    diff = t_i + A_BLOCK - s_i
    mask = (diff >= 0) & (diff <= WINDOW) & (has_prev | (s_i >= A_BLOCK))
    ws = []
    for g in range(A_KV_HEADS):
        sg = jnp.where(mask, s[:, g * nkeys:(g + 1) * nkeys] * (A_HD ** -0.5), -jnp.inf)
        sink = jnp.concatenate(
            [jnp.full((m, 1), sink_ref[A_GROUP * g + j], F32) for j in range(A_GROUP)], axis=0)
        mx = jnp.maximum(jnp.max(sg, axis=1, keepdims=True), sink)
        p = jnp.exp(sg - mx)
        den = jnp.sum(p, axis=1, keepdims=True) + jnp.exp(sink - mx)
        ws.append((p / den).astype(BF16))
    w = jnp.concatenate(ws, axis=1)
    o = jnp.dot(w, vbd, preferred_element_type=F32)
    return jnp.concatenate([o[j * m:(j + 1) * m] for j in range(A_GROUP)], axis=1)


def _swa_prompt_kernel(sink_ref, q_ref, kp_ref, kc_ref, vp_ref, vc_ref, o_ref):
    has_prev = pl.program_id(1) > 0
    o_ref[...] = _swa_core(q_ref[...], kp_ref[...], vp_ref[...], kc_ref[...], vc_ref[...],
                           has_prev, sink_ref).astype(o_ref.dtype)


def _swa_prompt(qa, ka, va, sinks, nb):
    t = qa.shape[0]
    nblk = t // (nb * A_BLOCK)
    cur = lambda b, i, s: (b * nblk + i, 0)
    prev = lambda b, i, s: (b * nblk + jnp.maximum(i - 1, 0), 0)
    return pl.pallas_call(
        _swa_prompt_kernel,
        grid_spec=pltpu.PrefetchScalarGridSpec(
            num_scalar_prefetch=1,
            grid=(nb, nblk),
            in_specs=[pl.BlockSpec((A_BLOCK, 1024), cur),
                      pl.BlockSpec((A_BLOCK, A_KV), prev),
                      pl.BlockSpec((A_BLOCK, A_KV), cur),
                      pl.BlockSpec((A_BLOCK, A_KV), prev),
                      pl.BlockSpec((A_BLOCK, A_KV), cur)],
            out_specs=pl.BlockSpec((A_BLOCK, 1024), cur)),
        out_shape=jax.ShapeDtypeStruct((t, 1024), BF16),
        compiler_params=_cparams(("arbitrary", "arbitrary")),
        name="swa_prompt",
    )(sinks, qa, ka, ka, va, va)


def _swa_sample_kernel(T, sink_ref, q_ref, kn_ref, vn_ref, kb_ref, vb_ref, o_ref, ko_ref, vo_ref):
    pad = jnp.zeros((A_BLOCK - T, A_KV), F32)
    k_new = kn_ref[...]
    v_new = vn_ref[...]
    k_buf = kb_ref[0]
    v_buf = vb_ref[0]
    o_ref[...] = _swa_core(q_ref[...], k_buf, v_buf, jnp.concatenate([k_new, pad], axis=0),
                           jnp.concatenate([v_new, pad], axis=0), True, sink_ref).astype(o_ref.dtype)
    ko_ref[0] = jnp.concatenate([k_buf[T:], k_new], axis=0)
    vo_ref[0] = jnp.concatenate([v_buf[T:], v_new], axis=0)


def _swa_sample(qa, ka, va, k_buf, v_buf, sinks, T):
    nb = k_buf.shape[0]
    tok = lambda b, s: (b, 0)
    buf = lambda b, s: (b, 0, 0)
    return pl.pallas_call(
        functools.partial(_swa_sample_kernel, T),
        grid_spec=pltpu.PrefetchScalarGridSpec(
            num_scalar_prefetch=1,
            grid=(nb,),
            in_specs=[pl.BlockSpec((T, 1024), tok),
                      pl.BlockSpec((T, A_KV), tok),
                      pl.BlockSpec((T, A_KV), tok),
                      pl.BlockSpec((1, WINDOW, A_KV), buf),
                      pl.BlockSpec((1, WINDOW, A_KV), buf)],
            out_specs=[pl.BlockSpec((T, 1024), tok),
                       pl.BlockSpec((1, WINDOW, A_KV), buf),
                       pl.BlockSpec((1, WINDOW, A_KV), buf)]),
        out_shape=[jax.ShapeDtypeStruct((nb * T, 1024), qa.dtype),
                   jax.ShapeDtypeStruct((nb, WINDOW, A_KV), F32),
                   jax.ShapeDtypeStruct((nb, WINDOW, A_KV), F32)],
        compiler_params=_cparams(("arbitrary",)),
        name="swa_sample",
    )(sinks, qa, ka, va, k_buf, v_buf)


def _merge_kernel(x_ref, hm_ref, ha_ref, gm_ref, ga_ref, wpm_ref, wpa_ref, wo_ref, g2_ref, wr_ref, br_ref,
                  x1_ref, xn_ref, idx_ref, gate_ref, cnt_ref):
    pm = jnp.dot(hm_ref[...].astype(BF16), wpm_ref[...], preferred_element_type=F32)
    pa = jnp.dot(ha_ref[...].astype(BF16), wpa_ref[...], preferred_element_type=F32)
    mixed = gm_ref[...].astype(F32) * pm + ga_ref[...].astype(F32) * pa
    x1 = x_ref[...] + jnp.dot(mixed.astype(BF16), wo_ref[...], preferred_element_type=F32)
    x1_ref[...] = x1
    xn = _rms(x1, g2_ref[...])
    xn_ref[...] = xn.astype(BF16)
    logits = jnp.dot(xn, wr_ref[...], preferred_element_type=F32, precision=lax.Precision.HIGHEST) + br_ref[...]
    tm = logits.shape[0]
    lane = lax.broadcasted_iota(I32, (tm, LANES), 1).astype(F32)
    cur = jnp.where(lane < N_EXPERTS, logits, -jnp.inf)
    vals, idxs = [], []
    onehot = jnp.zeros((tm, LANES), F32)
    for _ in range(TOP_K):
        mval = jnp.max(cur, axis=1, keepdims=True)
        sel = jnp.min(jnp.where(cur == mval, lane, float(LANES)), axis=1, keepdims=True)
        hit = lane == sel
        onehot = onehot + hit.astype(F32)
        cur = jnp.where(hit, -jnp.inf, cur)
        vals.append(mval)
        idxs.append(sel)
    es = [jnp.exp(v - vals[0]) for v in vals]
    tot = es[0] + es[1] + es[2] + es[3]
    col = lax.broadcasted_iota(I32, (tm, TOP_K), 1)
    idx_out = jnp.zeros((tm, TOP_K), F32)
    gate_out = jnp.zeros((tm, TOP_K), F32)
    for k in range(TOP_K):
        idx_out = jnp.where(col == k, idxs[k], idx_out)
        gate_out = jnp.where(col == k, es[k] / tot, gate_out)
    idx_ref[...] = idx_out.astype(I32)
    gate_ref[...] = gate_out
    cnt_ref[0] = jnp.sum(onehot, axis=0, keepdims=True).astype(I32)


def _merge(x, hm, ha, og, wpm, wpa, wo, g2, wr, br):
    t = x.shape[0]
    tm = TOK_TILE
    row = lambda n: pl.BlockSpec((tm, n), lambda i: (i, 0))
    const = lambda r, c: pl.BlockSpec((r, c), lambda i: (0, 0))
    return pl.pallas_call(
        _merge_kernel,
        grid=(t // tm,),
        in_specs=[row(1024), row(1024), row(1024),
                  pl.BlockSpec((tm, 1024), lambda i: (i, 1)),
                  pl.BlockSpec((tm, 1024), lambda i: (i, 2)),
                  const(1024, 1024), const(1024, 1024), const(1024, 1024), const(1, 1024),
                  const(1024, LANES), const(1, LANES)],
        out_specs=[row(1024), row(1024), row(TOP_K), row(TOP_K),
                   pl.BlockSpec((1, 1, LANES), lambda i: (i, 0, 0))],
        out_shape=[jax.ShapeDtypeStruct((t, 1024), F32),
                   jax.ShapeDtypeStruct((t, 1024), BF16),
                   jax.ShapeDtypeStruct((t, TOP_K), I32),
                   jax.ShapeDtypeStruct((t, TOP_K), F32),
                   jax.ShapeDtypeStruct((t // tm, 1, LANES), I32)],
        compiler_params=_cparams(("arbitrary",)),
        name="merge_route",
    )(x, hm, ha, og, og, wpm, wpa, wo, g2, wr, br)


def _stage_rows(idx, off_row):
    tm = idx.shape[0]
    lane = lax.broadcasted_iota(I32, (tm, LANES), 1)
    hits = [lane == idx[:, k:k + 1] for k in range(TOP_K)]
    onehot = sum(h.astype(F32) for h in hits)
    r_i = lax.broadcasted_iota(I32, (tm, tm), 0)
    c_i = lax.broadcasted_iota(I32, (tm, tm), 1)
    before = (c_i < r_i).astype(BF16)
    rank = jnp.dot(before, onehot.astype(BF16), preferred_element_type=F32)
    pos = rank + off_row
    return [jnp.sum(jnp.where(h, pos, 0.0), axis=1, keepdims=True) for h in hits]


HIGH_HALF = 0xFFFF0000


def _pack_rows(lo, hi):
    lo_u = lax.bitcast_convert_type(lo, jnp.uint32)
    hi_u = lax.bitcast_convert_type(hi, jnp.uint32)
    return (lo_u >> 16) | (hi_u & jnp.uint32(HIGH_HALF))


def _unpack_rows(u):
    lo = lax.bitcast_convert_type(u << 16, F32)
    hi = lax.bitcast_convert_type(u & jnp.uint32(HIGH_HALF), F32)
    return lo, hi


def _dispatch_kernel(ntile_a, cnt_ref, off_ref, base_ref, tail_ref, xa_ref, xb_ref, ia_ref, ib_ref, offv_ref,
                     xs_ref, stage, zeros, sem):
    i = pl.program_id(0)
    from_a = i < ntile_a
    xn = jnp.where(from_a, xa_ref[...], xb_ref[...])
    idx = jnp.where(from_a, ia_ref[...], ib_ref[...])
    rows = _stage_rows(idx, offv_ref[0])
    tm = idx.shape[0]
    lane = lax.broadcasted_iota(I32, (tm, LANES), 1)
    r4 = jnp.zeros((tm, LANES), F32)
    for k in range(TOP_K):
        r4 = jnp.where(lane == k, rows[k], r4)
    r4t = r4.T.astype(I32)
    r_iota = lax.broadcasted_iota(I32, (STAGE_ROWS, tm), 0)
    sel = jnp.zeros((STAGE_ROWS, tm), F32)
    for k in range(TOP_K):
        sel = sel + (r_iota == r4t[k:k + 1, :]).astype(F32)
    st = jnp.dot(sel.astype(BF16), xn, preferred_element_type=F32)
    stage[...] = _pack_rows(st[:, :512], st[:, 512:])
    copies = []
    for e in range(N_EXPERTS):
        n = pl.multiple_of(cnt_ref[i * N_EXPERTS + e], RUN_ALIGN)
        off = pl.multiple_of(off_ref[i * N_EXPERTS + e], RUN_ALIGN)
        base = pl.multiple_of(base_ref[i * N_EXPERTS + e], RUN_ALIGN)
        copies.append((n, pltpu.make_async_copy(stage.at[pl.ds(off, n)], xs_ref.at[pl.ds(base, n)], sem)))
    for n, cp in copies:
        @pl.when(n > 0)
        def _():
            cp.start()
    for n, cp in copies:
        @pl.when(n > 0)
        def _():
            cp.wait()

    @pl.when(i == pl.num_programs(0) - 1)
    def _():
        zeros[...] = jnp.zeros(zeros.shape, jnp.uint32)
        tails = []
        for e in range(N_EXPERTS):
            start = pl.multiple_of(tail_ref[e], RUN_ALIGN)
            n = pl.multiple_of(tail_ref[N_EXPERTS + e], RUN_ALIGN)
            tails.append((n, pltpu.make_async_copy(zeros.at[pl.ds(0, n)], xs_ref.at[pl.ds(start, n)], sem)))
        for n, cp in tails:
            @pl.when(n > 0)
            def _():
                cp.start()
        for n, cp in tails:
            @pl.when(n > 0)
            def _():
                cp.wait()
        first = tail_ref[2 * N_EXPERTS]
        n_unused = tail_ref[2 * N_EXPERTS + 1]

        def unused_block(j):
            row = pl.multiple_of((first + j) * MOE_BLOCK, MOE_BLOCK)
            return pltpu.make_async_copy(zeros, xs_ref.at[pl.ds(row, MOE_BLOCK)], sem)

        @pl.loop(0, n_unused)
        def _(j):
            unused_block(j).start()

        @pl.loop(0, n_unused)
        def _(j):
            unused_block(j).wait()


def _dispatch(xa, xb, ia, ib, cnt8, off, base, tails, offv, n_slots):
    tm = TOK_TILE
    ntile_a = xa.shape[0] // tm
    ntile_b = xb.shape[0] // tm
    amap = lambda i, *_: (jnp.minimum(i, ntile_a - 1), 0)
    bmap = lambda i, *_: (jnp.maximum(i - ntile_a, 0), 0)
    return pl.pallas_call(
        functools.partial(_dispatch_kernel, ntile_a),
        grid_spec=pltpu.PrefetchScalarGridSpec(
            num_scalar_prefetch=4,
            grid=(ntile_a + ntile_b,),
            in_specs=[pl.BlockSpec((tm, 1024), amap),
                      pl.BlockSpec((tm, 1024), bmap),
                      pl.BlockSpec((tm, TOP_K), amap),
                      pl.BlockSpec((tm, TOP_K), bmap),
                      pl.BlockSpec((1, 1, LANES), lambda i, *_: (i, 0, 0))],
            out_specs=pl.BlockSpec(memory_space=pl.ANY),
            scratch_shapes=[pltpu.VMEM((STAGE_ROWS, 512), jnp.uint32),
                            pltpu.VMEM((MOE_BLOCK, 512), jnp.uint32),
                            pltpu.SemaphoreType.DMA]),
        out_shape=jax.ShapeDtypeStruct((n_slots, 512), jnp.uint32),
        compiler_params=_cparams(("arbitrary",)),
        name="moe_dispatch",
    )(cnt8, off, base, tails, xa, xb, ia, ib, offv)


def _experts_kernel(be_ref, nb_ref, xs_ref, wgu_ref, bgu_ref, wd_ref, bd_ref, y_ref):
    i = pl.program_id(0)

    @pl.when(i < nb_ref[0])
    def _():
        lo, hi = _unpack_rows(xs_ref[...])
        h = (jnp.dot(lo.astype(BF16), wgu_ref[0, :512, :], preferred_element_type=F32)
             + jnp.dot(hi.astype(BF16), wgu_ref[0, 512:, :], preferred_element_type=F32) + bgu_ref[0])
        gate = jnp.minimum(h[:, :D_FF], SWIGLU_LIMIT)
        up = jnp.clip(h[:, D_FF:], -SWIGLU_LIMIT, SWIGLU_LIMIT)
        act = gate * jax.nn.sigmoid(SWIGLU_ALPHA * gate) * (up + 1.0)
        y_ref[...] = jnp.dot(act.astype(BF16), wd_ref[0], preferred_element_type=F32) + bd_ref[0]

    @pl.when(i >= nb_ref[0])
    def _():
        y_ref[...] = jnp.zeros(y_ref.shape, F32)


def _experts(block_expert, n_blocks, xs, wgu, bgu, wd, bd):
    nb_max = xs.shape[0] // MOE_BLOCK
    blk = lambda i, be, nb: (jnp.minimum(i, nb[0] - 1), 0)
    wmap = lambda i, be, nb: (be[i], 0, 0)
    return pl.pallas_call(
        _experts_kernel,
        grid_spec=pltpu.PrefetchScalarGridSpec(
            num_scalar_prefetch=2,
            grid=(nb_max,),
            in_specs=[pl.BlockSpec((MOE_BLOCK, 512), blk),
                      pl.BlockSpec((1, D_MODEL, 2 * D_FF), wmap),
                      pl.BlockSpec((1, 1, 2 * D_FF), wmap),
                      pl.BlockSpec((1, D_FF, D_MODEL), wmap),
                      pl.BlockSpec((1, 1, D_MODEL), wmap)],
            out_specs=pl.BlockSpec((MOE_BLOCK, D_MODEL), lambda i, be, nb: (i, 0))),
        out_shape=jax.ShapeDtypeStruct((xs.shape[0], D_MODEL), F32),
        compiler_params=_cparams(("arbitrary",)),
        name="moe_experts",
    )(block_expert, n_blocks, xs, wgu, bgu, wd, bd)


def _combine_kernel(cnt_ref, off_ref, base_ref, x1_ref, idx_ref, gate_ref, offv_ref, gf_ref, y_ref,
                    o_ref, stage, sem):
    i = pl.program_id(0)

    @pl.when(i == 0)
    def _():
        stage[...] = jnp.zeros(stage.shape, F32)

    copies = []
    for e in range(N_EXPERTS):
        n = pl.multiple_of(cnt_ref[i * N_EXPERTS + e], RUN_ALIGN)
        off = pl.multiple_of(off_ref[i * N_EXPERTS + e], RUN_ALIGN)
        base = pl.multiple_of(base_ref[i * N_EXPERTS + e], RUN_ALIGN)
        copies.append((n, pltpu.make_async_copy(y_ref.at[pl.ds(base, n)], stage.at[pl.ds(off, n)], sem)))
    for n, cp in copies:
        @pl.when(n > 0)
        def _():
            cp.start()
    rows = _stage_rows(idx_ref[...], offv_ref[0])
    tm = idx_ref.shape[0]
    lane = lax.broadcasted_iota(I32, (tm, STAGE_ROWS), 1)
    gates = gate_ref[...]
    gmat = jnp.zeros((tm, STAGE_ROWS), F32)
    for k in range(TOP_K):
        gmat = gmat + jnp.where(lane == rows[k].astype(I32), gates[:, k:k + 1], 0.0)
    for n, cp in copies:
        @pl.when(n > 0)
        def _():
            cp.wait()
    moe = jnp.dot(gmat.astype(BF16), stage[...].astype(BF16), preferred_element_type=F32)
    o_ref[...] = _rms(x1_ref[...] + moe, gf_ref[...])


def _combine(x1, idx, gates, cnt8, off, base, offv, gf, y):
    t = x1.shape[0]
    tm = TOK_TILE
    return pl.pallas_call(
        _combine_kernel,
        grid_spec=pltpu.PrefetchScalarGridSpec(
            num_scalar_prefetch=3,
            grid=(t // tm,),
            in_specs=[pl.BlockSpec((tm, 1024), lambda i, *_: (i, 0)),
                      pl.BlockSpec((tm, TOP_K), lambda i, *_: (i, 0)),
                      pl.BlockSpec((tm, TOP_K), lambda i, *_: (i, 0)),
                      pl.BlockSpec((1, 1, LANES), lambda i, *_: (i, 0, 0)),
                      pl.BlockSpec((1, 1024), lambda i, *_: (0, 0)),
                      pl.BlockSpec(memory_space=pl.ANY)],
            out_specs=pl.BlockSpec((tm, 1024), lambda i, *_: (i, 0)),
            scratch_shapes=[pltpu.VMEM((STAGE_ROWS, 1024), F32),
                            pltpu.SemaphoreType.DMA]),
        out_shape=jax.ShapeDtypeStruct((t, 1024), F32),
        compiler_params=_cparams(("arbitrary",)),
        name="moe_combine",
    )(cnt8, off, base, x1, idx, gates, offv, gf, y)


def _prep_w_in(w):
    q, k, v, o, ig, fg, qa, ka, va, gm, ga = jnp.split(w, [1024, 2048, 3072, 4096, 4100, 4104, 5128, 5384, 5640, 6664], axis=1)
    qa = qa.reshape(D_MODEL, A_KV_HEADS, A_GROUP, A_HD).transpose(0, 2, 1, 3).reshape(D_MODEL, 1024)
    pad = jnp.zeros((D_MODEL, LANES - 2 * M_HEADS), F32)
    return jnp.concatenate([q, k * (M_DK ** -0.5), v, o, gm, ga, qa, ka, va, ig, fg, pad], axis=1).astype(BF16)


def _rope_tables(pos):
    half = ROT_DIM // 2
    inv = ROPE_THETA ** (-(jnp.arange(half, dtype=F32) * 2.0 / ROT_DIM))
    ang = pos.astype(F32)[:, None] * inv[None, :]
    cos, sin = jnp.cos(ang), jnp.sin(ang)
    ones = jnp.ones((pos.shape[0], A_HD - ROT_DIM), F32)
    cos64 = jnp.concatenate([cos, cos, ones], axis=1)
    sin64 = jnp.concatenate([-sin, sin, 0.0 * ones], axis=1)
    return jnp.tile(cos64, (1, 2)), jnp.tile(sin64, (1, 2))


def _round_up(x, m):
    return (x + m - 1) // m * m


def kernel(x_prompt, x_sample, cache_k, cache_v, state_c, state_n, state_m, norm1, w_in, b_igate, b_fgate,
           mlstm_norm, w_proj_m, w_proj_a, attn_sinks, w_out, norm2, w_router, b_router, w_gate_up, b_gate_up,
           w_down, b_down, norm_f):
    bp, sp, _ = x_prompt.shape
    bs, ts, _ = x_sample.shape
    past_len = 16384
    l = 0
    w1 = _prep_w_in(w_in[l])
    g1 = norm1[l][None, :]
    g2 = norm2[l][None, :]
    gf = norm_f[None, :]
    b_i = b_igate[l][:, None]
    b_f = b_fgate[l][:, None]
    mn = mlstm_norm[l][None, :]
    wpm = w_proj_m[l].astype(BF16)
    wpa = w_proj_a[l].reshape(A_KV_HEADS, A_GROUP, A_HD, D_MODEL).transpose(1, 0, 2, 3).reshape(1024, D_MODEL).astype(BF16)
    wo = w_out[l].astype(BF16)
    wr = jnp.pad(w_router[l], ((0, 0), (0, LANES - N_EXPERTS)))
    br = jnp.pad(b_router[l], (0, LANES - N_EXPERTS))[None, :]
    wgu = w_gate_up[l].astype(BF16)
    bgu = b_gate_up[l][:, None, :]
    wd = w_down[l].astype(BF16)
    bd = b_down[l][:, None, :]
    sinks = attn_sinks[l]

    def mixer(x, nb, seq, pos0, L, c0, n0, m0, k_buf, v_buf, tm):
        t = nb * seq
        xf = x.reshape(t, D_MODEL)
        cos_t, sin_t = _rope_tables(pos0 + jnp.arange(seq))
        if seq < tm:
            cos_t, sin_t = jnp.tile(cos_t, (tm // seq, 1)), jnp.tile(sin_t, (tm // seq, 1))
        act = BF16 if L % 16 == 0 else F32
        qkv, og, qa, ka, va, gates = _inproj(xf, g1, w1, cos_t, sin_t, tm, max(seq // tm, 1), act)
        hm, c, n, m = _mlstm(qkv, gates, og, c0, n0, m0[:, :, None], b_i, b_f, mn, nb, L)
        if k_buf is None:
            ha = _swa_prompt(qa, ka, va, sinks, nb)
            k_keep = ka.reshape(nb, seq, A_KV_HEADS, A_HD)[:, seq - WINDOW:]
            v_keep = va.reshape(nb, seq, A_KV_HEADS, A_HD)[:, seq - WINDOW:]
        else:
            ha, k_keep, v_keep = _swa_sample(qa, ka, va, k_buf.reshape(nb, WINDOW, A_KV),
                                             v_buf.reshape(nb, WINDOW, A_KV), sinks, seq)
            k_keep = k_keep.reshape(nb, WINDOW, A_KV_HEADS, A_HD)
            v_keep = v_keep.reshape(nb, WINDOW, A_KV_HEADS, A_HD)
        x1, xn, idx, gate, cnt = _merge(xf, hm, ha, og, wpm, wpa, wo, g2, wr, br)
        return (x1, xn, idx, gate, cnt[:, 0, :N_EXPERTS]), (k_keep, v_keep, c, n, m[:, :, 0])

    zc = jnp.zeros((bp, M_HEADS, M_DV, M_DK), F32)
    zn = jnp.zeros((bp, M_HEADS, M_DK), F32)
    zm = jnp.zeros((bp, M_HEADS), F32)
    rp, sp_out = mixer(x_prompt, bp, sp, 0, M_CHUNK, zc, zn, zm, None, None, 512)
    rs, ss_out = mixer(x_sample, bs, ts, past_len, ts, state_c[l], state_n[l], state_m[l],
                       cache_k[l], cache_v[l], 256)

    cnt = jnp.concatenate([rp[4], rs[4]], axis=0)
    ntile_p = rp[4].shape[0]
    cnt8 = _round_up(cnt, RUN_ALIGN)
    off = jnp.cumsum(cnt8, axis=1) - cnt8
    per_expert = jnp.sum(cnt8, axis=0)
    padded = _round_up(per_expert, MOE_BLOCK)
    padded_end = jnp.cumsum(padded)
    expert_start = padded_end - padded
    base = expert_start[None, :] + jnp.cumsum(cnt8, axis=0) - cnt8
    n_tok = bp * sp + bs * ts
    n_slots = _round_up(n_tok * TOP_K + (n_tok // TOK_TILE) * N_EXPERTS * (RUN_ALIGN - 1), MOE_BLOCK) + N_EXPERTS * MOE_BLOCK
    nb_max = n_slots // MOE_BLOCK
    block_expert = jnp.minimum(
        jnp.searchsorted(padded_end, jnp.arange(nb_max) * MOE_BLOCK, side="right"), N_EXPERTS - 1).astype(I32)
    n_blocks = (padded_end[-1] // MOE_BLOCK).astype(I32)[None]
    tails = jnp.concatenate([expert_start + per_expert, padded - per_expert,
                             n_blocks, nb_max - n_blocks]).astype(I32)
    flat = lambda a: a.reshape(-1).astype(I32)
    offv = jnp.pad(off, ((0, 0), (0, LANES - N_EXPERTS))).astype(F32)[:, None, :]

    xs = _dispatch(rp[1], rs[1], rp[2], rs[2], flat(cnt8), flat(off), flat(base), tails, offv, n_slots)
    y = _experts(block_expert, n_blocks, xs, wgu, bgu, wd, bd)
    y_p = _combine(rp[0], rp[2], rp[3], flat(cnt8[:ntile_p]), flat(off[:ntile_p]), flat(base[:ntile_p]),
                   offv[:ntile_p], gf, y)
    y_s = _combine(rs[0], rs[2], rs[3], flat(cnt8[ntile_p:]), flat(off[ntile_p:]), flat(base[ntile_p:]),
                   offv[ntile_p:], gf, y)

    kp, vp, cp_, np_, mp = sp_out
    ks, vs, cs, ns, ms = ss_out
    return (y_p.reshape(bp, sp, D_MODEL), y_s.reshape(bs, ts, D_MODEL),
            kp[None], vp[None], cp_[None], np_[None], mp[None],
            ks[None], vs[None], cs[None], ns[None], ms[None])
```

```python
import functools

import jax
import jax.numpy as jnp
from jax import lax
from jax.experimental import pallas as pl
from jax.experimental.pallas import tpu as pltpu

F32 = jnp.float32
BF16 = jnp.bfloat16
I32 = jnp.int32

D_MODEL = 1024
M_HEADS = 4
M_DK = 256
M_DV = 256
M_CHUNK = 128
A_Q_HEADS = 16
A_KV_HEADS = 4
A_GROUP = 4
A_HD = 64
A_KV = A_KV_HEADS * A_HD
WINDOW = 128
A_BLOCK = 128
ROT_DIM = 16
ROPE_THETA = 500000.0
N_EXPERTS = 32
TOP_K = 4
D_FF = 1024
SWIGLU_LIMIT = 7.0
SWIGLU_ALPHA = 1.702
NORM_EPS = 1e-6

LANES = 128
SUBLANES = 8
VMEM_LIMIT = 56 * 1024 * 1024

C_QKV = 0
C_OG = 3072
C_QA = 6144
C_KA = 7168
C_VA = 7424
C_GATE = 7680
N_COLS = 7808

TOK_TILE = 256
RUN_ALIGN = SUBLANES
STAGE_ROWS = 1280
MOE_BLOCK = 256


def _cparams(sem):
    return pltpu.CompilerParams(dimension_semantics=sem, vmem_limit_bytes=VMEM_LIMIT)


def _rms(x, g):
    return x * lax.rsqrt(jnp.mean(x * x, axis=-1, keepdims=True) + NORM_EPS) * g


def _rope(x, cos_t, sin_t, first_half):
    n = x.shape[1]
    fwd = pltpu.roll(x, n - ROT_DIM // 2, axis=1)
    bwd = pltpu.roll(x, ROT_DIM // 2, axis=1)
    outs = []
    for b in range(n // LANES):
        sl = slice(b * LANES, (b + 1) * LANES)
        partner = jnp.where(first_half, fwd[:, sl], bwd[:, sl])
        outs.append(x[:, sl] * cos_t + partner * sin_t)
    return jnp.concatenate(outs, axis=1)


def _inproj_kernel(x_ref, g1_ref, w_ref, cos_ref, sin_ref,
                   qkv_ref, og_ref, qa_ref, ka_ref, va_ref, gate_ref):
    x = x_ref[...]
    xn = _rms(x, g1_ref[...]).astype(BF16)
    act = qkv_ref.dtype

    def proj(c0, n):
        return jnp.dot(xn, w_ref[:, c0:c0 + n], preferred_element_type=F32)

    for j in range(3):
        qkv_ref[:, j * 1024:(j + 1) * 1024] = proj(C_QKV + j * 1024, 1024).astype(act)
    for j in range(3):
        og_ref[:, j * 1024:(j + 1) * 1024] = jax.nn.sigmoid(proj(C_OG + j * 1024, 1024)).astype(act)
    cos_t = cos_ref[...]
    sin_t = sin_ref[...]
    lane = lax.broadcasted_iota(I32, (1, LANES), 1)
    first_half = (lane & (ROT_DIM - 1)) < (ROT_DIM // 2)
    qa_ref[...] = _rope(proj(C_QA, 1024), cos_t, sin_t, first_half).astype(act)
    ka_ref[...] = _rope(proj(C_KA, A_KV), cos_t, sin_t, first_half)
    va_ref[...] = proj(C_VA, A_KV)
    gate_ref[...] = proj(C_GATE, LANES)


def _inproj(x, g1, w, cos_t, sin_t, tm, rope_blocks, act):
    t = x.shape[0]
    tok = lambda n, dt: jax.ShapeDtypeStruct((t, n), dt)
    row = lambda n: pl.BlockSpec((tm, n), lambda i: (i, 0))
    return pl.pallas_call(
        _inproj_kernel,
        grid=(t // tm,),
        in_specs=[row(D_MODEL),
                  pl.BlockSpec((1, D_MODEL), lambda i: (0, 0)),
                  pl.BlockSpec((D_MODEL, N_COLS), lambda i: (0, 0), pipeline_mode=pl.Buffered(1)),
                  pl.BlockSpec((tm, LANES), lambda i: (i % rope_blocks, 0)),
                  pl.BlockSpec((tm, LANES), lambda i: (i % rope_blocks, 0))],
        out_specs=[row(3072), row(3072), row(1024), row(A_KV), row(A_KV), row(LANES)],
        out_shape=[tok(3072, act), tok(3072, act), tok(1024, act), tok(A_KV, F32), tok(A_KV, F32),
                   tok(LANES, F32)],
        compiler_params=_cparams(("arbitrary",)),
        name="inproj",
    )(x, g1, w, cos_t, sin_t)


def _mlstm_kernel(L, qkv_ref, gate_ref, og_ref, c0_ref, n0_ref, m0_ref, bi_ref, bf_ref, mn_ref,
                  h_ref, c_ref, n_ref, m_ref, c_s, n_s, m_s):
    ci = pl.program_id(1)

    @pl.when(ci == 0)
    def _():
        c_s[...] = c0_ref[0]
        n_s[...] = n0_ref[0]
        m_s[...] = m0_ref[0]

    gc = gate_ref[...]
    if L < LANES:
        gc = jnp.concatenate([gc, jnp.zeros((LANES - L, LANES), F32)], axis=0)
    gr = gc.T
    li = gr[0:M_HEADS] + bi_ref[...]
    z = gr[M_HEADS:2 * M_HEADS] + bf_ref[...]
    lf = jnp.minimum(z, 0.0) - jnp.log1p(jnp.exp(-jnp.abs(z)))
    r_i = lax.broadcasted_iota(I32, (LANES, LANES), 0)
    c_i = lax.broadcasted_iota(I32, (LANES, LANES), 1)
    upper = (r_i <= c_i).astype(F32)
    fcum = jnp.dot(lf, upper, preferred_element_type=F32, precision=lax.Precision.HIGHEST)
    a = li - fcum
    lane = lax.broadcasted_iota(I32, (M_HEADS, LANES), 1)
    cmx = a
    sh = 1
    while sh < L:
        cmx = jnp.maximum(cmx, jnp.where(lane >= sh, pltpu.roll(cmx, sh, axis=1), -jnp.inf))
        sh *= 2
    m_prev = m_s[...]
    m_t = jnp.maximum(fcum + m_prev, cmx + fcum)
    g_row = fcum - m_t
    inter_row = jnp.exp(fcum + m_prev - m_t)
    f_end = fcum[:, L - 1:L]
    m_end = m_t[:, L - 1:L]
    decay = jnp.exp(f_end + m_prev - m_end)
    wsrc_row = jnp.exp(a + (f_end - m_end))
    stack = jnp.concatenate([g_row, inter_row, wsrc_row, jnp.zeros((LANES - 3 * M_HEADS, LANES), F32)], axis=0)
    cols = stack.T[0:L]

    t_i = lax.broadcasted_iota(I32, (L, L), 0)
    s_i = lax.broadcasted_iota(I32, (L, L), 1)
    causal = s_i <= t_i
    nt = (((1,), (1,)), ((), ()))
    tn = (((0,), (0,)), ((), ()))
    for h in range(M_HEADS):
        q = qkv_ref[:, h * M_DK:(h + 1) * M_DK].astype(BF16)
        k = qkv_ref[:, 1024 + h * M_DK:1024 + (h + 1) * M_DK].astype(BF16)
        v = qkv_ref[:, 2048 + h * M_DV:2048 + (h + 1) * M_DV].astype(BF16)
        g_col = cols[:, h:h + 1]
        inter_col = cols[:, M_HEADS + h:M_HEADS + h + 1]
        w_col = cols[:, 2 * M_HEADS + h:2 * M_HEADS + h + 1]
        d = jnp.exp(jnp.where(causal, g_col + a[h:h + 1, 0:L], -jnp.inf))
        s = lax.dot_general(q, k, nt, preferred_element_type=F32)
        qk = s * d
        c_h = c_s[h]
        n_h = n_s[h:h + 1, :]
        cq = lax.dot_general(q, c_h.astype(BF16), nt, preferred_element_type=F32)
        pv = jnp.dot(qk.astype(BF16), v, preferred_element_type=F32)
        num = inter_col * cq + pv
        den = inter_col * jnp.sum(q.astype(F32) * n_h, axis=1, keepdims=True) + jnp.sum(qk, axis=1, keepdims=True)
        hh = num / jnp.maximum(jnp.abs(den), 1.0)
        hn = hh * lax.rsqrt(jnp.mean(hh * hh, axis=1, keepdims=True) + NORM_EPS)
        sl = slice(h * M_DV, (h + 1) * M_DV)
        h_ref[:, sl] = (hn * mn_ref[:, sl] * og_ref[:, sl].astype(F32)).astype(h_ref.dtype)
        kw = w_col * k.astype(F32)
        dec = decay[h:h + 1, 0:1]
        c_s[h] = dec * c_h + lax.dot_general(v, kw.astype(BF16), tn, preferred_element_type=F32)
        n_s[h:h + 1, :] = dec * n_h + jnp.sum(kw, axis=0, keepdims=True)
    m_s[...] = m_end

    @pl.when(ci == pl.num_programs(1) - 1)
    def _():
        c_ref[0] = c_s[...]
        n_ref[0] = n_s[...]
        m_ref[0] = m_s[...]


def _mlstm(qkv, gates, og, c0, n0, m0, b_i, b_f, mnorm, nb, L):
    t = qkv.shape[0]
    nc = t // (nb * L)
    tokmap = lambda b, c: (b * nc + c, 0)
    const2 = lambda b, c: (0, 0)
    return pl.pallas_call(
        functools.partial(_mlstm_kernel, L),
        grid=(nb, nc),
        in_specs=[pl.BlockSpec((L, 3072), tokmap),
                  pl.BlockSpec((L, LANES), tokmap),
                  pl.BlockSpec((L, 1024), tokmap),
                  pl.BlockSpec((1, M_HEADS, M_DV, M_DK), lambda b, c: (b, 0, 0, 0)),
                  pl.BlockSpec((1, M_HEADS, M_DK), lambda b, c: (b, 0, 0)),
                  pl.BlockSpec((1, M_HEADS, 1), lambda b, c: (b, 0, 0)),
                  pl.BlockSpec((M_HEADS, 1), const2),
                  pl.BlockSpec((M_HEADS, 1), const2),
                  pl.BlockSpec((1, 1024), const2)],
        out_specs=[pl.BlockSpec((L, 1024), tokmap),
                   pl.BlockSpec((1, M_HEADS, M_DV, M_DK), lambda b, c: (b, 0, 0, 0)),
                   pl.BlockSpec((1, M_HEADS, M_DK), lambda b, c: (b, 0, 0)),
                   pl.BlockSpec((1, M_HEADS, 1), lambda b, c: (b, 0, 0))],
        out_shape=[jax.ShapeDtypeStruct((t, 1024), qkv.dtype),
                   jax.ShapeDtypeStruct((nb, M_HEADS, M_DV, M_DK), F32),
                   jax.ShapeDtypeStruct((nb, M_HEADS, M_DK), F32),
                   jax.ShapeDtypeStruct((nb, M_HEADS, 1), F32)],
        scratch_shapes=[pltpu.VMEM((M_HEADS, M_DV, M_DK), F32),
                        pltpu.VMEM((M_HEADS, M_DK), F32),
                        pltpu.VMEM((M_HEADS, 1), F32)],
        compiler_params=_cparams(("arbitrary", "arbitrary")),
        name="mlstm_L%d" % L,
    )(qkv, gates, og, c0, n0, m0, b_i, b_f, mnorm)


def _swa_bias(m, has_prev):
    nkeys = 2 * A_BLOCK
    t_i = lax.broadcasted_iota(I32, (A_GROUP * m, nkeys), 0) & (m - 1)
    s_i = lax.broadcasted_iota(I32, (A_GROUP * m, nkeys), 1)
    diff = t_i + A_BLOCK - s_i
    mask = (diff >= 0) & (diff <= WINDOW)
    if not has_prev:
        mask = mask & (s_i >= A_BLOCK)
    return jnp.where(mask, 0.0, -jnp.inf).astype(F32)


def _swa_group_ones():
    nkeys = 2 * A_BLOCK
    r = lax.broadcasted_iota(I32, (A_KV_HEADS * nkeys, A_KV), 0) >> (nkeys.bit_length() - 1)
    c = lax.broadcasted_iota(I32, (A_KV_HEADS * nkeys, A_KV), 1) >> (A_HD.bit_length() - 1)
    return (r == c).astype(BF16)


def _swa_core(q, k_prev, v_prev, k_cur, v_cur, bias, group_ones, sink_ref):
    m = q.shape[0]
    q = q.astype(BF16)
    qst = jnp.concatenate([q[:, j * A_KV:(j + 1) * A_KV] for j in range(A_GROUP)], axis=0)
    k_all = jnp.concatenate([k_prev, k_cur], axis=0)
    v_all = jnp.concatenate([v_prev, v_cur], axis=0)
    lane = lax.broadcasted_iota(I32, (1, A_KV), 1)
    kbd, vbd = [], []
    for g in range(A_KV_HEADS):
        in_g = (lane >= g * A_HD) & (lane < (g + 1) * A_HD)
        kbd.append(jnp.where(in_g, k_all, 0.0).astype(BF16))
        vbd.append(jnp.where(in_g, v_all, 0.0).astype(BF16))
    kbd = jnp.concatenate(kbd, axis=0)
    vbd = jnp.concatenate(vbd, axis=0)
    s = lax.dot_general(qst, kbd, (((1,), (1,)), ((), ())), preferred_element_type=F32)
    nkeys = 2 * A_BLOCK
    ps, sink_terms = [], []
    for g in range(A_KV_HEADS):
        sg = s[:, g * nkeys:(g + 1) * nkeys] + bias
        sink = jnp.concatenate(
            [jnp.full((m, LANES), sink_ref[A_GROUP * g + j], F32) for j in range(A_GROUP)], axis=0)
        mx = jnp.maximum(jnp.broadcast_to(jnp.max(sg, axis=1, keepdims=True), (A_GROUP * m, LANES)), sink)
        ps += [jnp.exp(sg[:, :LANES] - mx).astype(BF16), jnp.exp(sg[:, LANES:] - mx).astype(BF16)]
        sink_terms.append(jnp.exp(sink - mx))
    p = jnp.concatenate(ps, axis=1)
    o = jnp.dot(p, vbd, preferred_element_type=F32)
    den = jnp.dot(p, group_ones, preferred_element_type=F32)
    low = lax.broadcasted_iota(I32, (1, LANES), 1) < A_HD
    den = den + jnp.concatenate([jnp.where(low, sink_terms[0], sink_terms[1]),
                                 jnp.where(low, sink_terms[2], sink_terms[3])], axis=1)
    o = o / den
    return jnp.concatenate([o[j * m:(j + 1) * m] for j in range(A_GROUP)], axis=1)


SWA_PAIR = 2 * A_BLOCK


def _swa_prompt_kernel(sink_ref, q_ref, kp_ref, kc_ref, vp_ref, vc_ref, o_ref, bias_s, bias0_s, ones_s):
    first = (pl.program_id(0) == 0) & (pl.program_id(1) == 0)

    @pl.when(first)
    def _():
        bias_s[...] = _swa_bias(A_BLOCK, True)
        bias0_s[...] = _swa_bias(A_BLOCK, False)
        ones_s[...] = _swa_group_ones()

    bias = bias_s[...]
    bias_a = jnp.where(pl.program_id(1) > 0, bias, bias0_s[...])
    ones = ones_s[...]
    k0, k1 = kc_ref[0:A_BLOCK], kc_ref[A_BLOCK:SWA_PAIR]
    v0, v1 = vc_ref[0:A_BLOCK], vc_ref[A_BLOCK:SWA_PAIR]
    o_ref[0:A_BLOCK] = _swa_core(q_ref[0:A_BLOCK], kp_ref[...], vp_ref[...], k0, v0,
                                 bias_a, ones, sink_ref).astype(o_ref.dtype)
    o_ref[A_BLOCK:SWA_PAIR] = _swa_core(q_ref[A_BLOCK:SWA_PAIR], k0, v0, k1, v1,
                                        bias, ones, sink_ref).astype(o_ref.dtype)


def _swa_prompt(qa, ka, va, sinks, nb):
    t = qa.shape[0]
    npair = t // (nb * SWA_PAIR)
    cur = lambda b, i, s: (b * npair + i, 0)
    prev = lambda b, i, s: (2 * (b * npair + i) - jnp.minimum(i, 1), 0)
    return pl.pallas_call(
        _swa_prompt_kernel,
        grid_spec=pltpu.PrefetchScalarGridSpec(
            num_scalar_prefetch=1,
            grid=(nb, npair),
            in_specs=[pl.BlockSpec((SWA_PAIR, 1024), cur),
                      pl.BlockSpec((A_BLOCK, A_KV), prev),
                      pl.BlockSpec((SWA_PAIR, A_KV), cur),
                      pl.BlockSpec((A_BLOCK, A_KV), prev),
                      pl.BlockSpec((SWA_PAIR, A_KV), cur)],
            out_specs=pl.BlockSpec((SWA_PAIR, 1024), cur),
            scratch_shapes=[pltpu.VMEM((A_GROUP * A_BLOCK, SWA_PAIR), F32),
                            pltpu.VMEM((A_GROUP * A_BLOCK, SWA_PAIR), F32),
                            pltpu.VMEM((A_KV_HEADS * SWA_PAIR, A_KV), BF16)]),
        out_shape=jax.ShapeDtypeStruct((t, 1024), BF16),
        compiler_params=_cparams(("arbitrary", "arbitrary")),
        name="swa_prompt",
    )(sinks, qa, ka, ka, va, va)


SWA_SAMPLE_GROUP = 4


def _swa_sample_kernel(T, sink_ref, q_ref, kn_ref, vn_ref, kb_ref, vb_ref, o_ref, ko_ref, vo_ref):
    pad = jnp.zeros((A_BLOCK - T, A_KV), F32)
    bias = _swa_bias(T, True)
    ones = _swa_group_ones()
    for j in range(SWA_SAMPLE_GROUP):
        rows = slice(j * T, (j + 1) * T)
        k_new = kn_ref[rows]
        v_new = vn_ref[rows]
        k_buf = kb_ref[j]
        v_buf = vb_ref[j]
        o_ref[rows] = _swa_core(q_ref[rows], k_buf, v_buf, jnp.concatenate([k_new, pad], axis=0),
                                jnp.concatenate([v_new, pad], axis=0), bias, ones, sink_ref).astype(o_ref.dtype)
        ko_ref[j] = jnp.concatenate([k_buf[T:], k_new], axis=0)
        vo_ref[j] = jnp.concatenate([v_buf[T:], v_new], axis=0)


def _swa_sample(qa, ka, va, k_buf, v_buf, sinks, T):
    nb = k_buf.shape[0]
    g = SWA_SAMPLE_GROUP
    tok = lambda b, s: (b, 0)
    buf = lambda b, s: (b, 0, 0)
    return pl.pallas_call(
        functools.partial(_swa_sample_kernel, T),
        grid_spec=pltpu.PrefetchScalarGridSpec(
            num_scalar_prefetch=1,
            grid=(nb // g,),
            in_specs=[pl.BlockSpec((g * T, 1024), tok),
                      pl.BlockSpec((g * T, A_KV), tok),
                      pl.BlockSpec((g * T, A_KV), tok),
                      pl.BlockSpec((g, WINDOW, A_KV), buf),
                      pl.BlockSpec((g, WINDOW, A_KV), buf)],
            out_specs=[pl.BlockSpec((g * T, 1024), tok),
                       pl.BlockSpec((g, WINDOW, A_KV), buf),
                       pl.BlockSpec((g, WINDOW, A_KV), buf)]),
        out_shape=[jax.ShapeDtypeStruct((nb * T, 1024), qa.dtype),
                   jax.ShapeDtypeStruct((nb, WINDOW, A_KV), F32),
                   jax.ShapeDtypeStruct((nb, WINDOW, A_KV), F32)],
        compiler_params=_cparams(("arbitrary",)),
        name="swa_sample",
    )(sinks, qa, ka, va, k_buf, v_buf)


def _merge_kernel(x_ref, hm_ref, ha_ref, gm_ref, ga_ref, wpm_ref, wpa_ref, wo_ref, g2_ref, wr_ref, br_ref,
                  x1_ref, xn_ref, idx_ref, gate_ref, cnt_ref):
    pm = jnp.dot(hm_ref[...].astype(BF16), wpm_ref[...], preferred_element_type=F32)
    pa = jnp.dot(ha_ref[...].astype(BF16), wpa_ref[...], preferred_element_type=F32)
    mixed = gm_ref[...].astype(F32) * pm + ga_ref[...].astype(F32) * pa
    x1 = x_ref[...] + jnp.dot(mixed.astype(BF16), wo_ref[...], preferred_element_type=F32)
    x1_ref[...] = x1
    xn = _rms(x1, g2_ref[...])
    xn_ref[...] = xn.astype(BF16)
    xn_hi = xn.astype(BF16)
    xn_lo = (xn - xn_hi.astype(F32)).astype(BF16)
    part = jnp.dot(xn_hi, wr_ref[...], preferred_element_type=F32)
    logits = (part[:, :LANES] + part[:, LANES:]
              + jnp.dot(xn_lo, wr_ref[:, :LANES], preferred_element_type=F32) + br_ref[...])
    tm = logits.shape[0]
    cur = logits.T[0:N_EXPERTS]
    eid = lax.broadcasted_iota(I32, (N_EXPERTS, tm), 0).astype(F32)
    vals, idxs = [], []
    onehot = jnp.zeros((N_EXPERTS, tm), F32)
    for _ in range(TOP_K):
        mval = jnp.max(cur, axis=0, keepdims=True)
        sel = jnp.min(jnp.where(cur == mval, eid, float(N_EXPERTS)), axis=0, keepdims=True)
        hit = eid == sel
        onehot = onehot + hit.astype(F32)
        cur = jnp.where(hit, -jnp.inf, cur)
        vals.append(mval)
        idxs.append(sel)
    es = [jnp.exp(v - vals[0]) for v in vals]
    tot = es[0] + es[1] + es[2] + es[3]
    row = lax.broadcasted_iota(I32, (SUBLANES, tm), 0)
    res = jnp.zeros((SUBLANES, tm), F32)
    for k in range(TOP_K):
        res = jnp.where(row == k, idxs[k], res)
        res = jnp.where(row == TOP_K + k, es[k] / tot, res)
    res_t = jnp.concatenate([res, jnp.zeros((LANES - SUBLANES, tm), F32)], axis=0).T
    idx_ref[...] = res_t[:, 0:TOP_K].astype(I32)
    gate_ref[...] = res_t[:, TOP_K:2 * TOP_K]
    cnt_ref[0] = jnp.sum(onehot, axis=1, keepdims=True).astype(I32)


def _merge(x, hm, ha, og, wpm, wpa, wo, g2, wr, br):
    t = x.shape[0]
    tm = TOK_TILE
    row = lambda n: pl.BlockSpec((tm, n), lambda i: (i, 0))
    const = lambda r, c: pl.BlockSpec((r, c), lambda i: (0, 0))
    return pl.pallas_call(
        _merge_kernel,
        grid=(t // tm,),
        in_specs=[row(1024), row(1024), row(1024),
                  pl.BlockSpec((tm, 1024), lambda i: (i, 1)),
                  pl.BlockSpec((tm, 1024), lambda i: (i, 2)),
                  const(1024, 1024), const(1024, 1024), const(1024, 1024), const(1, 1024),
                  const(1024, 2 * LANES), const(1, LANES)],
        out_specs=[row(1024), row(1024), row(TOP_K), row(TOP_K),
                   pl.BlockSpec((1, N_EXPERTS, 1), lambda i: (i, 0, 0))],
        out_shape=[jax.ShapeDtypeStruct((t, 1024), F32),
                   jax.ShapeDtypeStruct((t, 1024), BF16),
                   jax.ShapeDtypeStruct((t, TOP_K), I32),
                   jax.ShapeDtypeStruct((t, TOP_K), F32),
                   jax.ShapeDtypeStruct((t // tm, N_EXPERTS, 1), I32)],
        compiler_params=_cparams(("arbitrary",)),
        name="merge_route",
    )(x, hm, ha, og, og, wpm, wpa, wo, g2, wr, br)


def _stage_rows(idx, off_row):
    tm = idx.shape[0]
    lane = lax.broadcasted_iota(I32, (tm, LANES), 1)
    hits = [lane == idx[:, k:k + 1] for k in range(TOP_K)]
    onehot = sum(h.astype(F32) for h in hits)
    r_i = lax.broadcasted_iota(I32, (tm, tm), 0)
    c_i = lax.broadcasted_iota(I32, (tm, tm), 1)
    before = (c_i < r_i).astype(BF16)
    rank = jnp.dot(before, onehot.astype(BF16), preferred_element_type=F32)
    pos = rank + off_row
    return [jnp.sum(jnp.where(h, pos, 0.0), axis=1, keepdims=True) for h in hits]


def _dispatch_kernel(ntile_a, cnt_ref, off_ref, base_ref, tail_ref, xa_ref, xb_ref, ia_ref, ib_ref, offv_ref,
                     xs_ref, stage, zeros, sem, zsem):
    i = pl.program_id(0)
    from_a = i < ntile_a
    xn = jnp.where(from_a, xa_ref[...], xb_ref[...])
    idx = jnp.where(from_a, ia_ref[...], ib_ref[...])
    rows = _stage_rows(idx, offv_ref[0])
    tm = idx.shape[0]
    lane = lax.broadcasted_iota(I32, (tm, LANES), 1)
    r4 = jnp.zeros((tm, LANES), F32)
    for k in range(TOP_K):
        r4 = jnp.where(lane == k, rows[k], r4)
    r4t = r4.T.astype(I32)
    r_iota = lax.broadcasted_iota(I32, (STAGE_ROWS, tm), 0)
    sel = jnp.zeros((STAGE_ROWS, tm), F32)
    for k in range(TOP_K):
        sel = sel + (r_iota == r4t[k:k + 1, :]).astype(F32)
    staged = jnp.dot(sel.astype(BF16), xn, preferred_element_type=F32)

    def runs(tile, slot):
        out = []
        for e in range(N_EXPERTS):
            n = pl.multiple_of(cnt_ref[tile * N_EXPERTS + e], RUN_ALIGN)
            off = pl.multiple_of(off_ref[tile * N_EXPERTS + e], RUN_ALIGN)
            base = pl.multiple_of(base_ref[tile * N_EXPERTS + e], RUN_ALIGN)
            out.append((n, pltpu.make_async_copy(stage.at[slot, pl.ds(off, n)], xs_ref.at[pl.ds(base, n)],
                                                 sem.at[slot])))
        return out

    def wait_runs(tile, slot):
        for n, cp in runs(tile, slot):
            @pl.when(n > 0)
            def _():
                cp.wait()

    for slot in range(2):
        @pl.when((i & 1) == slot)
        def _():
            @pl.when(i >= 2)
            def _():
                wait_runs(i - 2, slot)
            stage[slot] = staged
            for n, cp in runs(i, slot):
                @pl.when(n > 0)
                def _():
                    cp.start()

    last = pl.num_programs(0) - 1

    @pl.when(i == last)
    def _():
        for slot in range(2):
            @pl.when(((last - 1) & 1) == slot)
            def _():
                @pl.when(last >= 1)
                def _():
                    wait_runs(last - 1, slot)

            @pl.when((last & 1) == slot)
            def _():
                wait_runs(last, slot)
        zeros[...] = jnp.zeros(zeros.shape, F32)
        tails = []
        for e in range(N_EXPERTS):
            start = pl.multiple_of(tail_ref[e], RUN_ALIGN)
            n = pl.multiple_of(tail_ref[N_EXPERTS + e], RUN_ALIGN)
            tails.append((n, pltpu.make_async_copy(zeros.at[pl.ds(0, n)], xs_ref.at[pl.ds(start, n)], zsem)))
        for n, cp in tails:
            @pl.when(n > 0)
            def _():
                cp.start()
        for n, cp in tails:
            @pl.when(n > 0)
            def _():
                cp.wait()
        first = tail_ref[2 * N_EXPERTS]
        n_unused = tail_ref[2 * N_EXPERTS + 1]

        def unused_block(j):
            row = pl.multiple_of((first + j) * MOE_BLOCK, MOE_BLOCK)
            return pltpu.make_async_copy(zeros, xs_ref.at[pl.ds(row, MOE_BLOCK)], zsem)

        @pl.loop(0, n_unused)
        def _(j):
            unused_block(j).start()

        @pl.loop(0, n_unused)
        def _(j):
            unused_block(j).wait()


def _dispatch(xa, xb, ia, ib, cnt8, off, base, tails, offv, n_slots):
    tm = TOK_TILE
    ntile_a = xa.shape[0] // tm
    ntile_b = xb.shape[0] // tm
    amap = lambda i, *_: (jnp.minimum(i, ntile_a - 1), 0)
    bmap = lambda i, *_: (jnp.maximum(i - ntile_a, 0), 0)
    return pl.pallas_call(
        functools.partial(_dispatch_kernel, ntile_a),
        grid_spec=pltpu.PrefetchScalarGridSpec(
            num_scalar_prefetch=4,
            grid=(ntile_a + ntile_b,),
            in_specs=[pl.BlockSpec((tm, 1024), amap),
                      pl.BlockSpec((tm, 1024), bmap),
                      pl.BlockSpec((tm, TOP_K), amap),
                      pl.BlockSpec((tm, TOP_K), bmap),
                      pl.BlockSpec((1, 1, LANES), lambda i, *_: (i, 0, 0))],
            out_specs=pl.BlockSpec(memory_space=pl.ANY),
            scratch_shapes=[pltpu.VMEM((2, STAGE_ROWS, D_MODEL), F32),
                            pltpu.VMEM((MOE_BLOCK, D_MODEL), F32),
                            pltpu.SemaphoreType.DMA((2,)),
                            pltpu.SemaphoreType.DMA]),
        out_shape=jax.ShapeDtypeStruct((n_slots, D_MODEL), F32),
        compiler_params=_cparams(("arbitrary",)),
        name="moe_dispatch",
    )(cnt8, off, base, tails, xa, xb, ia, ib, offv)


def _experts_kernel(be_ref, nb_ref, xs_ref, wgu_ref, bgu_ref, wd_ref, bd_ref, y_ref, wgu_s, wd_s):
    i = pl.program_id(0)
    live = i < nb_ref[0]
    new_expert = (i == 0) | (be_ref[i] != be_ref[jnp.maximum(i - 1, 0)])

    @pl.when(live & new_expert)
    def _():
        wgu_s[...] = wgu_ref[0].astype(BF16)
        wd_s[...] = wd_ref[0].astype(BF16)

    @pl.when(live)
    def _():
        h = jnp.dot(xs_ref[...].astype(BF16), wgu_s[...], preferred_element_type=F32) + bgu_ref[0]
        gate = jnp.minimum(h[:, :D_FF], SWIGLU_LIMIT)
        up = jnp.clip(h[:, D_FF:], -SWIGLU_LIMIT, SWIGLU_LIMIT)
        act = gate * jax.nn.sigmoid(SWIGLU_ALPHA * gate) * (up + 1.0)
        y_ref[...] = jnp.dot(act.astype(BF16), wd_s[...], preferred_element_type=F32) + bd_ref[0]

    @pl.when(i >= nb_ref[0])
    def _():
        y_ref[...] = jnp.zeros(y_ref.shape, F32)


def _experts(block_expert, n_blocks, xs, wgu, bgu, wd, bd):
    nb_max = xs.shape[0] // MOE_BLOCK
    blk = lambda i, be, nb: (jnp.minimum(i, nb[0] - 1), 0)
    wmap = lambda i, be, nb: (be[i], 0, 0)
    return pl.pallas_call(
        _experts_kernel,
        grid_spec=pltpu.PrefetchScalarGridSpec(
            num_scalar_prefetch=2,
            grid=(nb_max,),
            in_specs=[pl.BlockSpec((MOE_BLOCK, D_MODEL), blk),
                      pl.BlockSpec((1, D_MODEL, 2 * D_FF), wmap),
                      pl.BlockSpec((1, 1, 2 * D_FF), wmap),
                      pl.BlockSpec((1, D_FF, D_MODEL), wmap),
                      pl.BlockSpec((1, 1, D_MODEL), wmap)],
            out_specs=pl.BlockSpec((MOE_BLOCK, D_MODEL), lambda i, be, nb: (i, 0)),
            scratch_shapes=[pltpu.VMEM((D_MODEL, 2 * D_FF), BF16),
                            pltpu.VMEM((D_FF, D_MODEL), BF16)]),
        out_shape=jax.ShapeDtypeStruct((xs.shape[0], D_MODEL), F32),
        compiler_params=_cparams(("arbitrary",)),
        name="moe_experts",
    )(block_expert, n_blocks, xs, wgu, bgu, wd, bd)


def _combine_kernel(cnt_ref, off_ref, base_ref, x1_ref, idx_ref, gate_ref, offv_ref, gf_ref, y_ref,
                    o_ref, stage, sem):
    i = pl.program_id(0)
    last = pl.num_programs(0) - 1

    def runs(tile, slot):
        out = []
        for e in range(N_EXPERTS):
            n = pl.multiple_of(cnt_ref[tile * N_EXPERTS + e], RUN_ALIGN)
            off = pl.multiple_of(off_ref[tile * N_EXPERTS + e], RUN_ALIGN)
            base = pl.multiple_of(base_ref[tile * N_EXPERTS + e], RUN_ALIGN)
            out.append((n, pltpu.make_async_copy(y_ref.at[pl.ds(base, n)], stage.at[slot, pl.ds(off, n)],
                                                 sem.at[slot])))
        return out

    def start_runs(tile, slot):
        for n, cp in runs(tile, slot):
            @pl.when(n > 0)
            def _():
                cp.start()

    @pl.when(i == 0)
    def _():
        stage[...] = jnp.zeros(stage.shape, F32)
        start_runs(0, 0)

    for slot in range(2):
        @pl.when(((i & 1) != slot) & (i < last))
        def _():
            start_runs(i + 1, slot)

    rows = _stage_rows(idx_ref[...], offv_ref[0])
    tm = idx_ref.shape[0]
    lane = lax.broadcasted_iota(I32, (tm, STAGE_ROWS), 1)
    gates = gate_ref[...]
    gmat = jnp.zeros((tm, STAGE_ROWS), F32)
    for k in range(TOP_K):
        gmat = gmat + jnp.where(lane == rows[k].astype(I32), gates[:, k:k + 1], 0.0)
    gmat = gmat.astype(BF16)
    for slot in range(2):
        @pl.when((i & 1) == slot)
        def _():
            for n, cp in runs(i, slot):
                @pl.when(n > 0)
                def _():
                    cp.wait()
            moe = jnp.dot(gmat, stage[slot].astype(BF16), preferred_element_type=F32)
            o_ref[...] = _rms(x1_ref[...] + moe, gf_ref[...])


def _combine(x1, idx, gates, cnt8, off, base, offv, gf, y):
    t = x1.shape[0]
    tm = TOK_TILE
    return pl.pallas_call(
        _combine_kernel,
        grid_spec=pltpu.PrefetchScalarGridSpec(
            num_scalar_prefetch=3,
            grid=(t // tm,),
            in_specs=[pl.BlockSpec((tm, 1024), lambda i, *_: (i, 0)),
                      pl.BlockSpec((tm, TOP_K), lambda i, *_: (i, 0)),
                      pl.BlockSpec((tm, TOP_K), lambda i, *_: (i, 0)),
                      pl.BlockSpec((1, 1, LANES), lambda i, *_: (i, 0, 0)),
                      pl.BlockSpec((1, 1024), lambda i, *_: (0, 0)),
                      pl.BlockSpec(memory_space=pl.ANY)],
            out_specs=pl.BlockSpec((tm, 1024), lambda i, *_: (i, 0)),
            scratch_shapes=[pltpu.VMEM((2, STAGE_ROWS, 1024), F32),
                            pltpu.SemaphoreType.DMA((2,))]),
        out_shape=jax.ShapeDtypeStruct((t, 1024), F32),
        compiler_params=_cparams(("arbitrary",)),
        name="moe_combine",
    )(cnt8, off, base, x1, idx, gates, offv, gf, y)


def _prep_w_in(w):
    q, k, v, o, ig, fg, qa, ka, va, gm, ga = jnp.split(w, [1024, 2048, 3072, 4096, 4100, 4104, 5128, 5384, 5640, 6664], axis=1)
    qa = qa.reshape(D_MODEL, A_KV_HEADS, A_GROUP, A_HD).transpose(0, 2, 1, 3).reshape(D_MODEL, 1024)
    pad = jnp.zeros((D_MODEL, LANES - 2 * M_HEADS), F32)
    return jnp.concatenate([q, k * (M_DK ** -0.5), v, o, gm, ga, qa * (A_HD ** -0.5), ka, va, ig, fg, pad],
                           axis=1).astype(BF16)


def _rope_tables(pos):
    half = ROT_DIM // 2
    inv = ROPE_THETA ** (-(jnp.arange(half, dtype=F32) * 2.0 / ROT_DIM))
    ang = pos.astype(F32)[:, None] * inv[None, :]
    cos, sin = jnp.cos(ang), jnp.sin(ang)
    ones = jnp.ones((pos.shape[0], A_HD - ROT_DIM), F32)
    cos64 = jnp.concatenate([cos, cos, ones], axis=1)
    sin64 = jnp.concatenate([-sin, sin, 0.0 * ones], axis=1)
    return jnp.tile(cos64, (1, 2)), jnp.tile(sin64, (1, 2))


def _round_up(x, m):
    return (x + m - 1) // m * m


def kernel(x_prompt, x_sample, cache_k, cache_v, state_c, state_n, state_m, norm1, w_in, b_igate, b_fgate,
           mlstm_norm, w_proj_m, w_proj_a, attn_sinks, w_out, norm2, w_router, b_router, w_gate_up, b_gate_up,
           w_down, b_down, norm_f):
    bp, sp, _ = x_prompt.shape
    bs, ts, _ = x_sample.shape
    past_len = 16384
    l = 0
    w1 = _prep_w_in(w_in[l])
    g1 = norm1[l][None, :]
    g2 = norm2[l][None, :]
    gf = norm_f[None, :]
    b_i = b_igate[l][:, None]
    b_f = b_fgate[l][:, None]
    mn = mlstm_norm[l][None, :]
    wpm = w_proj_m[l].astype(BF16)
    wpa = w_proj_a[l].reshape(A_KV_HEADS, A_GROUP, A_HD, D_MODEL).transpose(1, 0, 2, 3).reshape(1024, D_MODEL).astype(BF16)
    wo = w_out[l].astype(BF16)
    wr_f = jnp.pad(w_router[l], ((0, 0), (0, LANES - N_EXPERTS)))
    wr_hi = wr_f.astype(BF16)
    wr = jnp.concatenate([wr_hi, (wr_f - wr_hi.astype(F32)).astype(BF16)], axis=1)
    br = jnp.pad(b_router[l], (0, LANES - N_EXPERTS))[None, :]
    wgu = w_gate_up[l]
    bgu = b_gate_up[l][:, None, :]
    wd = w_down[l]
    bd = b_down[l][:, None, :]
    sinks = attn_sinks[l]

    def mixer(x, nb, seq, pos0, L, c0, n0, m0, k_buf, v_buf, tm):
        t = nb * seq
        xf = x.reshape(t, D_MODEL)
        cos_t, sin_t = _rope_tables(pos0 + jnp.arange(seq))
        if seq < tm:
            cos_t, sin_t = jnp.tile(cos_t, (tm // seq, 1)), jnp.tile(sin_t, (tm // seq, 1))
        act = BF16 if L % 16 == 0 else F32
        qkv, og, qa, ka, va, gates = _inproj(xf, g1, w1, cos_t, sin_t, tm, max(seq // tm, 1), act)
        hm, c, n, m = _mlstm(qkv, gates, og, c0, n0, m0[:, :, None], b_i, b_f, mn, nb, L)
        if k_buf is None:
            ha = _swa_prompt(qa, ka, va, sinks, nb)
            k_keep = ka.reshape(nb, seq, A_KV_HEADS, A_HD)[:, seq - WINDOW:]
            v_keep = va.reshape(nb, seq, A_KV_HEADS, A_HD)[:, seq - WINDOW:]
        else:
            ha, k_keep, v_keep = _swa_sample(qa, ka, va, k_buf.reshape(nb, WINDOW, A_KV),
                                             v_buf.reshape(nb, WINDOW, A_KV), sinks, seq)
            k_keep = k_keep.reshape(nb, WINDOW, A_KV_HEADS, A_HD)
            v_keep = v_keep.reshape(nb, WINDOW, A_KV_HEADS, A_HD)
        x1, xn, idx, gate, cnt = _merge(xf, hm, ha, og, wpm, wpa, wo, g2, wr, br)
        return (x1, xn, idx, gate, cnt[:, :, 0]), (k_keep, v_keep, c, n, m[:, :, 0])

    zc = jnp.zeros((bp, M_HEADS, M_DV, M_DK), F32)
    zn = jnp.zeros((bp, M_HEADS, M_DK), F32)
    zm = jnp.zeros((bp, M_HEADS), F32)
    rp, sp_out = mixer(x_prompt, bp, sp, 0, M_CHUNK, zc, zn, zm, None, None, 512)
    rs, ss_out = mixer(x_sample, bs, ts, past_len, ts, state_c[l], state_n[l], state_m[l],
                       cache_k[l], cache_v[l], 256)

    cnt = jnp.concatenate([rp[4], rs[4]], axis=0)
    ntile_p = rp[4].shape[0]
    cnt8 = _round_up(cnt, RUN_ALIGN)
    off = jnp.cumsum(cnt8, axis=1) - cnt8
    per_expert = jnp.sum(cnt8, axis=0)
    padded = _round_up(per_expert, MOE_BLOCK)
    padded_end = jnp.cumsum(padded)
    expert_start = padded_end - padded
    base = expert_start[None, :] + jnp.cumsum(cnt8, axis=0) - cnt8
    n_tok = bp * sp + bs * ts
    n_slots = _round_up(n_tok * TOP_K + (n_tok // TOK_TILE) * N_EXPERTS * (RUN_ALIGN - 1), MOE_BLOCK) + N_EXPERTS * MOE_BLOCK
    nb_max = n_slots // MOE_BLOCK
    block_row = jnp.arange(nb_max, dtype=I32) * MOE_BLOCK
    block_expert = jnp.minimum(jnp.sum(block_row[:, None] >= padded_end[None, :], axis=1), N_EXPERTS - 1).astype(I32)
    n_blocks = (padded_end[-1] // MOE_BLOCK).astype(I32)[None]
    tails = jnp.concatenate([expert_start + per_expert, padded - per_expert,
                             n_blocks, nb_max - n_blocks]).astype(I32)
    flat = lambda a: a.reshape(-1).astype(I32)
    offv = jnp.pad(off, ((0, 0), (0, LANES - N_EXPERTS))).astype(F32)[:, None, :]

    xs = _dispatch(rp[1], rs[1], rp[2], rs[2], flat(cnt8), flat(off), flat(base), tails, offv, n_slots)
    y = _experts(block_expert, n_blocks, xs, wgu, bgu, wd, bd)
    y_p = _combine(rp[0], rp[2], rp[3], flat(cnt8[:ntile_p]), flat(off[:ntile_p]), flat(base[:ntile_p]),
                   offv[:ntile_p], gf, y)
    y_s = _combine(rs[0], rs[2], rs[3], flat(cnt8[ntile_p:]), flat(off[ntile_p:]), flat(base[ntile_p:]),
                   offv[ntile_p:], gf, y)

    kp, vp, cp_, np_, mp = sp_out
    ks, vs, cs, ns, ms = ss_out
    return (y_p.reshape(bp, sp, D_MODEL), y_s.reshape(bs, ts, D_MODEL),
            kp[None], vp[None], cp_[None], np_[None], mp[None],
            ks[None], vs[None], cs[None], ns[None], ms[None])
```

```python
import functools

import jax
import jax.numpy as jnp
from jax import lax
from jax.experimental import pallas as pl
from jax.experimental.pallas import tpu as pltpu

F32 = jnp.float32
BF16 = jnp.bfloat16
I32 = jnp.int32

D_MODEL = 1024
M_HEADS = 4
M_DK = 256
M_DV = 256
M_CHUNK = 128
A_Q_HEADS = 16
A_KV_HEADS = 4
A_GROUP = 4
A_HD = 64
A_KV = A_KV_HEADS * A_HD
WINDOW = 128
A_BLOCK = 128
ROT_DIM = 16
ROPE_THETA = 500000.0
N_EXPERTS = 32
TOP_K = 4
D_FF = 1024
SWIGLU_LIMIT = 7.0
SWIGLU_ALPHA = 1.702
NORM_EPS = 1e-6

LANES = 128
SUBLANES = 8
VMEM_LIMIT = 56 * 1024 * 1024

TOK_TILE = 256
RUN_ALIGN = SUBLANES
STAGE_ROWS = 1280
MOE_BLOCK = 256


def _cparams(sem):
    return pltpu.CompilerParams(dimension_semantics=sem, vmem_limit_bytes=VMEM_LIMIT)


def _rms(x, g):
    return x * lax.rsqrt(jnp.mean(x * x, axis=-1, keepdims=True) + NORM_EPS) * g


def _rope(x, cos_t, sin_t, first_half):
    n = x.shape[1]
    fwd = pltpu.roll(x, n - ROT_DIM // 2, axis=1)
    bwd = pltpu.roll(x, ROT_DIM // 2, axis=1)
    outs = []
    for b in range(n // LANES):
        sl = slice(b * LANES, (b + 1) * LANES)
        partner = jnp.where(first_half, fwd[:, sl], bwd[:, sl])
        outs.append(x[:, sl] * cos_t + partner * sin_t)
    return jnp.concatenate(outs, axis=1)


def _inproj_kernel(x_ref, g1_ref, wm_ref, wg_ref, wqa_ref, wkv_ref, wif_ref, cos_ref, sin_ref,
                   qkv_ref, og_ref, qa_ref, ka_ref, va_ref, gate_ref):
    x = x_ref[...]
    xn = _rms(x, g1_ref[...]).astype(BF16)
    act = qkv_ref.dtype

    def proj(w_ref, c0, n):
        return jnp.dot(xn, w_ref[:, c0:c0 + n], preferred_element_type=F32)

    qkv_ref[:, 0:1024] = proj(wm_ref, 0, 1024).astype(act)
    qkv_ref[:, 1024:2048] = (proj(wm_ref, 1024, 1024) * (M_DK ** -0.5)).astype(act)
    qkv_ref[:, 2048:3072] = proj(wm_ref, 2048, 1024).astype(act)
    og_ref[:, 0:1024] = jax.nn.sigmoid(proj(wm_ref, 3072, 1024)).astype(act)
    og_ref[:, 1024:2048] = jax.nn.sigmoid(proj(wg_ref, 0, 1024)).astype(act)
    og_ref[:, 2048:3072] = jax.nn.sigmoid(proj(wg_ref, 1024, 1024)).astype(act)
    cos_t = cos_ref[...]
    sin_t = sin_ref[...]
    lane = lax.broadcasted_iota(I32, (1, LANES), 1)
    first_half = (lane & (ROT_DIM - 1)) < (ROT_DIM // 2)
    qa_ref[...] = _rope(proj(wqa_ref, 0, 1024), cos_t, sin_t, first_half).astype(act)
    ka_ref[...] = _rope(proj(wkv_ref, 0, A_KV), cos_t, sin_t, first_half)
    va_ref[...] = proj(wkv_ref, A_KV, A_KV)
    gate_ref[...] = proj(wif_ref, 0, LANES)


def _inproj(x, g1, weights, cos_t, sin_t, tm, rope_blocks, act):
    t = x.shape[0]
    tok = lambda n, dt: jax.ShapeDtypeStruct((t, n), dt)
    row = lambda n: pl.BlockSpec((tm, n), lambda i: (i, 0))
    resident = lambda w: pl.BlockSpec(w.shape, lambda i: (0, 0), pipeline_mode=pl.Buffered(1))
    return pl.pallas_call(
        _inproj_kernel,
        grid=(t // tm,),
        in_specs=[row(D_MODEL), pl.BlockSpec((1, D_MODEL), lambda i: (0, 0))]
                 + [resident(w) for w in weights]
                 + [pl.BlockSpec((tm, LANES), lambda i: (i % rope_blocks, 0)),
                    pl.BlockSpec((tm, LANES), lambda i: (i % rope_blocks, 0))],
        out_specs=[row(3072), row(3072), row(1024), row(A_KV), row(A_KV), row(LANES)],
        out_shape=[tok(3072, act), tok(3072, act), tok(1024, act), tok(A_KV, F32), tok(A_KV, F32),
                   tok(LANES, F32)],
        compiler_params=_cparams(("arbitrary",)),
        name="inproj",
    )(x, g1, *weights, cos_t, sin_t)


def _mlstm_gates(gc, bias, m_prev, L):
    z = gc + bias
    lf = jnp.minimum(z, 0.0) - jnp.log1p(jnp.exp(-jnp.abs(z)))
    row = lax.broadcasted_iota(I32, (L, LANES), 0)
    fc = lf
    sh = 1
    while sh < L:
        fc = fc + jnp.where(row >= sh, pltpu.roll(fc, sh, axis=0), 0.0)
        sh *= 2
    fcum = pltpu.roll(fc, LANES - M_HEADS, axis=1)
    a = z - fcum
    cmx = a
    sh = 1
    while sh < L:
        cmx = jnp.maximum(cmx, jnp.where(row >= sh, pltpu.roll(cmx, sh, axis=0), -jnp.inf))
        sh *= 2
    mx = jnp.maximum(m_prev, cmx)
    inter = jnp.exp(m_prev - mx)
    f_end = fcum[L - 1:L]
    m_end = f_end + mx[L - 1:L]
    decay = jnp.exp(f_end + m_prev - m_end)
    wsrc = jnp.exp(a + (f_end - m_end))
    return a, mx, inter, wsrc, decay, m_end


NT_DIMS = (((1,), (1,)), ((), ()))
TN_DIMS = (((0,), (0,)), ((), ()))


def _mlstm_decay(h, L, gates):
    a, mx = gates[0], gates[1]
    t_i = lax.broadcasted_iota(I32, (L, L), 0)
    s_i = lax.broadcasted_iota(I32, (L, L), 1)
    a_row = jnp.sum(jnp.where(t_i == s_i, a[:, h:h + 1], 0.0), axis=0, keepdims=True)
    return jnp.exp(jnp.where(s_i <= t_i, a_row - mx[:, h:h + 1], -jnp.inf))


def _mlstm_output(h, q, og, mn, gates, qk_sum, pv, cq, n_h):
    inter_col = gates[2][:, h:h + 1]
    num = inter_col * cq + pv
    den = inter_col * jnp.sum(q.astype(F32) * n_h, axis=1, keepdims=True) + qk_sum
    hh = num / jnp.maximum(jnp.abs(den), 1.0)
    hn = hh * lax.rsqrt(jnp.mean(hh * hh, axis=1, keepdims=True) + NORM_EPS)
    return hn * mn * og


def _mlstm_kernel(L, G, carry, qkv_ref, gate_ref, og_ref, c0_ref, n0_ref, m0_ref, bias_ref, mn_ref,
                  h_ref, c_ref, n_ref, m_ref, *scratch):
    if carry:
        c_in, n_in, m_in = scratch
        ci = pl.program_id(1)

        @pl.when(ci == 0)
        def _():
            c_in[...] = c0_ref[...]
            n_in[...] = n0_ref[...]
            m_in[...] = m0_ref[...]
        c_out, n_out, m_out = scratch
    else:
        c_in, n_in, m_in = c0_ref, n0_ref, m0_ref
        c_out, n_out, m_out = c_ref, n_ref, m_ref

    def qkv(j, h, part):
        return qkv_ref[j, :, part * 1024 + h * M_DK:part * 1024 + (h + 1) * M_DK].astype(BF16)

    pairs = [(j, h) for j in range(G) for h in range(M_HEADS)]
    gates = [_mlstm_gates(gate_ref[j], bias_ref[...], m_in[j], L) for j in range(G)]
    scores = [lax.dot_general(qkv(j, h, 0), qkv(j, h, 1), NT_DIMS, preferred_element_type=F32) for j, h in pairs]
    cqs = [lax.dot_general(qkv(j, h, 0), c_in[j, h].astype(BF16), NT_DIMS, preferred_element_type=F32)
           for j, h in pairs]
    qks = [s * _mlstm_decay(h, L, gates[j]) for (j, h), s in zip(pairs, scores)]
    pvs = [jnp.dot(qk.astype(BF16), qkv(j, h, 2), preferred_element_type=F32) for (j, h), qk in zip(pairs, qks)]
    kws = [gates[j][3][:, h:h + 1] * qkv(j, h, 1).astype(F32) for j, h in pairs]
    for (j, h), qk, pv, cq in zip(pairs, qks, pvs, cqs):
        sl = slice(h * M_DV, (h + 1) * M_DV)
        out = _mlstm_output(h, qkv(j, h, 0), og_ref[j, :, sl].astype(F32), mn_ref[:, sl], gates[j],
                            jnp.sum(qk, axis=1, keepdims=True), pv, cq, n_in[j, h:h + 1, :])
        h_ref[j, :, sl] = out.astype(h_ref.dtype)
    for (j, h), kw in zip(pairs, kws):
        dec = gates[j][4][:, h:h + 1]
        c_out[j, h] = dec * c_in[j, h] + lax.dot_general(qkv(j, h, 2), kw.astype(BF16), TN_DIMS,
                                                         preferred_element_type=F32)
        n_out[j, h:h + 1, :] = dec * n_in[j, h:h + 1, :] + jnp.sum(kw, axis=0, keepdims=True)
    for j in range(G):
        m_out[j] = gates[j][5]

    if carry:
        @pl.when(ci == pl.num_programs(1) - 1)
        def _():
            c_ref[...] = c_in[...]
            n_ref[...] = n_in[...]
            m_ref[...] = m_in[...]


MLSTM_GROUP_CARRY = 2
MLSTM_GROUP_SINGLE = 4


def _mlstm(qkv, gates, og, c0, n0, m0, gate_bias, mnorm, nb, L):
    t = qkv.shape[0]
    nc = t // (nb * L)
    carry = nc > 1
    G = MLSTM_GROUP_CARRY if carry else MLSTM_GROUP_SINGLE
    qkv, gates, og = (a.reshape(nb, nc * L, a.shape[1]) for a in (qkv, gates, og))
    tokmap = lambda b, c: (b, c, 0)
    const2 = lambda b, c: (0, 0)
    state_specs = [pl.BlockSpec((G, M_HEADS, M_DV, M_DK), lambda b, c: (b, 0, 0, 0)),
                   pl.BlockSpec((G, M_HEADS, M_DK), lambda b, c: (b, 0, 0)),
                   pl.BlockSpec((G, 1, LANES), lambda b, c: (b, 0, 0))]
    scratch = [pltpu.VMEM((G, M_HEADS, M_DV, M_DK), F32),
               pltpu.VMEM((G, M_HEADS, M_DK), F32),
               pltpu.VMEM((G, 1, LANES), F32)] if carry else []
    h, c, n, m = pl.pallas_call(
        functools.partial(_mlstm_kernel, L, G, carry),
        grid=(nb // G, nc),
        in_specs=[pl.BlockSpec((G, L, 3072), tokmap),
                  pl.BlockSpec((G, L, LANES), tokmap),
                  pl.BlockSpec((G, L, 1024), tokmap)]
                 + state_specs
                 + [pl.BlockSpec((1, LANES), const2),
                    pl.BlockSpec((1, 1024), const2)],
        out_specs=[pl.BlockSpec((G, L, 1024), tokmap)] + state_specs,
        out_shape=[jax.ShapeDtypeStruct((nb, nc * L, 1024), qkv.dtype),
                   jax.ShapeDtypeStruct((nb, M_HEADS, M_DV, M_DK), F32),
                   jax.ShapeDtypeStruct((nb, M_HEADS, M_DK), F32),
                   jax.ShapeDtypeStruct((nb, 1, LANES), F32)],
        scratch_shapes=scratch,
        compiler_params=_cparams(("arbitrary", "arbitrary")),
        name="mlstm_L%d" % L,
    )(qkv, gates, og, c0, n0, m0, gate_bias, mnorm)
    return h.reshape(t, 1024), c, n, m


def _swa_bias(m, has_prev):
    nkeys = 2 * A_BLOCK
    t_i = lax.broadcasted_iota(I32, (A_GROUP * m, nkeys), 0) & (m - 1)
    s_i = lax.broadcasted_iota(I32, (A_GROUP * m, nkeys), 1)
    diff = t_i + A_BLOCK - s_i
    mask = (diff >= 0) & (diff <= WINDOW)
    if not has_prev:
        mask = mask & (s_i >= A_BLOCK)
    return jnp.where(mask, 0.0, -jnp.inf).astype(F32)


def _swa_group_ones():
    nkeys = 2 * A_BLOCK
    r = lax.broadcasted_iota(I32, (A_KV_HEADS * nkeys, A_KV), 0) >> (nkeys.bit_length() - 1)
    c = lax.broadcasted_iota(I32, (A_KV_HEADS * nkeys, A_KV), 1) >> (A_HD.bit_length() - 1)
    return (r == c).astype(BF16)


def _swa_core(q, k_prev, v_prev, k_cur, v_cur, bias, group_ones, sink_ref):
    m = q.shape[0]
    q = q.astype(BF16)
    qst = jnp.concatenate([q[:, j * A_KV:(j + 1) * A_KV] for j in range(A_GROUP)], axis=0)
    k_all = jnp.concatenate([k_prev, k_cur], axis=0)
    v_all = jnp.concatenate([v_prev, v_cur], axis=0)
    lane = lax.broadcasted_iota(I32, (1, A_KV), 1)
    kbd, vbd = [], []
    for g in range(A_KV_HEADS):
        in_g = (lane >= g * A_HD) & (lane < (g + 1) * A_HD)
        kbd.append(jnp.where(in_g, k_all, 0.0).astype(BF16))
        vbd.append(jnp.where(in_g, v_all, 0.0).astype(BF16))
    kbd = jnp.concatenate(kbd, axis=0)
    vbd = jnp.concatenate(vbd, axis=0)
    s = lax.dot_general(qst, kbd, (((1,), (1,)), ((), ())), preferred_element_type=F32)
    nkeys = 2 * A_BLOCK
    ps, sink_terms = [], []
    for g in range(A_KV_HEADS):
        sg = s[:, g * nkeys:(g + 1) * nkeys] + bias
        sink = jnp.concatenate(
            [jnp.full((m, LANES), sink_ref[A_GROUP * g + j], F32) for j in range(A_GROUP)], axis=0)
        mx = jnp.maximum(jnp.broadcast_to(jnp.max(sg, axis=1, keepdims=True), (A_GROUP * m, LANES)), sink)
        ps += [jnp.exp(sg[:, :LANES] - mx).astype(BF16), jnp.exp(sg[:, LANES:] - mx).astype(BF16)]
        sink_terms.append(jnp.exp(sink - mx))
    p = jnp.concatenate(ps, axis=1)
    o = jnp.dot(p, vbd, preferred_element_type=F32)
    den = jnp.dot(p, group_ones, preferred_element_type=F32)
    low = lax.broadcasted_iota(I32, (1, LANES), 1) < A_HD
    den = den + jnp.concatenate([jnp.where(low, sink_terms[0], sink_terms[1]),
                                 jnp.where(low, sink_terms[2], sink_terms[3])], axis=1)
    o = o / den
    return jnp.concatenate([o[j * m:(j + 1) * m] for j in range(A_GROUP)], axis=1)


SWA_PAIR = 2 * A_BLOCK


def _swa_prompt_kernel(sink_ref, q_ref, kp_ref, kc_ref, vp_ref, vc_ref, o_ref, bias_s, bias0_s, ones_s):
    first = (pl.program_id(0) == 0) & (pl.program_id(1) == 0)

    @pl.when(first)
    def _():
        bias_s[...] = _swa_bias(A_BLOCK, True)
        bias0_s[...] = _swa_bias(A_BLOCK, False)
        ones_s[...] = _swa_group_ones()

    bias = bias_s[...]
    bias_a = jnp.where(pl.program_id(1) > 0, bias, bias0_s[...])
    ones = ones_s[...]
    k0, k1 = kc_ref[0:A_BLOCK], kc_ref[A_BLOCK:SWA_PAIR]
    v0, v1 = vc_ref[0:A_BLOCK], vc_ref[A_BLOCK:SWA_PAIR]
    o_ref[0:A_BLOCK] = _swa_core(q_ref[0:A_BLOCK], kp_ref[...], vp_ref[...], k0, v0,
                                 bias_a, ones, sink_ref).astype(o_ref.dtype)
    o_ref[A_BLOCK:SWA_PAIR] = _swa_core(q_ref[A_BLOCK:SWA_PAIR], k0, v0, k1, v1,
                                        bias, ones, sink_ref).astype(o_ref.dtype)


def _swa_prompt(qa, ka, va, sinks, nb):
    t = qa.shape[0]
    npair = t // (nb * SWA_PAIR)
    cur = lambda b, i, s: (b * npair + i, 0)
    prev = lambda b, i, s: (2 * (b * npair + i) - jnp.minimum(i, 1), 0)
    return pl.pallas_call(
        _swa_prompt_kernel,
        grid_spec=pltpu.PrefetchScalarGridSpec(
            num_scalar_prefetch=1,
            grid=(nb, npair),
            in_specs=[pl.BlockSpec((SWA_PAIR, 1024), cur),
                      pl.BlockSpec((A_BLOCK, A_KV), prev),
                      pl.BlockSpec((SWA_PAIR, A_KV), cur),
                      pl.BlockSpec((A_BLOCK, A_KV), prev),
                      pl.BlockSpec((SWA_PAIR, A_KV), cur)],
            out_specs=pl.BlockSpec((SWA_PAIR, 1024), cur),
            scratch_shapes=[pltpu.VMEM((A_GROUP * A_BLOCK, SWA_PAIR), F32),
                            pltpu.VMEM((A_GROUP * A_BLOCK, SWA_PAIR), F32),
                            pltpu.VMEM((A_KV_HEADS * SWA_PAIR, A_KV), BF16)]),
        out_shape=jax.ShapeDtypeStruct((t, 1024), BF16),
        compiler_params=_cparams(("arbitrary", "arbitrary")),
        name="swa_prompt",
    )(sinks, qa, ka, ka, va, va)


SWA_SAMPLE_GROUP = 4


def _swa_sample_kernel(T, sink_ref, q_ref, kn_ref, vn_ref, kb_ref, vb_ref, o_ref, ko_ref, vo_ref):
    pad = jnp.zeros((A_BLOCK - T, A_KV), F32)
    bias = _swa_bias(T, True)
    ones = _swa_group_ones()
    for j in range(SWA_SAMPLE_GROUP):
        rows = slice(j * T, (j + 1) * T)
        k_new = kn_ref[rows]
        v_new = vn_ref[rows]
        k_buf = kb_ref[j]
        v_buf = vb_ref[j]
        o_ref[rows] = _swa_core(q_ref[rows], k_buf, v_buf, jnp.concatenate([k_new, pad], axis=0),
                                jnp.concatenate([v_new, pad], axis=0), bias, ones, sink_ref).astype(o_ref.dtype)
        ko_ref[j] = jnp.concatenate([k_buf[T:], k_new], axis=0)
        vo_ref[j] = jnp.concatenate([v_buf[T:], v_new], axis=0)


def _swa_sample(qa, ka, va, k_buf, v_buf, sinks, T):
    nb = k_buf.shape[0]
    g = SWA_SAMPLE_GROUP
    tok = lambda b, s: (b, 0)
    buf = lambda b, s: (b, 0, 0)
    return pl.pallas_call(
        functools.partial(_swa_sample_kernel, T),
        grid_spec=pltpu.PrefetchScalarGridSpec(
            num_scalar_prefetch=1,
            grid=(nb // g,),
            in_specs=[pl.BlockSpec((g * T, 1024), tok),
                      pl.BlockSpec((g * T, A_KV), tok),
                      pl.BlockSpec((g * T, A_KV), tok),
                      pl.BlockSpec((g, WINDOW, A_KV), buf),
                      pl.BlockSpec((g, WINDOW, A_KV), buf)],
            out_specs=[pl.BlockSpec((g * T, 1024), tok),
                       pl.BlockSpec((g, WINDOW, A_KV), buf),
                       pl.BlockSpec((g, WINDOW, A_KV), buf)]),
        out_shape=[jax.ShapeDtypeStruct((nb * T, 1024), qa.dtype),
                   jax.ShapeDtypeStruct((nb, WINDOW, A_KV), F32),
                   jax.ShapeDtypeStruct((nb, WINDOW, A_KV), F32)],
        compiler_params=_cparams(("arbitrary",)),
        name="swa_sample",
    )(sinks, qa, ka, va, k_buf, v_buf)


def _merge_kernel(x_ref, hm_ref, ha_ref, gm_ref, ga_ref, wpm_ref, wpa_ref, wo_ref, g2_ref, wr_ref, br_ref,
                  x1_ref, xn_ref, idx_ref, gate_ref, cnt_ref):
    pm = jnp.dot(hm_ref[...].astype(BF16), wpm_ref[...], preferred_element_type=F32)
    pa = jnp.dot(ha_ref[...].astype(BF16), wpa_ref[...], preferred_element_type=F32)
    mixed = gm_ref[...].astype(F32) * pm + ga_ref[...].astype(F32) * pa
    x1 = x_ref[...] + jnp.dot(mixed.astype(BF16), wo_ref[...], preferred_element_type=F32)
    x1_ref[...] = x1
    xn = _rms(x1, g2_ref[...])
    xn_ref[...] = xn.astype(BF16)
    xn_hi = xn.astype(BF16)
    xn_lo = (xn - xn_hi.astype(F32)).astype(BF16)
    part = jnp.dot(xn_hi, wr_ref[...], preferred_element_type=F32)
    logits = (part[:, :LANES] + part[:, LANES:]
              + jnp.dot(xn_lo, wr_ref[:, :LANES], preferred_element_type=F32) + br_ref[...])
    tm = logits.shape[0]
    cur = logits.T[0:N_EXPERTS]
    eid = lax.broadcasted_iota(I32, (N_EXPERTS, tm), 0).astype(F32)
    vals, idxs = [], []
    onehot = jnp.zeros((N_EXPERTS, tm), F32)
    for _ in range(TOP_K):
        mval = jnp.max(cur, axis=0, keepdims=True)
        sel = jnp.min(jnp.where(cur == mval, eid, float(N_EXPERTS)), axis=0, keepdims=True)
        hit = eid == sel
        onehot = onehot + hit.astype(F32)
        cur = jnp.where(hit, -jnp.inf, cur)
        vals.append(mval)
        idxs.append(sel)
    es = [jnp.exp(v - vals[0]) for v in vals]
    tot = es[0] + es[1] + es[2] + es[3]
    row = lax.broadcasted_iota(I32, (SUBLANES, tm), 0)
    res = jnp.zeros((SUBLANES, tm), F32)
    for k in range(TOP_K):
        res = jnp.where(row == k, idxs[k], res)
        res = jnp.where(row == TOP_K + k, es[k] / tot, res)
    res_t = jnp.concatenate([res, jnp.zeros((LANES - SUBLANES, tm), F32)], axis=0).T
    idx_ref[...] = res_t[:, 0:TOP_K].astype(I32)
    gate_ref[...] = res_t[:, TOP_K:2 * TOP_K]
    cnt_ref[0] = jnp.sum(onehot, axis=1, keepdims=True).astype(I32)


def _merge(x, hm, ha, og, wpm, wpa, wo, g2, wr, br):
    t = x.shape[0]
    tm = TOK_TILE
    row = lambda n: pl.BlockSpec((tm, n), lambda i: (i, 0))
    const = lambda r, c: pl.BlockSpec((r, c), lambda i: (0, 0))
    return pl.pallas_call(
        _merge_kernel,
        grid=(t // tm,),
        in_specs=[row(1024), row(1024), row(1024),
                  pl.BlockSpec((tm, 1024), lambda i: (i, 1)),
                  pl.BlockSpec((tm, 1024), lambda i: (i, 2)),
                  const(1024, 1024), const(1024, 1024), const(1024, 1024), const(1, 1024),
                  const(1024, 2 * LANES), const(1, LANES)],
        out_specs=[row(1024), row(1024), row(TOP_K), row(TOP_K),
                   pl.BlockSpec((1, N_EXPERTS, 1), lambda i: (i, 0, 0))],
        out_shape=[jax.ShapeDtypeStruct((t, 1024), F32),
                   jax.ShapeDtypeStruct((t, 1024), BF16),
                   jax.ShapeDtypeStruct((t, TOP_K), I32),
                   jax.ShapeDtypeStruct((t, TOP_K), F32),
                   jax.ShapeDtypeStruct((t // tm, N_EXPERTS, 1), I32)],
        compiler_params=_cparams(("arbitrary",)),
        name="merge_route",
    )(x, hm, ha, og, og, wpm, wpa, wo, g2, wr, br)


def _stage_rows(idx, off_row):
    tm = idx.shape[0]
    lane = lax.broadcasted_iota(I32, (tm, LANES), 1)
    hits = [lane == idx[:, k:k + 1] for k in range(TOP_K)]
    onehot = sum(h.astype(F32) for h in hits)
    r_i = lax.broadcasted_iota(I32, (tm, tm), 0)
    c_i = lax.broadcasted_iota(I32, (tm, tm), 1)
    before = (c_i < r_i).astype(BF16)
    rank = jnp.dot(before, onehot.astype(BF16), preferred_element_type=F32)
    pos = rank + off_row
    return [jnp.sum(jnp.where(h, pos, 0.0), axis=1, keepdims=True) for h in hits]


def _dispatch_kernel(ntile_a, cnt_ref, off_ref, base_ref, tail_ref, xa_ref, xb_ref, ia_ref, ib_ref, offv_ref,
                     xs_ref, stage, zeros, sem, zsem):
    i = pl.program_id(0)
    from_a = i < ntile_a
    xn = jnp.where(from_a, xa_ref[...], xb_ref[...])
    idx = jnp.where(from_a, ia_ref[...], ib_ref[...])
    rows = _stage_rows(idx, offv_ref[0])
    tm = idx.shape[0]
    lane = lax.broadcasted_iota(I32, (tm, LANES), 1)
    r4 = jnp.zeros((tm, LANES), F32)
    for k in range(TOP_K):
        r4 = jnp.where(lane == k, rows[k], r4)
    r4t = r4.T.astype(I32)
    r_iota = lax.broadcasted_iota(I32, (STAGE_ROWS, tm), 0)
    sel = jnp.zeros((STAGE_ROWS, tm), F32)
    for k in range(TOP_K):
        sel = sel + (r_iota == r4t[k:k + 1, :]).astype(F32)
    staged = jnp.dot(sel.astype(BF16), xn, preferred_element_type=F32)

    def runs(tile, slot):
        out = []
        for e in range(N_EXPERTS):
            n = pl.multiple_of(cnt_ref[tile * N_EXPERTS + e], RUN_ALIGN)
            off = pl.multiple_of(off_ref[tile * N_EXPERTS + e], RUN_ALIGN)
            base = pl.multiple_of(base_ref[tile * N_EXPERTS + e], RUN_ALIGN)
            out.append((n, pltpu.make_async_copy(stage.at[slot, pl.ds(off, n)], xs_ref.at[pl.ds(base, n)],
                                                 sem.at[slot])))
        return out

    def wait_runs(tile, slot):
        for n, cp in runs(tile, slot):
            @pl.when(n > 0)
            def _():
                cp.wait()

    for slot in range(2):
        @pl.when((i & 1) == slot)
        def _():
            @pl.when(i >= 2)
            def _():
                wait_runs(i - 2, slot)
            stage[slot] = staged
            for n, cp in runs(i, slot):
                @pl.when(n > 0)
                def _():
                    cp.start()

    last = pl.num_programs(0) - 1

    @pl.when(i == last)
    def _():
        for slot in range(2):
            @pl.when(((last - 1) & 1) == slot)
            def _():
                @pl.when(last >= 1)
                def _():
                    wait_runs(last - 1, slot)

            @pl.when((last & 1) == slot)
            def _():
                wait_runs(last, slot)
        zeros[...] = jnp.zeros(zeros.shape, F32)
        tails = []
        for e in range(N_EXPERTS):
            start = pl.multiple_of(tail_ref[e], RUN_ALIGN)
            n = pl.multiple_of(tail_ref[N_EXPERTS + e], RUN_ALIGN)
            tails.append((n, pltpu.make_async_copy(zeros.at[pl.ds(0, n)], xs_ref.at[pl.ds(start, n)], zsem)))
        for n, cp in tails:
            @pl.when(n > 0)
            def _():
                cp.start()
        for n, cp in tails:
            @pl.when(n > 0)
            def _():
                cp.wait()
        first = tail_ref[2 * N_EXPERTS]
        n_unused = tail_ref[2 * N_EXPERTS + 1]

        def unused_block(j):
            row = pl.multiple_of((first + j) * MOE_BLOCK, MOE_BLOCK)
            return pltpu.make_async_copy(zeros, xs_ref.at[pl.ds(row, MOE_BLOCK)], zsem)

        @pl.loop(0, n_unused)
        def _(j):
            unused_block(j).start()

        @pl.loop(0, n_unused)
        def _(j):
            unused_block(j).wait()


def _dispatch(xa, xb, ia, ib, cnt8, off, base, tails, offv, n_slots):
    tm = TOK_TILE
    ntile_a = xa.shape[0] // tm
    ntile_b = xb.shape[0] // tm
    amap = lambda i, *_: (jnp.minimum(i, ntile_a - 1), 0)
    bmap = lambda i, *_: (jnp.maximum(i - ntile_a, 0), 0)
    return pl.pallas_call(
        functools.partial(_dispatch_kernel, ntile_a),
        grid_spec=pltpu.PrefetchScalarGridSpec(
            num_scalar_prefetch=4,
            grid=(ntile_a + ntile_b,),
            in_specs=[pl.BlockSpec((tm, 1024), amap),
                      pl.BlockSpec((tm, 1024), bmap),
                      pl.BlockSpec((tm, TOP_K), amap),
                      pl.BlockSpec((tm, TOP_K), bmap),
                      pl.BlockSpec((1, 1, LANES), lambda i, *_: (i, 0, 0))],
            out_specs=pl.BlockSpec(memory_space=pl.ANY),
            scratch_shapes=[pltpu.VMEM((2, STAGE_ROWS, D_MODEL), F32),
                            pltpu.VMEM((MOE_BLOCK, D_MODEL), F32),
                            pltpu.SemaphoreType.DMA((2,)),
                            pltpu.SemaphoreType.DMA]),
        out_shape=jax.ShapeDtypeStruct((n_slots, D_MODEL), F32),
        compiler_params=_cparams(("arbitrary",)),
        name="moe_dispatch",
    )(cnt8, off, base, tails, xa, xb, ia, ib, offv)


def _experts_kernel(be_ref, nb_ref, next_ref, slot_ref, xs_ref, wgu_hbm, bgu_ref, wd_hbm, bd_ref, y_ref,
                    wgu_f, wd_f, wgu_s, wd_s, sem):
    i = pl.program_id(0)
    live = i < nb_ref[0]
    run_start = (i == 0) | (be_ref[i] != be_ref[jnp.maximum(i - 1, 0)])

    def fetch(expert, slot):
        return (pltpu.make_async_copy(wgu_hbm.at[expert], wgu_f.at[slot], sem.at[0, slot]),
                pltpu.make_async_copy(wd_hbm.at[expert], wd_f.at[slot], sem.at[1, slot]))

    for slot in range(2):
        @pl.when(live & run_start & (slot_ref[i] == slot))
        def _():
            @pl.when(i == 0)
            def _():
                for cp in fetch(be_ref[i], slot):
                    cp.start()
            for cp in fetch(be_ref[i], slot):
                cp.wait()

            @pl.when(next_ref[i] >= 0)
            def _():
                for cp in fetch(next_ref[i], 1 - slot):
                    cp.start()
            wgu_s[...] = wgu_f[slot].astype(BF16)
            wd_s[...] = wd_f[slot].astype(BF16)

    @pl.when(live)
    def _():
        h = jnp.dot(xs_ref[...].astype(BF16), wgu_s[...], preferred_element_type=F32) + bgu_ref[0]
        gate = jnp.minimum(h[:, :D_FF], SWIGLU_LIMIT)
        up = jnp.clip(h[:, D_FF:], -SWIGLU_LIMIT, SWIGLU_LIMIT)
        act = gate * jax.nn.sigmoid(SWIGLU_ALPHA * gate) * (up + 1.0)
        y_ref[...] = jnp.dot(act.astype(BF16), wd_s[...], preferred_element_type=F32) + bd_ref[0]

    @pl.when(i >= nb_ref[0])
    def _():
        y_ref[...] = jnp.zeros(y_ref.shape, F32)


def _experts(block_expert, n_blocks, next_expert, slot, xs, wgu, bgu, wd, bd):
    nb_max = xs.shape[0] // MOE_BLOCK
    blk = lambda i, be, nb, *_: (jnp.minimum(i, nb[0] - 1), 0)
    wmap = lambda i, be, *_: (be[i], 0, 0)
    return pl.pallas_call(
        _experts_kernel,
        grid_spec=pltpu.PrefetchScalarGridSpec(
            num_scalar_prefetch=4,
            grid=(nb_max,),
            in_specs=[pl.BlockSpec((MOE_BLOCK, D_MODEL), blk),
                      pl.BlockSpec(memory_space=pl.ANY),
                      pl.BlockSpec((1, 1, 2 * D_FF), wmap),
                      pl.BlockSpec(memory_space=pl.ANY),
                      pl.BlockSpec((1, 1, D_MODEL), wmap)],
            out_specs=pl.BlockSpec((MOE_BLOCK, D_MODEL), lambda i, *_: (i, 0)),
            scratch_shapes=[pltpu.VMEM((2, D_MODEL, 2 * D_FF), F32),
                            pltpu.VMEM((2, D_FF, D_MODEL), F32),
                            pltpu.VMEM((D_MODEL, 2 * D_FF), BF16),
                            pltpu.VMEM((D_FF, D_MODEL), BF16),
                            pltpu.SemaphoreType.DMA((2, 2))]),
        out_shape=jax.ShapeDtypeStruct((xs.shape[0], D_MODEL), F32),
        compiler_params=_cparams(("arbitrary",)),
        name="moe_experts",
    )(block_expert, n_blocks, next_expert, slot, xs, wgu, bgu, wd, bd)


def _combine_kernel(cnt_ref, off_ref, base_ref, x1_ref, idx_ref, gate_ref, offv_ref, gf_ref, y_ref,
                    o_ref, stage, sem):
    i = pl.program_id(0)
    last = pl.num_programs(0) - 1

    def runs(tile, slot):
        out = []
        for e in range(N_EXPERTS):
            n = pl.multiple_of(cnt_ref[tile * N_EXPERTS + e], RUN_ALIGN)
            off = pl.multiple_of(off_ref[tile * N_EXPERTS + e], RUN_ALIGN)
            base = pl.multiple_of(base_ref[tile * N_EXPERTS + e], RUN_ALIGN)
            out.append((n, pltpu.make_async_copy(y_ref.at[pl.ds(base, n)], stage.at[slot, pl.ds(off, n)],
                                                 sem.at[slot])))
        return out

    def start_runs(tile, slot):
        for n, cp in runs(tile, slot):
            @pl.when(n > 0)
            def _():
                cp.start()

    @pl.when(i == 0)
    def _():
        stage[...] = jnp.zeros(stage.shape, F32)
        start_runs(0, 0)

    for slot in range(2):
        @pl.when(((i & 1) != slot) & (i < last))
        def _():
            start_runs(i + 1, slot)

    rows = _stage_rows(idx_ref[...], offv_ref[0])
    tm = idx_ref.shape[0]
    lane = lax.broadcasted_iota(I32, (tm, STAGE_ROWS), 1)
    gates = gate_ref[...]
    gmat = jnp.zeros((tm, STAGE_ROWS), F32)
    for k in range(TOP_K):
        gmat = gmat + jnp.where(lane == rows[k].astype(I32), gates[:, k:k + 1], 0.0)
    gmat = gmat.astype(BF16)
    for slot in range(2):
        @pl.when((i & 1) == slot)
        def _():
            for n, cp in runs(i, slot):
                @pl.when(n > 0)
                def _():
                    cp.wait()
            moe = jnp.dot(gmat, stage[slot].astype(BF16), preferred_element_type=F32)
            o_ref[...] = _rms(x1_ref[...] + moe, gf_ref[...])


def _combine(x1, idx, gates, cnt8, off, base, offv, gf, y):
    t = x1.shape[0]
    tm = TOK_TILE
    return pl.pallas_call(
        _combine_kernel,
        grid_spec=pltpu.PrefetchScalarGridSpec(
            num_scalar_prefetch=3,
            grid=(t // tm,),
            in_specs=[pl.BlockSpec((tm, 1024), lambda i, *_: (i, 0)),
                      pl.BlockSpec((tm, TOP_K), lambda i, *_: (i, 0)),
                      pl.BlockSpec((tm, TOP_K), lambda i, *_: (i, 0)),
                      pl.BlockSpec((1, 1, LANES), lambda i, *_: (i, 0, 0)),
                      pl.BlockSpec((1, 1024), lambda i, *_: (0, 0)),
                      pl.BlockSpec(memory_space=pl.ANY)],
            out_specs=pl.BlockSpec((tm, 1024), lambda i, *_: (i, 0)),
            scratch_shapes=[pltpu.VMEM((2, STAGE_ROWS, 1024), F32),
                            pltpu.SemaphoreType.DMA((2,))]),
        out_shape=jax.ShapeDtypeStruct((t, 1024), F32),
        compiler_params=_cparams(("arbitrary",)),
        name="moe_combine",
    )(cnt8, off, base, x1, idx, gates, offv, gf, y)


def _prep_w_in(w):
    c_if, c_qa, c_kv, c_g = 4096, 4104, 5128, 5640
    wm = w[:, :c_if].astype(BF16)
    wif = jnp.pad(w[:, c_if:c_qa], ((0, 0), (0, LANES - 2 * M_HEADS))).astype(BF16)
    wqa = (w[:, c_qa:c_kv].reshape(D_MODEL, A_KV_HEADS, A_GROUP, A_HD).transpose(0, 2, 1, 3)
           .reshape(D_MODEL, 1024) * (A_HD ** -0.5)).astype(BF16)
    wkv = w[:, c_kv:c_g].astype(BF16)
    wg = w[:, c_g:].astype(BF16)
    return wm, wg, wqa, wkv, wif


def _rope_tables(pos):
    half = ROT_DIM // 2
    inv = ROPE_THETA ** (-(jnp.arange(half, dtype=F32) * 2.0 / ROT_DIM))
    ang = pos.astype(F32)[:, None] * inv[None, :]
    cos, sin = jnp.cos(ang), jnp.sin(ang)
    ones = jnp.ones((pos.shape[0], A_HD - ROT_DIM), F32)
    cos64 = jnp.concatenate([cos, cos, ones], axis=1)
    sin64 = jnp.concatenate([-sin, sin, 0.0 * ones], axis=1)
    return jnp.tile(cos64, (1, 2)), jnp.tile(sin64, (1, 2))


def _round_up(x, m):
    return (x + m - 1) // m * m


def kernel(x_prompt, x_sample, cache_k, cache_v, state_c, state_n, state_m, norm1, w_in, b_igate, b_fgate,
           mlstm_norm, w_proj_m, w_proj_a, attn_sinks, w_out, norm2, w_router, b_router, w_gate_up, b_gate_up,
           w_down, b_down, norm_f):
    bp, sp, _ = x_prompt.shape
    bs, ts, _ = x_sample.shape
    past_len = 16384
    l = 0
    w1 = _prep_w_in(w_in[l])
    g1 = norm1[l][None, :]
    g2 = norm2[l][None, :]
    gf = norm_f[None, :]
    gate_bias = jnp.pad(jnp.concatenate([b_igate[l], b_fgate[l]]), (0, LANES - 2 * M_HEADS))[None, :]
    mn = mlstm_norm[l][None, :]
    wpm = w_proj_m[l].astype(BF16)
    wpa = w_proj_a[l].reshape(A_KV_HEADS, A_GROUP, A_HD, D_MODEL).transpose(1, 0, 2, 3).reshape(1024, D_MODEL).astype(BF16)
    wo = w_out[l].astype(BF16)
    wr_f = jnp.pad(w_router[l], ((0, 0), (0, LANES - N_EXPERTS)))
    wr_hi = wr_f.astype(BF16)
    wr = jnp.concatenate([wr_hi, (wr_f - wr_hi.astype(F32)).astype(BF16)], axis=1)
    br = jnp.pad(b_router[l], (0, LANES - N_EXPERTS))[None, :]
    wgu = w_gate_up[l]
    bgu = b_gate_up[l][:, None, :]
    wd = w_down[l]
    bd = b_down[l][:, None, :]
    sinks = attn_sinks[l]

    def mixer(x, nb, seq, pos0, L, c0, n0, m0, k_buf, v_buf, tm):
        t = nb * seq
        xf = x.reshape(t, D_MODEL)
        cos_t, sin_t = _rope_tables(pos0 + jnp.arange(seq))
        if seq < tm:
            cos_t, sin_t = jnp.tile(cos_t, (tm // seq, 1)), jnp.tile(sin_t, (tm // seq, 1))
        act = BF16 if L % 16 == 0 else F32
        qkv, og, qa, ka, va, gates = _inproj(xf, g1, w1, cos_t, sin_t, tm, max(seq // tm, 1), act)
        m0p = jnp.pad(m0, ((0, 0), (0, LANES - M_HEADS)))[:, None, :]
        hm, c, n, m = _mlstm(qkv, gates, og, c0, n0, m0p, gate_bias, mn, nb, L)
        if k_buf is None:
            ha = _swa_prompt(qa, ka, va, sinks, nb)
            k_keep = ka.reshape(nb, seq, A_KV_HEADS, A_HD)[:, seq - WINDOW:]
            v_keep = va.reshape(nb, seq, A_KV_HEADS, A_HD)[:, seq - WINDOW:]
        else:
            ha, k_keep, v_keep = _swa_sample(qa, ka, va, k_buf.reshape(nb, WINDOW, A_KV),
                                             v_buf.reshape(nb, WINDOW, A_KV), sinks, seq)
            k_keep = k_keep.reshape(nb, WINDOW, A_KV_HEADS, A_HD)
            v_keep = v_keep.reshape(nb, WINDOW, A_KV_HEADS, A_HD)
        x1, xn, idx, gate, cnt = _merge(xf, hm, ha, og, wpm, wpa, wo, g2, wr, br)
        return (x1, xn, idx, gate, cnt[:, :, 0]), (k_keep, v_keep, c, n, m[:, 0, :M_HEADS])

    zc = jnp.zeros((bp, M_HEADS, M_DV, M_DK), F32)
    zn = jnp.zeros((bp, M_HEADS, M_DK), F32)
    zm = jnp.zeros((bp, M_HEADS), F32)
    rp, sp_out = mixer(x_prompt, bp, sp, 0, M_CHUNK, zc, zn, zm, None, None, 512)
    rs, ss_out = mixer(x_sample, bs, ts, past_len, ts, state_c[l], state_n[l], state_m[l],
                       cache_k[l], cache_v[l], 256)

    cnt = jnp.concatenate([rp[4], rs[4]], axis=0)
    ntile_p = rp[4].shape[0]
    cnt8 = _round_up(cnt, RUN_ALIGN)
    off = jnp.cumsum(cnt8, axis=1) - cnt8
    per_expert = jnp.sum(cnt8, axis=0)
    padded = _round_up(per_expert, MOE_BLOCK)
    padded_end = jnp.cumsum(padded)
    expert_start = padded_end - padded
    base = expert_start[None, :] + jnp.cumsum(cnt8, axis=0) - cnt8
    n_tok = bp * sp + bs * ts
    n_slots = _round_up(n_tok * TOP_K + (n_tok // TOK_TILE) * N_EXPERTS * (RUN_ALIGN - 1), MOE_BLOCK) + N_EXPERTS * MOE_BLOCK
    nb_max = n_slots // MOE_BLOCK
    block_row = jnp.arange(nb_max, dtype=I32) * MOE_BLOCK
    block_expert = jnp.minimum(jnp.sum(block_row[:, None] >= padded_end[None, :], axis=1), N_EXPERTS - 1).astype(I32)
    n_blocks = (padded_end[-1] // MOE_BLOCK).astype(I32)[None]
    eid = jnp.arange(N_EXPERTS, dtype=I32)
    later_used = (eid[None, :] > eid[:, None]) & (padded[None, :] > 0)
    next_used = jnp.min(jnp.where(later_used, eid[None, :], N_EXPERTS), axis=1)
    next_used = jnp.where(next_used < N_EXPERTS, next_used, -1).astype(I32)
    run_parity = ((jnp.cumsum(padded > 0) - 1) & 1).astype(I32)
    tails = jnp.concatenate([expert_start + per_expert, padded - per_expert,
                             n_blocks, nb_max - n_blocks]).astype(I32)
    flat = lambda a: a.reshape(-1).astype(I32)
    offv = jnp.pad(off, ((0, 0), (0, LANES - N_EXPERTS))).astype(F32)[:, None, :]

    xs = _dispatch(rp[1], rs[1], rp[2], rs[2], flat(cnt8), flat(off), flat(base), tails, offv, n_slots)
    y = _experts(block_expert, n_blocks, next_used[block_expert], run_parity[block_expert], xs, wgu, bgu, wd, bd)
    y_p = _combine(rp[0], rp[2], rp[3], flat(cnt8[:ntile_p]), flat(off[:ntile_p]), flat(base[:ntile_p]),
                   offv[:ntile_p], gf, y)
    y_s = _combine(rs[0], rs[2], rs[3], flat(cnt8[ntile_p:]), flat(off[ntile_p:]), flat(base[ntile_p:]),
                   offv[ntile_p:], gf, y)

    kp, vp, cp_, np_, mp = sp_out
    ks, vs, cs, ns, ms = ss_out
    return (y_p.reshape(bp, sp, D_MODEL), y_s.reshape(bs, ts, D_MODEL),
            kp[None], vp[None], cp_[None], np_[None], mp[None],
            ks[None], vs[None], cs[None], ns[None], ms[None])
```

```python
import functools

import jax
import jax.numpy as jnp
from jax import lax
from jax.experimental import pallas as pl
from jax.experimental.pallas import tpu as pltpu

F32 = jnp.float32
BF16 = jnp.bfloat16
I32 = jnp.int32

D_MODEL = 1024
M_HEADS = 4
M_DK = 256
M_DV = 256
M_CHUNK = 128
A_Q_HEADS = 16
A_KV_HEADS = 4
A_GROUP = 4
A_HD = 64
A_KV = A_KV_HEADS * A_HD
WINDOW = 128
A_BLOCK = 128
ROT_DIM = 16
ROPE_THETA = 500000.0
N_EXPERTS = 32
TOP_K = 4
D_FF = 1024
SWIGLU_LIMIT = 7.0
SWIGLU_ALPHA = 1.702
NORM_EPS = 1e-6

LANES = 128
SUBLANES = 8
VMEM_LIMIT = 56 * 1024 * 1024

MERGE_TILE = 512
TOK_TILE = 256
RUN_ALIGN = SUBLANES
STAGE_ROWS = 1280
MOE_BLOCK = 512


def _cparams(sem):
    return pltpu.CompilerParams(dimension_semantics=sem, vmem_limit_bytes=VMEM_LIMIT)


def _rms(x, g):
    return x * lax.rsqrt(jnp.mean(x * x, axis=-1, keepdims=True) + NORM_EPS) * g


def _rope(x, cos_t, sin_t, first_half):
    n = x.shape[1]
    fwd = pltpu.roll(x, n - ROT_DIM // 2, axis=1)
    bwd = pltpu.roll(x, ROT_DIM // 2, axis=1)
    outs = []
    for b in range(n // LANES):
        sl = slice(b * LANES, (b + 1) * LANES)
        partner = jnp.where(first_half, fwd[:, sl], bwd[:, sl])
        outs.append(x[:, sl] * cos_t + partner * sin_t)
    return jnp.concatenate(outs, axis=1)


def _inproj_kernel(x_ref, g1_ref, wm_ref, wg_ref, wqa_ref, wkv_ref, wif_ref, cos_ref, sin_ref,
                   qkv_ref, og_ref, qa_ref, ka_ref, va_ref, gate_ref):
    x = x_ref[...]
    xn = _rms(x, g1_ref[...]).astype(BF16)
    act = qkv_ref.dtype

    def proj(w_ref, c0, n):
        return jnp.dot(xn, w_ref[:, c0:c0 + n], preferred_element_type=F32)

    qkv_ref[:, 0:1024] = proj(wm_ref, 0, 1024).astype(act)
    qkv_ref[:, 1024:2048] = (proj(wm_ref, 1024, 1024) * (M_DK ** -0.5)).astype(act)
    qkv_ref[:, 2048:3072] = proj(wm_ref, 2048, 1024).astype(act)
    og_ref[:, 0:1024] = jax.nn.sigmoid(proj(wm_ref, 3072, 1024)).astype(act)
    og_ref[:, 1024:2048] = jax.nn.sigmoid(proj(wg_ref, 0, 1024)).astype(act)
    og_ref[:, 2048:3072] = jax.nn.sigmoid(proj(wg_ref, 1024, 1024)).astype(act)
    cos_t = cos_ref[...]
    sin_t = sin_ref[...]
    lane = lax.broadcasted_iota(I32, (1, LANES), 1)
    first_half = (lane & (ROT_DIM - 1)) < (ROT_DIM // 2)
    qa_ref[...] = _rope(proj(wqa_ref, 0, 1024), cos_t, sin_t, first_half).astype(act)
    ka_ref[...] = _rope(proj(wkv_ref, 0, A_KV), cos_t, sin_t, first_half)
    va_ref[...] = proj(wkv_ref, A_KV, A_KV)
    gate_ref[...] = proj(wif_ref, 0, LANES)


def _inproj(x, g1, weights, cos_t, sin_t, tm, rope_blocks, act):
    t = x.shape[0]
    tok = lambda n, dt: jax.ShapeDtypeStruct((t, n), dt)
    row = lambda n: pl.BlockSpec((tm, n), lambda i: (i, 0))
    resident = lambda w: pl.BlockSpec(w.shape, lambda i: (0, 0), pipeline_mode=pl.Buffered(1))
    return pl.pallas_call(
        _inproj_kernel,
        grid=(t // tm,),
        in_specs=[row(D_MODEL), pl.BlockSpec((1, D_MODEL), lambda i: (0, 0))]
                 + [resident(w) for w in weights]
                 + [pl.BlockSpec((tm, LANES), lambda i: (i % rope_blocks, 0)),
                    pl.BlockSpec((tm, LANES), lambda i: (i % rope_blocks, 0))],
        out_specs=[row(3072), row(3072), row(1024), row(A_KV), row(A_KV), row(LANES)],
        out_shape=[tok(3072, act), tok(3072, act), tok(1024, act), tok(A_KV, F32), tok(A_KV, F32),
                   tok(LANES, F32)],
        compiler_params=_cparams(("arbitrary",)),
        name="inproj",
    )(x, g1, *weights, cos_t, sin_t)


def _mlstm_gates(gc, bias, m_prev, L):
    z = gc + bias
    lf = jnp.minimum(z, 0.0) - jnp.log1p(jnp.exp(-jnp.abs(z)))
    row = lax.broadcasted_iota(I32, (L, LANES), 0)
    fc = lf
    sh = 1
    while sh < L:
        fc = fc + jnp.where(row >= sh, pltpu.roll(fc, sh, axis=0), 0.0)
        sh *= 2
    fcum = pltpu.roll(fc, LANES - M_HEADS, axis=1)
    a = z - fcum
    cmx = a
    sh = 1
    while sh < L:
        cmx = jnp.maximum(cmx, jnp.where(row >= sh, pltpu.roll(cmx, sh, axis=0), -jnp.inf))
        sh *= 2
    mx = jnp.maximum(m_prev, cmx)
    inter = jnp.exp(m_prev - mx)
    f_end = fcum[L - 1:L]
    m_end = f_end + mx[L - 1:L]
    decay = jnp.exp(f_end + m_prev - m_end)
    wsrc = jnp.exp(a + (f_end - m_end))
    return a, mx, inter, wsrc, decay, m_end


NT_DIMS = (((1,), (1,)), ((), ()))
TN_DIMS = (((0,), (0,)), ((), ()))


def _mlstm_decay(h, L, gates):
    a, mx = gates[0], gates[1]
    t_i = lax.broadcasted_iota(I32, (L, L), 0)
    s_i = lax.broadcasted_iota(I32, (L, L), 1)
    a_row = jnp.sum(jnp.where(t_i == s_i, a[:, h:h + 1], 0.0), axis=0, keepdims=True)
    return jnp.exp(jnp.where(s_i <= t_i, a_row - mx[:, h:h + 1], -jnp.inf))


def _mlstm_output(h, q, og, mn, gates, qk_sum, pv, cq, n_h):
    inter_col = gates[2][:, h:h + 1]
    num = inter_col * cq + pv
    den = inter_col * jnp.sum(q.astype(F32) * n_h, axis=1, keepdims=True) + qk_sum
    hh = num / jnp.maximum(jnp.abs(den), 1.0)
    hn = hh * lax.rsqrt(jnp.mean(hh * hh, axis=1, keepdims=True) + NORM_EPS)
    return hn * mn * og


def _mlstm_kernel(L, G, carry, qkv_ref, gate_ref, og_ref, c0_ref, n0_ref, m0_ref, bias_ref, mn_ref,
                  h_ref, c_ref, n_ref, m_ref, *scratch):
    if carry:
        c_in, n_in, m_in = scratch
        ci = pl.program_id(1)

        @pl.when(ci == 0)
        def _():
            c_in[...] = c0_ref[...]
            n_in[...] = n0_ref[...]
            m_in[...] = m0_ref[...]
        c_out, n_out, m_out = scratch
    else:
        c_in, n_in, m_in = c0_ref, n0_ref, m0_ref
        c_out, n_out, m_out = c_ref, n_ref, m_ref

    def qkv(j, h, part):
        return qkv_ref[j, :, part * 1024 + h * M_DK:part * 1024 + (h + 1) * M_DK].astype(BF16)

    pairs = [(j, h) for j in range(G) for h in range(M_HEADS)]
    gates = [_mlstm_gates(gate_ref[j], bias_ref[...], m_in[j], L) for j in range(G)]
    scores = [lax.dot_general(qkv(j, h, 0), qkv(j, h, 1), NT_DIMS, preferred_element_type=F32) for j, h in pairs]
    cqs = [lax.dot_general(qkv(j, h, 0), c_in[j, h].astype(BF16), NT_DIMS, preferred_element_type=F32)
           for j, h in pairs]
    qks = [s * _mlstm_decay(h, L, gates[j]) for (j, h), s in zip(pairs, scores)]
    pvs = [jnp.dot(qk.astype(BF16), qkv(j, h, 2), preferred_element_type=F32) for (j, h), qk in zip(pairs, qks)]
    kws = [gates[j][3][:, h:h + 1] * qkv(j, h, 1).astype(F32) for j, h in pairs]
    for (j, h), qk, pv, cq in zip(pairs, qks, pvs, cqs):
        sl = slice(h * M_DV, (h + 1) * M_DV)
        out = _mlstm_output(h, qkv(j, h, 0), og_ref[j, :, sl].astype(F32), mn_ref[:, sl], gates[j],
                            jnp.sum(qk, axis=1, keepdims=True), pv, cq, n_in[j, h:h + 1, :])
        h_ref[j, :, sl] = out.astype(h_ref.dtype)
    for (j, h), kw in zip(pairs, kws):
        dec = gates[j][4][:, h:h + 1]
        c_out[j, h] = dec * c_in[j, h] + lax.dot_general(qkv(j, h, 2), kw.astype(BF16), TN_DIMS,
                                                         preferred_element_type=F32)
        n_out[j, h:h + 1, :] = dec * n_in[j, h:h + 1, :] + jnp.sum(kw, axis=0, keepdims=True)
    for j in range(G):
        m_out[j] = gates[j][5]

    if carry:
        @pl.when(ci == pl.num_programs(1) - 1)
        def _():
            c_ref[...] = c_in[...]
            n_ref[...] = n_in[...]
            m_ref[...] = m_in[...]


MLSTM_GROUP_CARRY = 2
MLSTM_GROUP_SINGLE = 4


def _mlstm(qkv, gates, og, c0, n0, m0, gate_bias, mnorm, nb, L):
    t = qkv.shape[0]
    nc = t // (nb * L)
    carry = nc > 1
    G = MLSTM_GROUP_CARRY if carry else MLSTM_GROUP_SINGLE
    qkv, gates, og = (a.reshape(nb, nc * L, a.shape[1]) for a in (qkv, gates, og))
    tokmap = lambda b, c: (b, c, 0)
    const2 = lambda b, c: (0, 0)
    state_specs = [pl.BlockSpec((G, M_HEADS, M_DV, M_DK), lambda b, c: (b, 0, 0, 0)),
                   pl.BlockSpec((G, M_HEADS, M_DK), lambda b, c: (b, 0, 0)),
                   pl.BlockSpec((G, 1, LANES), lambda b, c: (b, 0, 0))]
    scratch = [pltpu.VMEM((G, M_HEADS, M_DV, M_DK), F32),
               pltpu.VMEM((G, M_HEADS, M_DK), F32),
               pltpu.VMEM((G, 1, LANES), F32)] if carry else []
    h, c, n, m = pl.pallas_call(
        functools.partial(_mlstm_kernel, L, G, carry),
        grid=(nb // G, nc),
        in_specs=[pl.BlockSpec((G, L, 3072), tokmap),
                  pl.BlockSpec((G, L, LANES), tokmap),
                  pl.BlockSpec((G, L, 1024), tokmap)]
                 + state_specs
                 + [pl.BlockSpec((1, LANES), const2),
                    pl.BlockSpec((1, 1024), const2)],
        out_specs=[pl.BlockSpec((G, L, 1024), tokmap)] + state_specs,
        out_shape=[jax.ShapeDtypeStruct((nb, nc * L, 1024), qkv.dtype),
                   jax.ShapeDtypeStruct((nb, M_HEADS, M_DV, M_DK), F32),
                   jax.ShapeDtypeStruct((nb, M_HEADS, M_DK), F32),
                   jax.ShapeDtypeStruct((nb, 1, LANES), F32)],
        scratch_shapes=scratch,
        compiler_params=_cparams(("arbitrary", "arbitrary")),
        name="mlstm_L%d" % L,
    )(qkv, gates, og, c0, n0, m0, gate_bias, mnorm)
    return h.reshape(t, 1024), c, n, m


def _swa_bias(m, has_prev):
    nkeys = 2 * A_BLOCK
    t_i = lax.broadcasted_iota(I32, (A_GROUP * m, nkeys), 0) & (m - 1)
    s_i = lax.broadcasted_iota(I32, (A_GROUP * m, nkeys), 1)
    diff = t_i + A_BLOCK - s_i
    mask = (diff >= 0) & (diff <= WINDOW)
    if not has_prev:
        mask = mask & (s_i >= A_BLOCK)
    return jnp.where(mask, 0.0, -jnp.inf).astype(F32)


def _swa_group_ones():
    nkeys = 2 * A_BLOCK
    r = lax.broadcasted_iota(I32, (A_KV_HEADS * nkeys, A_KV), 0) >> (nkeys.bit_length() - 1)
    c = lax.broadcasted_iota(I32, (A_KV_HEADS * nkeys, A_KV), 1) >> (A_HD.bit_length() - 1)
    return (r == c).astype(BF16)


def _swa_core(q, k_prev, v_prev, k_cur, v_cur, bias, group_ones, sink_ref):
    m = q.shape[0]
    q = q.astype(BF16)
    qst = jnp.concatenate([q[:, j * A_KV:(j + 1) * A_KV] for j in range(A_GROUP)], axis=0)
    k_all = jnp.concatenate([k_prev, k_cur], axis=0)
    v_all = jnp.concatenate([v_prev, v_cur], axis=0)
    lane = lax.broadcasted_iota(I32, (1, A_KV), 1)
    kbd, vbd = [], []
    for g in range(A_KV_HEADS):
        in_g = (lane >= g * A_HD) & (lane < (g + 1) * A_HD)
        kbd.append(jnp.where(in_g, k_all, 0.0).astype(BF16))
        vbd.append(jnp.where(in_g, v_all, 0.0).astype(BF16))
    kbd = jnp.concatenate(kbd, axis=0)
    vbd = jnp.concatenate(vbd, axis=0)
    s = lax.dot_general(qst, kbd, (((1,), (1,)), ((), ())), preferred_element_type=F32)
    nkeys = 2 * A_BLOCK
    ps, sink_terms = [], []
    for g in range(A_KV_HEADS):
        sg = s[:, g * nkeys:(g + 1) * nkeys] + bias
        sink = jnp.concatenate(
            [jnp.full((m, LANES), sink_ref[A_GROUP * g + j], F32) for j in range(A_GROUP)], axis=0)
        mx = jnp.maximum(jnp.broadcast_to(jnp.max(sg, axis=1, keepdims=True), (A_GROUP * m, LANES)), sink)
        ps += [jnp.exp(sg[:, :LANES] - mx).astype(BF16), jnp.exp(sg[:, LANES:] - mx).astype(BF16)]
        sink_terms.append(jnp.exp(sink - mx))
    p = jnp.concatenate(ps, axis=1)
    o = jnp.dot(p, vbd, preferred_element_type=F32)
    den = jnp.dot(p, group_ones, preferred_element_type=F32)
    low = lax.broadcasted_iota(I32, (1, LANES), 1) < A_HD
    den = den + jnp.concatenate([jnp.where(low, sink_terms[0], sink_terms[1]),
                                 jnp.where(low, sink_terms[2], sink_terms[3])], axis=1)
    o = o / den
    return jnp.concatenate([o[j * m:(j + 1) * m] for j in range(A_GROUP)], axis=1)


SWA_PAIR = 2 * A_BLOCK


def _swa_prompt_kernel(sink_ref, q_ref, kp_ref, kc_ref, vp_ref, vc_ref, o_ref, bias_s, bias0_s, ones_s):
    first = (pl.program_id(0) == 0) & (pl.program_id(1) == 0)

    @pl.when(first)
    def _():
        bias_s[...] = _swa_bias(A_BLOCK, True)
        bias0_s[...] = _swa_bias(A_BLOCK, False)
        ones_s[...] = _swa_group_ones()

    bias = bias_s[...]
    bias_a = jnp.where(pl.program_id(1) > 0, bias, bias0_s[...])
    ones = ones_s[...]
    k0, k1 = kc_ref[0:A_BLOCK], kc_ref[A_BLOCK:SWA_PAIR]
    v0, v1 = vc_ref[0:A_BLOCK], vc_ref[A_BLOCK:SWA_PAIR]
    o_ref[0:A_BLOCK] = _swa_core(q_ref[0:A_BLOCK], kp_ref[...], vp_ref[...], k0, v0,
                                 bias_a, ones, sink_ref).astype(o_ref.dtype)
    o_ref[A_BLOCK:SWA_PAIR] = _swa_core(q_ref[A_BLOCK:SWA_PAIR], k0, v0, k1, v1,
                                        bias, ones, sink_ref).astype(o_ref.dtype)


def _swa_prompt(qa, ka, va, sinks, nb):
    t = qa.shape[0]
    npair = t // (nb * SWA_PAIR)
    cur = lambda b, i, s: (b * npair + i, 0)
    prev = lambda b, i, s: (2 * (b * npair + i) - jnp.minimum(i, 1), 0)
    return pl.pallas_call(
        _swa_prompt_kernel,
        grid_spec=pltpu.PrefetchScalarGridSpec(
            num_scalar_prefetch=1,
            grid=(nb, npair),
            in_specs=[pl.BlockSpec((SWA_PAIR, 1024), cur),
                      pl.BlockSpec((A_BLOCK, A_KV), prev),
                      pl.BlockSpec((SWA_PAIR, A_KV), cur),
                      pl.BlockSpec((A_BLOCK, A_KV), prev),
                      pl.BlockSpec((SWA_PAIR, A_KV), cur)],
            out_specs=pl.BlockSpec((SWA_PAIR, 1024), cur),
            scratch_shapes=[pltpu.VMEM((A_GROUP * A_BLOCK, SWA_PAIR), F32),
                            pltpu.VMEM((A_GROUP * A_BLOCK, SWA_PAIR), F32),
                            pltpu.VMEM((A_KV_HEADS * SWA_PAIR, A_KV), BF16)]),
        out_shape=jax.ShapeDtypeStruct((t, 1024), BF16),
        compiler_params=_cparams(("arbitrary", "arbitrary")),
        name="swa_prompt",
    )(sinks, qa, ka, ka, va, va)


SWA_SAMPLE_GROUP = 4


def _swa_sample_kernel(T, sink_ref, q_ref, kn_ref, vn_ref, kb_ref, vb_ref, o_ref, ko_ref, vo_ref):
    pad = jnp.zeros((A_BLOCK - T, A_KV), F32)
    bias = _swa_bias(T, True)
    ones = _swa_group_ones()
    for j in range(SWA_SAMPLE_GROUP):
        rows = slice(j * T, (j + 1) * T)
        k_new = kn_ref[rows]
        v_new = vn_ref[rows]
        k_buf = kb_ref[j]
        v_buf = vb_ref[j]
        o_ref[rows] = _swa_core(q_ref[rows], k_buf, v_buf, jnp.concatenate([k_new, pad], axis=0),
                                jnp.concatenate([v_new, pad], axis=0), bias, ones, sink_ref).astype(o_ref.dtype)
        ko_ref[j] = jnp.concatenate([k_buf[T:], k_new], axis=0)
        vo_ref[j] = jnp.concatenate([v_buf[T:], v_new], axis=0)


def _swa_sample(qa, ka, va, k_buf, v_buf, sinks, T):
    nb = k_buf.shape[0]
    g = SWA_SAMPLE_GROUP
    tok = lambda b, s: (b, 0)
    buf = lambda b, s: (b, 0, 0)
    return pl.pallas_call(
        functools.partial(_swa_sample_kernel, T),
        grid_spec=pltpu.PrefetchScalarGridSpec(
            num_scalar_prefetch=1,
            grid=(nb // g,),
            in_specs=[pl.BlockSpec((g * T, 1024), tok),
                      pl.BlockSpec((g * T, A_KV), tok),
                      pl.BlockSpec((g * T, A_KV), tok),
                      pl.BlockSpec((g, WINDOW, A_KV), buf),
                      pl.BlockSpec((g, WINDOW, A_KV), buf)],
            out_specs=[pl.BlockSpec((g * T, 1024), tok),
                       pl.BlockSpec((g, WINDOW, A_KV), buf),
                       pl.BlockSpec((g, WINDOW, A_KV), buf)]),
        out_shape=[jax.ShapeDtypeStruct((nb * T, 1024), qa.dtype),
                   jax.ShapeDtypeStruct((nb, WINDOW, A_KV), F32),
                   jax.ShapeDtypeStruct((nb, WINDOW, A_KV), F32)],
        compiler_params=_cparams(("arbitrary",)),
        name="swa_sample",
    )(sinks, qa, ka, va, k_buf, v_buf)


def _merge_kernel(x_ref, hm_ref, ha_ref, gm_ref, ga_ref, wpm_ref, wpa_ref, wo_ref, g2_ref, wr_ref, br_ref,
                  x1_ref, xn_ref, idx_ref, gate_ref, cnt_ref):
    pm = jnp.dot(hm_ref[...].astype(BF16), wpm_ref[...], preferred_element_type=F32)
    pa = jnp.dot(ha_ref[...].astype(BF16), wpa_ref[...], preferred_element_type=F32)
    mixed = gm_ref[...].astype(F32) * pm + ga_ref[...].astype(F32) * pa
    x1 = x_ref[...] + jnp.dot(mixed.astype(BF16), wo_ref[...], preferred_element_type=F32)
    x1_ref[...] = x1
    xn = _rms(x1, g2_ref[...])
    xn_ref[...] = xn.astype(BF16)
    xn_hi = xn.astype(BF16)
    xn_lo = (xn - xn_hi.astype(F32)).astype(BF16)
    part = jnp.dot(xn_hi, wr_ref[...], preferred_element_type=F32)
    logits = (part[:, :LANES] + part[:, LANES:]
              + jnp.dot(xn_lo, wr_ref[:, :LANES], preferred_element_type=F32) + br_ref[...])
    tm = logits.shape[0]
    cur = logits.T[0:N_EXPERTS]
    eid = lax.broadcasted_iota(I32, (N_EXPERTS, tm), 0).astype(F32)
    vals, idxs = [], []
    onehot = jnp.zeros((N_EXPERTS, tm), F32)
    for _ in range(TOP_K):
        mval = jnp.max(cur, axis=0, keepdims=True)
        sel = jnp.min(jnp.where(cur == mval, eid, float(N_EXPERTS)), axis=0, keepdims=True)
        hit = eid == sel
        onehot = onehot + hit.astype(F32)
        cur = jnp.where(hit, -jnp.inf, cur)
        vals.append(mval)
        idxs.append(sel)
    es = [jnp.exp(v - vals[0]) for v in vals]
    tot = es[0] + es[1] + es[2] + es[3]
    row = lax.broadcasted_iota(I32, (SUBLANES, tm), 0)
    res = jnp.zeros((SUBLANES, tm), F32)
    for k in range(TOP_K):
        res = jnp.where(row == k, idxs[k], res)
        res = jnp.where(row == TOP_K + k, es[k] / tot, res)
    res_t = jnp.concatenate([res, jnp.zeros((LANES - SUBLANES, tm), F32)], axis=0).T
    idx_ref[...] = res_t[:, 0:TOP_K].astype(I32)
    gate_ref[...] = res_t[:, TOP_K:2 * TOP_K]
    for j in range(tm // TOK_TILE):
        cnt_ref[j] = jnp.sum(onehot[:, j * TOK_TILE:(j + 1) * TOK_TILE], axis=1, keepdims=True).astype(I32)


def _merge(x, hm, ha, og, wpm, wpa, wo, g2, wr, br):
    t = x.shape[0]
    tm = MERGE_TILE
    row = lambda n: pl.BlockSpec((tm, n), lambda i: (i, 0))
    const = lambda r, c: pl.BlockSpec((r, c), lambda i: (0, 0))
    return pl.pallas_call(
        _merge_kernel,
        grid=(t // tm,),
        in_specs=[row(1024), row(1024), row(1024),
                  pl.BlockSpec((tm, 1024), lambda i: (i, 1)),
                  pl.BlockSpec((tm, 1024), lambda i: (i, 2)),
                  const(1024, 1024), const(1024, 1024), const(1024, 1024), const(1, 1024),
                  const(1024, 2 * LANES), const(1, LANES)],
        out_specs=[row(1024), row(1024), row(TOP_K), row(TOP_K),
                   pl.BlockSpec((tm // TOK_TILE, N_EXPERTS, 1), lambda i: (i, 0, 0))],
        out_shape=[jax.ShapeDtypeStruct((t, 1024), F32),
                   jax.ShapeDtypeStruct((t, 1024), BF16),
                   jax.ShapeDtypeStruct((t, TOP_K), I32),
                   jax.ShapeDtypeStruct((t, TOP_K), F32),
                   jax.ShapeDtypeStruct((t // TOK_TILE, N_EXPERTS, 1), I32)],
        compiler_params=_cparams(("arbitrary",)),
        name="merge_route",
    )(x, hm, ha, og, og, wpm, wpa, wo, g2, wr, br)


def _stage_rows(idx, off_row):
    tm = idx.shape[0]
    lane = lax.broadcasted_iota(I32, (tm, LANES), 1)
    hits = [lane == idx[:, k:k + 1] for k in range(TOP_K)]
    onehot = sum(h.astype(F32) for h in hits)
    r_i = lax.broadcasted_iota(I32, (tm, tm), 0)
    c_i = lax.broadcasted_iota(I32, (tm, tm), 1)
    before = (c_i < r_i).astype(BF16)
    rank = jnp.dot(before, onehot.astype(BF16), preferred_element_type=F32)
    pos = rank + off_row
    return [jnp.sum(jnp.where(h, pos, 0.0), axis=1, keepdims=True) for h in hits]


def _dispatch_kernel(ntile_a, cnt_ref, off_ref, base_ref, tail_ref, xa_ref, xb_ref, ia_ref, ib_ref, offv_ref,
                     xs_ref, stage, zeros, sem, zsem):
    i = pl.program_id(0)
    from_a = i < ntile_a
    xn = jnp.where(from_a, xa_ref[...], xb_ref[...])
    idx = jnp.where(from_a, ia_ref[...], ib_ref[...])
    rows = _stage_rows(idx, offv_ref[0])
    tm = idx.shape[0]
    lane = lax.broadcasted_iota(I32, (tm, LANES), 1)
    r4 = jnp.zeros((tm, LANES), F32)
    for k in range(TOP_K):
        r4 = jnp.where(lane == k, rows[k], r4)
    r4t = r4.T.astype(I32)
    r_iota = lax.broadcasted_iota(I32, (STAGE_ROWS, tm), 0)
    sel = jnp.zeros((STAGE_ROWS, tm), F32)
    for k in range(TOP_K):
        sel = sel + (r_iota == r4t[k:k + 1, :]).astype(F32)
    staged = jnp.dot(sel.astype(BF16), xn, preferred_element_type=F32)

    def runs(tile, slot):
        out = []
        for e in range(N_EXPERTS):
            n = pl.multiple_of(cnt_ref[tile * N_EXPERTS + e], RUN_ALIGN)
            off = pl.multiple_of(off_ref[tile * N_EXPERTS + e], RUN_ALIGN)
            base = pl.multiple_of(base_ref[tile * N_EXPERTS + e], RUN_ALIGN)
            out.append((n, pltpu.make_async_copy(stage.at[slot, pl.ds(off, n)], xs_ref.at[pl.ds(base, n)],
                                                 sem.at[slot])))
        return out

    def wait_runs(tile, slot):
        last_e = tile * N_EXPERTS + N_EXPERTS - 1
        total = pl.multiple_of(off_ref[last_e] + cnt_ref[last_e], RUN_ALIGN)
        pltpu.make_async_copy(stage.at[slot, pl.ds(0, total)], xs_ref.at[pl.ds(0, total)], sem.at[slot]).wait()

    for slot in range(2):
        @pl.when((i & 1) == slot)
        def _():
            @pl.when(i >= 2)
            def _():
                wait_runs(i - 2, slot)
            stage[slot] = staged
            for n, cp in runs(i, slot):
                @pl.when(n > 0)
                def _():
                    cp.start()

    last = pl.num_programs(0) - 1

    @pl.when(i == last)
    def _():
        for slot in range(2):
            @pl.when(((last - 1) & 1) == slot)
            def _():
                @pl.when(last >= 1)
                def _():
                    wait_runs(last - 1, slot)

            @pl.when((last & 1) == slot)
            def _():
                wait_runs(last, slot)
        zeros[...] = jnp.zeros(zeros.shape, F32)
        tails = []
        for e in range(N_EXPERTS):
            start = pl.multiple_of(tail_ref[e], RUN_ALIGN)
            n = pl.multiple_of(tail_ref[N_EXPERTS + e], RUN_ALIGN)
            tails.append((n, pltpu.make_async_copy(zeros.at[pl.ds(0, n)], xs_ref.at[pl.ds(start, n)], zsem)))
        for n, cp in tails:
            @pl.when(n > 0)
            def _():
                cp.start()
        for n, cp in tails:
            @pl.when(n > 0)
            def _():
                cp.wait()
        first = tail_ref[2 * N_EXPERTS]
        n_unused = tail_ref[2 * N_EXPERTS + 1]

        def unused_block(j):
            row = pl.multiple_of((first + j) * MOE_BLOCK, MOE_BLOCK)
            return pltpu.make_async_copy(zeros, xs_ref.at[pl.ds(row, MOE_BLOCK)], zsem)

        @pl.loop(0, n_unused)
        def _(j):
            unused_block(j).start()

        @pl.loop(0, n_unused)
        def _(j):
            unused_block(j).wait()


def _dispatch(xa, xb, ia, ib, cnt8, off, base, tails, offv, n_slots):
    tm = TOK_TILE
    ntile_a = xa.shape[0] // tm
    ntile_b = xb.shape[0] // tm
    amap = lambda i, *_: (jnp.minimum(i, ntile_a - 1), 0)
    bmap = lambda i, *_: (jnp.maximum(i - ntile_a, 0), 0)
    return pl.pallas_call(
        functools.partial(_dispatch_kernel, ntile_a),
        grid_spec=pltpu.PrefetchScalarGridSpec(
            num_scalar_prefetch=4,
            grid=(ntile_a + ntile_b,),
            in_specs=[pl.BlockSpec((tm, 1024), amap),
                      pl.BlockSpec((tm, 1024), bmap),
                      pl.BlockSpec((tm, TOP_K), amap),
                      pl.BlockSpec((tm, TOP_K), bmap),
                      pl.BlockSpec((1, 1, LANES), lambda i, *_: (i, 0, 0))],
            out_specs=pl.BlockSpec(memory_space=pl.ANY),
            scratch_shapes=[pltpu.VMEM((2, STAGE_ROWS, D_MODEL), F32),
                            pltpu.VMEM((MOE_BLOCK, D_MODEL), F32),
                            pltpu.SemaphoreType.DMA((2,)),
                            pltpu.SemaphoreType.DMA]),
        out_shape=jax.ShapeDtypeStruct((n_slots, D_MODEL), F32),
        compiler_params=_cparams(("arbitrary",)),
        name="moe_dispatch",
    )(cnt8, off, base, tails, xa, xb, ia, ib, offv)


MOE_HALF = MOE_BLOCK // 2


def _experts_kernel(be_ref, rows_ref, next_ref, slot_ref, xs_ref, wgu_hbm, bgu_ref, wd_hbm, bd_ref, y_ref,
                    wgu_f, wd_f, wgu_s, wd_s, sem):
    i = pl.program_id(0)
    rows = rows_ref[i]
    live = rows > 0
    run_start = (i == 0) | (be_ref[i] != be_ref[jnp.maximum(i - 1, 0)])

    def fetch(expert, slot):
        return (pltpu.make_async_copy(wgu_hbm.at[expert], wgu_f.at[slot], sem.at[0, slot]),
                pltpu.make_async_copy(wd_hbm.at[expert], wd_f.at[slot], sem.at[1, slot]))

    for slot in range(2):
        @pl.when(live & run_start & (slot_ref[i] == slot))
        def _():
            @pl.when(i == 0)
            def _():
                for cp in fetch(be_ref[i], slot):
                    cp.start()
            for cp in fetch(be_ref[i], slot):
                cp.wait()

            @pl.when(next_ref[i] >= 0)
            def _():
                for cp in fetch(next_ref[i], 1 - slot):
                    cp.start()
            wgu_s[...] = wgu_f[slot].astype(BF16)
            wd_s[...] = wd_f[slot].astype(BF16)

    def ffn(x):
        h = jnp.dot(x.astype(BF16), wgu_s[...], preferred_element_type=F32) + bgu_ref[0]
        gate = jnp.minimum(h[:, :D_FF], SWIGLU_LIMIT)
        up = jnp.clip(h[:, D_FF:], -SWIGLU_LIMIT, SWIGLU_LIMIT)
        act = gate * jax.nn.sigmoid(SWIGLU_ALPHA * gate) * (up + 1.0)
        return jnp.dot(act.astype(BF16), wd_s[...], preferred_element_type=F32) + bd_ref[0]

    @pl.when(rows > MOE_HALF)
    def _():
        y_ref[...] = ffn(xs_ref[...])

    @pl.when(live & (rows <= MOE_HALF))
    def _():
        y_ref[0:MOE_HALF] = ffn(xs_ref[0:MOE_HALF])
        y_ref[MOE_HALF:MOE_BLOCK] = jnp.zeros((MOE_HALF, D_MODEL), F32)

    @pl.when(rows == 0)
    def _():
        y_ref[...] = jnp.zeros(y_ref.shape, F32)


def _experts(block_expert, block_rows, next_expert, slot, xs, wgu, bgu, wd, bd):
    nb_max = xs.shape[0] // MOE_BLOCK
    blk = lambda i, be, rows, *_: (jnp.where(rows[i] > 0, i, 0), 0)
    wmap = lambda i, be, *_: (be[i], 0, 0)
    return pl.pallas_call(
        _experts_kernel,
        grid_spec=pltpu.PrefetchScalarGridSpec(
            num_scalar_prefetch=4,
            grid=(nb_max,),
            in_specs=[pl.BlockSpec((MOE_BLOCK, D_MODEL), blk),
                      pl.BlockSpec(memory_space=pl.ANY),
                      pl.BlockSpec((1, 1, 2 * D_FF), wmap),
                      pl.BlockSpec(memory_space=pl.ANY),
                      pl.BlockSpec((1, 1, D_MODEL), wmap)],
            out_specs=pl.BlockSpec((MOE_BLOCK, D_MODEL), lambda i, *_: (i, 0)),
            scratch_shapes=[pltpu.VMEM((2, D_MODEL, 2 * D_FF), F32),
                            pltpu.VMEM((2, D_FF, D_MODEL), F32),
                            pltpu.VMEM((D_MODEL, 2 * D_FF), BF16),
                            pltpu.VMEM((D_FF, D_MODEL), BF16),
                            pltpu.SemaphoreType.DMA((2, 2))]),
        out_shape=jax.ShapeDtypeStruct((xs.shape[0], D_MODEL), F32),
        compiler_params=_cparams(("arbitrary",)),
        name="moe_experts",
    )(block_expert, block_rows, next_expert, slot, xs, wgu, bgu, wd, bd)


def _combine_kernel(cnt_ref, off_ref, base_ref, x1_ref, idx_ref, gate_ref, offv_ref, gf_ref, y_ref,
                    o_ref, stage, sem):
    i = pl.program_id(0)
    last = pl.num_programs(0) - 1

    def runs(tile, slot):
        out = []
        for e in range(N_EXPERTS):
            n = pl.multiple_of(cnt_ref[tile * N_EXPERTS + e], RUN_ALIGN)
            off = pl.multiple_of(off_ref[tile * N_EXPERTS + e], RUN_ALIGN)
            base = pl.multiple_of(base_ref[tile * N_EXPERTS + e], RUN_ALIGN)
            out.append((n, pltpu.make_async_copy(y_ref.at[pl.ds(base, n)], stage.at[slot, pl.ds(off, n)],
                                                 sem.at[slot])))
        return out

    def start_runs(tile, slot):
        for n, cp in runs(tile, slot):
            @pl.when(n > 0)
            def _():
                cp.start()

    @pl.when(i == 0)
    def _():
        stage[...] = jnp.zeros(stage.shape, F32)
        start_runs(0, 0)

    for slot in range(2):
        @pl.when(((i & 1) != slot) & (i < last))
        def _():
            start_runs(i + 1, slot)

    rows = _stage_rows(idx_ref[...], offv_ref[0])
    tm = idx_ref.shape[0]
    lane = lax.broadcasted_iota(I32, (tm, STAGE_ROWS), 1)
    gates = gate_ref[...]
    gmat = jnp.zeros((tm, STAGE_ROWS), F32)
    for k in range(TOP_K):
        gmat = gmat + jnp.where(lane == rows[k].astype(I32), gates[:, k:k + 1], 0.0)
    gmat = gmat.astype(BF16)
    for slot in range(2):
        @pl.when((i & 1) == slot)
        def _():
            last_e = i * N_EXPERTS + N_EXPERTS - 1
            total = pl.multiple_of(off_ref[last_e] + cnt_ref[last_e], RUN_ALIGN)
            pltpu.make_async_copy(y_ref.at[pl.ds(0, total)], stage.at[slot, pl.ds(0, total)], sem.at[slot]).wait()
            moe = jnp.dot(gmat, stage[slot].astype(BF16), preferred_element_type=F32)
            o_ref[...] = _rms(x1_ref[...] + moe, gf_ref[...])


def _combine(x1, idx, gates, cnt8, off, base, offv, gf, y):
    t = x1.shape[0]
    tm = TOK_TILE
    return pl.pallas_call(
        _combine_kernel,
        grid_spec=pltpu.PrefetchScalarGridSpec(
            num_scalar_prefetch=3,
            grid=(t // tm,),
            in_specs=[pl.BlockSpec((tm, 1024), lambda i, *_: (i, 0)),
                      pl.BlockSpec((tm, TOP_K), lambda i, *_: (i, 0)),
                      pl.BlockSpec((tm, TOP_K), lambda i, *_: (i, 0)),
                      pl.BlockSpec((1, 1, LANES), lambda i, *_: (i, 0, 0)),
                      pl.BlockSpec((1, 1024), lambda i, *_: (0, 0)),
                      pl.BlockSpec(memory_space=pl.ANY)],
            out_specs=pl.BlockSpec((tm, 1024), lambda i, *_: (i, 0)),
            scratch_shapes=[pltpu.VMEM((2, STAGE_ROWS, 1024), F32),
                            pltpu.SemaphoreType.DMA((2,))]),
        out_shape=jax.ShapeDtypeStruct((t, 1024), F32),
        compiler_params=_cparams(("arbitrary",)),
        name="moe_combine",
    )(cnt8, off, base, x1, idx, gates, offv, gf, y)


def _prep_w_in(w):
    c_if, c_qa, c_kv, c_g = 4096, 4104, 5128, 5640
    wm = w[:, :c_if].astype(BF16)
    wif = jnp.pad(w[:, c_if:c_qa], ((0, 0), (0, LANES - 2 * M_HEADS))).astype(BF16)
    wqa = (w[:, c_qa:c_kv].reshape(D_MODEL, A_KV_HEADS, A_GROUP, A_HD).transpose(0, 2, 1, 3)
           .reshape(D_MODEL, 1024) * (A_HD ** -0.5)).astype(BF16)
    wkv = w[:, c_kv:c_g].astype(BF16)
    wg = w[:, c_g:].astype(BF16)
    return wm, wg, wqa, wkv, wif


def _rope_tables(pos):
    half = ROT_DIM // 2
    inv = ROPE_THETA ** (-(jnp.arange(half, dtype=F32) * 2.0 / ROT_DIM))
    ang = pos.astype(F32)[:, None] * inv[None, :]
    cos, sin = jnp.cos(ang), jnp.sin(ang)
    ones = jnp.ones((pos.shape[0], A_HD - ROT_DIM), F32)
    cos64 = jnp.concatenate([cos, cos, ones], axis=1)
    sin64 = jnp.concatenate([-sin, sin, 0.0 * ones], axis=1)
    return jnp.tile(cos64, (1, 2)), jnp.tile(sin64, (1, 2))


def _round_up(x, m):
    return (x + m - 1) // m * m


def kernel(x_prompt, x_sample, cache_k, cache_v, state_c, state_n, state_m, norm1, w_in, b_igate, b_fgate,
           mlstm_norm, w_proj_m, w_proj_a, attn_sinks, w_out, norm2, w_router, b_router, w_gate_up, b_gate_up,
           w_down, b_down, norm_f):
    bp, sp, _ = x_prompt.shape
    bs, ts, _ = x_sample.shape
    past_len = 16384
    l = 0
    w1 = _prep_w_in(w_in[l])
    g1 = norm1[l][None, :]
    g2 = norm2[l][None, :]
    gf = norm_f[None, :]
    gate_bias = jnp.pad(jnp.concatenate([b_igate[l], b_fgate[l]]), (0, LANES - 2 * M_HEADS))[None, :]
    mn = mlstm_norm[l][None, :]
    wpm = w_proj_m[l].astype(BF16)
    wpa = w_proj_a[l].reshape(A_KV_HEADS, A_GROUP, A_HD, D_MODEL).transpose(1, 0, 2, 3).reshape(1024, D_MODEL).astype(BF16)
    wo = w_out[l].astype(BF16)
    wr_f = jnp.pad(w_router[l], ((0, 0), (0, LANES - N_EXPERTS)))
    wr_hi = wr_f.astype(BF16)
    wr = jnp.concatenate([wr_hi, (wr_f - wr_hi.astype(F32)).astype(BF16)], axis=1)
    br = jnp.pad(b_router[l], (0, LANES - N_EXPERTS))[None, :]
    wgu = w_gate_up[l]
    bgu = b_gate_up[l][:, None, :]
    wd = w_down[l]
    bd = b_down[l][:, None, :]
    sinks = attn_sinks[l]

    def mixer(x, nb, seq, pos0, L, c0, n0, m0, k_buf, v_buf, tm):
        t = nb * seq
        xf = x.reshape(t, D_MODEL)
        cos_t, sin_t = _rope_tables(pos0 + jnp.arange(seq))
        if seq < tm:
            cos_t, sin_t = jnp.tile(cos_t, (tm // seq, 1)), jnp.tile(sin_t, (tm // seq, 1))
        act = BF16 if L % 16 == 0 else F32
        qkv, og, qa, ka, va, gates = _inproj(xf, g1, w1, cos_t, sin_t, tm, max(seq // tm, 1), act)
        m0p = jnp.pad(m0, ((0, 0), (0, LANES - M_HEADS)))[:, None, :]
        hm, c, n, m = _mlstm(qkv, gates, og, c0, n0, m0p, gate_bias, mn, nb, L)
        if k_buf is None:
            ha = _swa_prompt(qa, ka, va, sinks, nb)
            k_keep = ka.reshape(nb, seq, A_KV)[:, seq - WINDOW:].reshape(nb, WINDOW, A_KV_HEADS, A_HD)
            v_keep = va.reshape(nb, seq, A_KV)[:, seq - WINDOW:].reshape(nb, WINDOW, A_KV_HEADS, A_HD)
        else:
            ha, k_keep, v_keep = _swa_sample(qa, ka, va, k_buf.reshape(nb, WINDOW, A_KV),
                                             v_buf.reshape(nb, WINDOW, A_KV), sinks, seq)
            k_keep = k_keep.reshape(nb, WINDOW, A_KV_HEADS, A_HD)
            v_keep = v_keep.reshape(nb, WINDOW, A_KV_HEADS, A_HD)
        x1, xn, idx, gate, cnt = _merge(xf, hm, ha, og, wpm, wpa, wo, g2, wr, br)
        return (x1, xn, idx, gate, cnt[:, :, 0]), (k_keep, v_keep, c, n, m[:, 0, :M_HEADS])

    zc = jnp.zeros((bp, M_HEADS, M_DV, M_DK), F32)
    zn = jnp.zeros((bp, M_HEADS, M_DK), F32)
    zm = jnp.zeros((bp, M_HEADS), F32)
    rp, sp_out = mixer(x_prompt, bp, sp, 0, M_CHUNK, zc, zn, zm, None, None, 512)
    rs, ss_out = mixer(x_sample, bs, ts, past_len, ts, state_c[l], state_n[l], state_m[l],
                       cache_k[l], cache_v[l], 256)

    cnt = jnp.concatenate([rp[4], rs[4]], axis=0)
    ntile_p = rp[4].shape[0]
    cnt8 = _round_up(cnt, RUN_ALIGN)
    off = jnp.cumsum(cnt8, axis=1) - cnt8
    per_expert = jnp.sum(cnt8, axis=0)
    padded = _round_up(per_expert, MOE_BLOCK)
    padded_end = jnp.cumsum(padded)
    expert_start = padded_end - padded
    base = expert_start[None, :] + jnp.cumsum(cnt8, axis=0) - cnt8
    n_tok = bp * sp + bs * ts
    n_slots = _round_up(n_tok * TOP_K + (n_tok // TOK_TILE) * N_EXPERTS * (RUN_ALIGN - 1), MOE_BLOCK) + N_EXPERTS * MOE_BLOCK
    nb_max = n_slots // MOE_BLOCK
    block_row = jnp.arange(nb_max, dtype=I32) * MOE_BLOCK
    block_expert = jnp.minimum(jnp.sum(block_row[:, None] >= padded_end[None, :], axis=1), N_EXPERTS - 1).astype(I32)
    n_blocks = (padded_end[-1] // MOE_BLOCK).astype(I32)[None]
    eid = jnp.arange(N_EXPERTS, dtype=I32)
    later_used = (eid[None, :] > eid[:, None]) & (padded[None, :] > 0)
    next_used = jnp.min(jnp.where(later_used, eid[None, :], N_EXPERTS), axis=1)
    next_used = jnp.where(next_used < N_EXPERTS, next_used, -1).astype(I32)
    run_parity = ((jnp.cumsum(padded > 0) - 1) & 1).astype(I32)
    of_block = lambda table: jnp.sum(jnp.where(block_expert[:, None] == eid[None, :], table[None, :], 0),
                                     axis=1).astype(I32)
    block_rows = jnp.clip(of_block(per_expert) - (block_row - of_block(expert_start)), 0, MOE_BLOCK).astype(I32)
    tails = jnp.concatenate([expert_start + per_expert, padded - per_expert,
                             n_blocks, nb_max - n_blocks]).astype(I32)
    flat = lambda a: a.reshape(-1).astype(I32)
    offv = jnp.pad(off, ((0, 0), (0, LANES - N_EXPERTS))).astype(F32)[:, None, :]

    xs = _dispatch(rp[1], rs[1], rp[2], rs[2], flat(cnt8), flat(off), flat(base), tails, offv, n_slots)
    y = _experts(block_expert, block_rows, of_block(next_used), of_block(run_parity), xs, wgu, bgu, wd, bd)
    y_p = _combine(rp[0], rp[2], rp[3], flat(cnt8[:ntile_p]), flat(off[:ntile_p]), flat(base[:ntile_p]),
                   offv[:ntile_p], gf, y)
    y_s = _combine(rs[0], rs[2], rs[3], flat(cnt8[ntile_p:]), flat(off[ntile_p:]), flat(base[ntile_p:]),
                   offv[ntile_p:], gf, y)

    kp, vp, cp_, np_, mp = sp_out
    ks, vs, cs, ns, ms = ss_out
    return (y_p.reshape(bp, sp, D_MODEL), y_s.reshape(bs, ts, D_MODEL),
            kp[None], vp[None], cp_[None], np_[None], mp[None],
            ks[None], vs[None], cs[None], ns[None], ms[None])
```

```python
import functools

import jax
import jax.numpy as jnp
from jax import lax
from jax.experimental import pallas as pl
from jax.experimental.pallas import tpu as pltpu

F32 = jnp.float32
BF16 = jnp.bfloat16
I32 = jnp.int32

D_MODEL = 1024
M_HEADS = 4
M_DK = 256
M_DV = 256
M_CHUNK = 128
A_Q_HEADS = 16
A_KV_HEADS = 4
A_GROUP = 4
A_HD = 64
A_KV = A_KV_HEADS * A_HD
WINDOW = 128
A_BLOCK = 128
ROT_DIM = 16
ROPE_THETA = 500000.0
N_EXPERTS = 32
TOP_K = 4
D_FF = 1024
SWIGLU_LIMIT = 7.0
SWIGLU_ALPHA = 1.702
NORM_EPS = 1e-6

LANES = 128
SUBLANES = 8
VMEM_LIMIT = 56 * 1024 * 1024

MERGE_TILE = 512
TOK_TILE = 256
STAGE_ROWS = 4 * TOK_TILE
ROW_TILE = (8, 128)
MOE_BLOCK = 512


def _cparams(sem):
    return pltpu.CompilerParams(dimension_semantics=sem, vmem_limit_bytes=VMEM_LIMIT)


def _rms(x, g):
    return x * lax.rsqrt(jnp.mean(x * x, axis=-1, keepdims=True) + NORM_EPS) * g


def _rope(x, cos_t, sin_t, first_half):
    n = x.shape[1]
    fwd = pltpu.roll(x, n - ROT_DIM // 2, axis=1)
    bwd = pltpu.roll(x, ROT_DIM // 2, axis=1)
    outs = []
    for b in range(n // LANES):
        sl = slice(b * LANES, (b + 1) * LANES)
        partner = jnp.where(first_half, fwd[:, sl], bwd[:, sl])
        outs.append(x[:, sl] * cos_t + partner * sin_t)
    return jnp.concatenate(outs, axis=1)


def _inproj_kernel(x_ref, g1_ref, wm_ref, wg_ref, wqa_ref, wkv_ref, wif_ref, cos_ref, sin_ref,
                   qkv_ref, og_ref, qa_ref, ka_ref, va_ref, gate_ref):
    x = x_ref[...]
    xn = _rms(x, g1_ref[...]).astype(BF16)
    act = qkv_ref.dtype

    def proj(w_ref, c0, n):
        return jnp.dot(xn, w_ref[:, c0:c0 + n], preferred_element_type=F32)

    qkv_ref[:, 0:1024] = proj(wm_ref, 0, 1024).astype(act)
    qkv_ref[:, 1024:2048] = (proj(wm_ref, 1024, 1024) * (M_DK ** -0.5)).astype(act)
    qkv_ref[:, 2048:3072] = proj(wm_ref, 2048, 1024).astype(act)
    og_ref[:, 0:1024] = jax.nn.sigmoid(proj(wm_ref, 3072, 1024)).astype(act)
    og_ref[:, 1024:2048] = jax.nn.sigmoid(proj(wg_ref, 0, 1024)).astype(act)
    og_ref[:, 2048:3072] = jax.nn.sigmoid(proj(wg_ref, 1024, 1024)).astype(act)
    cos_t = cos_ref[...]
    sin_t = sin_ref[...]
    lane = lax.broadcasted_iota(I32, (1, LANES), 1)
    first_half = (lane & (ROT_DIM - 1)) < (ROT_DIM // 2)
    qa = _rope(proj(wqa_ref, 0, 1024), cos_t, sin_t, first_half).astype(act)
    for g in range(A_KV_HEADS):
        for j in range(A_GROUP):
            src = (A_GROUP * g + j) * A_HD
            dst = (A_KV_HEADS * j + g) * A_HD
            qa_ref[:, dst:dst + A_HD] = qa[:, src:src + A_HD]
    ka_ref[...] = _rope(proj(wkv_ref, 0, A_KV), cos_t, sin_t, first_half)
    va_ref[...] = proj(wkv_ref, A_KV, A_KV)
    gate_ref[...] = proj(wif_ref, 0, LANES)


def _inproj(x, g1, weights, cos_t, sin_t, tm, rope_blocks, act):
    t = x.shape[0]
    tok = lambda n, dt: jax.ShapeDtypeStruct((t, n), dt)
    row = lambda n: pl.BlockSpec((tm, n), lambda i: (i, 0))
    resident = lambda w: pl.BlockSpec(w.shape, lambda i: (0, 0), pipeline_mode=pl.Buffered(1))
    return pl.pallas_call(
        _inproj_kernel,
        grid=(t // tm,),
        in_specs=[row(D_MODEL), pl.BlockSpec((1, D_MODEL), lambda i: (0, 0))]
                 + [resident(w) for w in weights]
                 + [pl.BlockSpec((tm, LANES), lambda i: (i % rope_blocks, 0)),
                    pl.BlockSpec((tm, LANES), lambda i: (i % rope_blocks, 0))],
        out_specs=[row(3072), row(3072), row(1024), row(A_KV), row(A_KV), row(LANES)],
        out_shape=[tok(3072, act), tok(3072, act), tok(1024, act), tok(A_KV, F32), tok(A_KV, F32),
                   tok(LANES, F32)],
        compiler_params=_cparams(("arbitrary",)),
        name="inproj",
    )(x, g1, *weights, cos_t, sin_t)


def _mlstm_gates(gc, bias, m_prev, L):
    z = gc + bias
    lf = jnp.minimum(z, 0.0) - jnp.log1p(jnp.exp(-jnp.abs(z)))
    row = lax.broadcasted_iota(I32, (L, LANES), 0)
    fc = lf
    sh = 1
    while sh < L:
        fc = fc + jnp.where(row >= sh, pltpu.roll(fc, sh, axis=0), 0.0)
        sh *= 2
    fcum = pltpu.roll(fc, LANES - M_HEADS, axis=1)
    a = z - fcum
    cmx = a
    sh = 1
    while sh < L:
        cmx = jnp.maximum(cmx, jnp.where(row >= sh, pltpu.roll(cmx, sh, axis=0), -jnp.inf))
        sh *= 2
    mx = jnp.maximum(m_prev, cmx)
    inter = jnp.exp(m_prev - mx)
    f_end = fcum[L - 1:L]
    m_end = f_end + mx[L - 1:L]
    decay = jnp.exp(f_end + m_prev - m_end)
    wsrc = jnp.exp(a + (f_end - m_end))
    return a, mx, inter, wsrc, decay, m_end


NT_DIMS = (((1,), (1,)), ((), ()))
TN_DIMS = (((0,), (0,)), ((), ()))


def _mlstm_decay(h, L, gates):
    a, mx = gates[0], gates[1]
    t_i = lax.broadcasted_iota(I32, (L, L), 0)
    s_i = lax.broadcasted_iota(I32, (L, L), 1)
    a_row = jnp.sum(jnp.where(t_i == s_i, a[:, h:h + 1], 0.0), axis=0, keepdims=True)
    return jnp.exp(jnp.where(s_i <= t_i, a_row - mx[:, h:h + 1], -jnp.inf))


def _mlstm_output(h, q, og, mn, gates, qk_sum, pv, cq, n_h):
    inter_col = gates[2][:, h:h + 1]
    num = inter_col * cq + pv
    den = inter_col * jnp.sum(q.astype(F32) * n_h, axis=1, keepdims=True) + qk_sum
    hh = num / jnp.maximum(jnp.abs(den), 1.0)
    hn = hh * lax.rsqrt(jnp.mean(hh * hh, axis=1, keepdims=True) + NORM_EPS)
    return hn * mn * og


def _mlstm_kernel(L, G, carry, qkv_ref, gate_ref, og_ref, c0_ref, n0_ref, m0_ref, bias_ref, mn_ref,
                  h_ref, c_ref, n_ref, m_ref, *scratch):
    if carry:
        c_in, n_in, m_in = scratch
        ci = pl.program_id(1)

        @pl.when(ci == 0)
        def _():
            c_in[...] = c0_ref[...]
            n_in[...] = n0_ref[...]
            m_in[...] = m0_ref[...]
        c_out, n_out, m_out = scratch
    else:
        c_in, n_in, m_in = c0_ref, n0_ref, m0_ref
        c_out, n_out, m_out = c_ref, n_ref, m_ref

    def qkv(j, h, part):
        return qkv_ref[j, :, part * 1024 + h * M_DK:part * 1024 + (h + 1) * M_DK].astype(BF16)

    pairs = [(j, h) for j in range(G) for h in range(M_HEADS)]
    gates = [_mlstm_gates(gate_ref[j], bias_ref[...], m_in[j], L) for j in range(G)]
    scores = [lax.dot_general(qkv(j, h, 0), qkv(j, h, 1), NT_DIMS, preferred_element_type=F32) for j, h in pairs]
    cqs = [lax.dot_general(qkv(j, h, 0), c_in[j, h].astype(BF16), NT_DIMS, preferred_element_type=F32)
           for j, h in pairs]
    qks = [s * _mlstm_decay(h, L, gates[j]) for (j, h), s in zip(pairs, scores)]
    pvs = [jnp.dot(qk.astype(BF16), qkv(j, h, 2), preferred_element_type=F32) for (j, h), qk in zip(pairs, qks)]
    kws = [gates[j][3][:, h:h + 1] * qkv(j, h, 1).astype(F32) for j, h in pairs]
    for (j, h), qk, pv, cq in zip(pairs, qks, pvs, cqs):
        sl = slice(h * M_DV, (h + 1) * M_DV)
        out = _mlstm_output(h, qkv(j, h, 0), og_ref[j, :, sl].astype(F32), mn_ref[:, sl], gates[j],
                            jnp.sum(qk, axis=1, keepdims=True), pv, cq, n_in[j, h:h + 1, :])
        h_ref[j, :, sl] = out.astype(h_ref.dtype)
    for (j, h), kw in zip(pairs, kws):
        dec = gates[j][4][:, h:h + 1]
        c_out[j, h] = dec * c_in[j, h] + lax.dot_general(qkv(j, h, 2), kw.astype(BF16), TN_DIMS,
                                                         preferred_element_type=F32)
        n_out[j, h:h + 1, :] = dec * n_in[j, h:h + 1, :] + jnp.sum(kw, axis=0, keepdims=True)
    for j in range(G):
        m_out[j] = gates[j][5]

    if carry:
        @pl.when(ci == pl.num_programs(1) - 1)
        def _():
            c_ref[...] = c_in[...]
            n_ref[...] = n_in[...]
            m_ref[...] = m_in[...]


MLSTM_GROUP_CARRY = 2
MLSTM_GROUP_SINGLE = 4


def _mlstm(qkv, gates, og, c0, n0, m0, gate_bias, mnorm, nb, L):
    t = qkv.shape[0]
    nc = t // (nb * L)
    carry = nc > 1
    G = MLSTM_GROUP_CARRY if carry else MLSTM_GROUP_SINGLE
    qkv, gates, og = (a.reshape(nb, nc * L, a.shape[1]) for a in (qkv, gates, og))
    tokmap = lambda b, c: (b, c, 0)
    const2 = lambda b, c: (0, 0)
    state_specs = [pl.BlockSpec((G, M_HEADS, M_DV, M_DK), lambda b, c: (b, 0, 0, 0)),
                   pl.BlockSpec((G, M_HEADS, M_DK), lambda b, c: (b, 0, 0)),
                   pl.BlockSpec((G, 1, LANES), lambda b, c: (b, 0, 0))]
    scratch = [pltpu.VMEM((G, M_HEADS, M_DV, M_DK), F32),
               pltpu.VMEM((G, M_HEADS, M_DK), F32),
               pltpu.VMEM((G, 1, LANES), F32)] if carry else []
    h, c, n, m = pl.pallas_call(
        functools.partial(_mlstm_kernel, L, G, carry),
        grid=(nb // G, nc),
        in_specs=[pl.BlockSpec((G, L, 3072), tokmap),
                  pl.BlockSpec((G, L, LANES), tokmap),
                  pl.BlockSpec((G, L, 1024), tokmap)]
                 + state_specs
                 + [pl.BlockSpec((1, LANES), const2),
                    pl.BlockSpec((1, 1024), const2)],
        out_specs=[pl.BlockSpec((G, L, 1024), tokmap)] + state_specs,
        out_shape=[jax.ShapeDtypeStruct((nb, nc * L, 1024), qkv.dtype),
                   jax.ShapeDtypeStruct((nb, M_HEADS, M_DV, M_DK), F32),
                   jax.ShapeDtypeStruct((nb, M_HEADS, M_DK), F32),
                   jax.ShapeDtypeStruct((nb, 1, LANES), F32)],
        scratch_shapes=scratch,
        compiler_params=_cparams(("arbitrary", "arbitrary")),
        name="mlstm_L%d" % L,
    )(qkv, gates, og, c0, n0, m0, gate_bias, mnorm)
    return h.reshape(t, 1024), c, n, m


def _swa_bias(m, has_prev):
    nkeys = 2 * A_BLOCK
    t_i = lax.broadcasted_iota(I32, (A_GROUP * m, nkeys), 0) & (m - 1)
    s_i = lax.broadcasted_iota(I32, (A_GROUP * m, nkeys), 1)
    diff = t_i + A_BLOCK - s_i
    mask = (diff >= 0) & (diff <= WINDOW)
    if not has_prev:
        mask = mask & (s_i >= A_BLOCK)
    return jnp.where(mask, 0.0, -jnp.inf).astype(F32)


def _swa_group_ones():
    nkeys = 2 * A_BLOCK
    r = lax.broadcasted_iota(I32, (A_KV_HEADS * nkeys, A_KV), 0) >> (nkeys.bit_length() - 1)
    c = lax.broadcasted_iota(I32, (A_KV_HEADS * nkeys, A_KV), 1) >> (A_HD.bit_length() - 1)
    return (r == c).astype(BF16)


def _swa_core(q, k_prev, v_prev, k_cur, v_cur, bias, group_ones, sink_ref):
    m = q.shape[0]
    q = q.astype(BF16)
    qst = jnp.concatenate([q[:, j * A_KV:(j + 1) * A_KV] for j in range(A_GROUP)], axis=0)
    k_all = jnp.concatenate([k_prev, k_cur], axis=0)
    v_all = jnp.concatenate([v_prev, v_cur], axis=0)
    lane = lax.broadcasted_iota(I32, (1, A_KV), 1)
    kbd, vbd = [], []
    for g in range(A_KV_HEADS):
        in_g = (lane >= g * A_HD) & (lane < (g + 1) * A_HD)
        kbd.append(jnp.where(in_g, k_all, 0.0).astype(BF16))
        vbd.append(jnp.where(in_g, v_all, 0.0).astype(BF16))
    kbd = jnp.concatenate(kbd, axis=0)
    vbd = jnp.concatenate(vbd, axis=0)
    s = lax.dot_general(qst, kbd, (((1,), (1,)), ((), ())), preferred_element_type=F32)
    nkeys = 2 * A_BLOCK
    ps, sink_terms = [], []
    for g in range(A_KV_HEADS):
        sg = s[:, g * nkeys:(g + 1) * nkeys] + bias
        sink = jnp.concatenate(
            [jnp.full((m, LANES), sink_ref[A_GROUP * g + j], F32) for j in range(A_GROUP)], axis=0)
        mx = jnp.maximum(jnp.broadcast_to(jnp.max(sg, axis=1, keepdims=True), (A_GROUP * m, LANES)), sink)
        ps += [jnp.exp(sg[:, :LANES] - mx).astype(BF16), jnp.exp(sg[:, LANES:] - mx).astype(BF16)]
        sink_terms.append(jnp.exp(sink - mx))
    p = jnp.concatenate(ps, axis=1)
    o = jnp.dot(p, vbd, preferred_element_type=F32)
    den = jnp.dot(p, group_ones, preferred_element_type=F32)
    low = lax.broadcasted_iota(I32, (1, LANES), 1) < A_HD
    den = den + jnp.concatenate([jnp.where(low, sink_terms[0], sink_terms[1]),
                                 jnp.where(low, sink_terms[2], sink_terms[3])], axis=1)
    o = o / den
    return jnp.concatenate([o[j * m:(j + 1) * m] for j in range(A_GROUP)], axis=1)


SWA_PAIR = 2 * A_BLOCK


def _swa_prompt_kernel(sink_ref, q_ref, kp_ref, kc_ref, vp_ref, vc_ref, o_ref, bias_s, bias0_s, ones_s):
    first = (pl.program_id(0) == 0) & (pl.program_id(1) == 0)

    @pl.when(first)
    def _():
        bias_s[...] = _swa_bias(A_BLOCK, True)
        bias0_s[...] = _swa_bias(A_BLOCK, False)
        ones_s[...] = _swa_group_ones()

    bias = bias_s[...]
    bias_a = jnp.where(pl.program_id(1) > 0, bias, bias0_s[...])
    ones = ones_s[...]
    k0, k1 = kc_ref[0:A_BLOCK], kc_ref[A_BLOCK:SWA_PAIR]
    v0, v1 = vc_ref[0:A_BLOCK], vc_ref[A_BLOCK:SWA_PAIR]
    o_ref[0:A_BLOCK] = _swa_core(q_ref[0:A_BLOCK], kp_ref[...], vp_ref[...], k0, v0,
                                 bias_a, ones, sink_ref).astype(o_ref.dtype)
    o_ref[A_BLOCK:SWA_PAIR] = _swa_core(q_ref[A_BLOCK:SWA_PAIR], k0, v0, k1, v1,
                                        bias, ones, sink_ref).astype(o_ref.dtype)


def _swa_prompt(qa, ka, va, sinks, nb):
    t = qa.shape[0]
    npair = t // (nb * SWA_PAIR)
    cur = lambda b, i, s: (b * npair + i, 0)
    prev = lambda b, i, s: (2 * (b * npair + i) - jnp.minimum(i, 1), 0)
    return pl.pallas_call(
        _swa_prompt_kernel,
        grid_spec=pltpu.PrefetchScalarGridSpec(
            num_scalar_prefetch=1,
            grid=(nb, npair),
            in_specs=[pl.BlockSpec((SWA_PAIR, 1024), cur),
                      pl.BlockSpec((A_BLOCK, A_KV), prev),
                      pl.BlockSpec((SWA_PAIR, A_KV), cur),
                      pl.BlockSpec((A_BLOCK, A_KV), prev),
                      pl.BlockSpec((SWA_PAIR, A_KV), cur)],
            out_specs=pl.BlockSpec((SWA_PAIR, 1024), cur),
            scratch_shapes=[pltpu.VMEM((A_GROUP * A_BLOCK, SWA_PAIR), F32),
                            pltpu.VMEM((A_GROUP * A_BLOCK, SWA_PAIR), F32),
                            pltpu.VMEM((A_KV_HEADS * SWA_PAIR, A_KV), BF16)]),
        out_shape=jax.ShapeDtypeStruct((t, 1024), BF16),
        compiler_params=_cparams(("arbitrary", "arbitrary")),
        name="swa_prompt",
    )(sinks, qa, ka, ka, va, va)


SWA_SAMPLE_GROUP = 4


def _swa_sample_kernel(T, sink_ref, q_ref, kn_ref, vn_ref, kb_ref, vb_ref, o_ref, ko_ref, vo_ref):
    pad = jnp.zeros((A_BLOCK - T, A_KV), F32)
    bias = _swa_bias(T, True)
    ones = _swa_group_ones()
    for j in range(SWA_SAMPLE_GROUP):
        rows = slice(j * T, (j + 1) * T)
        k_new = kn_ref[rows]
        v_new = vn_ref[rows]
        k_buf = kb_ref[j]
        v_buf = vb_ref[j]
        o_ref[rows] = _swa_core(q_ref[rows], k_buf, v_buf, jnp.concatenate([k_new, pad], axis=0),
                                jnp.concatenate([v_new, pad], axis=0), bias, ones, sink_ref).astype(o_ref.dtype)
        ko_ref[j] = jnp.concatenate([k_buf[T:], k_new], axis=0)
        vo_ref[j] = jnp.concatenate([v_buf[T:], v_new], axis=0)


def _swa_sample(qa, ka, va, k_buf, v_buf, sinks, T):
    nb = k_buf.shape[0]
    g = SWA_SAMPLE_GROUP
    tok = lambda b, s: (b, 0)
    buf = lambda b, s: (b, 0, 0)
    return pl.pallas_call(
        functools.partial(_swa_sample_kernel, T),
        grid_spec=pltpu.PrefetchScalarGridSpec(
            num_scalar_prefetch=1,
            grid=(nb // g,),
            in_specs=[pl.BlockSpec((g * T, 1024), tok),
                      pl.BlockSpec((g * T, A_KV), tok),
                      pl.BlockSpec((g * T, A_KV), tok),
                      pl.BlockSpec((g, WINDOW, A_KV), buf),
                      pl.BlockSpec((g, WINDOW, A_KV), buf)],
            out_specs=[pl.BlockSpec((g * T, 1024), tok),
                       pl.BlockSpec((g, WINDOW, A_KV), buf),
                       pl.BlockSpec((g, WINDOW, A_KV), buf)]),
        out_shape=[jax.ShapeDtypeStruct((nb * T, 1024), qa.dtype),
                   jax.ShapeDtypeStruct((nb, WINDOW, A_KV), F32),
                   jax.ShapeDtypeStruct((nb, WINDOW, A_KV), F32)],
        compiler_params=_cparams(("arbitrary",)),
        name="swa_sample",
    )(sinks, qa, ka, va, k_buf, v_buf)


def _merge_kernel(x_ref, hm_ref, ha_ref, gm_ref, ga_ref, wpm_ref, wpa_ref, wo_ref, g2_ref, wr_ref, br_ref,
                  x1_ref, xn_ref, idx_ref, gate_ref, cnt_ref):
    pm = jnp.dot(hm_ref[...].astype(BF16), wpm_ref[...], preferred_element_type=F32)
    pa = jnp.dot(ha_ref[...].astype(BF16), wpa_ref[...], preferred_element_type=F32)
    mixed = gm_ref[...].astype(F32) * pm + ga_ref[...].astype(F32) * pa
    x1 = x_ref[...] + jnp.dot(mixed.astype(BF16), wo_ref[...], preferred_element_type=F32)
    x1_ref[...] = x1
    xn = _rms(x1, g2_ref[...])
    xn_ref[...] = xn.astype(BF16)
    xn_hi = xn.astype(BF16)
    xn_lo = (xn - xn_hi.astype(F32)).astype(BF16)
    part = jnp.dot(xn_hi, wr_ref[...], preferred_element_type=F32)
    logits = (part[:, :LANES] + part[:, LANES:]
              + jnp.dot(xn_lo, wr_ref[:, :LANES], preferred_element_type=F32) + br_ref[...])
    tm = logits.shape[0]
    cur = logits.T[0:N_EXPERTS]
    eid = lax.broadcasted_iota(I32, (N_EXPERTS, tm), 0).astype(F32)
    vals, idxs = [], []
    onehot = jnp.zeros((N_EXPERTS, tm), F32)
    for _ in range(TOP_K):
        mval = jnp.max(cur, axis=0, keepdims=True)
        sel = jnp.min(jnp.where(cur == mval, eid, float(N_EXPERTS)), axis=0, keepdims=True)
        hit = eid == sel
        onehot = onehot + hit.astype(F32)
        cur = jnp.where(hit, -jnp.inf, cur)
        vals.append(mval)
        idxs.append(sel)
    es = [jnp.exp(v - vals[0]) for v in vals]
    tot = es[0] + es[1] + es[2] + es[3]
    row = lax.broadcasted_iota(I32, (SUBLANES, tm), 0)
    res = jnp.zeros((SUBLANES, tm), F32)
    for k in range(TOP_K):
        res = jnp.where(row == k, idxs[k], res)
        res = jnp.where(row == TOP_K + k, es[k] / tot, res)
    res_t = jnp.concatenate([res, jnp.zeros((LANES - SUBLANES, tm), F32)], axis=0).T
    idx_ref[...] = res_t[:, 0:TOP_K].astype(I32)
    gate_ref[...] = res_t[:, TOP_K:2 * TOP_K]
    for j in range(tm // TOK_TILE):
        cnt_ref[j] = jnp.sum(onehot[:, j * TOK_TILE:(j + 1) * TOK_TILE], axis=1, keepdims=True).astype(I32)


def _merge(x, hm, ha, og, wpm, wpa, wo, g2, wr, br):
    t = x.shape[0]
    tm = MERGE_TILE
    row = lambda n: pl.BlockSpec((tm, n), lambda i: (i, 0))
    const = lambda r, c: pl.BlockSpec((r, c), lambda i: (0, 0))
    return pl.pallas_call(
        _merge_kernel,
        grid=(t // tm,),
        in_specs=[row(1024), row(1024), row(1024),
                  pl.BlockSpec((tm, 1024), lambda i: (i, 1)),
                  pl.BlockSpec((tm, 1024), lambda i: (i, 2)),
                  const(1024, 1024), const(1024, 1024), const(1024, 1024), const(1, 1024),
                  const(1024, 2 * LANES), const(1, LANES)],
        out_specs=[row(1024), row(1024), row(TOP_K), row(TOP_K),
                   pl.BlockSpec((tm // TOK_TILE, N_EXPERTS, 1), lambda i: (i, 0, 0))],
        out_shape=[jax.ShapeDtypeStruct((t, 1024), F32),
                   jax.ShapeDtypeStruct((t, 1024), BF16),
                   jax.ShapeDtypeStruct((t, TOP_K), I32),
                   jax.ShapeDtypeStruct((t, TOP_K), F32),
                   jax.ShapeDtypeStruct((t // TOK_TILE, N_EXPERTS, 1), I32)],
        compiler_params=_cparams(("arbitrary",)),
        name="merge_route",
    )(x, hm, ha, og, og, wpm, wpa, wo, g2, wr, br)


def _rows_to_tiles(ref, lead, x):
    chunks = jnp.stack([x[:, s * LANES:(s + 1) * LANES] for s in range(ROW_TILE[0])], axis=0)
    ref[lead] = jnp.swapaxes(chunks, 0, 1)


def _tiles_to_rows(ref, lead, rows=slice(None)):
    chunks = jnp.swapaxes(ref[lead + (rows,)], 0, 1)
    return jnp.concatenate([chunks[s] for s in range(ROW_TILE[0])], axis=1)


def _stage_rows(idx, off_row):
    tm = idx.shape[0]
    lane = lax.broadcasted_iota(I32, (tm, LANES), 1)
    hits = [lane == idx[:, k:k + 1] for k in range(TOP_K)]
    onehot = sum(h.astype(F32) for h in hits)
    r_i = lax.broadcasted_iota(I32, (tm, tm), 0)
    c_i = lax.broadcasted_iota(I32, (tm, tm), 1)
    before = (c_i < r_i).astype(BF16)
    rank = jnp.dot(before, onehot.astype(BF16), preferred_element_type=F32)
    pos = rank + off_row
    return [jnp.sum(jnp.where(h, pos, 0.0), axis=1, keepdims=True) for h in hits]


def _dispatch_kernel(ntile_a, cnt_ref, off_ref, base_ref, tail_ref, xa_ref, xb_ref, ia_ref, ib_ref, offv_ref,
                     xs_ref, stage, zeros, sem, zsem):
    i = pl.program_id(0)
    from_a = i < ntile_a
    xn = jnp.where(from_a, xa_ref[...], xb_ref[...])
    idx = jnp.where(from_a, ia_ref[...], ib_ref[...])
    rows = _stage_rows(idx, offv_ref[0])
    tm = idx.shape[0]
    lane = lax.broadcasted_iota(I32, (tm, LANES), 1)
    r4 = jnp.zeros((tm, LANES), F32)
    for k in range(TOP_K):
        r4 = jnp.where(lane == k, rows[k], r4)
    r4t = r4.T.astype(I32)
    r_iota = lax.broadcasted_iota(I32, (STAGE_ROWS, tm), 0)
    sel = r_iota == r4t[0:1, :]
    for k in range(1, TOP_K):
        sel = sel | (r_iota == r4t[k:k + 1, :])
    sel = jnp.where(sel, 1.0, 0.0).astype(BF16)
    staged = jnp.dot(sel, xn, preferred_element_type=F32)

    def runs(tile, slot):
        out = []
        for e in range(N_EXPERTS):
            n = cnt_ref[tile * N_EXPERTS + e]
            off = off_ref[tile * N_EXPERTS + e]
            base = base_ref[tile * N_EXPERTS + e]
            out.append((n, pltpu.make_async_copy(stage.at[slot, pl.ds(off, n)], xs_ref.at[pl.ds(base, n)],
                                                 sem.at[slot])))
        return out

    def wait_runs(slot):
        pltpu.make_async_copy(stage.at[slot], xs_ref.at[pl.ds(0, STAGE_ROWS)], sem.at[slot]).wait()

    for slot in range(2):
        @pl.when((i & 1) == slot)
        def _():
            @pl.when(i >= 2)
            def _():
                wait_runs(slot)
            _rows_to_tiles(stage, (slot,), staged)
            for n, cp in runs(i, slot):
                @pl.when(n > 0)
                def _():
                    cp.start()

    last = pl.num_programs(0) - 1

    @pl.when(i == last)
    def _():
        for slot in range(2):
            @pl.when(((last & 1) == slot) | (last >= 1))
            def _():
                wait_runs(slot)
        zeros[...] = jnp.zeros(zeros.shape, F32)
        tails = []
        for e in range(N_EXPERTS):
            start = tail_ref[e]
            n = tail_ref[N_EXPERTS + e]
            tails.append((n, pltpu.make_async_copy(zeros.at[pl.ds(0, n)], xs_ref.at[pl.ds(start, n)], zsem)))
        for n, cp in tails:
            @pl.when(n > 0)
            def _():
                cp.start()
        for n, cp in tails:
            @pl.when(n > 0)
            def _():
                cp.wait()
        first = tail_ref[2 * N_EXPERTS]
        n_unused = tail_ref[2 * N_EXPERTS + 1]

        def unused_block(j):
            return pltpu.make_async_copy(zeros, xs_ref.at[pl.ds((first + j) * MOE_BLOCK, MOE_BLOCK)], zsem)

        @pl.loop(0, n_unused)
        def _(j):
            unused_block(j).start()

        @pl.loop(0, n_unused)
        def _(j):
            unused_block(j).wait()


def _dispatch(xa, xb, ia, ib, cnt, off, base, tails, offv, n_slots):
    tm = TOK_TILE
    ntile_a = xa.shape[0] // tm
    ntile_b = xb.shape[0] // tm
    amap = lambda i, *_: (jnp.minimum(i, ntile_a - 1), 0)
    bmap = lambda i, *_: (jnp.maximum(i - ntile_a, 0), 0)
    return pl.pallas_call(
        functools.partial(_dispatch_kernel, ntile_a),
        grid_spec=pltpu.PrefetchScalarGridSpec(
            num_scalar_prefetch=4,
            grid=(ntile_a + ntile_b,),
            in_specs=[pl.BlockSpec((tm, 1024), amap),
                      pl.BlockSpec((tm, 1024), bmap),
                      pl.BlockSpec((tm, TOP_K), amap),
                      pl.BlockSpec((tm, TOP_K), bmap),
                      pl.BlockSpec((1, 1, LANES), lambda i, *_: (i, 0, 0))],
            out_specs=pl.BlockSpec(memory_space=pl.ANY),
            scratch_shapes=[pltpu.VMEM((2, STAGE_ROWS) + ROW_TILE, F32),
                            pltpu.VMEM((MOE_BLOCK,) + ROW_TILE, F32),
                            pltpu.SemaphoreType.DMA((2,)),
                            pltpu.SemaphoreType.DMA]),
        out_shape=jax.ShapeDtypeStruct((n_slots,) + ROW_TILE, F32),
        compiler_params=_cparams(("arbitrary",)),
        name="moe_dispatch",
    )(cnt, off, base, tails, xa, xb, ia, ib, offv)


MOE_HALF = MOE_BLOCK // 2


def _experts_kernel(be_ref, rows_ref, next_ref, slot_ref, xs_ref, wgu_hbm, bgu_ref, wd_hbm, bd_ref, y_ref,
                    wgu_f, wd_f, wgu_s, wd_s, sem):
    i = pl.program_id(0)
    rows = rows_ref[i]
    live = rows > 0
    run_start = (i == 0) | (be_ref[i] != be_ref[jnp.maximum(i - 1, 0)])

    def fetch(expert, slot):
        return (pltpu.make_async_copy(wgu_hbm.at[expert], wgu_f.at[slot], sem.at[0, slot]),
                pltpu.make_async_copy(wd_hbm.at[expert], wd_f.at[slot], sem.at[1, slot]))

    for slot in range(2):
        @pl.when(live & run_start & (slot_ref[i] == slot))
        def _():
            @pl.when(i == 0)
            def _():
                for cp in fetch(be_ref[i], slot):
                    cp.start()
            for cp in fetch(be_ref[i], slot):
                cp.wait()

            @pl.when(next_ref[i] >= 0)
            def _():
                for cp in fetch(next_ref[i], 1 - slot):
                    cp.start()
            wgu_s[...] = wgu_f[slot].astype(BF16)
            wd_s[...] = wd_f[slot].astype(BF16)

    def ffn(x):
        h = jnp.dot(x.astype(BF16), wgu_s[...], preferred_element_type=F32) + bgu_ref[0]
        gate = jnp.minimum(h[:, :D_FF], SWIGLU_LIMIT)
        up = jnp.clip(h[:, D_FF:], -SWIGLU_LIMIT, SWIGLU_LIMIT)
        act = gate * jax.nn.sigmoid(SWIGLU_ALPHA * gate) * (up + 1.0)
        return jnp.dot(act.astype(BF16), wd_s[...], preferred_element_type=F32) + bd_ref[0]

    @pl.when(rows > MOE_HALF)
    def _():
        _rows_to_tiles(y_ref, (), ffn(_tiles_to_rows(xs_ref, ())))

    @pl.when(live & (rows <= MOE_HALF))
    def _():
        y_half = ffn(_tiles_to_rows(xs_ref, (), slice(0, MOE_HALF)))
        _rows_to_tiles(y_ref, (), jnp.concatenate([y_half, jnp.zeros((MOE_HALF, D_MODEL), F32)], axis=0))

    @pl.when(rows == 0)
    def _():
        y_ref[...] = jnp.zeros(y_ref.shape, F32)


def _experts(block_expert, block_rows, next_expert, slot, xs, wgu, bgu, wd, bd):
    nb_max = xs.shape[0] // MOE_BLOCK
    blk = lambda i, be, rows, *_: (jnp.where(rows[i] > 0, i, 0), 0, 0)
    wmap = lambda i, be, *_: (be[i], 0, 0)
    return pl.pallas_call(
        _experts_kernel,
        grid_spec=pltpu.PrefetchScalarGridSpec(
            num_scalar_prefetch=4,
            grid=(nb_max,),
            in_specs=[pl.BlockSpec((MOE_BLOCK,) + ROW_TILE, blk),
                      pl.BlockSpec(memory_space=pl.ANY),
                      pl.BlockSpec((1, 1, 2 * D_FF), wmap),
                      pl.BlockSpec(memory_space=pl.ANY),
                      pl.BlockSpec((1, 1, D_MODEL), wmap)],
            out_specs=pl.BlockSpec((MOE_BLOCK,) + ROW_TILE, lambda i, *_: (i, 0, 0)),
            scratch_shapes=[pltpu.VMEM((2, D_MODEL, 2 * D_FF), F32),
                            pltpu.VMEM((2, D_FF, D_MODEL), F32),
                            pltpu.VMEM((D_MODEL, 2 * D_FF), BF16),
                            pltpu.VMEM((D_FF, D_MODEL), BF16),
                            pltpu.SemaphoreType.DMA((2, 2))]),
        out_shape=jax.ShapeDtypeStruct(xs.shape, F32),
        compiler_params=_cparams(("arbitrary",)),
        name="moe_experts",
    )(block_expert, block_rows, next_expert, slot, xs, wgu, bgu, wd, bd)


def _combine_kernel(cnt_ref, off_ref, base_ref, x1_ref, idx_ref, gate_ref, offv_ref, gf_ref, y_ref,
                    o_ref, stage, sem):
    i = pl.program_id(0)
    last = pl.num_programs(0) - 1

    def runs(tile, slot):
        out = []
        for e in range(N_EXPERTS):
            n = cnt_ref[tile * N_EXPERTS + e]
            off = off_ref[tile * N_EXPERTS + e]
            base = base_ref[tile * N_EXPERTS + e]
            out.append((n, pltpu.make_async_copy(y_ref.at[pl.ds(base, n)], stage.at[slot, pl.ds(off, n)],
                                                 sem.at[slot])))
        return out

    def start_runs(tile, slot):
        for n, cp in runs(tile, slot):
            @pl.when(n > 0)
            def _():
                cp.start()

    @pl.when(i == 0)
    def _():
        start_runs(0, 0)

    for slot in range(2):
        @pl.when(((i & 1) != slot) & (i < last))
        def _():
            start_runs(i + 1, slot)

    rows = _stage_rows(idx_ref[...], offv_ref[0])
    tm = idx_ref.shape[0]
    lane = lax.broadcasted_iota(I32, (tm, STAGE_ROWS), 1)
    gates = gate_ref[...]
    gmat = jnp.zeros((tm, STAGE_ROWS), F32)
    for k in range(TOP_K):
        gmat = gmat + jnp.where(lane == rows[k].astype(I32), gates[:, k:k + 1], 0.0)
    gmat = gmat.astype(BF16)
    for slot in range(2):
        @pl.when((i & 1) == slot)
        def _():
            pltpu.make_async_copy(y_ref.at[pl.ds(0, STAGE_ROWS)], stage.at[slot], sem.at[slot]).wait()
            moe = jnp.dot(gmat, _tiles_to_rows(stage, (slot,)).astype(BF16), preferred_element_type=F32)
            o_ref[...] = _rms(x1_ref[...] + moe, gf_ref[...])


def _combine(x1, idx, gates, cnt, off, base, offv, gf, y):
    t = x1.shape[0]
    tm = TOK_TILE
    return pl.pallas_call(
        _combine_kernel,
        grid_spec=pltpu.PrefetchScalarGridSpec(
            num_scalar_prefetch=3,
            grid=(t // tm,),
            in_specs=[pl.BlockSpec((tm, 1024), lambda i, *_: (i, 0)),
                      pl.BlockSpec((tm, TOP_K), lambda i, *_: (i, 0)),
                      pl.BlockSpec((tm, TOP_K), lambda i, *_: (i, 0)),
                      pl.BlockSpec((1, 1, LANES), lambda i, *_: (i, 0, 0)),
                      pl.BlockSpec((1, 1024), lambda i, *_: (0, 0)),
                      pl.BlockSpec(memory_space=pl.ANY)],
            out_specs=pl.BlockSpec((tm, 1024), lambda i, *_: (i, 0)),
            scratch_shapes=[pltpu.VMEM((2, STAGE_ROWS) + ROW_TILE, F32),
                            pltpu.SemaphoreType.DMA((2,))]),
        out_shape=jax.ShapeDtypeStruct((t, 1024), F32),
        compiler_params=_cparams(("arbitrary",)),
        name="moe_combine",
    )(cnt, off, base, x1, idx, gates, offv, gf, y)


def _prep_w_in(w):
    c_if, c_qa, c_kv, c_g = 4096, 4104, 5128, 5640
    wm = w[:, :c_if].astype(BF16)
    wif = jnp.pad(w[:, c_if:c_qa], ((0, 0), (0, LANES - 2 * M_HEADS))).astype(BF16)
    wqa = (w[:, c_qa:c_kv] * (A_HD ** -0.5)).astype(BF16)
    wkv = w[:, c_kv:c_g].astype(BF16)
    wg = w[:, c_g:].astype(BF16)
    return wm, wg, wqa, wkv, wif


def _rope_tables(pos):
    half = ROT_DIM // 2
    inv = ROPE_THETA ** (-(jnp.arange(half, dtype=F32) * 2.0 / ROT_DIM))
    ang = pos.astype(F32)[:, None] * inv[None, :]
    cos, sin = jnp.cos(ang), jnp.sin(ang)
    ones = jnp.ones((pos.shape[0], A_HD - ROT_DIM), F32)
    cos64 = jnp.concatenate([cos, cos, ones], axis=1)
    sin64 = jnp.concatenate([-sin, sin, 0.0 * ones], axis=1)
    return jnp.tile(cos64, (1, 2)), jnp.tile(sin64, (1, 2))


def _round_up(x, m):
    return (x + m - 1) // m * m


def kernel(x_prompt, x_sample, cache_k, cache_v, state_c, state_n, state_m, norm1, w_in, b_igate, b_fgate,
           mlstm_norm, w_proj_m, w_proj_a, attn_sinks, w_out, norm2, w_router, b_router, w_gate_up, b_gate_up,
           w_down, b_down, norm_f):
    bp, sp, _ = x_prompt.shape
    bs, ts, _ = x_sample.shape
    past_len = 16384
    l = 0
    w1 = _prep_w_in(w_in[l])
    g1 = norm1[l][None, :]
    g2 = norm2[l][None, :]
    gf = norm_f[None, :]
    gate_bias = jnp.pad(jnp.concatenate([b_igate[l], b_fgate[l]]), (0, LANES - 2 * M_HEADS))[None, :]
    mn = mlstm_norm[l][None, :]
    wpm = w_proj_m[l].astype(BF16)
    wpa = w_proj_a[l].reshape(A_KV_HEADS, A_GROUP, A_HD, D_MODEL).transpose(1, 0, 2, 3).reshape(1024, D_MODEL).astype(BF16)
    wo = w_out[l].astype(BF16)
    wr_f = jnp.pad(w_router[l], ((0, 0), (0, LANES - N_EXPERTS)))
    wr_hi = wr_f.astype(BF16)
    wr = jnp.concatenate([wr_hi, (wr_f - wr_hi.astype(F32)).astype(BF16)], axis=1)
    br = jnp.pad(b_router[l], (0, LANES - N_EXPERTS))[None, :]
    wgu = w_gate_up[l]
    bgu = b_gate_up[l][:, None, :]
    wd = w_down[l]
    bd = b_down[l][:, None, :]
    sinks = attn_sinks[l]

    def mixer(x, nb, seq, pos0, L, c0, n0, m0, k_buf, v_buf, tm):
        t = nb * seq
        xf = x.reshape(t, D_MODEL)
        cos_t, sin_t = _rope_tables(pos0 + jnp.arange(seq))
        if seq < tm:
            cos_t, sin_t = jnp.tile(cos_t, (tm // seq, 1)), jnp.tile(sin_t, (tm // seq, 1))
        act = BF16 if L % 16 == 0 else F32
        qkv, og, qa, ka, va, gates = _inproj(xf, g1, w1, cos_t, sin_t, tm, max(seq // tm, 1), act)
        m0p = jnp.pad(m0, ((0, 0), (0, LANES - M_HEADS)))[:, None, :]
        hm, c, n, m = _mlstm(qkv, gates, og, c0, n0, m0p, gate_bias, mn, nb, L)
        if k_buf is None:
            ha = _swa_prompt(qa, ka, va, sinks, nb)
            k_keep = ka.reshape(nb, seq, A_KV)[:, seq - WINDOW:].reshape(nb, WINDOW, A_KV_HEADS, A_HD)
            v_keep = va.reshape(nb, seq, A_KV)[:, seq - WINDOW:].reshape(nb, WINDOW, A_KV_HEADS, A_HD)
        else:
            ha, k_keep, v_keep = _swa_sample(qa, ka, va, k_buf.reshape(nb, WINDOW, A_KV),
                                             v_buf.reshape(nb, WINDOW, A_KV), sinks, seq)
            k_keep = k_keep.reshape(nb, WINDOW, A_KV_HEADS, A_HD)
            v_keep = v_keep.reshape(nb, WINDOW, A_KV_HEADS, A_HD)
        x1, xn, idx, gate, cnt = _merge(xf, hm, ha, og, wpm, wpa, wo, g2, wr, br)
        return (x1, xn, idx, gate, cnt[:, :, 0]), (k_keep, v_keep, c, n, m[:, 0, :M_HEADS])

    zc = jnp.zeros((bp, M_HEADS, M_DV, M_DK), F32)
    zn = jnp.zeros((bp, M_HEADS, M_DK), F32)
    zm = jnp.zeros((bp, M_HEADS), F32)
    rp, sp_out = mixer(x_prompt, bp, sp, 0, M_CHUNK, zc, zn, zm, None, None, 512)
    rs, ss_out = mixer(x_sample, bs, ts, past_len, ts, state_c[l], state_n[l], state_m[l],
                       cache_k[l], cache_v[l], 256)

    cnt = jnp.concatenate([rp[4], rs[4]], axis=0)
    ntile_p = rp[4].shape[0]
    off = jnp.cumsum(cnt, axis=1) - cnt
    per_expert = jnp.sum(cnt, axis=0)
    padded = _round_up(per_expert, MOE_BLOCK)
    padded_end = jnp.cumsum(padded)
    expert_start = padded_end - padded
    base = expert_start[None, :] + jnp.cumsum(cnt, axis=0) - cnt
    n_tok = bp * sp + bs * ts
    n_slots = _round_up(n_tok * TOP_K, MOE_BLOCK) + N_EXPERTS * MOE_BLOCK
    nb_max = n_slots // MOE_BLOCK
    block_row = jnp.arange(nb_max, dtype=I32) * MOE_BLOCK
    block_expert = jnp.minimum(jnp.sum(block_row[:, None] >= padded_end[None, :], axis=1), N_EXPERTS - 1).astype(I32)
    n_blocks = (padded_end[-1] // MOE_BLOCK).astype(I32)[None]
    eid = jnp.arange(N_EXPERTS, dtype=I32)
    later_used = (eid[None, :] > eid[:, None]) & (padded[None, :] > 0)
    next_used = jnp.min(jnp.where(later_used, eid[None, :], N_EXPERTS), axis=1)
    next_used = jnp.where(next_used < N_EXPERTS, next_used, -1).astype(I32)
    run_parity = ((jnp.cumsum(padded > 0) - 1) & 1).astype(I32)
    of_block = lambda table: jnp.sum(jnp.where(block_expert[:, None] == eid[None, :], table[None, :], 0),
                                     axis=1).astype(I32)
    block_rows = jnp.clip(of_block(per_expert) - (block_row - of_block(expert_start)), 0, MOE_BLOCK).astype(I32)
    tails = jnp.concatenate([expert_start + per_expert, padded - per_expert,
                             n_blocks, nb_max - n_blocks]).astype(I32)
    flat = lambda a: a.reshape(-1).astype(I32)
    offv = jnp.pad(off, ((0, 0), (0, LANES - N_EXPERTS))).astype(F32)[:, None, :]

    xs = _dispatch(rp[1], rs[1], rp[2], rs[2], flat(cnt), flat(off), flat(base), tails, offv, n_slots)
    y = _experts(block_expert, block_rows, of_block(next_used), of_block(run_parity), xs, wgu, bgu, wd, bd)
    y_p = _combine(rp[0], rp[2], rp[3], flat(cnt[:ntile_p]), flat(off[:ntile_p]), flat(base[:ntile_p]),
                   offv[:ntile_p], gf, y)
    y_s = _combine(rs[0], rs[2], rs[3], flat(cnt[ntile_p:]), flat(off[ntile_p:]), flat(base[ntile_p:]),
                   offv[ntile_p:], gf, y)

    kp, vp, cp_, np_, mp = sp_out
    ks, vs, cs, ns, ms = ss_out
    return (y_p.reshape(bp, sp, D_MODEL), y_s.reshape(bs, ts, D_MODEL),
            kp[None], vp[None], cp_[None], np_[None], mp[None],
            ks[None], vs[None], cs[None], ns[None], ms[None])
```

```python
import functools

import jax
import jax.numpy as jnp
from jax import lax
from jax.experimental import pallas as pl
from jax.experimental.pallas import tpu as pltpu

F32 = jnp.float32
BF16 = jnp.bfloat16
I32 = jnp.int32

D_MODEL = 1024
M_HEADS = 4
M_DK = 256
M_DV = 256
M_CHUNK = 128
A_Q_HEADS = 16
A_KV_HEADS = 4
A_GROUP = 4
A_HD = 64
A_KV = A_KV_HEADS * A_HD
WINDOW = 128
A_BLOCK = 128
ROT_DIM = 16
ROPE_THETA = 500000.0
N_EXPERTS = 32
TOP_K = 4
D_FF = 1024
SWIGLU_LIMIT = 7.0
SWIGLU_ALPHA = 1.702
NORM_EPS = 1e-6

LANES = 128
SUBLANES = 8
VMEM_LIMIT = 56 * 1024 * 1024

MERGE_TILE = 512
TOK_TILE = 256
STAGE_ROWS = 4 * TOK_TILE
ROW_TILE = (8, 128)
ROW_DTYPE = F32
MOE_BLOCK = 512


def _cparams(sem):
    return pltpu.CompilerParams(dimension_semantics=sem, vmem_limit_bytes=VMEM_LIMIT)


def _rms(x, g):
    return x * lax.rsqrt(jnp.mean(x * x, axis=-1, keepdims=True) + NORM_EPS) * g


def _rope(x, cos_t, sin_t, first_half):
    n = x.shape[1]
    fwd = pltpu.roll(x, n - ROT_DIM // 2, axis=1)
    bwd = pltpu.roll(x, ROT_DIM // 2, axis=1)
    outs = []
    for b in range(n // LANES):
        sl = slice(b * LANES, (b + 1) * LANES)
        partner = jnp.where(first_half, fwd[:, sl], bwd[:, sl])
        outs.append(x[:, sl] * cos_t + partner * sin_t)
    return jnp.concatenate(outs, axis=1)


def _inproj_kernel(x_ref, g1_ref, wm_ref, wg_ref, wqa_ref, wkv_ref, wif_ref, cos_ref, sin_ref,
                   qkv_ref, og_ref, qa_ref, ka_ref, va_ref, gate_ref):
    x = x_ref[...]
    xn = _rms(x, g1_ref[...]).astype(BF16)
    act = qkv_ref.dtype

    def proj(w_ref, c0, n):
        return jnp.dot(xn, w_ref[:, c0:c0 + n], preferred_element_type=F32)

    qkv_ref[:, 0:1024] = proj(wm_ref, 0, 1024).astype(act)
    qkv_ref[:, 1024:2048] = (proj(wm_ref, 1024, 1024) * (M_DK ** -0.5)).astype(act)
    qkv_ref[:, 2048:3072] = proj(wm_ref, 2048, 1024).astype(act)
    og_ref[:, 0:1024] = jax.nn.sigmoid(proj(wm_ref, 3072, 1024)).astype(act)
    og_ref[:, 1024:2048] = jax.nn.sigmoid(proj(wg_ref, 0, 1024)).astype(act)
    og_ref[:, 2048:3072] = jax.nn.sigmoid(proj(wg_ref, 1024, 1024)).astype(act)
    cos_t = cos_ref[...]
    sin_t = sin_ref[...]
    lane = lax.broadcasted_iota(I32, (1, LANES), 1)
    first_half = (lane & (ROT_DIM - 1)) < (ROT_DIM // 2)
    qa_ref[...] = _rope(proj(wqa_ref, 0, 1024), cos_t, sin_t, first_half).astype(act)
    ka_ref[...] = _rope(proj(wkv_ref, 0, A_KV), cos_t, sin_t, first_half)
    va_ref[...] = proj(wkv_ref, A_KV, A_KV)
    gate_ref[...] = proj(wif_ref, 0, LANES)


def _inproj(x, g1, weights, cos_t, sin_t, tm, rope_blocks, act):
    t = x.shape[0]
    tok = lambda n, dt: jax.ShapeDtypeStruct((t, n), dt)
    row = lambda n: pl.BlockSpec((tm, n), lambda i: (i, 0))
    resident = lambda w: pl.BlockSpec(w.shape, lambda i: (0, 0), pipeline_mode=pl.Buffered(1))
    return pl.pallas_call(
        _inproj_kernel,
        grid=(t // tm,),
        in_specs=[row(D_MODEL), pl.BlockSpec((1, D_MODEL), lambda i: (0, 0))]
                 + [resident(w) for w in weights]
                 + [pl.BlockSpec((tm, LANES), lambda i: (i % rope_blocks, 0)),
                    pl.BlockSpec((tm, LANES), lambda i: (i % rope_blocks, 0))],
        out_specs=[row(3072), row(3072), row(1024), row(A_KV), row(A_KV), row(LANES)],
        out_shape=[tok(3072, act), tok(3072, act), tok(1024, act), tok(A_KV, F32), tok(A_KV, F32),
                   tok(LANES, F32)],
        compiler_params=_cparams(("arbitrary",)),
        name="inproj",
    )(x, g1, *weights, cos_t, sin_t)


def _mlstm_gates(gc, bias, m_prev, L):
    z = gc + bias
    lf = jnp.minimum(z, 0.0) - jnp.log1p(jnp.exp(-jnp.abs(z)))
    row = lax.broadcasted_iota(I32, (L, LANES), 0)
    fc = lf
    sh = 1
    while sh < L:
        fc = fc + jnp.where(row >= sh, pltpu.roll(fc, sh, axis=0), 0.0)
        sh *= 2
    fcum = pltpu.roll(fc, LANES - M_HEADS, axis=1)
    a = z - fcum
    cmx = a
    sh = 1
    while sh < L:
        cmx = jnp.maximum(cmx, jnp.where(row >= sh, pltpu.roll(cmx, sh, axis=0), -jnp.inf))
        sh *= 2
    mx = jnp.maximum(m_prev, cmx)
    inter = jnp.exp(m_prev - mx)
    f_end = fcum[L - 1:L]
    m_end = f_end + mx[L - 1:L]
    decay = jnp.exp(f_end + m_prev - m_end)
    wsrc = jnp.exp(a + (f_end - m_end))
    return a, mx, inter, wsrc, decay, m_end


NT_DIMS = (((1,), (1,)), ((), ()))
TN_DIMS = (((0,), (0,)), ((), ()))


def _mlstm_decay(h, L, gates):
    a, mx = gates[0], gates[1]
    t_i = lax.broadcasted_iota(I32, (L, L), 0)
    s_i = lax.broadcasted_iota(I32, (L, L), 1)
    a_row = jnp.sum(jnp.where(t_i == s_i, a[:, h:h + 1], 0.0), axis=0, keepdims=True)
    return jnp.exp(jnp.where(s_i <= t_i, a_row - mx[:, h:h + 1], -jnp.inf))


def _mlstm_output(h, q, og, mn, gates, qk_sum, pv, cq, n_h):
    inter_col = gates[2][:, h:h + 1]
    num = inter_col * cq + pv
    den = inter_col * jnp.sum(q.astype(F32) * n_h, axis=1, keepdims=True) + qk_sum
    hh = num / jnp.maximum(jnp.abs(den), 1.0)
    hn = hh * lax.rsqrt(jnp.mean(hh * hh, axis=1, keepdims=True) + NORM_EPS)
    return hn * mn * og


def _mlstm_kernel(L, G, CH, carry, qkv_ref, gate_ref, og_ref, c0_ref, n0_ref, m0_ref, bias_ref, mn_ref,
                  h_ref, c_ref, n_ref, m_ref, *scratch):
    if carry:
        c_in, n_in, m_in = scratch
        ci = pl.program_id(1)

        @pl.when(ci == 0)
        def _():
            c_in[...] = c0_ref[...]
            n_in[...] = n0_ref[...]
            m_in[...] = m0_ref[...]
        c_out, n_out, m_out = scratch
    else:
        c_in, n_in, m_in = c0_ref, n0_ref, m0_ref
        c_out, n_out, m_out = c_ref, n_ref, m_ref

    def qkv(j, cc, h, part):
        cols = slice(part * 1024 + h * M_DK, part * 1024 + (h + 1) * M_DK)
        return qkv_ref[j, cc * L:(cc + 1) * L, cols].astype(BF16)

    units = [(j, cc, h) for cc in range(CH) for j in range(G) for h in range(M_HEADS)]
    gates = {}
    for j in range(G):
        m_cur = m_in[j]
        for cc in range(CH):
            gates[j, cc] = _mlstm_gates(gate_ref[j, cc * L:(cc + 1) * L], bias_ref[...], m_cur, L)
            m_cur = gates[j, cc][5]
        m_out[j] = m_cur
    scores = {u: lax.dot_general(qkv(*u, 0), qkv(*u, 1), NT_DIMS, preferred_element_type=F32) for u in units}
    qks = {(j, cc, h): scores[j, cc, h] * _mlstm_decay(h, L, gates[j, cc]) for j, cc, h in units}
    pvs = {u: jnp.dot(qks[u].astype(BF16), qkv(*u, 2), preferred_element_type=F32) for u in units}
    kws = {(j, cc, h): gates[j, cc][3][:, h:h + 1] * qkv(j, cc, h, 1).astype(F32) for j, cc, h in units}
    c_cur = {(j, h): c_in[j, h] for j in range(G) for h in range(M_HEADS)}
    n_cur = {(j, h): n_in[j, h:h + 1, :] for j in range(G) for h in range(M_HEADS)}
    for cc in range(CH):
        now = [u for u in units if u[1] == cc]
        cqs = {(j, h): lax.dot_general(qkv(j, cc, h, 0), c_cur[j, h].astype(BF16), NT_DIMS,
                                       preferred_element_type=F32) for j, _, h in now}
        for j, _, h in now:
            sl = slice(h * M_DV, (h + 1) * M_DV)
            rows = slice(cc * L, (cc + 1) * L)
            out = _mlstm_output(h, qkv(j, cc, h, 0), og_ref[j, rows, sl].astype(F32), mn_ref[:, sl], gates[j, cc],
                                jnp.sum(qks[j, cc, h], axis=1, keepdims=True), pvs[j, cc, h], cqs[j, h],
                                n_cur[j, h])
            h_ref[j, rows, sl] = out.astype(h_ref.dtype)
        for j, _, h in now:
            dec = gates[j, cc][4][:, h:h + 1]
            c_cur[j, h] = dec * c_cur[j, h] + lax.dot_general(qkv(j, cc, h, 2), kws[j, cc, h].astype(BF16),
                                                              TN_DIMS, preferred_element_type=F32)
            n_cur[j, h] = dec * n_cur[j, h] + jnp.sum(kws[j, cc, h], axis=0, keepdims=True)
    for j in range(G):
        for h in range(M_HEADS):
            c_out[j, h] = c_cur[j, h]
            n_out[j, h:h + 1, :] = n_cur[j, h]

    if carry:
        @pl.when(ci == pl.num_programs(1) - 1)
        def _():
            c_ref[...] = c_in[...]
            n_ref[...] = n_in[...]
            m_ref[...] = m_in[...]


MLSTM_GROUP_CARRY = 2
MLSTM_CHUNKS_CARRY = 2
MLSTM_GROUP_SINGLE = 4


def _mlstm(qkv, gates, og, c0, n0, m0, gate_bias, mnorm, nb, L):
    t = qkv.shape[0]
    nc = t // (nb * L)
    carry = nc > 1
    G = MLSTM_GROUP_CARRY if carry else MLSTM_GROUP_SINGLE
    CH = MLSTM_CHUNKS_CARRY if carry else 1
    qkv, gates, og = (a.reshape(nb, nc * L, a.shape[1]) for a in (qkv, gates, og))
    tokmap = lambda b, c: (b, c, 0)
    const2 = lambda b, c: (0, 0)
    state_specs = [pl.BlockSpec((G, M_HEADS, M_DV, M_DK), lambda b, c: (b, 0, 0, 0)),
                   pl.BlockSpec((G, M_HEADS, M_DK), lambda b, c: (b, 0, 0)),
                   pl.BlockSpec((G, 1, LANES), lambda b, c: (b, 0, 0))]
    scratch = [pltpu.VMEM((G, M_HEADS, M_DV, M_DK), F32),
               pltpu.VMEM((G, M_HEADS, M_DK), F32),
               pltpu.VMEM((G, 1, LANES), F32)] if carry else []
    h, c, n, m = pl.pallas_call(
        functools.partial(_mlstm_kernel, L, G, CH, carry),
        grid=(nb // G, nc // CH),
        in_specs=[pl.BlockSpec((G, CH * L, 3072), tokmap),
                  pl.BlockSpec((G, CH * L, LANES), tokmap),
                  pl.BlockSpec((G, CH * L, 1024), tokmap)]
                 + state_specs
                 + [pl.BlockSpec((1, LANES), const2),
                    pl.BlockSpec((1, 1024), const2)],
        out_specs=[pl.BlockSpec((G, CH * L, 1024), tokmap)] + state_specs,
        out_shape=[jax.ShapeDtypeStruct((nb, nc * L, 1024), qkv.dtype),
                   jax.ShapeDtypeStruct((nb, M_HEADS, M_DV, M_DK), F32),
                   jax.ShapeDtypeStruct((nb, M_HEADS, M_DK), F32),
                   jax.ShapeDtypeStruct((nb, 1, LANES), F32)],
        scratch_shapes=scratch,
        compiler_params=_cparams(("arbitrary", "arbitrary")),
        name="mlstm_L%d" % L,
    )(qkv, gates, og, c0, n0, m0, gate_bias, mnorm)
    return h.reshape(t, 1024), c, n, m


def _swa_bias(m, has_prev):
    nkeys = 2 * A_BLOCK
    t_i = lax.broadcasted_iota(I32, (A_GROUP * m, nkeys), 0) & (m - 1)
    s_i = lax.broadcasted_iota(I32, (A_GROUP * m, nkeys), 1)
    diff = t_i + A_BLOCK - s_i
    mask = (diff >= 0) & (diff <= WINDOW)
    if not has_prev:
        mask = mask & (s_i >= A_BLOCK)
    return jnp.where(mask, 0.0, -jnp.inf).astype(F32)


def _swa_group_ones():
    nkeys = 2 * A_BLOCK
    r = lax.broadcasted_iota(I32, (A_KV_HEADS * nkeys, A_KV), 0) >> (nkeys.bit_length() - 1)
    c = lax.broadcasted_iota(I32, (A_KV_HEADS * nkeys, A_KV), 1) >> (A_HD.bit_length() - 1)
    return (r == c).astype(BF16)


def _swa_core(q, k_prev, v_prev, k_cur, v_cur, bias, group_ones, sink_ref):
    m = q.shape[0]
    q = q.astype(BF16)
    qst = jnp.concatenate([q[:, j * A_KV:(j + 1) * A_KV] for j in range(A_GROUP)], axis=0)
    k_all = jnp.concatenate([k_prev, k_cur], axis=0)
    v_all = jnp.concatenate([v_prev, v_cur], axis=0)
    lane = lax.broadcasted_iota(I32, (1, A_KV), 1)
    kbd, vbd = [], []
    for g in range(A_KV_HEADS):
        in_g = (lane >= g * A_HD) & (lane < (g + 1) * A_HD)
        kbd.append(jnp.where(in_g, k_all, 0.0).astype(BF16))
        vbd.append(jnp.where(in_g, v_all, 0.0).astype(BF16))
    kbd = jnp.concatenate(kbd, axis=0)
    vbd = jnp.concatenate(vbd, axis=0)
    s = lax.dot_general(qst, kbd, (((1,), (1,)), ((), ())), preferred_element_type=F32)
    nkeys = 2 * A_BLOCK
    ps, sink_terms = [], []
    for g in range(A_KV_HEADS):
        sg = s[:, g * nkeys:(g + 1) * nkeys] + bias
        sink = jnp.concatenate(
            [jnp.full((m, LANES), sink_ref[A_GROUP * g + j], F32) for j in range(A_GROUP)], axis=0)
        mx = jnp.maximum(jnp.broadcast_to(jnp.max(sg, axis=1, keepdims=True), (A_GROUP * m, LANES)), sink)
        ps += [jnp.exp(sg[:, :LANES] - mx).astype(BF16), jnp.exp(sg[:, LANES:] - mx).astype(BF16)]
        sink_terms.append(jnp.exp(sink - mx))
    p = jnp.concatenate(ps, axis=1)
    o = jnp.dot(p, vbd, preferred_element_type=F32)
    den = jnp.dot(p, group_ones, preferred_element_type=F32)
    low = lax.broadcasted_iota(I32, (1, LANES), 1) < A_HD
    den = den + jnp.concatenate([jnp.where(low, sink_terms[0], sink_terms[1]),
                                 jnp.where(low, sink_terms[2], sink_terms[3])], axis=1)
    o = o / den
    return jnp.concatenate([o[j * m:(j + 1) * m] for j in range(A_GROUP)], axis=1)


SWA_STEP_BLOCKS = 4
SWA_STEP = SWA_STEP_BLOCKS * A_BLOCK


def _swa_prompt_kernel(sink_ref, q_ref, kp_ref, kc_ref, vp_ref, vc_ref, o_ref, bias_s, bias0_s, ones_s):
    first = (pl.program_id(0) == 0) & (pl.program_id(1) == 0)

    @pl.when(first)
    def _():
        bias_s[...] = _swa_bias(A_BLOCK, True)
        bias0_s[...] = _swa_bias(A_BLOCK, False)
        ones_s[...] = _swa_group_ones()

    bias = bias_s[...]
    bias_first = jnp.where(pl.program_id(1) > 0, bias, bias0_s[...])
    ones = ones_s[...]
    k_prev, v_prev = kp_ref[...], vp_ref[...]
    for b in range(SWA_STEP_BLOCKS):
        rows = slice(b * A_BLOCK, (b + 1) * A_BLOCK)
        k_cur, v_cur = kc_ref[rows], vc_ref[rows]
        o_ref[rows] = _swa_core(q_ref[rows], k_prev, v_prev, k_cur, v_cur, bias_first if b == 0 else bias,
                                ones, sink_ref).astype(o_ref.dtype)
        k_prev, v_prev = k_cur, v_cur


def _swa_prompt(qa, ka, va, sinks, nb):
    t = qa.shape[0]
    nstep = t // (nb * SWA_STEP)
    cur = lambda b, i, s: (b * nstep + i, 0)
    prev = lambda b, i, s: (SWA_STEP_BLOCKS * (b * nstep + i) - jnp.minimum(i, 1), 0)
    return pl.pallas_call(
        _swa_prompt_kernel,
        grid_spec=pltpu.PrefetchScalarGridSpec(
            num_scalar_prefetch=1,
            grid=(nb, nstep),
            in_specs=[pl.BlockSpec((SWA_STEP, 1024), cur),
                      pl.BlockSpec((A_BLOCK, A_KV), prev),
                      pl.BlockSpec((SWA_STEP, A_KV), cur),
                      pl.BlockSpec((A_BLOCK, A_KV), prev),
                      pl.BlockSpec((SWA_STEP, A_KV), cur)],
            out_specs=pl.BlockSpec((SWA_STEP, 1024), cur),
            scratch_shapes=[pltpu.VMEM((A_GROUP * A_BLOCK, 2 * A_BLOCK), F32),
                            pltpu.VMEM((A_GROUP * A_BLOCK, 2 * A_BLOCK), F32),
                            pltpu.VMEM((A_KV_HEADS * 2 * A_BLOCK, A_KV), BF16)]),
        out_shape=jax.ShapeDtypeStruct((t, 1024), BF16),
        compiler_params=_cparams(("arbitrary", "arbitrary")),
        name="swa_prompt",
    )(sinks, qa, ka, ka, va, va)


SWA_SAMPLE_GROUP = 8


def _swa_sample_kernel(T, sink_ref, q_ref, kn_ref, vn_ref, kb_ref, vb_ref, o_ref, ko_ref, vo_ref):
    pad = jnp.zeros((A_BLOCK - T, A_KV), F32)
    bias = _swa_bias(T, True)
    ones = _swa_group_ones()
    for j in range(SWA_SAMPLE_GROUP):
        rows = slice(j * T, (j + 1) * T)
        k_new = kn_ref[rows]
        v_new = vn_ref[rows]
        k_buf = kb_ref[j]
        v_buf = vb_ref[j]
        o_ref[rows] = _swa_core(q_ref[rows], k_buf, v_buf, jnp.concatenate([k_new, pad], axis=0),
                                jnp.concatenate([v_new, pad], axis=0), bias, ones, sink_ref).astype(o_ref.dtype)
        ko_ref[j] = jnp.concatenate([k_buf[T:], k_new], axis=0)
        vo_ref[j] = jnp.concatenate([v_buf[T:], v_new], axis=0)


def _swa_sample(qa, ka, va, k_buf, v_buf, sinks, T):
    nb = k_buf.shape[0]
    g = SWA_SAMPLE_GROUP
    tok = lambda b, s: (b, 0)
    buf = lambda b, s: (b, 0, 0)
    return pl.pallas_call(
        functools.partial(_swa_sample_kernel, T),
        grid_spec=pltpu.PrefetchScalarGridSpec(
            num_scalar_prefetch=1,
            grid=(nb // g,),
            in_specs=[pl.BlockSpec((g * T, 1024), tok),
                      pl.BlockSpec((g * T, A_KV), tok),
                      pl.BlockSpec((g * T, A_KV), tok),
                      pl.BlockSpec((g, WINDOW, A_KV), buf),
                      pl.BlockSpec((g, WINDOW, A_KV), buf)],
            out_specs=[pl.BlockSpec((g * T, 1024), tok),
                       pl.BlockSpec((g, WINDOW, A_KV), buf),
                       pl.BlockSpec((g, WINDOW, A_KV), buf)]),
        out_shape=[jax.ShapeDtypeStruct((nb * T, 1024), qa.dtype),
                   jax.ShapeDtypeStruct((nb, WINDOW, A_KV), F32),
                   jax.ShapeDtypeStruct((nb, WINDOW, A_KV), F32)],
        compiler_params=_cparams(("arbitrary",)),
        name="swa_sample",
    )(sinks, qa, ka, va, k_buf, v_buf)


def _merge_kernel(x_ref, hm_ref, ha_ref, gm_ref, ga_ref, wpm_ref, wpa_ref, wo_ref, g2_ref, wr_ref, br_ref,
                  x1_ref, xn_ref, idx_ref, gate_ref, cnt_ref):
    pm = jnp.dot(hm_ref[...].astype(BF16), wpm_ref[...], preferred_element_type=F32)
    pa = jnp.dot(ha_ref[...].astype(BF16), wpa_ref[...], preferred_element_type=F32)
    mixed = gm_ref[...].astype(F32) * pm + ga_ref[...].astype(F32) * pa
    x1 = x_ref[...] + jnp.dot(mixed.astype(BF16), wo_ref[...], preferred_element_type=F32)
    x1_ref[...] = x1
    xn = _rms(x1, g2_ref[...])
    xn_ref[...] = xn.astype(BF16)
    xn_hi = xn.astype(BF16)
    xn_lo = (xn - xn_hi.astype(F32)).astype(BF16)
    part = jnp.dot(xn_hi, wr_ref[...], preferred_element_type=F32)
    logits = (part[:, :LANES] + part[:, LANES:]
              + jnp.dot(xn_lo, wr_ref[:, :LANES], preferred_element_type=F32) + br_ref[...])
    tm = logits.shape[0]
    cur = logits.T[0:N_EXPERTS]
    eid = lax.broadcasted_iota(I32, (N_EXPERTS, tm), 0).astype(F32)
    vals, idxs = [], []
    onehot = jnp.zeros((N_EXPERTS, tm), F32)
    for _ in range(TOP_K):
        mval = jnp.max(cur, axis=0, keepdims=True)
        sel = jnp.min(jnp.where(cur == mval, eid, float(N_EXPERTS)), axis=0, keepdims=True)
        hit = eid == sel
        onehot = onehot + hit.astype(F32)
        cur = jnp.where(hit, -jnp.inf, cur)
        vals.append(mval)
        idxs.append(sel)
    es = [jnp.exp(v - vals[0]) for v in vals]
    tot = es[0] + es[1] + es[2] + es[3]
    row = lax.broadcasted_iota(I32, (SUBLANES, tm), 0)
    res = jnp.zeros((SUBLANES, tm), F32)
    for k in range(TOP_K):
        res = jnp.where(row == k, idxs[k], res)
        res = jnp.where(row == TOP_K + k, es[k] / tot, res)
    res_t = jnp.concatenate([res, jnp.zeros((LANES - SUBLANES, tm), F32)], axis=0).T
    idx_ref[...] = res_t[:, 0:TOP_K].astype(I32)
    gate_ref[...] = res_t[:, TOP_K:2 * TOP_K]
    for j in range(tm // TOK_TILE):
        cnt_ref[j] = jnp.sum(onehot[:, j * TOK_TILE:(j + 1) * TOK_TILE], axis=1, keepdims=True).astype(I32)


def _merge(x, hm, ha, og, wpm, wpa, wo, g2, wr, br):
    t = x.shape[0]
    tm = MERGE_TILE
    row = lambda n: pl.BlockSpec((tm, n), lambda i: (i, 0))
    const = lambda r, c: pl.BlockSpec((r, c), lambda i: (0, 0))
    return pl.pallas_call(
        _merge_kernel,
        grid=(t // tm,),
        in_specs=[row(1024), row(1024), row(1024),
                  pl.BlockSpec((tm, 1024), lambda i: (i, 1)),
                  pl.BlockSpec((tm, 1024), lambda i: (i, 2)),
                  const(1024, 1024), const(1024, 1024), const(1024, 1024), const(1, 1024),
                  const(1024, 2 * LANES), const(1, LANES)],
        out_specs=[row(1024), row(1024), row(TOP_K), row(TOP_K),
                   pl.BlockSpec((tm // TOK_TILE, N_EXPERTS, 1), lambda i: (i, 0, 0))],
        out_shape=[jax.ShapeDtypeStruct((t, 1024), F32),
                   jax.ShapeDtypeStruct((t, 1024), BF16),
                   jax.ShapeDtypeStruct((t, TOP_K), I32),
                   jax.ShapeDtypeStruct((t, TOP_K), F32),
                   jax.ShapeDtypeStruct((t // TOK_TILE, N_EXPERTS, 1), I32)],
        compiler_params=_cparams(("arbitrary",)),
        name="merge_route",
    )(x, hm, ha, og, og, wpm, wpa, wo, g2, wr, br)


def _rows_to_tiles(ref, lead, x):
    chunks = jnp.stack([x[:, s * LANES:(s + 1) * LANES] for s in range(ROW_TILE[0])], axis=0)
    ref[lead] = jnp.swapaxes(chunks, 0, 1).astype(ref.dtype)


def _tiles_to_rows(ref, lead, rows=slice(None)):
    chunks = jnp.swapaxes(ref[lead + (rows,)].astype(F32), 0, 1)
    return jnp.concatenate([chunks[s] for s in range(ROW_TILE[0])], axis=1)


def _stage_rows(idx, off_row):
    tm = idx.shape[0]
    lane = lax.broadcasted_iota(I32, (tm, LANES), 1)
    hits = [lane == idx[:, k:k + 1] for k in range(TOP_K)]
    onehot = sum(h.astype(F32) for h in hits)
    r_i = lax.broadcasted_iota(I32, (tm, tm), 0)
    c_i = lax.broadcasted_iota(I32, (tm, tm), 1)
    before = (c_i < r_i).astype(BF16)
    rank = jnp.dot(before, onehot.astype(BF16), preferred_element_type=F32)
    pos = rank + off_row
    return [jnp.sum(jnp.where(h, pos, 0.0), axis=1, keepdims=True) for h in hits]


def _dispatch_kernel(ntile_a, cnt_ref, off_ref, base_ref, tail_ref, xa_ref, xb_ref, ia_ref, ib_ref, offv_ref,
                     xs_ref, stage, zeros, sem, zsem):
    i = pl.program_id(0)
    from_a = i < ntile_a
    xn = jnp.where(from_a, xa_ref[...], xb_ref[...])
    idx = jnp.where(from_a, ia_ref[...], ib_ref[...])
    rows = _stage_rows(idx, offv_ref[0])
    tm = idx.shape[0]
    lane = lax.broadcasted_iota(I32, (tm, LANES), 1)
    r4 = jnp.zeros((tm, LANES), F32)
    for k in range(TOP_K):
        r4 = jnp.where(lane == k, rows[k], r4)
    r4t = r4.T.astype(I32)
    r_iota = lax.broadcasted_iota(I32, (STAGE_ROWS, tm), 0)
    sel = r_iota == r4t[0:1, :]
    for k in range(1, TOP_K):
        sel = sel | (r_iota == r4t[k:k + 1, :])
    sel = jnp.where(sel, 1.0, 0.0).astype(BF16)
    staged = jnp.dot(sel, xn, preferred_element_type=F32)

    def runs(tile, slot):
        out = []
        for e in range(N_EXPERTS):
            n = cnt_ref[tile * N_EXPERTS + e]
            off = off_ref[tile * N_EXPERTS + e]
            base = base_ref[tile * N_EXPERTS + e]
            out.append((n, pltpu.make_async_copy(stage.at[slot, pl.ds(off, n)], xs_ref.at[pl.ds(base, n)],
                                                 sem.at[slot])))
        return out

    def wait_runs(slot):
        pltpu.make_async_copy(stage.at[slot], xs_ref.at[pl.ds(0, STAGE_ROWS)], sem.at[slot]).wait()

    for slot in range(2):
        @pl.when((i & 1) == slot)
        def _():
            @pl.when(i >= 2)
            def _():
                wait_runs(slot)
            _rows_to_tiles(stage, (slot,), staged)
            for n, cp in runs(i, slot):
                @pl.when(n > 0)
                def _():
                    cp.start()

    last = pl.num_programs(0) - 1

    @pl.when(i == last)
    def _():
        for slot in range(2):
            @pl.when(((last & 1) == slot) | (last >= 1))
            def _():
                wait_runs(slot)
        zeros[...] = jnp.zeros(zeros.shape, zeros.dtype)
        tails = []
        for e in range(N_EXPERTS):
            start = tail_ref[e]
            n = tail_ref[N_EXPERTS + e]
            tails.append((n, pltpu.make_async_copy(zeros.at[pl.ds(0, n)], xs_ref.at[pl.ds(start, n)], zsem)))
        for n, cp in tails:
            @pl.when(n > 0)
            def _():
                cp.start()
        for n, cp in tails:
            @pl.when(n > 0)
            def _():
                cp.wait()
        first = tail_ref[2 * N_EXPERTS]
        n_unused = tail_ref[2 * N_EXPERTS + 1]

        def unused_block(j):
            return pltpu.make_async_copy(zeros, xs_ref.at[pl.ds((first + j) * MOE_BLOCK, MOE_BLOCK)], zsem)

        @pl.loop(0, n_unused)
        def _(j):
            unused_block(j).start()

        @pl.loop(0, n_unused)
        def _(j):
            unused_block(j).wait()


def _dispatch(xa, xb, ia, ib, cnt, off, base, tails, offv, n_slots):
    tm = TOK_TILE
    ntile_a = xa.shape[0] // tm
    ntile_b = xb.shape[0] // tm
    amap = lambda i, *_: (jnp.minimum(i, ntile_a - 1), 0)
    bmap = lambda i, *_: (jnp.maximum(i - ntile_a, 0), 0)
    return pl.pallas_call(
        functools.partial(_dispatch_kernel, ntile_a),
        grid_spec=pltpu.PrefetchScalarGridSpec(
            num_scalar_prefetch=4,
            grid=(ntile_a + ntile_b,),
            in_specs=[pl.BlockSpec((tm, 1024), amap),
                      pl.BlockSpec((tm, 1024), bmap),
                      pl.BlockSpec((tm, TOP_K), amap),
                      pl.BlockSpec((tm, TOP_K), bmap),
                      pl.BlockSpec((1, 1, LANES), lambda i, *_: (i, 0, 0))],
            out_specs=pl.BlockSpec(memory_space=pl.ANY),
            scratch_shapes=[pltpu.VMEM((2, STAGE_ROWS) + ROW_TILE, ROW_DTYPE),
                            pltpu.VMEM((MOE_BLOCK,) + ROW_TILE, ROW_DTYPE),
                            pltpu.SemaphoreType.DMA((2,)),
                            pltpu.SemaphoreType.DMA]),
        out_shape=jax.ShapeDtypeStruct((n_slots,) + ROW_TILE, ROW_DTYPE),
        compiler_params=_cparams(("arbitrary",)),
        name="moe_dispatch",
    )(cnt, off, base, tails, xa, xb, ia, ib, offv)


MOE_HALF = MOE_BLOCK // 2


def _experts_kernel(be_ref, rows_ref, next_ref, slot_ref, xs_ref, wgu_hbm, bgu_ref, wd_hbm, bd_ref, y_ref,
                    wgu_f, wd_f, wgu_s, wd_s, sem):
    i = pl.program_id(0)
    rows = rows_ref[i]
    live = rows > 0
    run_start = (i == 0) | (be_ref[i] != be_ref[jnp.maximum(i - 1, 0)])

    def fetch(expert, slot):
        return (pltpu.make_async_copy(wgu_hbm.at[expert], wgu_f.at[slot], sem.at[0, slot]),
                pltpu.make_async_copy(wd_hbm.at[expert], wd_f.at[slot], sem.at[1, slot]))

    for slot in range(2):
        @pl.when(live & run_start & (slot_ref[i] == slot))
        def _():
            @pl.when(i == 0)
            def _():
                for cp in fetch(be_ref[i], slot):
                    cp.start()
            for cp in fetch(be_ref[i], slot):
                cp.wait()

            @pl.when(next_ref[i] >= 0)
            def _():
                for cp in fetch(next_ref[i], 1 - slot):
                    cp.start()
            wgu_s[...] = wgu_f[slot].astype(BF16)
            wd_s[...] = wd_f[slot].astype(BF16)

    def ffn(x):
        h = jnp.dot(x.astype(BF16), wgu_s[...], preferred_element_type=F32) + bgu_ref[0]
        gate = jnp.minimum(h[:, :D_FF], SWIGLU_LIMIT)
        up = jnp.clip(h[:, D_FF:], -SWIGLU_LIMIT, SWIGLU_LIMIT)
        act = gate * jax.nn.sigmoid(SWIGLU_ALPHA * gate) * (up + 1.0)
        return jnp.dot(act.astype(BF16), wd_s[...], preferred_element_type=F32) + bd_ref[0]

    @pl.when(rows > MOE_HALF)
    def _():
        _rows_to_tiles(y_ref, (), ffn(_tiles_to_rows(xs_ref, ())))

    @pl.when(live & (rows <= MOE_HALF))
    def _():
        y_half = ffn(_tiles_to_rows(xs_ref, (), slice(0, MOE_HALF)))
        _rows_to_tiles(y_ref, (), jnp.concatenate([y_half, jnp.zeros((MOE_HALF, D_MODEL), F32)], axis=0))

    @pl.when(rows == 0)
    def _():
        y_ref[...] = jnp.zeros(y_ref.shape, y_ref.dtype)


def _experts(block_expert, block_rows, next_expert, slot, xs, wgu, bgu, wd, bd):
    nb_max = xs.shape[0] // MOE_BLOCK
    blk = lambda i, be, rows, *_: (jnp.where(rows[i] > 0, i, 0), 0, 0)
    wmap = lambda i, be, *_: (be[i], 0, 0)
    return pl.pallas_call(
        _experts_kernel,
        grid_spec=pltpu.PrefetchScalarGridSpec(
            num_scalar_prefetch=4,
            grid=(nb_max,),
            in_specs=[pl.BlockSpec((MOE_BLOCK,) + ROW_TILE, blk),
                      pl.BlockSpec(memory_space=pl.ANY),
                      pl.BlockSpec((1, 1, 2 * D_FF), wmap),
                      pl.BlockSpec(memory_space=pl.ANY),
                      pl.BlockSpec((1, 1, D_MODEL), wmap)],
            out_specs=pl.BlockSpec((MOE_BLOCK,) + ROW_TILE, lambda i, *_: (i, 0, 0)),
            scratch_shapes=[pltpu.VMEM((2, D_MODEL, 2 * D_FF), F32),
                            pltpu.VMEM((2, D_FF, D_MODEL), F32),
                            pltpu.VMEM((D_MODEL, 2 * D_FF), BF16),
                            pltpu.VMEM((D_FF, D_MODEL), BF16),
                            pltpu.SemaphoreType.DMA((2, 2))]),
        out_shape=jax.ShapeDtypeStruct(xs.shape, ROW_DTYPE),
        compiler_params=_cparams(("arbitrary",)),
        name="moe_experts",
    )(block_expert, block_rows, next_expert, slot, xs, wgu, bgu, wd, bd)


def _combine_kernel(cnt_ref, off_ref, base_ref, x1_ref, idx_ref, gate_ref, offv_ref, gf_ref, y_ref,
                    o_ref, stage, sem):
    i = pl.program_id(0)
    last = pl.num_programs(0) - 1

    def runs(tile, slot):
        out = []
        for e in range(N_EXPERTS):
            n = cnt_ref[tile * N_EXPERTS + e]
            off = off_ref[tile * N_EXPERTS + e]
            base = base_ref[tile * N_EXPERTS + e]
            out.append((n, pltpu.make_async_copy(y_ref.at[pl.ds(base, n)], stage.at[slot, pl.ds(off, n)],
                                                 sem.at[slot])))
        return out

    def start_runs(tile, slot):
        for n, cp in runs(tile, slot):
            @pl.when(n > 0)
            def _():
                cp.start()

    @pl.when(i == 0)
    def _():
        start_runs(0, 0)

    for slot in range(2):
        @pl.when(((i & 1) != slot) & (i < last))
        def _():
            start_runs(i + 1, slot)

    rows = _stage_rows(idx_ref[...], offv_ref[0])
    tm = idx_ref.shape[0]
    lane = lax.broadcasted_iota(I32, (tm, STAGE_ROWS), 1)
    gates = gate_ref[...]
    gmat = jnp.zeros((tm, STAGE_ROWS), F32)
    for k in range(TOP_K):
        gmat = gmat + jnp.where(lane == rows[k].astype(I32), gates[:, k:k + 1], 0.0)
    gmat = gmat.astype(BF16)
    for slot in range(2):
        @pl.when((i & 1) == slot)
        def _():
            pltpu.make_async_copy(y_ref.at[pl.ds(0, STAGE_ROWS)], stage.at[slot], sem.at[slot]).wait()
            moe = jnp.dot(gmat, _tiles_to_rows(stage, (slot,)).astype(BF16), preferred_element_type=F32)
            o_ref[...] = _rms(x1_ref[...] + moe, gf_ref[...])


def _combine(x1, idx, gates, cnt, off, base, offv, gf, y):
    t = x1.shape[0]
    tm = TOK_TILE
    return pl.pallas_call(
        _combine_kernel,
        grid_spec=pltpu.PrefetchScalarGridSpec(
            num_scalar_prefetch=3,
            grid=(t // tm,),
            in_specs=[pl.BlockSpec((tm, 1024), lambda i, *_: (i, 0)),
                      pl.BlockSpec((tm, TOP_K), lambda i, *_: (i, 0)),
                      pl.BlockSpec((tm, TOP_K), lambda i, *_: (i, 0)),
                      pl.BlockSpec((1, 1, LANES), lambda i, *_: (i, 0, 0)),
                      pl.BlockSpec((1, 1024), lambda i, *_: (0, 0)),
                      pl.BlockSpec(memory_space=pl.ANY)],
            out_specs=pl.BlockSpec((tm, 1024), lambda i, *_: (i, 0)),
            scratch_shapes=[pltpu.VMEM((2, STAGE_ROWS) + ROW_TILE, ROW_DTYPE),
                            pltpu.SemaphoreType.DMA((2,))]),
        out_shape=jax.ShapeDtypeStruct((t, 1024), F32),
        compiler_params=_cparams(("arbitrary",)),
        name="moe_combine",
    )(cnt, off, base, x1, idx, gates, offv, gf, y)


def _prep_w_in(w):
    c_if, c_qa, c_kv, c_g = 4096, 4104, 5128, 5640
    wm = w[:, :c_if].astype(BF16)
    wif = jnp.pad(w[:, c_if:c_qa], ((0, 0), (0, LANES - 2 * M_HEADS))).astype(BF16)
    wqa = (w[:, c_qa:c_kv].reshape(D_MODEL, A_KV_HEADS, A_GROUP, A_HD).transpose(0, 2, 1, 3)
           .reshape(D_MODEL, 1024) * (A_HD ** -0.5)).astype(BF16)
    wkv = w[:, c_kv:c_g].astype(BF16)
    wg = w[:, c_g:].astype(BF16)
    return wm, wg, wqa, wkv, wif


def _rope_tables(pos):
    half = ROT_DIM // 2
    inv = ROPE_THETA ** (-(jnp.arange(half, dtype=F32) * 2.0 / ROT_DIM))
    ang = pos.astype(F32)[:, None] * inv[None, :]
    cos, sin = jnp.cos(ang), jnp.sin(ang)
    ones = jnp.ones((pos.shape[0], A_HD - ROT_DIM), F32)
    cos64 = jnp.concatenate([cos, cos, ones], axis=1)
    sin64 = jnp.concatenate([-sin, sin, 0.0 * ones], axis=1)
    return jnp.tile(cos64, (1, 2)), jnp.tile(sin64, (1, 2))


def _round_up(x, m):
    return (x + m - 1) // m * m


def kernel(x_prompt, x_sample, cache_k, cache_v, state_c, state_n, state_m, norm1, w_in, b_igate, b_fgate,
           mlstm_norm, w_proj_m, w_proj_a, attn_sinks, w_out, norm2, w_router, b_router, w_gate_up, b_gate_up,
           w_down, b_down, norm_f):
    bp, sp, _ = x_prompt.shape
    bs, ts, _ = x_sample.shape
    past_len = 16384
    l = 0
    w1 = _prep_w_in(w_in[l])
    g1 = norm1[l][None, :]
    g2 = norm2[l][None, :]
    gf = norm_f[None, :]
    gate_bias = jnp.pad(jnp.concatenate([b_igate[l], b_fgate[l]]), (0, LANES - 2 * M_HEADS))[None, :]
    mn = mlstm_norm[l][None, :]
    wpm = w_proj_m[l].astype(BF16)
    wpa = jnp.concatenate([w_proj_a[l][(A_GROUP * g + j) * A_HD:(A_GROUP * g + j + 1) * A_HD]
                           for j in range(A_GROUP) for g in range(A_KV_HEADS)], axis=0).astype(BF16)
    wo = w_out[l].astype(BF16)
    wr_f = jnp.pad(w_router[l], ((0, 0), (0, LANES - N_EXPERTS)))
    wr_hi = wr_f.astype(BF16)
    wr = jnp.concatenate([wr_hi, (wr_f - wr_hi.astype(F32)).astype(BF16)], axis=1)
    br = jnp.pad(b_router[l], (0, LANES - N_EXPERTS))[None, :]
    wgu = w_gate_up[l]
    bgu = b_gate_up[l][:, None, :]
    wd = w_down[l]
    bd = b_down[l][:, None, :]
    sinks = attn_sinks[l]

    def mixer(x, nb, seq, pos0, L, c0, n0, m0, k_buf, v_buf, tm):
        t = nb * seq
        xf = x.reshape(t, D_MODEL)
        cos_t, sin_t = _rope_tables(pos0 + jnp.arange(seq))
        if seq < tm:
            cos_t, sin_t = jnp.tile(cos_t, (tm // seq, 1)), jnp.tile(sin_t, (tm // seq, 1))
        act = BF16 if L % 16 == 0 else F32
        qkv, og, qa, ka, va, gates = _inproj(xf, g1, w1, cos_t, sin_t, tm, max(seq // tm, 1), act)
        m0p = jnp.pad(m0, ((0, 0), (0, LANES - M_HEADS)))[:, None, :]
        hm, c, n, m = _mlstm(qkv, gates, og, c0, n0, m0p, gate_bias, mn, nb, L)
        if k_buf is None:
            ha = _swa_prompt(qa, ka, va, sinks, nb)
            k_keep = ka.reshape(nb, seq, A_KV)[:, seq - WINDOW:].reshape(nb, WINDOW, A_KV_HEADS, A_HD)
            v_keep = va.reshape(nb, seq, A_KV)[:, seq - WINDOW:].reshape(nb, WINDOW, A_KV_HEADS, A_HD)
        else:
            ha, k_keep, v_keep = _swa_sample(qa, ka, va, k_buf.reshape(nb, WINDOW, A_KV),
                                             v_buf.reshape(nb, WINDOW, A_KV), sinks, seq)
            k_keep = k_keep.reshape(nb, WINDOW, A_KV_HEADS, A_HD)
            v_keep = v_keep.reshape(nb, WINDOW, A_KV_HEADS, A_HD)
        x1, xn, idx, gate, cnt = _merge(xf, hm, ha, og, wpm, wpa, wo, g2, wr, br)
        return (x1, xn, idx, gate, cnt[:, :, 0]), (k_keep, v_keep, c, n, m[:, 0, :M_HEADS])

    zc = jnp.zeros((bp, M_HEADS, M_DV, M_DK), F32)
    zn = jnp.zeros((bp, M_HEADS, M_DK), F32)
    zm = jnp.zeros((bp, M_HEADS), F32)
    rp, sp_out = mixer(x_prompt, bp, sp, 0, M_CHUNK, zc, zn, zm, None, None, 512)
    rs, ss_out = mixer(x_sample, bs, ts, past_len, ts, state_c[l], state_n[l], state_m[l],
                       cache_k[l], cache_v[l], 256)

    cnt = jnp.concatenate([rp[4], rs[4]], axis=0)
    ntile_p = rp[4].shape[0]
    off = jnp.cumsum(cnt, axis=1) - cnt
    per_expert = jnp.sum(cnt, axis=0)
    padded = _round_up(per_expert, MOE_BLOCK)
    padded_end = jnp.cumsum(padded)
    expert_start = padded_end - padded
    base = expert_start[None, :] + jnp.cumsum(cnt, axis=0) - cnt
    n_tok = bp * sp + bs * ts
    n_slots = _round_up(n_tok * TOP_K, MOE_BLOCK) + N_EXPERTS * MOE_BLOCK
    nb_max = n_slots // MOE_BLOCK
    block_row = jnp.arange(nb_max, dtype=I32) * MOE_BLOCK
    block_expert = jnp.minimum(jnp.sum(block_row[:, None] >= padded_end[None, :], axis=1), N_EXPERTS - 1).astype(I32)
    n_blocks = (padded_end[-1] // MOE_BLOCK).astype(I32)[None]
    eid = jnp.arange(N_EXPERTS, dtype=I32)
    later_used = (eid[None, :] > eid[:, None]) & (padded[None, :] > 0)
    next_used = jnp.min(jnp.where(later_used, eid[None, :], N_EXPERTS), axis=1)
    next_used = jnp.where(next_used < N_EXPERTS, next_used, -1).astype(I32)
    run_parity = ((jnp.cumsum(padded > 0) - 1) & 1).astype(I32)
    of_block = lambda table: jnp.sum(jnp.where(block_expert[:, None] == eid[None, :], table[None, :], 0),
                                     axis=1).astype(I32)
    block_rows = jnp.clip(of_block(per_expert) - (block_row - of_block(expert_start)), 0, MOE_BLOCK).astype(I32)
    tails = jnp.concatenate([expert_start + per_expert, padded - per_expert,
                             n_blocks, nb_max - n_blocks]).astype(I32)
    flat = lambda a: a.reshape(-1).astype(I32)
    offv = jnp.pad(off, ((0, 0), (0, LANES - N_EXPERTS))).astype(F32)[:, None, :]

    xs = _dispatch(rp[1], rs[1], rp[2], rs[2], flat(cnt), flat(off), flat(base), tails, offv, n_slots)
    y = _experts(block_expert, block_rows, of_block(next_used), of_block(run_parity), xs, wgu, bgu, wd, bd)
    y_p = _combine(rp[0], rp[2], rp[3], flat(cnt[:ntile_p]), flat(off[:ntile_p]), flat(base[:ntile_p]),
                   offv[:ntile_p], gf, y)
    y_s = _combine(rs[0], rs[2], rs[3], flat(cnt[ntile_p:]), flat(off[ntile_p:]), flat(base[ntile_p:]),
                   offv[ntile_p:], gf, y)

    kp, vp, cp_, np_, mp = sp_out
    ks, vs, cs, ns, ms = ss_out
    return (y_p.reshape(bp, sp, D_MODEL), y_s.reshape(bs, ts, D_MODEL),
            kp[None], vp[None], cp_[None], np_[None], mp[None],
            ks[None], vs[None], cs[None], ns[None], ms[None])
```

```python
import functools

import jax
import jax.numpy as jnp
from jax import lax
from jax.experimental import pallas as pl
from jax.experimental.pallas import tpu as pltpu

F32 = jnp.float32
BF16 = jnp.bfloat16
I32 = jnp.int32

D_MODEL = 1024
M_HEADS = 4
M_DK = 256
M_DV = 256
M_CHUNK = 128
A_Q_HEADS = 16
A_KV_HEADS = 4
A_GROUP = 4
A_HD = 64
A_KV = A_KV_HEADS * A_HD
WINDOW = 128
A_BLOCK = 128
ROT_DIM = 16
ROPE_THETA = 500000.0
N_EXPERTS = 32
TOP_K = 4
D_FF = 1024
SWIGLU_LIMIT = 7.0
SWIGLU_ALPHA = 1.702
NORM_EPS = 1e-6

LANES = 128
SUBLANES = 8
VMEM_LIMIT = 56 * 1024 * 1024

MERGE_TILE = 512
TOK_TILE = 256
STAGE_ROWS = 4 * TOK_TILE
ROW_TILE = (8, 128)
ROW_DTYPE = F32
MOE_BLOCK = 512


def _cparams(sem):
    return pltpu.CompilerParams(dimension_semantics=sem, vmem_limit_bytes=VMEM_LIMIT)


def _rms(x, g):
    return x * lax.rsqrt(jnp.mean(x * x, axis=-1, keepdims=True) + NORM_EPS) * g


def _rope(x, cos_t, sin_t, first_half):
    n = x.shape[1]
    fwd = pltpu.roll(x, n - ROT_DIM // 2, axis=1)
    bwd = pltpu.roll(x, ROT_DIM // 2, axis=1)
    outs = []
    for b in range(n // LANES):
        sl = slice(b * LANES, (b + 1) * LANES)
        partner = jnp.where(first_half, fwd[:, sl], bwd[:, sl])
        outs.append(x[:, sl] * cos_t + partner * sin_t)
    return jnp.concatenate(outs, axis=1)


def _inproj_kernel(x_ref, g1_ref, wm_ref, wg_ref, wqa_ref, wkv_ref, wif_ref, cos_ref, sin_ref,
                   qkv_ref, og_ref, qa_ref, ka_ref, va_ref, gate_ref):
    x = x_ref[...]
    xn = _rms(x, g1_ref[...]).astype(BF16)
    act = qkv_ref.dtype

    def proj(w_ref, c0, n):
        return jnp.dot(xn, w_ref[:, c0:c0 + n], preferred_element_type=F32)

    qkv_ref[:, 0:1024] = proj(wm_ref, 0, 1024).astype(act)
    qkv_ref[:, 1024:2048] = (proj(wm_ref, 1024, 1024) * (M_DK ** -0.5)).astype(act)
    qkv_ref[:, 2048:3072] = proj(wm_ref, 2048, 1024).astype(act)
    og_ref[:, 0:1024] = jax.nn.sigmoid(proj(wm_ref, 3072, 1024)).astype(act)
    og_ref[:, 1024:2048] = jax.nn.sigmoid(proj(wg_ref, 0, 1024)).astype(act)
    og_ref[:, 2048:3072] = jax.nn.sigmoid(proj(wg_ref, 1024, 1024)).astype(act)
    cos_t = cos_ref[...]
    sin_t = sin_ref[...]
    lane = lax.broadcasted_iota(I32, (1, LANES), 1)
    first_half = (lane & (ROT_DIM - 1)) < (ROT_DIM // 2)
    qa_ref[...] = _rope(proj(wqa_ref, 0, 1024), cos_t, sin_t, first_half).astype(act)
    ka_ref[...] = _rope(proj(wkv_ref, 0, A_KV), cos_t, sin_t, first_half)
    va_ref[...] = proj(wkv_ref, A_KV, A_KV)
    gate_ref[...] = proj(wif_ref, 0, LANES)


def _inproj(x, g1, weights, cos_t, sin_t, tm, rope_blocks, act):
    t = x.shape[0]
    tok = lambda n, dt: jax.ShapeDtypeStruct((t, n), dt)
    row = lambda n: pl.BlockSpec((tm, n), lambda i: (i, 0))
    resident = lambda w: pl.BlockSpec(w.shape, lambda i: (0, 0), pipeline_mode=pl.Buffered(1))
    return pl.pallas_call(
        _inproj_kernel,
        grid=(t // tm,),
        in_specs=[row(D_MODEL), pl.BlockSpec((1, D_MODEL), lambda i: (0, 0))]
                 + [resident(w) for w in weights]
                 + [pl.BlockSpec((tm, LANES), lambda i: (i % rope_blocks, 0)),
                    pl.BlockSpec((tm, LANES), lambda i: (i % rope_blocks, 0))],
        out_specs=[row(3072), row(3072), row(1024), row(A_KV), row(A_KV), row(LANES)],
        out_shape=[tok(3072, act), tok(3072, act), tok(1024, act), tok(A_KV, F32), tok(A_KV, F32),
                   tok(LANES, F32)],
        compiler_params=_cparams(("arbitrary",)),
        name="inproj",
    )(x, g1, *weights, cos_t, sin_t)


def _mlstm_gates(gc, bias, m_prev, L):
    z = gc + bias
    lf = jnp.minimum(z, 0.0) - jnp.log1p(jnp.exp(-jnp.abs(z)))
    row = lax.broadcasted_iota(I32, (L, LANES), 0)
    fc = lf
    sh = 1
    while sh < L:
        fc = fc + jnp.where(row >= sh, pltpu.roll(fc, sh, axis=0), 0.0)
        sh *= 2
    fcum = pltpu.roll(fc, LANES - M_HEADS, axis=1)
    a = z - fcum
    cmx = a
    sh = 1
    while sh < L:
        cmx = jnp.maximum(cmx, jnp.where(row >= sh, pltpu.roll(cmx, sh, axis=0), -jnp.inf))
        sh *= 2
    mx = jnp.maximum(m_prev, cmx)
    inter = jnp.exp(m_prev - mx)
    f_end = fcum[L - 1:L]
    m_end = f_end + mx[L - 1:L]
    decay = jnp.exp(f_end + m_prev - m_end)
    wsrc = jnp.exp(a + (f_end - m_end))
    return a, mx, inter, wsrc, decay, m_end


NT_DIMS = (((1,), (1,)), ((), ()))
TN_DIMS = (((0,), (0,)), ((), ()))


def _mlstm_decay(h, L, gates):
    a, mx = gates[0], gates[1]
    t_i = lax.broadcasted_iota(I32, (L, L), 0)
    s_i = lax.broadcasted_iota(I32, (L, L), 1)
    a_row = jnp.sum(jnp.where(t_i == s_i, a[:, h:h + 1], 0.0), axis=0, keepdims=True)
    return jnp.exp(jnp.where(s_i <= t_i, a_row - mx[:, h:h + 1], -jnp.inf))


def _mlstm_output(h, q, og, mn, gates, qk_sum, pv, cq, n_h):
    inter_col = gates[2][:, h:h + 1]
    num = inter_col * cq + pv
    den = inter_col * jnp.sum(q.astype(F32) * n_h, axis=1, keepdims=True) + qk_sum
    hh = num / jnp.maximum(jnp.abs(den), 1.0)
    hn = hh * lax.rsqrt(jnp.mean(hh * hh, axis=1, keepdims=True) + NORM_EPS)
    return hn * mn * og


def _mlstm_kernel(L, G, CH, carry, qkv_ref, gate_ref, og_ref, c0_ref, n0_ref, m0_ref, bias_ref, mn_ref,
                  h_ref, c_ref, n_ref, m_ref, *scratch):
    if carry:
        c_in, n_in, m_in = scratch
        ci = pl.program_id(1)

        @pl.when(ci == 0)
        def _():
            c_in[...] = c0_ref[...]
            n_in[...] = n0_ref[...]
            m_in[...] = m0_ref[...]
        c_out, n_out, m_out = scratch
    else:
        c_in, n_in, m_in = c0_ref, n0_ref, m0_ref
        c_out, n_out, m_out = c_ref, n_ref, m_ref

    def qkv(j, cc, h, part):
        cols = slice(part * 1024 + h * M_DK, part * 1024 + (h + 1) * M_DK)
        return qkv_ref[j, cc * L:(cc + 1) * L, cols].astype(BF16)

    units = [(j, cc, h) for cc in range(CH) for j in range(G) for h in range(M_HEADS)]
    gates = {}
    for j in range(G):
        m_cur = m_in[j]
        for cc in range(CH):
            gates[j, cc] = _mlstm_gates(gate_ref[j, cc * L:(cc + 1) * L], bias_ref[...], m_cur, L)
            m_cur = gates[j, cc][5]
        m_out[j] = m_cur
    scores = {u: lax.dot_general(qkv(*u, 0), qkv(*u, 1), NT_DIMS, preferred_element_type=F32) for u in units}
    qks = {(j, cc, h): scores[j, cc, h] * _mlstm_decay(h, L, gates[j, cc]) for j, cc, h in units}
    pvs = {u: jnp.dot(qks[u].astype(BF16), qkv(*u, 2), preferred_element_type=F32) for u in units}
    kws = {(j, cc, h): gates[j, cc][3][:, h:h + 1] * qkv(j, cc, h, 1).astype(F32) for j, cc, h in units}
    c_cur = {(j, h): c_in[j, h] for j in range(G) for h in range(M_HEADS)}
    n_cur = {(j, h): n_in[j, h:h + 1, :] for j in range(G) for h in range(M_HEADS)}
    for cc in range(CH):
        now = [u for u in units if u[1] == cc]
        cqs = {(j, h): lax.dot_general(qkv(j, cc, h, 0), c_cur[j, h].astype(BF16), NT_DIMS,
                                       preferred_element_type=F32) for j, _, h in now}
        for j, _, h in now:
            sl = slice(h * M_DV, (h + 1) * M_DV)
            rows = slice(cc * L, (cc + 1) * L)
            out = _mlstm_output(h, qkv(j, cc, h, 0), og_ref[j, rows, sl].astype(F32), mn_ref[:, sl], gates[j, cc],
                                jnp.sum(qks[j, cc, h], axis=1, keepdims=True), pvs[j, cc, h], cqs[j, h],
                                n_cur[j, h])
            h_ref[j, rows, sl] = out.astype(h_ref.dtype)
        for j, _, h in now:
            dec = gates[j, cc][4][:, h:h + 1]
            c_cur[j, h] = dec * c_cur[j, h] + lax.dot_general(qkv(j, cc, h, 2), kws[j, cc, h].astype(BF16),
                                                              TN_DIMS, preferred_element_type=F32)
            n_cur[j, h] = dec * n_cur[j, h] + jnp.sum(kws[j, cc, h], axis=0, keepdims=True)
    for j in range(G):
        for h in range(M_HEADS):
            c_out[j, h] = c_cur[j, h]
            n_out[j, h:h + 1, :] = n_cur[j, h]

    if carry:
        @pl.when(ci == pl.num_programs(1) - 1)
        def _():
            c_ref[...] = c_in[...]
            n_ref[...] = n_in[...]
            m_ref[...] = m_in[...]


MLSTM_GROUP_CARRY = 2
MLSTM_CHUNKS_CARRY = 1
MLSTM_GROUP_SINGLE = 4


def _mlstm(qkv, gates, og, c0, n0, m0, gate_bias, mnorm, nb, L):
    t = qkv.shape[0]
    nc = t // (nb * L)
    carry = nc > 1
    G = MLSTM_GROUP_CARRY if carry else MLSTM_GROUP_SINGLE
    CH = MLSTM_CHUNKS_CARRY if carry else 1
    qkv, gates, og = (a.reshape(nb, nc * L, a.shape[1]) for a in (qkv, gates, og))
    tokmap = lambda b, c: (b, c, 0)
    const2 = lambda b, c: (0, 0)
    state_specs = [pl.BlockSpec((G, M_HEADS, M_DV, M_DK), lambda b, c: (b, 0, 0, 0)),
                   pl.BlockSpec((G, M_HEADS, M_DK), lambda b, c: (b, 0, 0)),
                   pl.BlockSpec((G, 1, LANES), lambda b, c: (b, 0, 0))]
    scratch = [pltpu.VMEM((G, M_HEADS, M_DV, M_DK), F32),
               pltpu.VMEM((G, M_HEADS, M_DK), F32),
               pltpu.VMEM((G, 1, LANES), F32)] if carry else []
    h, c, n, m = pl.pallas_call(
        functools.partial(_mlstm_kernel, L, G, CH, carry),
        grid=(nb // G, nc // CH),
        in_specs=[pl.BlockSpec((G, CH * L, 3072), tokmap),
                  pl.BlockSpec((G, CH * L, LANES), tokmap),
                  pl.BlockSpec((G, CH * L, 1024), tokmap)]
                 + state_specs
                 + [pl.BlockSpec((1, LANES), const2),
                    pl.BlockSpec((1, 1024), const2)],
        out_specs=[pl.BlockSpec((G, CH * L, 1024), tokmap)] + state_specs,
        out_shape=[jax.ShapeDtypeStruct((nb, nc * L, 1024), qkv.dtype),
                   jax.ShapeDtypeStruct((nb, M_HEADS, M_DV, M_DK), F32),
                   jax.ShapeDtypeStruct((nb, M_HEADS, M_DK), F32),
                   jax.ShapeDtypeStruct((nb, 1, LANES), F32)],
        scratch_shapes=scratch,
        compiler_params=_cparams(("arbitrary", "arbitrary")),
        name="mlstm_L%d" % L,
    )(qkv, gates, og, c0, n0, m0, gate_bias, mnorm)
    return h.reshape(t, 1024), c, n, m


def _swa_bias(m, has_prev):
    nkeys = 2 * A_BLOCK
    t_i = lax.broadcasted_iota(I32, (A_GROUP * m, nkeys), 0) & (m - 1)
    s_i = lax.broadcasted_iota(I32, (A_GROUP * m, nkeys), 1)
    diff = t_i + A_BLOCK - s_i
    mask = (diff >= 0) & (diff <= WINDOW)
    if not has_prev:
        mask = mask & (s_i >= A_BLOCK)
    return jnp.where(mask, 0.0, -jnp.inf).astype(F32)


def _swa_group_ones():
    nkeys = 2 * A_BLOCK
    r = lax.broadcasted_iota(I32, (A_KV_HEADS * nkeys, A_KV), 0) >> (nkeys.bit_length() - 1)
    c = lax.broadcasted_iota(I32, (A_KV_HEADS * nkeys, A_KV), 1) >> (A_HD.bit_length() - 1)
    return (r == c).astype(BF16)


def _swa_core(q, k_prev, v_prev, k_cur, v_cur, bias, group_ones, sink_ref):
    m = q.shape[0]
    q = q.astype(BF16)
    qst = jnp.concatenate([q[:, j * A_KV:(j + 1) * A_KV] for j in range(A_GROUP)], axis=0)
    k_all = jnp.concatenate([k_prev, k_cur], axis=0)
    v_all = jnp.concatenate([v_prev, v_cur], axis=0)
    lane = lax.broadcasted_iota(I32, (1, A_KV), 1)
    kbd, vbd = [], []
    for g in range(A_KV_HEADS):
        in_g = (lane >= g * A_HD) & (lane < (g + 1) * A_HD)
        kbd.append(jnp.where(in_g, k_all, 0.0).astype(BF16))
        vbd.append(jnp.where(in_g, v_all, 0.0).astype(BF16))
    kbd = jnp.concatenate(kbd, axis=0)
    vbd = jnp.concatenate(vbd, axis=0)
    s = lax.dot_general(qst, kbd, (((1,), (1,)), ((), ())), preferred_element_type=F32)
    nkeys = 2 * A_BLOCK
    ps, sink_terms = [], []
    for g in range(A_KV_HEADS):
        sg = s[:, g * nkeys:(g + 1) * nkeys] + bias
        sink = jnp.concatenate(
            [jnp.full((m, LANES), sink_ref[A_GROUP * g + j], F32) for j in range(A_GROUP)], axis=0)
        mx = jnp.maximum(jnp.broadcast_to(jnp.max(sg, axis=1, keepdims=True), (A_GROUP * m, LANES)), sink)
        ps += [jnp.exp(sg[:, :LANES] - mx).astype(BF16), jnp.exp(sg[:, LANES:] - mx).astype(BF16)]
        sink_terms.append(jnp.exp(sink - mx))
    p = jnp.concatenate(ps, axis=1)
    o = jnp.dot(p, vbd, preferred_element_type=F32)
    den = jnp.dot(p, group_ones, preferred_element_type=F32)
    low = lax.broadcasted_iota(I32, (1, LANES), 1) < A_HD
    den = den + jnp.concatenate([jnp.where(low, sink_terms[0], sink_terms[1]),
                                 jnp.where(low, sink_terms[2], sink_terms[3])], axis=1)
    o = o / den
    return jnp.concatenate([o[j * m:(j + 1) * m] for j in range(A_GROUP)], axis=1)


SWA_STEP_BLOCKS = 4
SWA_STEP = SWA_STEP_BLOCKS * A_BLOCK


def _swa_prompt_kernel(sink_ref, q_ref, kp_ref, kc_ref, vp_ref, vc_ref, o_ref, bias_s, bias0_s, ones_s):
    first = (pl.program_id(0) == 0) & (pl.program_id(1) == 0)

    @pl.when(first)
    def _():
        bias_s[...] = _swa_bias(A_BLOCK, True)
        bias0_s[...] = _swa_bias(A_BLOCK, False)
        ones_s[...] = _swa_group_ones()

    bias = bias_s[...]
    bias_first = jnp.where(pl.program_id(1) > 0, bias, bias0_s[...])
    ones = ones_s[...]
    k_prev, v_prev = kp_ref[...], vp_ref[...]
    for b in range(SWA_STEP_BLOCKS):
        rows = slice(b * A_BLOCK, (b + 1) * A_BLOCK)
        k_cur, v_cur = kc_ref[rows], vc_ref[rows]
        o_ref[rows] = _swa_core(q_ref[rows], k_prev, v_prev, k_cur, v_cur, bias_first if b == 0 else bias,
                                ones, sink_ref).astype(o_ref.dtype)
        k_prev, v_prev = k_cur, v_cur


def _swa_prompt(qa, ka, va, sinks, nb):
    t = qa.shape[0]
    nstep = t // (nb * SWA_STEP)
    cur = lambda b, i, s: (b * nstep + i, 0)
    prev = lambda b, i, s: (SWA_STEP_BLOCKS * (b * nstep + i) - jnp.minimum(i, 1), 0)
    return pl.pallas_call(
        _swa_prompt_kernel,
        grid_spec=pltpu.PrefetchScalarGridSpec(
            num_scalar_prefetch=1,
            grid=(nb, nstep),
            in_specs=[pl.BlockSpec((SWA_STEP, 1024), cur),
                      pl.BlockSpec((A_BLOCK, A_KV), prev),
                      pl.BlockSpec((SWA_STEP, A_KV), cur),
                      pl.BlockSpec((A_BLOCK, A_KV), prev),
                      pl.BlockSpec((SWA_STEP, A_KV), cur)],
            out_specs=pl.BlockSpec((SWA_STEP, 1024), cur),
            scratch_shapes=[pltpu.VMEM((A_GROUP * A_BLOCK, 2 * A_BLOCK), F32),
                            pltpu.VMEM((A_GROUP * A_BLOCK, 2 * A_BLOCK), F32),
                            pltpu.VMEM((A_KV_HEADS * 2 * A_BLOCK, A_KV), BF16)]),
        out_shape=jax.ShapeDtypeStruct((t, 1024), BF16),
        compiler_params=_cparams(("arbitrary", "arbitrary")),
        name="swa_prompt",
    )(sinks, qa, ka, ka, va, va)


SWA_SAMPLE_GROUP = 8


def _swa_sample_kernel(T, sink_ref, q_ref, kn_ref, vn_ref, kb_ref, vb_ref, o_ref, ko_ref, vo_ref):
    pad = jnp.zeros((A_BLOCK - T, A_KV), F32)
    bias = _swa_bias(T, True)
    ones = _swa_group_ones()
    for j in range(SWA_SAMPLE_GROUP):
        rows = slice(j * T, (j + 1) * T)
        k_new = kn_ref[rows]
        v_new = vn_ref[rows]
        k_buf = kb_ref[j]
        v_buf = vb_ref[j]
        o_ref[rows] = _swa_core(q_ref[rows], k_buf, v_buf, jnp.concatenate([k_new, pad], axis=0),
                                jnp.concatenate([v_new, pad], axis=0), bias, ones, sink_ref).astype(o_ref.dtype)
        ko_ref[j] = jnp.concatenate([k_buf[T:], k_new], axis=0)
        vo_ref[j] = jnp.concatenate([v_buf[T:], v_new], axis=0)


def _swa_sample(qa, ka, va, k_buf, v_buf, sinks, T):
    nb = k_buf.shape[0]
    g = SWA_SAMPLE_GROUP
    tok = lambda b, s: (b, 0)
    buf = lambda b, s: (b, 0, 0)
    return pl.pallas_call(
        functools.partial(_swa_sample_kernel, T),
        grid_spec=pltpu.PrefetchScalarGridSpec(
            num_scalar_prefetch=1,
            grid=(nb // g,),
            in_specs=[pl.BlockSpec((g * T, 1024), tok),
                      pl.BlockSpec((g * T, A_KV), tok),
                      pl.BlockSpec((g * T, A_KV), tok),
                      pl.BlockSpec((g, WINDOW, A_KV), buf),
                      pl.BlockSpec((g, WINDOW, A_KV), buf)],
            out_specs=[pl.BlockSpec((g * T, 1024), tok),
                       pl.BlockSpec((g, WINDOW, A_KV), buf),
                       pl.BlockSpec((g, WINDOW, A_KV), buf)]),
        out_shape=[jax.ShapeDtypeStruct((nb * T, 1024), qa.dtype),
                   jax.ShapeDtypeStruct((nb, WINDOW, A_KV), F32),
                   jax.ShapeDtypeStruct((nb, WINDOW, A_KV), F32)],
        compiler_params=_cparams(("arbitrary",)),
        name="swa_sample",
    )(sinks, qa, ka, va, k_buf, v_buf)


def _merge_kernel(x_ref, hm_ref, ha_ref, gm_ref, ga_ref, wpm_ref, wpa_ref, wo_ref, g2_ref, wr_ref, br_ref,
                  x1_ref, xn_ref, idx_ref, gate_ref, cnt_ref):
    pm = jnp.dot(hm_ref[...].astype(BF16), wpm_ref[...], preferred_element_type=F32)
    pa = jnp.dot(ha_ref[...].astype(BF16), wpa_ref[...], preferred_element_type=F32)
    mixed = gm_ref[...].astype(F32) * pm + ga_ref[...].astype(F32) * pa
    x1 = x_ref[...] + jnp.dot(mixed.astype(BF16), wo_ref[...], preferred_element_type=F32)
    x1_ref[...] = x1
    xn = _rms(x1, g2_ref[...])
    xn_ref[...] = xn.astype(BF16)
    xn_hi = xn.astype(BF16)
    xn_lo = (xn - xn_hi.astype(F32)).astype(BF16)
    part = jnp.dot(xn_hi, wr_ref[...], preferred_element_type=F32)
    logits = (part[:, :LANES] + part[:, LANES:]
              + jnp.dot(xn_lo, wr_ref[:, :LANES], preferred_element_type=F32) + br_ref[...])
    tm = logits.shape[0]
    cur = logits.T[0:N_EXPERTS]
    eid = lax.broadcasted_iota(I32, (N_EXPERTS, tm), 0).astype(F32)
    vals, idxs = [], []
    onehot = jnp.zeros((N_EXPERTS, tm), F32)
    for _ in range(TOP_K):
        mval = jnp.max(cur, axis=0, keepdims=True)
        sel = jnp.min(jnp.where(cur == mval, eid, float(N_EXPERTS)), axis=0, keepdims=True)
        hit = eid == sel
        onehot = onehot + hit.astype(F32)
        cur = jnp.where(hit, -jnp.inf, cur)
        vals.append(mval)
        idxs.append(sel)
    es = [jnp.exp(v - vals[0]) for v in vals]
    tot = es[0] + es[1] + es[2] + es[3]
    row = lax.broadcasted_iota(I32, (SUBLANES, tm), 0)
    res = jnp.zeros((SUBLANES, tm), F32)
    for k in range(TOP_K):
        res = jnp.where(row == k, idxs[k], res)
        res = jnp.where(row == TOP_K + k, es[k] / tot, res)
    res_t = jnp.concatenate([res, jnp.zeros((LANES - SUBLANES, tm), F32)], axis=0).T
    idx_ref[...] = res_t[:, 0:TOP_K].astype(I32)
    gate_ref[...] = res_t[:, TOP_K:2 * TOP_K]
    for j in range(tm // TOK_TILE):
        cnt_ref[j] = jnp.sum(onehot[:, j * TOK_TILE:(j + 1) * TOK_TILE], axis=1, keepdims=True).astype(I32)


def _merge(x, hm, ha, og, wpm, wpa, wo, g2, wr, br):
    t = x.shape[0]
    tm = MERGE_TILE
    row = lambda n: pl.BlockSpec((tm, n), lambda i: (i, 0))
    const = lambda r, c: pl.BlockSpec((r, c), lambda i: (0, 0))
    return pl.pallas_call(
        _merge_kernel,
        grid=(t // tm,),
        in_specs=[row(1024), row(1024), row(1024),
                  pl.BlockSpec((tm, 1024), lambda i: (i, 1)),
                  pl.BlockSpec((tm, 1024), lambda i: (i, 2)),
                  const(1024, 1024), const(1024, 1024), const(1024, 1024), const(1, 1024),
                  const(1024, 2 * LANES), const(1, LANES)],
        out_specs=[row(1024), row(1024), row(TOP_K), row(TOP_K),
                   pl.BlockSpec((tm // TOK_TILE, N_EXPERTS, 1), lambda i: (i, 0, 0))],
        out_shape=[jax.ShapeDtypeStruct((t, 1024), F32),
                   jax.ShapeDtypeStruct((t, 1024), BF16),
                   jax.ShapeDtypeStruct((t, TOP_K), I32),
                   jax.ShapeDtypeStruct((t, TOP_K), F32),
                   jax.ShapeDtypeStruct((t // TOK_TILE, N_EXPERTS, 1), I32)],
        compiler_params=_cparams(("arbitrary",)),
        name="merge_route",
    )(x, hm, ha, og, og, wpm, wpa, wo, g2, wr, br)


def _rows_to_tiles(ref, lead, x):
    chunks = jnp.stack([x[:, s * LANES:(s + 1) * LANES] for s in range(ROW_TILE[0])], axis=0)
    ref[lead] = jnp.swapaxes(chunks, 0, 1).astype(ref.dtype)


def _tiles_to_rows(ref, lead, rows=slice(None)):
    chunks = jnp.swapaxes(ref[lead + (rows,)].astype(F32), 0, 1)
    return jnp.concatenate([chunks[s] for s in range(ROW_TILE[0])], axis=1)


def _stage_rows(idx, off_row):
    tm = idx.shape[0]
    lane = lax.broadcasted_iota(I32, (tm, LANES), 1)
    hits = [lane == idx[:, k:k + 1] for k in range(TOP_K)]
    onehot = sum(h.astype(F32) for h in hits)
    r_i = lax.broadcasted_iota(I32, (tm, tm), 0)
    c_i = lax.broadcasted_iota(I32, (tm, tm), 1)
    before = (c_i < r_i).astype(BF16)
    rank = jnp.dot(before, onehot.astype(BF16), preferred_element_type=F32)
    pos = rank + off_row
    return [jnp.sum(jnp.where(h, pos, 0.0), axis=1, keepdims=True) for h in hits]


def _dispatch_kernel(ntile_a, cnt_ref, off_ref, base_ref, tail_ref, xa_ref, xb_ref, ia_ref, ib_ref, offv_ref,
                     xs_ref, stage, zeros, sem, zsem):
    i = pl.program_id(0)
    from_a = i < ntile_a
    xn = jnp.where(from_a, xa_ref[...], xb_ref[...])
    idx = jnp.where(from_a, ia_ref[...], ib_ref[...])
    rows = _stage_rows(idx, offv_ref[0])
    tm = idx.shape[0]
    lane = lax.broadcasted_iota(I32, (tm, LANES), 1)
    r4 = jnp.zeros((tm, LANES), F32)
    for k in range(TOP_K):
        r4 = jnp.where(lane == k, rows[k], r4)
    r4t = r4.T.astype(I32)
    r_iota = lax.broadcasted_iota(I32, (STAGE_ROWS, tm), 0)
    sel = r_iota == r4t[0:1, :]
    for k in range(1, TOP_K):
        sel = sel | (r_iota == r4t[k:k + 1, :])
    sel = jnp.where(sel, 1.0, 0.0).astype(BF16)
    staged = jnp.dot(sel, xn, preferred_element_type=F32)

    def runs(tile, slot):
        out = []
        for e in range(N_EXPERTS):
            n = cnt_ref[tile * N_EXPERTS + e]
            off = off_ref[tile * N_EXPERTS + e]
            base = base_ref[tile * N_EXPERTS + e]
            out.append((n, pltpu.make_async_copy(stage.at[slot, pl.ds(off, n)], xs_ref.at[pl.ds(base, n)],
                                                 sem.at[slot])))
        return out

    def wait_runs(slot):
        pltpu.make_async_copy(stage.at[slot], xs_ref.at[pl.ds(0, STAGE_ROWS)], sem.at[slot]).wait()

    for slot in range(2):
        @pl.when((i & 1) == slot)
        def _():
            @pl.when(i >= 2)
            def _():
                wait_runs(slot)
            _rows_to_tiles(stage, (slot,), staged)
            for n, cp in runs(i, slot):
                @pl.when(n > 0)
                def _():
                    cp.start()

    last = pl.num_programs(0) - 1

    @pl.when(i == last)
    def _():
        for slot in range(2):
            @pl.when(((last & 1) == slot) | (last >= 1))
            def _():
                wait_runs(slot)
        zeros[...] = jnp.zeros(zeros.shape, zeros.dtype)
        tails = []
        for e in range(N_EXPERTS):
            start = tail_ref[e]
            n = tail_ref[N_EXPERTS + e]
            tails.append((n, pltpu.make_async_copy(zeros.at[pl.ds(0, n)], xs_ref.at[pl.ds(start, n)], zsem)))
        for n, cp in tails:
            @pl.when(n > 0)
            def _():
                cp.start()
        for n, cp in tails:
            @pl.when(n > 0)
            def _():
                cp.wait()
        first = tail_ref[2 * N_EXPERTS]
        n_unused = tail_ref[2 * N_EXPERTS + 1]

        def unused_block(j):
            return pltpu.make_async_copy(zeros, xs_ref.at[pl.ds((first + j) * MOE_BLOCK, MOE_BLOCK)], zsem)

        @pl.loop(0, n_unused)
        def _(j):
            unused_block(j).start()

        @pl.loop(0, n_unused)
        def _(j):
            unused_block(j).wait()


def _dispatch(xa, xb, ia, ib, cnt, off, base, tails, offv, n_slots):
    tm = TOK_TILE
    ntile_a = xa.shape[0] // tm
    ntile_b = xb.shape[0] // tm
    amap = lambda i, *_: (jnp.minimum(i, ntile_a - 1), 0)
    bmap = lambda i, *_: (jnp.maximum(i - ntile_a, 0), 0)
    return pl.pallas_call(
        functools.partial(_dispatch_kernel, ntile_a),
        grid_spec=pltpu.PrefetchScalarGridSpec(
            num_scalar_prefetch=4,
            grid=(ntile_a + ntile_b,),
            in_specs=[pl.BlockSpec((tm, 1024), amap),
                      pl.BlockSpec((tm, 1024), bmap),
                      pl.BlockSpec((tm, TOP_K), amap),
                      pl.BlockSpec((tm, TOP_K), bmap),
                      pl.BlockSpec((1, 1, LANES), lambda i, *_: (i, 0, 0))],
            out_specs=pl.BlockSpec(memory_space=pl.ANY),
            scratch_shapes=[pltpu.VMEM((2, STAGE_ROWS) + ROW_TILE, ROW_DTYPE),
                            pltpu.VMEM((MOE_BLOCK,) + ROW_TILE, ROW_DTYPE),
                            pltpu.SemaphoreType.DMA((2,)),
                            pltpu.SemaphoreType.DMA]),
        out_shape=jax.ShapeDtypeStruct((n_slots,) + ROW_TILE, ROW_DTYPE),
        compiler_params=_cparams(("arbitrary",)),
        name="moe_dispatch",
    )(cnt, off, base, tails, xa, xb, ia, ib, offv)


MOE_HALF = MOE_BLOCK // 2


def _experts_kernel(be_ref, rows_ref, next_ref, slot_ref, xs_ref, wgu_hbm, bgu_ref, wd_hbm, bd_ref, y_ref,
                    wgu_f, wd_f, wgu_s, wd_s, sem):
    i = pl.program_id(0)
    rows = rows_ref[i]
    live = rows > 0
    run_start = (i == 0) | (be_ref[i] != be_ref[jnp.maximum(i - 1, 0)])

    def fetch(expert, slot):
        return (pltpu.make_async_copy(wgu_hbm.at[expert], wgu_f.at[slot], sem.at[0, slot]),
                pltpu.make_async_copy(wd_hbm.at[expert], wd_f.at[slot], sem.at[1, slot]))

    for slot in range(2):
        @pl.when(live & run_start & (slot_ref[i] == slot))
        def _():
            @pl.when(i == 0)
            def _():
                for cp in fetch(be_ref[i], slot):
                    cp.start()
            for cp in fetch(be_ref[i], slot):
                cp.wait()

            @pl.when(next_ref[i] >= 0)
            def _():
                for cp in fetch(next_ref[i], 1 - slot):
                    cp.start()
            wgu_s[...] = wgu_f[slot].astype(BF16)
            wd_s[...] = wd_f[slot].astype(BF16)

    def ffn(x):
        h = jnp.dot(x.astype(BF16), wgu_s[...], preferred_element_type=F32) + bgu_ref[0]
        gate = jnp.minimum(h[:, :D_FF], SWIGLU_LIMIT)
        up = jnp.clip(h[:, D_FF:], -SWIGLU_LIMIT, SWIGLU_LIMIT)
        act = gate * jax.nn.sigmoid(SWIGLU_ALPHA * gate) * (up + 1.0)
        return jnp.dot(act.astype(BF16), wd_s[...], preferred_element_type=F32) + bd_ref[0]

    @pl.when(rows > MOE_HALF)
    def _():
        _rows_to_tiles(y_ref, (), ffn(_tiles_to_rows(xs_ref, ())))

    @pl.when(live & (rows <= MOE_HALF))
    def _():
        y_half = ffn(_tiles_to_rows(xs_ref, (), slice(0, MOE_HALF)))
        _rows_to_tiles(y_ref, (), jnp.concatenate([y_half, jnp.zeros((MOE_HALF, D_MODEL), F32)], axis=0))

    @pl.when(rows == 0)
    def _():
        y_ref[...] = jnp.zeros(y_ref.shape, y_ref.dtype)


def _experts(block_expert, block_rows, next_expert, slot, xs, wgu, bgu, wd, bd):
    nb_max = xs.shape[0] // MOE_BLOCK
    blk = lambda i, be, rows, *_: (jnp.where(rows[i] > 0, i, 0), 0, 0)
    wmap = lambda i, be, *_: (be[i], 0, 0)
    return pl.pallas_call(
        _experts_kernel,
        grid_spec=pltpu.PrefetchScalarGridSpec(
            num_scalar_prefetch=4,
            grid=(nb_max,),
            in_specs=[pl.BlockSpec((MOE_BLOCK,) + ROW_TILE, blk),
                      pl.BlockSpec(memory_space=pl.ANY),
                      pl.BlockSpec((1, 1, 2 * D_FF), wmap),
                      pl.BlockSpec(memory_space=pl.ANY),
                      pl.BlockSpec((1, 1, D_MODEL), wmap)],
            out_specs=pl.BlockSpec((MOE_BLOCK,) + ROW_TILE, lambda i, *_: (i, 0, 0)),
            scratch_shapes=[pltpu.VMEM((2, D_MODEL, 2 * D_FF), F32),
                            pltpu.VMEM((2, D_FF, D_MODEL), F32),
                            pltpu.VMEM((D_MODEL, 2 * D_FF), BF16),
                            pltpu.VMEM((D_FF, D_MODEL), BF16),
                            pltpu.SemaphoreType.DMA((2, 2))]),
        out_shape=jax.ShapeDtypeStruct(xs.shape, ROW_DTYPE),
        compiler_params=_cparams(("arbitrary",)),
        name="moe_experts",
    )(block_expert, block_rows, next_expert, slot, xs, wgu, bgu, wd, bd)


def _combine_kernel(cnt_ref, off_ref, base_ref, x1_ref, idx_ref, gate_ref, offv_ref, gf_ref, y_ref,
                    o_ref, stage, sem):
    i = pl.program_id(0)
    last = pl.num_programs(0) - 1

    def runs(tile, slot):
        out = []
        for e in range(N_EXPERTS):
            n = cnt_ref[tile * N_EXPERTS + e]
            off = off_ref[tile * N_EXPERTS + e]
            base = base_ref[tile * N_EXPERTS + e]
            out.append((n, pltpu.make_async_copy(y_ref.at[pl.ds(base, n)], stage.at[slot, pl.ds(off, n)],
                                                 sem.at[slot])))
        return out

    def start_runs(tile, slot):
        for n, cp in runs(tile, slot):
            @pl.when(n > 0)
            def _():
                cp.start()

    @pl.when(i == 0)
    def _():
        start_runs(0, 0)

    for slot in range(2):
        @pl.when(((i & 1) != slot) & (i < last))
        def _():
            start_runs(i + 1, slot)

    rows = _stage_rows(idx_ref[...], offv_ref[0])
    tm = idx_ref.shape[0]
    lane = lax.broadcasted_iota(I32, (tm, STAGE_ROWS), 1)
    gates = gate_ref[...]
    gmat = jnp.zeros((tm, STAGE_ROWS), F32)
    for k in range(TOP_K):
        gmat = gmat + jnp.where(lane == rows[k].astype(I32), gates[:, k:k + 1], 0.0)
    gmat = gmat.astype(BF16)
    for slot in range(2):
        @pl.when((i & 1) == slot)
        def _():
            pltpu.make_async_copy(y_ref.at[pl.ds(0, STAGE_ROWS)], stage.at[slot], sem.at[slot]).wait()
            moe = jnp.dot(gmat, _tiles_to_rows(stage, (slot,)).astype(BF16), preferred_element_type=F32)
            o_ref[...] = _rms(x1_ref[...] + moe, gf_ref[...])


def _combine(x1, idx, gates, cnt, off, base, offv, gf, y):
    t = x1.shape[0]
    tm = TOK_TILE
    return pl.pallas_call(
        _combine_kernel,
        grid_spec=pltpu.PrefetchScalarGridSpec(
            num_scalar_prefetch=3,
            grid=(t // tm,),
            in_specs=[pl.BlockSpec((tm, 1024), lambda i, *_: (i, 0)),
                      pl.BlockSpec((tm, TOP_K), lambda i, *_: (i, 0)),
                      pl.BlockSpec((tm, TOP_K), lambda i, *_: (i, 0)),
                      pl.BlockSpec((1, 1, LANES), lambda i, *_: (i, 0, 0)),
                      pl.BlockSpec((1, 1024), lambda i, *_: (0, 0)),
                      pl.BlockSpec(memory_space=pl.ANY)],
            out_specs=pl.BlockSpec((tm, 1024), lambda i, *_: (i, 0)),
            scratch_shapes=[pltpu.VMEM((2, STAGE_ROWS) + ROW_TILE, ROW_DTYPE),
                            pltpu.SemaphoreType.DMA((2,))]),
        out_shape=jax.ShapeDtypeStruct((t, 1024), F32),
        compiler_params=_cparams(("arbitrary",)),
        name="moe_combine",
    )(cnt, off, base, x1, idx, gates, offv, gf, y)


def _prep_w_in(w):
    c_if, c_qa, c_kv, c_g = 4096, 4104, 5128, 5640
    wm = w[:, :c_if].astype(BF16)
    wif = jnp.pad(w[:, c_if:c_qa], ((0, 0), (0, LANES - 2 * M_HEADS))).astype(BF16)
    wqa = (jnp.concatenate([w[:, c_qa + (A_GROUP * g + j) * A_HD:c_qa + (A_GROUP * g + j + 1) * A_HD]
                            for j in range(A_GROUP) for g in range(A_KV_HEADS)], axis=1)
           * (A_HD ** -0.5)).astype(BF16)
    wkv = w[:, c_kv:c_g].astype(BF16)
    wg = w[:, c_g:].astype(BF16)
    return wm, wg, wqa, wkv, wif


def _rope_tables(pos):
    half = ROT_DIM // 2
    lane = jnp.arange(LANES) % A_HD
    inv = ROPE_THETA ** (-((lane % half).astype(F32) * 2.0 / ROT_DIM))
    ang = pos.astype(F32)[:, None] * inv[None, :]
    rotated = (lane < ROT_DIM)[None, :]
    sign = jnp.where(lane < half, -1.0, 1.0)[None, :]
    return jnp.where(rotated, jnp.cos(ang), 1.0), jnp.where(rotated, sign * jnp.sin(ang), 0.0)


def _round_up(x, m):
    return (x + m - 1) // m * m


def kernel(x_prompt, x_sample, cache_k, cache_v, state_c, state_n, state_m, norm1, w_in, b_igate, b_fgate,
           mlstm_norm, w_proj_m, w_proj_a, attn_sinks, w_out, norm2, w_router, b_router, w_gate_up, b_gate_up,
           w_down, b_down, norm_f):
    bp, sp, _ = x_prompt.shape
    bs, ts, _ = x_sample.shape
    past_len = 16384
    l = 0
    w1 = _prep_w_in(w_in[l])
    g1 = norm1[l][None, :]
    g2 = norm2[l][None, :]
    gf = norm_f[None, :]
    gate_bias = jnp.pad(jnp.concatenate([b_igate[l], b_fgate[l]]), (0, LANES - 2 * M_HEADS))[None, :]
    mn = mlstm_norm[l][None, :]
    wpm = w_proj_m[l].astype(BF16)
    wpa = jnp.concatenate([w_proj_a[l][(A_GROUP * g + j) * A_HD:(A_GROUP * g + j + 1) * A_HD]
                           for j in range(A_GROUP) for g in range(A_KV_HEADS)], axis=0).astype(BF16)
    wo = w_out[l].astype(BF16)
    wr_f = jnp.pad(w_router[l], ((0, 0), (0, LANES - N_EXPERTS)))
    wr_hi = wr_f.astype(BF16)
    wr = jnp.concatenate([wr_hi, (wr_f - wr_hi.astype(F32)).astype(BF16)], axis=1)
    br = jnp.pad(b_router[l], (0, LANES - N_EXPERTS))[None, :]
    wgu = w_gate_up[l]
    bgu = b_gate_up[l][:, None, :]
    wd = w_down[l]
    bd = b_down[l][:, None, :]
    sinks = attn_sinks[l]

    def mixer(x, nb, seq, pos0, L, c0, n0, m0, k_buf, v_buf, tm):
        t = nb * seq
        xf = x.reshape(t, D_MODEL)
        cos_t, sin_t = _rope_tables(pos0 + jnp.arange(seq))
        if seq < tm:
            cos_t, sin_t = jnp.tile(cos_t, (tm // seq, 1)), jnp.tile(sin_t, (tm // seq, 1))
        act = BF16 if L % 16 == 0 else F32
        qkv, og, qa, ka, va, gates = _inproj(xf, g1, w1, cos_t, sin_t, tm, max(seq // tm, 1), act)
        m0p = jnp.pad(m0, ((0, 0), (0, LANES - M_HEADS)))[:, None, :]
        hm, c, n, m = _mlstm(qkv, gates, og, c0, n0, m0p, gate_bias, mn, nb, L)
        if k_buf is None:
            ha = _swa_prompt(qa, ka, va, sinks, nb)
            k_keep = ka.reshape(nb, seq, A_KV)[:, seq - WINDOW:].reshape(nb, WINDOW, A_KV_HEADS, A_HD)
            v_keep = va.reshape(nb, seq, A_KV)[:, seq - WINDOW:].reshape(nb, WINDOW, A_KV_HEADS, A_HD)
        else:
            ha, k_keep, v_keep = _swa_sample(qa, ka, va, k_buf.reshape(nb, WINDOW, A_KV),
                                             v_buf.reshape(nb, WINDOW, A_KV), sinks, seq)
            k_keep = k_keep.reshape(nb, WINDOW, A_KV_HEADS, A_HD)
            v_keep = v_keep.reshape(nb, WINDOW, A_KV_HEADS, A_HD)
        x1, xn, idx, gate, cnt = _merge(xf, hm, ha, og, wpm, wpa, wo, g2, wr, br)
        return (x1, xn, idx, gate, cnt[:, :, 0]), (k_keep, v_keep, c, n, m[:, 0, :M_HEADS])

    zc = jnp.zeros((bp, M_HEADS, M_DV, M_DK), F32)
    zn = jnp.zeros((bp, M_HEADS, M_DK), F32)
    zm = jnp.zeros((bp, M_HEADS), F32)
    rp, sp_out = mixer(x_prompt, bp, sp, 0, M_CHUNK, zc, zn, zm, None, None, 512)
    rs, ss_out = mixer(x_sample, bs, ts, past_len, ts, state_c[l], state_n[l], state_m[l],
                       cache_k[l], cache_v[l], 256)

    cnt = jnp.concatenate([rp[4], rs[4]], axis=0)
    ntile_p = rp[4].shape[0]
    off = jnp.cumsum(cnt, axis=1) - cnt
    per_expert = jnp.sum(cnt, axis=0)
    padded = _round_up(per_expert, MOE_BLOCK)
    padded_end = jnp.cumsum(padded)
    expert_start = padded_end - padded
    base = expert_start[None, :] + jnp.cumsum(cnt, axis=0) - cnt
    n_tok = bp * sp + bs * ts
    n_slots = _round_up(n_tok * TOP_K, MOE_BLOCK) + N_EXPERTS * MOE_BLOCK
    nb_max = n_slots // MOE_BLOCK
    block_row = jnp.arange(nb_max, dtype=I32) * MOE_BLOCK
    block_expert = jnp.minimum(jnp.sum(block_row[:, None] >= padded_end[None, :], axis=1), N_EXPERTS - 1).astype(I32)
    n_blocks = (padded_end[-1] // MOE_BLOCK).astype(I32)[None]
    eid = jnp.arange(N_EXPERTS, dtype=I32)
    later_used = (eid[None, :] > eid[:, None]) & (padded[None, :] > 0)
    next_used = jnp.min(jnp.where(later_used, eid[None, :], N_EXPERTS), axis=1)
    next_used = jnp.where(next_used < N_EXPERTS, next_used, -1).astype(I32)
    run_parity = ((jnp.cumsum(padded > 0) - 1) & 1).astype(I32)
    of_block = lambda table: jnp.sum(jnp.where(block_expert[:, None] == eid[None, :], table[None, :], 0),
                                     axis=1).astype(I32)
    block_rows = jnp.clip(of_block(per_expert) - (block_row - of_block(expert_start)), 0, MOE_BLOCK).astype(I32)
    tails = jnp.concatenate([expert_start + per_expert, padded - per_expert,
                             n_blocks, nb_max - n_blocks]).astype(I32)
    flat = lambda a: a.reshape(-1).astype(I32)
    offv = jnp.pad(off, ((0, 0), (0, LANES - N_EXPERTS))).astype(F32)[:, None, :]

    xs = _dispatch(rp[1], rs[1], rp[2], rs[2], flat(cnt), flat(off), flat(base), tails, offv, n_slots)
    y = _experts(block_expert, block_rows, of_block(next_used), of_block(run_parity), xs, wgu, bgu, wd, bd)
    y_p = _combine(rp[0], rp[2], rp[3], flat(cnt[:ntile_p]), flat(off[:ntile_p]), flat(base[:ntile_p]),
                   offv[:ntile_p], gf, y)
    y_s = _combine(rs[0], rs[2], rs[3], flat(cnt[ntile_p:]), flat(off[ntile_p:]), flat(base[ntile_p:]),
                   offv[ntile_p:], gf, y)

    kp, vp, cp_, np_, mp = sp_out
    ks, vs, cs, ns, ms = ss_out
    return (y_p.reshape(bp, sp, D_MODEL), y_s.reshape(bs, ts, D_MODEL),
            kp[None], vp[None], cp_[None], np_[None], mp[None],
            ks[None], vs[None], cs[None], ns[None], ms[None])
```

```python
import functools

import jax
import jax.numpy as jnp
from jax import lax
from jax.experimental import pallas as pl
from jax.experimental.pallas import tpu as pltpu

F32 = jnp.float32
BF16 = jnp.bfloat16
I32 = jnp.int32

D_MODEL = 1024
M_HEADS = 4
M_DK = 256
M_DV = 256
M_CHUNK = 128
A_Q_HEADS = 16
A_KV_HEADS = 4
A_GROUP = 4
A_HD = 64
A_KV = A_KV_HEADS * A_HD
WINDOW = 128
A_BLOCK = 128
ROT_DIM = 16
ROPE_THETA = 500000.0
N_EXPERTS = 32
TOP_K = 4
D_FF = 1024
SWIGLU_LIMIT = 7.0
SWIGLU_ALPHA = 1.702
NORM_EPS = 1e-6

LANES = 128
SUBLANES = 8
VMEM_LIMIT = 56 * 1024 * 1024

MERGE_TILE = 512
TOK_TILE = 256
STAGE_ROWS = 4 * TOK_TILE
ROW_TILE = (8, 128)
ROW_DTYPE = F32
MOE_BLOCK = 512


def _cparams(sem):
    return pltpu.CompilerParams(dimension_semantics=sem, vmem_limit_bytes=VMEM_LIMIT)


def _rms(x, g):
    return x * lax.rsqrt(jnp.mean(x * x, axis=-1, keepdims=True) + NORM_EPS) * g


def _rope(x, cos_t, sin_t, first_half):
    n = x.shape[1]
    fwd = pltpu.roll(x, n - ROT_DIM // 2, axis=1)
    bwd = pltpu.roll(x, ROT_DIM // 2, axis=1)
    outs = []
    for b in range(n // LANES):
        sl = slice(b * LANES, (b + 1) * LANES)
        partner = jnp.where(first_half, fwd[:, sl], bwd[:, sl])
        outs.append(x[:, sl] * cos_t + partner * sin_t)
    return jnp.concatenate(outs, axis=1)


def _inproj_kernel(x_ref, g1_ref, wm_ref, wg_ref, wqa_ref, wkv_ref, wif_ref, cos_ref, sin_ref,
                   qkv_ref, og_ref, qa_ref, ka_ref, va_ref, gate_ref):
    x = x_ref[...]
    xn = _rms(x, g1_ref[...]).astype(BF16)
    act = qkv_ref.dtype

    def proj(w_ref, c0, n):
        return lax.dot_general(xn, w_ref[c0:c0 + n, :], (((1,), (1,)), ((), ())), preferred_element_type=F32)

    qkv_ref[:, 0:1024] = proj(wm_ref, 0, 1024).astype(act)
    qkv_ref[:, 1024:2048] = (proj(wm_ref, 1024, 1024) * (M_DK ** -0.5)).astype(act)
    qkv_ref[:, 2048:3072] = proj(wm_ref, 2048, 1024).astype(act)
    og_ref[:, 0:1024] = jax.nn.sigmoid(proj(wm_ref, 3072, 1024)).astype(act)
    og_ref[:, 1024:2048] = jax.nn.sigmoid(proj(wg_ref, 0, 1024)).astype(act)
    og_ref[:, 2048:3072] = jax.nn.sigmoid(proj(wg_ref, 1024, 1024)).astype(act)
    cos_t = cos_ref[...]
    sin_t = sin_ref[...]
    lane = lax.broadcasted_iota(I32, (1, LANES), 1)
    first_half = (lane & (ROT_DIM - 1)) < (ROT_DIM // 2)
    qa_ref[...] = _rope(proj(wqa_ref, 0, 1024), cos_t, sin_t, first_half).astype(act)
    ka_ref[...] = _rope(proj(wkv_ref, 0, A_KV), cos_t, sin_t, first_half)
    va_ref[...] = proj(wkv_ref, A_KV, A_KV)
    gate_ref[...] = proj(wif_ref, 0, LANES)


def _inproj(x, g1, weights, cos_t, sin_t, tm, rope_blocks, act):
    t = x.shape[0]
    tok = lambda n, dt: jax.ShapeDtypeStruct((t, n), dt)
    row = lambda n: pl.BlockSpec((tm, n), lambda i: (i, 0))
    resident = lambda w: pl.BlockSpec(w.shape, lambda i: (0, 0), pipeline_mode=pl.Buffered(1))
    return pl.pallas_call(
        _inproj_kernel,
        grid=(t // tm,),
        in_specs=[row(D_MODEL), pl.BlockSpec((1, D_MODEL), lambda i: (0, 0))]
                 + [resident(w) for w in weights]
                 + [pl.BlockSpec((tm, LANES), lambda i: (i % rope_blocks, 0)),
                    pl.BlockSpec((tm, LANES), lambda i: (i % rope_blocks, 0))],
        out_specs=[row(3072), row(3072), row(1024), row(A_KV), row(A_KV), row(LANES)],
        out_shape=[tok(3072, act), tok(3072, act), tok(1024, act), tok(A_KV, F32), tok(A_KV, F32),
                   tok(LANES, F32)],
        compiler_params=_cparams(("arbitrary",)),
        name="inproj",
    )(x, g1, *weights, cos_t, sin_t)


def _mlstm_gates(gc, bias, m_prev, L):
    z = gc + bias
    lf = jnp.minimum(z, 0.0) - jnp.log1p(jnp.exp(-jnp.abs(z)))
    row = lax.broadcasted_iota(I32, (L, LANES), 0)
    fc = lf
    sh = 1
    while sh < L:
        fc = fc + jnp.where(row >= sh, pltpu.roll(fc, sh, axis=0), 0.0)
        sh *= 2
    fcum = pltpu.roll(fc, LANES - M_HEADS, axis=1)
    a = z - fcum
    cmx = a
    sh = 1
    while sh < L:
        cmx = jnp.maximum(cmx, jnp.where(row >= sh, pltpu.roll(cmx, sh, axis=0), -jnp.inf))
        sh *= 2
    mx = jnp.maximum(m_prev, cmx)
    inter = jnp.exp(m_prev - mx)
    f_end = fcum[L - 1:L]
    m_end = f_end + mx[L - 1:L]
    decay = jnp.exp(f_end + m_prev - m_end)
    wsrc = jnp.exp(a + (f_end - m_end))
    return a, mx, inter, wsrc, decay, m_end


NT_DIMS = (((1,), (1,)), ((), ()))
TN_DIMS = (((0,), (0,)), ((), ()))


def _mlstm_decay(h, L, gates):
    a, mx = gates[0], gates[1]
    t_i = lax.broadcasted_iota(I32, (L, L), 0)
    s_i = lax.broadcasted_iota(I32, (L, L), 1)
    a_row = jnp.sum(jnp.where(t_i == s_i, a[:, h:h + 1], 0.0), axis=0, keepdims=True)
    return jnp.exp(jnp.where(s_i <= t_i, a_row - mx[:, h:h + 1], -jnp.inf))


def _mlstm_output(h, q, og, mn, gates, qk_sum, pv, cq, n_h):
    inter_col = gates[2][:, h:h + 1]
    num = inter_col * cq + pv
    den = inter_col * jnp.sum(q.astype(F32) * n_h, axis=1, keepdims=True) + qk_sum
    hh = num / jnp.maximum(jnp.abs(den), 1.0)
    hn = hh * lax.rsqrt(jnp.mean(hh * hh, axis=1, keepdims=True) + NORM_EPS)
    return hn * mn * og


def _mlstm_kernel(L, G, CH, carry, qkv_ref, gate_ref, og_ref, c0_ref, n0_ref, m0_ref, bias_ref, mn_ref,
                  h_ref, c_ref, n_ref, m_ref, *scratch):
    if carry:
        c_in, n_in, m_in = scratch
        ci = pl.program_id(1)

        @pl.when(ci == 0)
        def _():
            c_in[...] = c0_ref[...]
            n_in[...] = n0_ref[...]
            m_in[...] = m0_ref[...]
        c_out, n_out, m_out = scratch
    else:
        c_in, n_in, m_in = c0_ref, n0_ref, m0_ref
        c_out, n_out, m_out = c_ref, n_ref, m_ref

    def qkv(j, cc, h, part):
        cols = slice(part * 1024 + h * M_DK, part * 1024 + (h + 1) * M_DK)
        return qkv_ref[j, cc * L:(cc + 1) * L, cols].astype(BF16)

    units = [(j, cc, h) for cc in range(CH) for j in range(G) for h in range(M_HEADS)]
    gates = {}
    for j in range(G):
        m_cur = m_in[j]
        for cc in range(CH):
            gates[j, cc] = _mlstm_gates(gate_ref[j, cc * L:(cc + 1) * L], bias_ref[...], m_cur, L)
            m_cur = gates[j, cc][5]
        m_out[j] = m_cur
    scores = {u: lax.dot_general(qkv(*u, 0), qkv(*u, 1), NT_DIMS, preferred_element_type=F32) for u in units}
    qks = {(j, cc, h): scores[j, cc, h] * _mlstm_decay(h, L, gates[j, cc]) for j, cc, h in units}
    pvs = {u: jnp.dot(qks[u].astype(BF16), qkv(*u, 2), preferred_element_type=F32) for u in units}
    kws = {(j, cc, h): gates[j, cc][3][:, h:h + 1] * qkv(j, cc, h, 1).astype(F32) for j, cc, h in units}
    c_cur = {(j, h): c_in[j, h] for j in range(G) for h in range(M_HEADS)}
    n_cur = {(j, h): n_in[j, h:h + 1, :] for j in range(G) for h in range(M_HEADS)}
    for cc in range(CH):
        now = [u for u in units if u[1] == cc]
        cqs = {(j, h): lax.dot_general(qkv(j, cc, h, 0), c_cur[j, h].astype(BF16), NT_DIMS,
                                       preferred_element_type=F32) for j, _, h in now}
        for j, _, h in now:
            sl = slice(h * M_DV, (h + 1) * M_DV)
            rows = slice(cc * L, (cc + 1) * L)
            out = _mlstm_output(h, qkv(j, cc, h, 0), og_ref[j, rows, sl].astype(F32), mn_ref[:, sl], gates[j, cc],
                                jnp.sum(qks[j, cc, h], axis=1, keepdims=True), pvs[j, cc, h], cqs[j, h],
                                n_cur[j, h])
            h_ref[j, rows, sl] = out.astype(h_ref.dtype)
        for j, _, h in now:
            dec = gates[j, cc][4][:, h:h + 1]
            c_cur[j, h] = dec * c_cur[j, h] + lax.dot_general(qkv(j, cc, h, 2), kws[j, cc, h].astype(BF16),
                                                              TN_DIMS, preferred_element_type=F32)
            n_cur[j, h] = dec * n_cur[j, h] + jnp.sum(kws[j, cc, h], axis=0, keepdims=True)
    for j in range(G):
        for h in range(M_HEADS):
            c_out[j, h] = c_cur[j, h]
            n_out[j, h:h + 1, :] = n_cur[j, h]

    if carry:
        @pl.when(ci == pl.num_programs(1) - 1)
        def _():
            c_ref[...] = c_in[...]
            n_ref[...] = n_in[...]
            m_ref[...] = m_in[...]


MLSTM_GROUP_CARRY = 2
MLSTM_CHUNKS_CARRY = 1
MLSTM_GROUP_SINGLE = 4


def _mlstm(qkv, gates, og, c0, n0, m0, gate_bias, mnorm, nb, L):
    t = qkv.shape[0]
    nc = t // (nb * L)
    carry = nc > 1
    G = MLSTM_GROUP_CARRY if carry else MLSTM_GROUP_SINGLE
    CH = MLSTM_CHUNKS_CARRY if carry else 1
    qkv, gates, og = (a.reshape(nb, nc * L, a.shape[1]) for a in (qkv, gates, og))
    tokmap = lambda b, c: (b, c, 0)
    const2 = lambda b, c: (0, 0)
    state_specs = [pl.BlockSpec((G, M_HEADS, M_DV, M_DK), lambda b, c: (b, 0, 0, 0)),
                   pl.BlockSpec((G, M_HEADS, M_DK), lambda b, c: (b, 0, 0)),
                   pl.BlockSpec((G, 1, LANES), lambda b, c: (b, 0, 0))]
    scratch = [pltpu.VMEM((G, M_HEADS, M_DV, M_DK), F32),
               pltpu.VMEM((G, M_HEADS, M_DK), F32),
               pltpu.VMEM((G, 1, LANES), F32)] if carry else []
    h, c, n, m = pl.pallas_call(
        functools.partial(_mlstm_kernel, L, G, CH, carry),
        grid=(nb // G, nc // CH),
        in_specs=[pl.BlockSpec((G, CH * L, 3072), tokmap),
                  pl.BlockSpec((G, CH * L, LANES), tokmap),
                  pl.BlockSpec((G, CH * L, 1024), tokmap)]
                 + state_specs
                 + [pl.BlockSpec((1, LANES), const2),
                    pl.BlockSpec((1, 1024), const2)],
        out_specs=[pl.BlockSpec((G, CH * L, 1024), tokmap)] + state_specs,
        out_shape=[jax.ShapeDtypeStruct((nb, nc * L, 1024), qkv.dtype),
                   jax.ShapeDtypeStruct((nb, M_HEADS, M_DV, M_DK), F32),
                   jax.ShapeDtypeStruct((nb, M_HEADS, M_DK), F32),
                   jax.ShapeDtypeStruct((nb, 1, LANES), F32)],
        scratch_shapes=scratch,
        compiler_params=_cparams(("arbitrary", "arbitrary")),
        name="mlstm_L%d" % L,
    )(qkv, gates, og, c0, n0, m0, gate_bias, mnorm)
    return h.reshape(t, 1024), c, n, m


def _swa_bias(m, has_prev):
    nkeys = 2 * A_BLOCK
    t_i = lax.broadcasted_iota(I32, (A_GROUP * m, nkeys), 0) & (m - 1)
    s_i = lax.broadcasted_iota(I32, (A_GROUP * m, nkeys), 1)
    diff = t_i + A_BLOCK - s_i
    mask = (diff >= 0) & (diff <= WINDOW)
    if not has_prev:
        mask = mask & (s_i >= A_BLOCK)
    return jnp.where(mask, 0.0, -jnp.inf).astype(F32)


def _swa_group_ones():
    nkeys = 2 * A_BLOCK
    r = lax.broadcasted_iota(I32, (A_KV_HEADS * nkeys, A_KV), 0) >> (nkeys.bit_length() - 1)
    c = lax.broadcasted_iota(I32, (A_KV_HEADS * nkeys, A_KV), 1) >> (A_HD.bit_length() - 1)
    return (r == c).astype(BF16)


def _swa_core(q, k_prev, v_prev, k_cur, v_cur, bias, group_ones, sink_ref, keys_on_lanes=False):
    m = q.shape[0]
    q = q.astype(BF16)
    qst = jnp.concatenate([q[:, j * A_KV:(j + 1) * A_KV] for j in range(A_GROUP)], axis=0)
    key_axis = 1 if keys_on_lanes else 0
    k_all = jnp.concatenate([k_prev, k_cur], axis=key_axis)
    v_all = jnp.concatenate([v_prev, v_cur], axis=key_axis)
    feat = lax.broadcasted_iota(I32, (A_KV, 1) if keys_on_lanes else (1, A_KV), 1 - key_axis)
    kbd, vbd = [], []
    for g in range(A_KV_HEADS):
        in_g = (feat >= g * A_HD) & (feat < (g + 1) * A_HD)
        kbd.append(jnp.where(in_g, k_all, 0.0).astype(BF16))
        vbd.append(jnp.where(in_g, v_all, 0.0).astype(BF16))
    kbd = jnp.concatenate(kbd, axis=key_axis)
    vbd = jnp.concatenate(vbd, axis=key_axis)
    if keys_on_lanes:
        s = jnp.dot(qst, kbd, preferred_element_type=F32)
    else:
        s = lax.dot_general(qst, kbd, NT_DIMS, preferred_element_type=F32)
    nkeys = 2 * A_BLOCK
    ps, sink_terms = [], []
    for g in range(A_KV_HEADS):
        sg = s[:, g * nkeys:(g + 1) * nkeys] + bias
        sink = jnp.concatenate(
            [jnp.full((m, LANES), sink_ref[A_GROUP * g + j], F32) for j in range(A_GROUP)], axis=0)
        mx = jnp.maximum(jnp.broadcast_to(jnp.max(sg, axis=1, keepdims=True), (A_GROUP * m, LANES)), sink)
        ps += [jnp.exp(sg[:, :LANES] - mx).astype(BF16), jnp.exp(sg[:, LANES:] - mx).astype(BF16)]
        sink_terms.append(jnp.exp(sink - mx))
    p = jnp.concatenate(ps, axis=1)
    if keys_on_lanes:
        o = lax.dot_general(p, vbd, NT_DIMS, preferred_element_type=F32)
    else:
        o = jnp.dot(p, vbd, preferred_element_type=F32)
    den = jnp.dot(p, group_ones, preferred_element_type=F32)
    low = lax.broadcasted_iota(I32, (1, LANES), 1) < A_HD
    den = den + jnp.concatenate([jnp.where(low, sink_terms[0], sink_terms[1]),
                                 jnp.where(low, sink_terms[2], sink_terms[3])], axis=1)
    o = o / den
    return jnp.concatenate([o[j * m:(j + 1) * m] for j in range(A_GROUP)], axis=1)


SWA_STEP_BLOCKS = 4
SWA_STEP = SWA_STEP_BLOCKS * A_BLOCK


def _swa_prompt_kernel(sink_ref, q_ref, kp_ref, kc_ref, vp_ref, vc_ref, o_ref, bias_s, bias0_s, ones_s):
    first = (pl.program_id(0) == 0) & (pl.program_id(1) == 0)

    @pl.when(first)
    def _():
        bias_s[...] = _swa_bias(A_BLOCK, True)
        bias0_s[...] = _swa_bias(A_BLOCK, False)
        ones_s[...] = _swa_group_ones()

    bias = bias_s[...]
    bias_first = jnp.where(pl.program_id(1) > 0, bias, bias0_s[...])
    ones = ones_s[...]
    k_prev, v_prev = kp_ref[...], vp_ref[...]
    for b in range(SWA_STEP_BLOCKS):
        rows = slice(b * A_BLOCK, (b + 1) * A_BLOCK)
        k_cur, v_cur = kc_ref[rows], vc_ref[rows]
        o_ref[rows] = _swa_core(q_ref[rows], k_prev, v_prev, k_cur, v_cur, bias_first if b == 0 else bias,
                                ones, sink_ref).astype(o_ref.dtype)
        k_prev, v_prev = k_cur, v_cur


def _swa_prompt(qa, ka, va, sinks, nb):
    t = qa.shape[0]
    nstep = t // (nb * SWA_STEP)
    cur = lambda b, i, s: (b * nstep + i, 0)
    prev = lambda b, i, s: (SWA_STEP_BLOCKS * (b * nstep + i) - jnp.minimum(i, 1), 0)
    return pl.pallas_call(
        _swa_prompt_kernel,
        grid_spec=pltpu.PrefetchScalarGridSpec(
            num_scalar_prefetch=1,
            grid=(nb, nstep),
            in_specs=[pl.BlockSpec((SWA_STEP, 1024), cur),
                      pl.BlockSpec((A_BLOCK, A_KV), prev),
                      pl.BlockSpec((SWA_STEP, A_KV), cur),
                      pl.BlockSpec((A_BLOCK, A_KV), prev),
                      pl.BlockSpec((SWA_STEP, A_KV), cur)],
            out_specs=pl.BlockSpec((SWA_STEP, 1024), cur),
            scratch_shapes=[pltpu.VMEM((A_GROUP * A_BLOCK, 2 * A_BLOCK), F32),
                            pltpu.VMEM((A_GROUP * A_BLOCK, 2 * A_BLOCK), F32),
                            pltpu.VMEM((A_KV_HEADS * 2 * A_BLOCK, A_KV), BF16)]),
        out_shape=jax.ShapeDtypeStruct((t, 1024), BF16),
        compiler_params=_cparams(("arbitrary", "arbitrary")),
        name="swa_prompt",
    )(sinks, qa, ka, ka, va, va)


SWA_SAMPLE_GROUP = 8


def _swa_sample_kernel(T, sink_ref, q_ref, kn_ref, vn_ref, kb_ref, vb_ref, o_ref, ko_ref, vo_ref):
    pad = jnp.zeros((A_BLOCK - T, A_KV), F32)
    bias = _swa_bias(T, True)
    ones = _swa_group_ones()
    lane = lax.broadcasted_iota(I32, (1, WINDOW), 1)
    for j in range(SWA_SAMPLE_GROUP):
        rows = slice(j * T, (j + 1) * T)
        k_new_t = jnp.concatenate([kn_ref[rows], pad], axis=0).T
        v_new_t = jnp.concatenate([vn_ref[rows], pad], axis=0).T
        k_buf_t = kb_ref[j]
        v_buf_t = vb_ref[j]
        o_ref[rows] = _swa_core(q_ref[rows], k_buf_t, v_buf_t, k_new_t, v_new_t, bias, ones, sink_ref,
                                keys_on_lanes=True).astype(o_ref.dtype)
        keep = lane < WINDOW - T
        ko_ref[j] = jnp.where(keep, pltpu.roll(k_buf_t, WINDOW - T, axis=1), pltpu.roll(k_new_t, WINDOW - T, axis=1))
        vo_ref[j] = jnp.where(keep, pltpu.roll(v_buf_t, WINDOW - T, axis=1), pltpu.roll(v_new_t, WINDOW - T, axis=1))


def _swa_sample(qa, ka, va, k_buf, v_buf, sinks, T):
    nb = k_buf.shape[0]
    g = SWA_SAMPLE_GROUP
    tok = lambda b, s: (b, 0)
    buf = lambda b, s: (b, 0, 0)
    return pl.pallas_call(
        functools.partial(_swa_sample_kernel, T),
        grid_spec=pltpu.PrefetchScalarGridSpec(
            num_scalar_prefetch=1,
            grid=(nb // g,),
            in_specs=[pl.BlockSpec((g * T, 1024), tok),
                      pl.BlockSpec((g * T, A_KV), tok),
                      pl.BlockSpec((g * T, A_KV), tok),
                      pl.BlockSpec((g, A_KV, WINDOW), buf),
                      pl.BlockSpec((g, A_KV, WINDOW), buf)],
            out_specs=[pl.BlockSpec((g * T, 1024), tok),
                       pl.BlockSpec((g, A_KV, WINDOW), buf),
                       pl.BlockSpec((g, A_KV, WINDOW), buf)]),
        out_shape=[jax.ShapeDtypeStruct((nb * T, 1024), qa.dtype),
                   jax.ShapeDtypeStruct((nb, A_KV, WINDOW), F32),
                   jax.ShapeDtypeStruct((nb, A_KV, WINDOW), F32)],
        compiler_params=_cparams(("arbitrary",)),
        name="swa_sample",
    )(sinks, qa, ka, va, k_buf, v_buf)


def _merge_kernel(x_ref, hm_ref, ha_ref, gm_ref, ga_ref, wpm_ref, wpa_ref, wo_ref, g2_ref, wr_ref, br_ref,
                  x1_ref, xn_ref, idx_ref, gate_ref, cnt_ref):
    pm = jnp.dot(hm_ref[...].astype(BF16), wpm_ref[...], preferred_element_type=F32)
    pa = jnp.dot(ha_ref[...].astype(BF16), wpa_ref[...], preferred_element_type=F32)
    mixed = gm_ref[...].astype(F32) * pm + ga_ref[...].astype(F32) * pa
    x1 = x_ref[...] + jnp.dot(mixed.astype(BF16), wo_ref[...], preferred_element_type=F32)
    x1_ref[...] = x1
    xn = _rms(x1, g2_ref[...])
    xn_ref[...] = xn.astype(BF16)
    xn_hi = xn.astype(BF16)
    xn_lo = (xn - xn_hi.astype(F32)).astype(BF16)
    part = jnp.dot(xn_hi, wr_ref[...], preferred_element_type=F32)
    logits = (part[:, :LANES] + part[:, LANES:]
              + jnp.dot(xn_lo, wr_ref[:, :LANES], preferred_element_type=F32) + br_ref[...])
    tm = logits.shape[0]
    cur = logits.T[0:N_EXPERTS]
    eid = lax.broadcasted_iota(I32, (N_EXPERTS, tm), 0).astype(F32)
    vals, idxs = [], []
    onehot = jnp.zeros((N_EXPERTS, tm), F32)
    for _ in range(TOP_K):
        mval = jnp.max(cur, axis=0, keepdims=True)
        sel = jnp.min(jnp.where(cur == mval, eid, float(N_EXPERTS)), axis=0, keepdims=True)
        hit = eid == sel
        onehot = onehot + hit.astype(F32)
        cur = jnp.where(hit, -jnp.inf, cur)
        vals.append(mval)
        idxs.append(sel)
    es = [jnp.exp(v - vals[0]) for v in vals]
    tot = es[0] + es[1] + es[2] + es[3]
    row = lax.broadcasted_iota(I32, (SUBLANES, tm), 0)
    res = jnp.zeros((SUBLANES, tm), F32)
    for k in range(TOP_K):
        res = jnp.where(row == k, idxs[k], res)
        res = jnp.where(row == TOP_K + k, es[k] / tot, res)
    res_t = jnp.concatenate([res, jnp.zeros((LANES - SUBLANES, tm), F32)], axis=0).T
    idx_ref[...] = res_t[:, 0:TOP_K].astype(I32)
    gate_ref[...] = res_t[:, TOP_K:2 * TOP_K]
    for j in range(tm // TOK_TILE):
        cnt_ref[j] = jnp.sum(onehot[:, j * TOK_TILE:(j + 1) * TOK_TILE], axis=1, keepdims=True).astype(I32)


def _merge(x, hm, ha, og, wpm, wpa, wo, g2, wr, br):
    t = x.shape[0]
    tm = MERGE_TILE
    row = lambda n: pl.BlockSpec((tm, n), lambda i: (i, 0))
    const = lambda r, c: pl.BlockSpec((r, c), lambda i: (0, 0))
    return pl.pallas_call(
        _merge_kernel,
        grid=(t // tm,),
        in_specs=[row(1024), row(1024), row(1024),
                  pl.BlockSpec((tm, 1024), lambda i: (i, 1)),
                  pl.BlockSpec((tm, 1024), lambda i: (i, 2)),
                  const(1024, 1024), const(1024, 1024), const(1024, 1024), const(1, 1024),
                  const(1024, 2 * LANES), const(1, LANES)],
        out_specs=[row(1024), row(1024), row(TOP_K), row(TOP_K),
                   pl.BlockSpec((tm // TOK_TILE, N_EXPERTS, 1), lambda i: (i, 0, 0))],
        out_shape=[jax.ShapeDtypeStruct((t, 1024), F32),
                   jax.ShapeDtypeStruct((t, 1024), BF16),
                   jax.ShapeDtypeStruct((t, TOP_K), I32),
                   jax.ShapeDtypeStruct((t, TOP_K), F32),
                   jax.ShapeDtypeStruct((t // TOK_TILE, N_EXPERTS, 1), I32)],
        compiler_params=_cparams(("arbitrary",)),
        name="merge_route",
    )(x, hm, ha, og, og, wpm, wpa, wo, g2, wr, br)


def _rows_to_tiles(ref, lead, x):
    chunks = jnp.stack([x[:, s * LANES:(s + 1) * LANES] for s in range(ROW_TILE[0])], axis=0)
    ref[lead] = jnp.swapaxes(chunks, 0, 1).astype(ref.dtype)


def _tiles_to_rows(ref, lead, rows=slice(None)):
    chunks = jnp.swapaxes(ref[lead + (rows,)].astype(F32), 0, 1)
    return jnp.concatenate([chunks[s] for s in range(ROW_TILE[0])], axis=1)


def _stage_rows(idx, off_row):
    tm = idx.shape[0]
    lane = lax.broadcasted_iota(I32, (tm, LANES), 1)
    hits = [lane == idx[:, k:k + 1] for k in range(TOP_K)]
    onehot = sum(h.astype(F32) for h in hits)
    r_i = lax.broadcasted_iota(I32, (tm, tm), 0)
    c_i = lax.broadcasted_iota(I32, (tm, tm), 1)
    before = (c_i < r_i).astype(BF16)
    rank = jnp.dot(before, onehot.astype(BF16), preferred_element_type=F32)
    pos = rank + off_row
    return [jnp.sum(jnp.where(h, pos, 0.0), axis=1, keepdims=True) for h in hits]


def _dispatch_kernel(ntile_a, cnt_ref, off_ref, base_ref, tail_ref, xa_ref, xb_ref, ia_ref, ib_ref, offv_ref,
                     xs_ref, stage, zeros, sem, zsem):
    i = pl.program_id(0)
    from_a = i < ntile_a
    xn = jnp.where(from_a, xa_ref[...], xb_ref[...])
    idx = jnp.where(from_a, ia_ref[...], ib_ref[...])
    rows = _stage_rows(idx, offv_ref[0])
    tm = idx.shape[0]
    lane = lax.broadcasted_iota(I32, (tm, LANES), 1)
    r4 = jnp.zeros((tm, LANES), F32)
    for k in range(TOP_K):
        r4 = jnp.where(lane == k, rows[k], r4)
    r4t = r4.T.astype(I32)
    r_iota = lax.broadcasted_iota(I32, (STAGE_ROWS, tm), 0)
    sel = r_iota == r4t[0:1, :]
    for k in range(1, TOP_K):
        sel = sel | (r_iota == r4t[k:k + 1, :])
    sel = jnp.where(sel, 1.0, 0.0).astype(BF16)
    staged = jnp.dot(sel, xn, preferred_element_type=F32)

    def runs(tile, slot):
        out = []
        for e in range(N_EXPERTS):
            n = cnt_ref[tile * N_EXPERTS + e]
            off = off_ref[tile * N_EXPERTS + e]
            base = base_ref[tile * N_EXPERTS + e]
            out.append((n, pltpu.make_async_copy(stage.at[slot, pl.ds(off, n)], xs_ref.at[pl.ds(base, n)],
                                                 sem.at[slot])))
        return out

    def wait_runs(slot):
        pltpu.make_async_copy(stage.at[slot], xs_ref.at[pl.ds(0, STAGE_ROWS)], sem.at[slot]).wait()

    for slot in range(2):
        @pl.when((i & 1) == slot)
        def _():
            @pl.when(i >= 2)
            def _():
                wait_runs(slot)
            _rows_to_tiles(stage, (slot,), staged)
            for n, cp in runs(i, slot):
                @pl.when(n > 0)
                def _():
                    cp.start()

    last = pl.num_programs(0) - 1

    @pl.when(i == last)
    def _():
        for slot in range(2):
            @pl.when(((last & 1) == slot) | (last >= 1))
            def _():
                wait_runs(slot)
        zeros[...] = jnp.zeros(zeros.shape, zeros.dtype)
        tails = []
        for e in range(N_EXPERTS):
            start = tail_ref[e]
            n = tail_ref[N_EXPERTS + e]
            tails.append((n, pltpu.make_async_copy(zeros.at[pl.ds(0, n)], xs_ref.at[pl.ds(start, n)], zsem)))
        for n, cp in tails:
            @pl.when(n > 0)
            def _():
                cp.start()
        for n, cp in tails:
            @pl.when(n > 0)
            def _():
                cp.wait()
        first = tail_ref[2 * N_EXPERTS]
        n_unused = tail_ref[2 * N_EXPERTS + 1]

        def unused_block(j):
            return pltpu.make_async_copy(zeros, xs_ref.at[pl.ds((first + j) * MOE_BLOCK, MOE_BLOCK)], zsem)

        @pl.loop(0, n_unused)
        def _(j):
            unused_block(j).start()

        @pl.loop(0, n_unused)
        def _(j):
            unused_block(j).wait()


def _dispatch(xa, xb, ia, ib, cnt, off, base, tails, offv, n_slots):
    tm = TOK_TILE
    ntile_a = xa.shape[0] // tm
    ntile_b = xb.shape[0] // tm
    amap = lambda i, *_: (jnp.minimum(i, ntile_a - 1), 0)
    bmap = lambda i, *_: (jnp.maximum(i - ntile_a, 0), 0)
    return pl.pallas_call(
        functools.partial(_dispatch_kernel, ntile_a),
        grid_spec=pltpu.PrefetchScalarGridSpec(
            num_scalar_prefetch=4,
            grid=(ntile_a + ntile_b,),
            in_specs=[pl.BlockSpec((tm, 1024), amap),
                      pl.BlockSpec((tm, 1024), bmap),
                      pl.BlockSpec((tm, TOP_K), amap),
                      pl.BlockSpec((tm, TOP_K), bmap),
                      pl.BlockSpec((1, 1, LANES), lambda i, *_: (i, 0, 0))],
            out_specs=pl.BlockSpec(memory_space=pl.ANY),
            scratch_shapes=[pltpu.VMEM((2, STAGE_ROWS) + ROW_TILE, ROW_DTYPE),
                            pltpu.VMEM((MOE_BLOCK,) + ROW_TILE, ROW_DTYPE),
                            pltpu.SemaphoreType.DMA((2,)),
                            pltpu.SemaphoreType.DMA]),
        out_shape=jax.ShapeDtypeStruct((n_slots,) + ROW_TILE, ROW_DTYPE),
        compiler_params=_cparams(("arbitrary",)),
        name="moe_dispatch",
    )(cnt, off, base, tails, xa, xb, ia, ib, offv)


MOE_HALF = MOE_BLOCK // 2


def _experts_kernel(be_ref, rows_ref, next_ref, slot_ref, xs_ref, wgu_hbm, bgu_ref, wd_hbm, bd_ref, y_ref,
                    wgu_f, wd_f, wgu_s, wd_s, sem):
    i = pl.program_id(0)
    rows = rows_ref[i]
    live = rows > 0
    run_start = (i == 0) | (be_ref[i] != be_ref[jnp.maximum(i - 1, 0)])

    def fetch(expert, slot):
        return (pltpu.make_async_copy(wgu_hbm.at[expert], wgu_f.at[slot], sem.at[0, slot]),
                pltpu.make_async_copy(wd_hbm.at[expert], wd_f.at[slot], sem.at[1, slot]))

    for slot in range(2):
        @pl.when(live & run_start & (slot_ref[i] == slot))
        def _():
            @pl.when(i == 0)
            def _():
                for cp in fetch(be_ref[i], slot):
                    cp.start()
            for cp in fetch(be_ref[i], slot):
                cp.wait()

            @pl.when(next_ref[i] >= 0)
            def _():
                for cp in fetch(next_ref[i], 1 - slot):
                    cp.start()
            wgu_s[...] = wgu_f[slot].astype(BF16)
            wd_s[...] = wd_f[slot].astype(BF16)

    def ffn(x):
        h = jnp.dot(x.astype(BF16), wgu_s[...], preferred_element_type=F32) + bgu_ref[0]
        gate = jnp.minimum(h[:, :D_FF], SWIGLU_LIMIT)
        up = jnp.clip(h[:, D_FF:], -SWIGLU_LIMIT, SWIGLU_LIMIT)
        act = gate * jax.nn.sigmoid(SWIGLU_ALPHA * gate) * (up + 1.0)
        return jnp.dot(act.astype(BF16), wd_s[...], preferred_element_type=F32) + bd_ref[0]

    @pl.when(rows > MOE_HALF)
    def _():
        _rows_to_tiles(y_ref, (), ffn(_tiles_to_rows(xs_ref, ())))

    @pl.when(live & (rows <= MOE_HALF))
    def _():
        y_half = ffn(_tiles_to_rows(xs_ref, (), slice(0, MOE_HALF)))
        _rows_to_tiles(y_ref, (), jnp.concatenate([y_half, jnp.zeros((MOE_HALF, D_MODEL), F32)], axis=0))

    @pl.when(rows == 0)
    def _():
        y_ref[...] = jnp.zeros(y_ref.shape, y_ref.dtype)


def _experts(block_expert, block_rows, next_expert, slot, xs, wgu, bgu, wd, bd):
    nb_max = xs.shape[0] // MOE_BLOCK
    blk = lambda i, be, rows, *_: (jnp.where(rows[i] > 0, i, 0), 0, 0)
    wmap = lambda i, be, *_: (be[i], 0, 0)
    return pl.pallas_call(
        _experts_kernel,
        grid_spec=pltpu.PrefetchScalarGridSpec(
            num_scalar_prefetch=4,
            grid=(nb_max,),
            in_specs=[pl.BlockSpec((MOE_BLOCK,) + ROW_TILE, blk),
                      pl.BlockSpec(memory_space=pl.ANY),
                      pl.BlockSpec((1, 1, 2 * D_FF), wmap),
                      pl.BlockSpec(memory_space=pl.ANY),
                      pl.BlockSpec((1, 1, D_MODEL), wmap)],
            out_specs=pl.BlockSpec((MOE_BLOCK,) + ROW_TILE, lambda i, *_: (i, 0, 0)),
            scratch_shapes=[pltpu.VMEM((2, D_MODEL, 2 * D_FF), F32),
                            pltpu.VMEM((2, D_FF, D_MODEL), F32),
                            pltpu.VMEM((D_MODEL, 2 * D_FF), BF16),
                            pltpu.VMEM((D_FF, D_MODEL), BF16),
                            pltpu.SemaphoreType.DMA((2, 2))]),
        out_shape=jax.ShapeDtypeStruct(xs.shape, ROW_DTYPE),
        compiler_params=_cparams(("arbitrary",)),
        name="moe_experts",
    )(block_expert, block_rows, next_expert, slot, xs, wgu, bgu, wd, bd)


def _combine_kernel(cnt_ref, off_ref, base_ref, x1_ref, idx_ref, gate_ref, offv_ref, gf_ref, y_ref,
                    o_ref, stage, sem):
    i = pl.program_id(0)
    last = pl.num_programs(0) - 1

    def runs(tile, slot):
        out = []
        for e in range(N_EXPERTS):
            n = cnt_ref[tile * N_EXPERTS + e]
            off = off_ref[tile * N_EXPERTS + e]
            base = base_ref[tile * N_EXPERTS + e]
            out.append((n, pltpu.make_async_copy(y_ref.at[pl.ds(base, n)], stage.at[slot, pl.ds(off, n)],
                                                 sem.at[slot])))
        return out

    def start_runs(tile, slot):
        for n, cp in runs(tile, slot):
            @pl.when(n > 0)
            def _():
                cp.start()

    @pl.when(i == 0)
    def _():
        start_runs(0, 0)

    for slot in range(2):
        @pl.when(((i & 1) != slot) & (i < last))
        def _():
            start_runs(i + 1, slot)

    rows = _stage_rows(idx_ref[...], offv_ref[0])
    tm = idx_ref.shape[0]
    lane = lax.broadcasted_iota(I32, (tm, STAGE_ROWS), 1)
    gates = gate_ref[...]
    gmat = jnp.zeros((tm, STAGE_ROWS), F32)
    for k in range(TOP_K):
        gmat = gmat + jnp.where(lane == rows[k].astype(I32), gates[:, k:k + 1], 0.0)
    gmat = gmat.astype(BF16)
    for slot in range(2):
        @pl.when((i & 1) == slot)
        def _():
            pltpu.make_async_copy(y_ref.at[pl.ds(0, STAGE_ROWS)], stage.at[slot], sem.at[slot]).wait()
            moe = jnp.dot(gmat, _tiles_to_rows(stage, (slot,)).astype(BF16), preferred_element_type=F32)
            o_ref[...] = _rms(x1_ref[...] + moe, gf_ref[...])


def _combine(x1, idx, gates, cnt, off, base, offv, gf, y):
    t = x1.shape[0]
    tm = TOK_TILE
    return pl.pallas_call(
        _combine_kernel,
        grid_spec=pltpu.PrefetchScalarGridSpec(
            num_scalar_prefetch=3,
            grid=(t // tm,),
            in_specs=[pl.BlockSpec((tm, 1024), lambda i, *_: (i, 0)),
                      pl.BlockSpec((tm, TOP_K), lambda i, *_: (i, 0)),
                      pl.BlockSpec((tm, TOP_K), lambda i, *_: (i, 0)),
                      pl.BlockSpec((1, 1, LANES), lambda i, *_: (i, 0, 0)),
                      pl.BlockSpec((1, 1024), lambda i, *_: (0, 0)),
                      pl.BlockSpec(memory_space=pl.ANY)],
            out_specs=pl.BlockSpec((tm, 1024), lambda i, *_: (i, 0)),
            scratch_shapes=[pltpu.VMEM((2, STAGE_ROWS) + ROW_TILE, ROW_DTYPE),
                            pltpu.SemaphoreType.DMA((2,))]),
        out_shape=jax.ShapeDtypeStruct((t, 1024), F32),
        compiler_params=_cparams(("arbitrary",)),
        name="moe_combine",
    )(cnt, off, base, x1, idx, gates, offv, gf, y)


def _prep_w_in(w):
    wt = w.T
    c_if, c_qa, c_kv, c_g = 4096, 4104, 5128, 5640
    wm = wt[:c_if].astype(BF16)
    wif = jnp.pad(wt[c_if:c_qa], ((0, LANES - 2 * M_HEADS), (0, 0))).astype(BF16)
    wqa = (jnp.concatenate([wt[c_qa + (A_GROUP * g + j) * A_HD:c_qa + (A_GROUP * g + j + 1) * A_HD]
                            for j in range(A_GROUP) for g in range(A_KV_HEADS)], axis=0)
           * (A_HD ** -0.5)).astype(BF16)
    wkv = wt[c_kv:c_g].astype(BF16)
    wg = wt[c_g:].astype(BF16)
    return wm, wg, wqa, wkv, wif


def _rope_tables(pos):
    half = ROT_DIM // 2
    inv = ROPE_THETA ** (-(jnp.arange(half, dtype=F32) * 2.0 / ROT_DIM))
    ang = pos.astype(F32)[:, None] * inv[None, :]
    lane = jnp.arange(LANES) % A_HD
    hit = (lane[None, :] % half == jnp.arange(half)[:, None]) & (lane[None, :] < ROT_DIM)
    spread_cos = hit.astype(F32)
    spread_sin = jnp.where(hit, jnp.where(lane < half, -1.0, 1.0)[None, :], 0.0)
    cos_t = jnp.dot(jnp.cos(ang), spread_cos, precision=lax.Precision.HIGHEST) + (lane >= ROT_DIM).astype(F32)[None, :]
    sin_t = jnp.dot(jnp.sin(ang), spread_sin, precision=lax.Precision.HIGHEST)
    return cos_t, sin_t


def _round_up(x, m):
    return (x + m - 1) // m * m


def kernel(x_prompt, x_sample, cache_k, cache_v, state_c, state_n, state_m, norm1, w_in, b_igate, b_fgate,
           mlstm_norm, w_proj_m, w_proj_a, attn_sinks, w_out, norm2, w_router, b_router, w_gate_up, b_gate_up,
           w_down, b_down, norm_f):
    bp, sp, _ = x_prompt.shape
    bs, ts, _ = x_sample.shape
    past_len = 16384
    l = 0
    w1 = _prep_w_in(w_in[l])
    g1 = norm1[l][None, :]
    g2 = norm2[l][None, :]
    gf = norm_f[None, :]
    gate_bias = jnp.pad(jnp.concatenate([b_igate[l], b_fgate[l]]), (0, LANES - 2 * M_HEADS))[None, :]
    mn = mlstm_norm[l][None, :]
    wpm = w_proj_m[l].astype(BF16)
    wpa = jnp.concatenate([w_proj_a[l][(A_GROUP * g + j) * A_HD:(A_GROUP * g + j + 1) * A_HD]
                           for j in range(A_GROUP) for g in range(A_KV_HEADS)], axis=0).astype(BF16)
    wo = w_out[l].astype(BF16)
    wr_f = jnp.pad(w_router[l], ((0, 0), (0, LANES - N_EXPERTS)))
    wr_hi = wr_f.astype(BF16)
    wr = jnp.concatenate([wr_hi, (wr_f - wr_hi.astype(F32)).astype(BF16)], axis=1)
    br = jnp.pad(b_router[l], (0, LANES - N_EXPERTS))[None, :]
    wgu = w_gate_up[l]
    bgu = b_gate_up[l][:, None, :]
    wd = w_down[l]
    bd = b_down[l][:, None, :]
    sinks = attn_sinks[l]

    def mixer(x, nb, seq, pos0, L, c0, n0, m0, k_buf, v_buf, tm):
        t = nb * seq
        xf = x.reshape(t, D_MODEL)
        cos_t, sin_t = _rope_tables(pos0 + jnp.arange(seq))
        if seq < tm:
            cos_t, sin_t = jnp.tile(cos_t, (tm // seq, 1)), jnp.tile(sin_t, (tm // seq, 1))
        act = BF16 if L % 16 == 0 else F32
        qkv, og, qa, ka, va, gates = _inproj(xf, g1, w1, cos_t, sin_t, tm, max(seq // tm, 1), act)
        m0p = jnp.pad(m0, ((0, 0), (0, LANES - M_HEADS)))[:, None, :]
        hm, c, n, m = _mlstm(qkv, gates, og, c0, n0, m0p, gate_bias, mn, nb, L)
        if k_buf is None:
            ha = _swa_prompt(qa, ka, va, sinks, nb)
            k_keep = ka.reshape(nb, seq, A_KV)[:, seq - WINDOW:].reshape(nb, WINDOW, A_KV_HEADS, A_HD)
            v_keep = va.reshape(nb, seq, A_KV)[:, seq - WINDOW:].reshape(nb, WINDOW, A_KV_HEADS, A_HD)
        else:
            to_t = lambda a: a.transpose(0, 2, 3, 1).reshape(nb, A_KV, WINDOW)
            from_t = lambda a: a.reshape(nb, A_KV_HEADS, A_HD, WINDOW).transpose(0, 3, 1, 2)
            ha, k_keep, v_keep = _swa_sample(qa, ka, va, to_t(k_buf), to_t(v_buf), sinks, seq)
            k_keep, v_keep = from_t(k_keep), from_t(v_keep)
        x1, xn, idx, gate, cnt = _merge(xf, hm, ha, og, wpm, wpa, wo, g2, wr, br)
        return (x1, xn, idx, gate, cnt[:, :, 0]), (k_keep, v_keep, c, n, m[:, 0, :M_HEADS])

    zc = jnp.zeros((bp, M_HEADS, M_DV, M_DK), F32)
    zn = jnp.zeros((bp, M_HEADS, M_DK), F32)
    zm = jnp.zeros((bp, M_HEADS), F32)
    rp, sp_out = mixer(x_prompt, bp, sp, 0, M_CHUNK, zc, zn, zm, None, None, 512)
    rs, ss_out = mixer(x_sample, bs, ts, past_len, ts, state_c[l], state_n[l], state_m[l],
                       cache_k[l], cache_v[l], 256)

    cnt = jnp.concatenate([rp[4], rs[4]], axis=0)
    ntile_p = rp[4].shape[0]
    off = jnp.cumsum(cnt, axis=1) - cnt
    per_expert = jnp.sum(cnt, axis=0)
    padded = _round_up(per_expert, MOE_BLOCK)
    padded_end = jnp.cumsum(padded)
    expert_start = padded_end - padded
    base = expert_start[None, :] + jnp.cumsum(cnt, axis=0) - cnt
    n_tok = bp * sp + bs * ts
    n_slots = _round_up(n_tok * TOP_K, MOE_BLOCK) + N_EXPERTS * MOE_BLOCK
    nb_max = n_slots // MOE_BLOCK
    block_row = jnp.arange(nb_max, dtype=I32) * MOE_BLOCK
    block_expert = jnp.minimum(jnp.sum(block_row[:, None] >= padded_end[None, :], axis=1), N_EXPERTS - 1).astype(I32)
    n_blocks = (padded_end[-1] // MOE_BLOCK).astype(I32)[None]
    eid = jnp.arange(N_EXPERTS, dtype=I32)
    later_used = (eid[None, :] > eid[:, None]) & (padded[None, :] > 0)
    next_used = jnp.min(jnp.where(later_used, eid[None, :], N_EXPERTS), axis=1)
    next_used = jnp.where(next_used < N_EXPERTS, next_used, -1).astype(I32)
    run_parity = ((jnp.cumsum(padded > 0) - 1) & 1).astype(I32)
    of_block = lambda table: jnp.sum(jnp.where(block_expert[:, None] == eid[None, :], table[None, :], 0),
                                     axis=1).astype(I32)
    block_rows = jnp.clip(of_block(per_expert) - (block_row - of_block(expert_start)), 0, MOE_BLOCK).astype(I32)
    tails = jnp.concatenate([expert_start + per_expert, padded - per_expert,
                             n_blocks, nb_max - n_blocks]).astype(I32)
    flat = lambda a: a.reshape(-1).astype(I32)
    offv = jnp.pad(off, ((0, 0), (0, LANES - N_EXPERTS))).astype(F32)[:, None, :]

    xs = _dispatch(rp[1], rs[1], rp[2], rs[2], flat(cnt), flat(off), flat(base), tails, offv, n_slots)
    y = _experts(block_expert, block_rows, of_block(next_used), of_block(run_parity), xs, wgu, bgu, wd, bd)
    y_p = _combine(rp[0], rp[2], rp[3], flat(cnt[:ntile_p]), flat(off[:ntile_p]), flat(base[:ntile_p]),
                   offv[:ntile_p], gf, y)
    y_s = _combine(rs[0], rs[2], rs[3], flat(cnt[ntile_p:]), flat(off[ntile_p:]), flat(base[ntile_p:]),
                   offv[ntile_p:], gf, y)

    kp, vp, cp_, np_, mp = sp_out
    ks, vs, cs, ns, ms = ss_out
    return (y_p.reshape(bp, sp, D_MODEL), y_s.reshape(bs, ts, D_MODEL),
            kp[None], vp[None], cp_[None], np_[None], mp[None],
            ks[None], vs[None], cs[None], ns[None], ms[None])
```

```python
import functools

import jax
import jax.numpy as jnp
from jax import lax
from jax.experimental import pallas as pl
from jax.experimental.pallas import tpu as pltpu

F32 = jnp.float32
BF16 = jnp.bfloat16
I32 = jnp.int32

D_MODEL = 1024
M_HEADS = 4
M_DK = 256
M_DV = 256
M_CHUNK = 128
A_Q_HEADS = 16
A_KV_HEADS = 4
A_GROUP = 4
A_HD = 64
A_KV = A_KV_HEADS * A_HD
WINDOW = 128
A_BLOCK = 128
ROT_DIM = 16
ROPE_THETA = 500000.0
N_EXPERTS = 32
TOP_K = 4
D_FF = 1024
SWIGLU_LIMIT = 7.0
SWIGLU_ALPHA = 1.702
NORM_EPS = 1e-6

LANES = 128
SUBLANES = 8
VMEM_LIMIT = 56 * 1024 * 1024

MERGE_TILE = 512
TOK_TILE = 256
STAGE_ROWS = 4 * TOK_TILE
ROW_TILE = (8, 128)
ROW_DTYPE = F32
MOE_BLOCK = 512


def _cparams(sem):
    return pltpu.CompilerParams(dimension_semantics=sem, vmem_limit_bytes=VMEM_LIMIT)


def _rms(x, g):
    return x * lax.rsqrt(jnp.mean(x * x, axis=-1, keepdims=True) + NORM_EPS) * g


def _rope(x, cos_t, sin_t, first_half):
    n = x.shape[1]
    fwd = pltpu.roll(x, n - ROT_DIM // 2, axis=1)
    bwd = pltpu.roll(x, ROT_DIM // 2, axis=1)
    outs = []
    for b in range(n // LANES):
        sl = slice(b * LANES, (b + 1) * LANES)
        partner = jnp.where(first_half, fwd[:, sl], bwd[:, sl])
        outs.append(x[:, sl] * cos_t + partner * sin_t)
    return jnp.concatenate(outs, axis=1)


def _inproj_kernel(x_ref, g1_ref, wm_ref, wg_ref, wqa_ref, wkv_ref, wif_ref, cos_ref, sin_ref,
                   qkv_ref, og_ref, qa_ref, ka_ref, va_ref, gate_ref):
    x = x_ref[...]
    xn = _rms(x, g1_ref[...]).astype(BF16)
    act = qkv_ref.dtype

    def proj(w_ref, c0, n):
        return lax.dot_general(xn, w_ref[c0:c0 + n, :], (((1,), (1,)), ((), ())), preferred_element_type=F32)

    qkv_ref[:, 0:1024] = proj(wm_ref, 0, 1024).astype(act)
    qkv_ref[:, 1024:2048] = (proj(wm_ref, 1024, 1024) * (M_DK ** -0.5)).astype(act)
    qkv_ref[:, 2048:3072] = proj(wm_ref, 2048, 1024).astype(act)
    og_ref[:, 0:1024] = jax.nn.sigmoid(proj(wm_ref, 3072, 1024)).astype(act)
    og_ref[:, 1024:2048] = jax.nn.sigmoid(proj(wg_ref, 0, 1024)).astype(act)
    og_ref[:, 2048:3072] = jax.nn.sigmoid(proj(wg_ref, 1024, 1024)).astype(act)
    cos_t = cos_ref[...]
    sin_t = sin_ref[...]
    lane = lax.broadcasted_iota(I32, (1, LANES), 1)
    first_half = (lane & (ROT_DIM - 1)) < (ROT_DIM // 2)
    qa_ref[...] = _rope(proj(wqa_ref, 0, 1024), cos_t, sin_t, first_half).astype(act)
    ka_ref[...] = _rope(proj(wkv_ref, 0, A_KV), cos_t, sin_t, first_half)
    va_ref[...] = proj(wkv_ref, A_KV, A_KV)
    gate_ref[...] = proj(wif_ref, 0, LANES)


def _inproj(x, g1, weights, cos_t, sin_t, tm, rope_blocks, act):
    t = x.shape[0]
    tok = lambda n, dt: jax.ShapeDtypeStruct((t, n), dt)
    row = lambda n: pl.BlockSpec((tm, n), lambda i: (i, 0))
    resident = lambda w: pl.BlockSpec(w.shape, lambda i: (0, 0), pipeline_mode=pl.Buffered(1))
    return pl.pallas_call(
        _inproj_kernel,
        grid=(t // tm,),
        in_specs=[row(D_MODEL), pl.BlockSpec((1, D_MODEL), lambda i: (0, 0))]
                 + [resident(w) for w in weights]
                 + [pl.BlockSpec((tm, LANES), lambda i: (i % rope_blocks, 0)),
                    pl.BlockSpec((tm, LANES), lambda i: (i % rope_blocks, 0))],
        out_specs=[row(3072), row(3072), row(1024), row(A_KV), row(A_KV), row(LANES)],
        out_shape=[tok(3072, act), tok(3072, act), tok(1024, act), tok(A_KV, F32), tok(A_KV, F32),
                   tok(LANES, F32)],
        compiler_params=_cparams(("arbitrary",)),
        name="inproj",
    )(x, g1, *weights, cos_t, sin_t)


def _mlstm_gates(gc, bias, m_prev, L):
    z = gc + bias
    lf = jnp.minimum(z, 0.0) - jnp.log1p(jnp.exp(-jnp.abs(z)))
    row = lax.broadcasted_iota(I32, (L, LANES), 0)
    fc = lf
    sh = 1
    while sh < L:
        fc = fc + jnp.where(row >= sh, pltpu.roll(fc, sh, axis=0), 0.0)
        sh *= 2
    fcum = pltpu.roll(fc, LANES - M_HEADS, axis=1)
    a = z - fcum
    cmx = a
    sh = 1
    while sh < L:
        cmx = jnp.maximum(cmx, jnp.where(row >= sh, pltpu.roll(cmx, sh, axis=0), -jnp.inf))
        sh *= 2
    mx = jnp.maximum(m_prev, cmx)
    inter = jnp.exp(m_prev - mx)
    f_end = fcum[L - 1:L]
    m_end = f_end + mx[L - 1:L]
    decay = jnp.exp(f_end + m_prev - m_end)
    wsrc = jnp.exp(a + (f_end - m_end))
    return a, mx, inter, wsrc, decay, m_end


NT_DIMS = (((1,), (1,)), ((), ()))
TN_DIMS = (((0,), (0,)), ((), ()))


def _mlstm_decay(h, L, gates):
    a, mx = gates[0], gates[1]
    t_i = lax.broadcasted_iota(I32, (L, L), 0)
    s_i = lax.broadcasted_iota(I32, (L, L), 1)
    a_row = jnp.sum(jnp.where(t_i == s_i, a[:, h:h + 1], 0.0), axis=0, keepdims=True)
    return jnp.exp(jnp.where(s_i <= t_i, a_row - mx[:, h:h + 1], -jnp.inf))


def _mlstm_output(h, q, og, mn, gates, qk_sum, pv, cq, n_h):
    inter_col = gates[2][:, h:h + 1]
    num = inter_col * cq + pv
    den = inter_col * jnp.sum(q.astype(F32) * n_h, axis=1, keepdims=True) + qk_sum
    hh = num / jnp.maximum(jnp.abs(den), 1.0)
    hn = hh * lax.rsqrt(jnp.mean(hh * hh, axis=1, keepdims=True) + NORM_EPS)
    return hn * mn * og


def _mlstm_kernel(L, G, CH, carry, qkv_ref, gate_ref, og_ref, c0_ref, n0_ref, m0_ref, bias_ref, mn_ref,
                  h_ref, c_ref, n_ref, m_ref, *scratch):
    if carry:
        c_in, n_in, m_in = scratch
        ci = pl.program_id(1)

        @pl.when(ci == 0)
        def _():
            c_in[...] = c0_ref[...]
            n_in[...] = n0_ref[...]
            m_in[...] = m0_ref[...]
        c_out, n_out, m_out = scratch
    else:
        c_in, n_in, m_in = c0_ref, n0_ref, m0_ref
        c_out, n_out, m_out = c_ref, n_ref, m_ref

    def qkv(j, cc, h, part):
        cols = slice(part * 1024 + h * M_DK, part * 1024 + (h + 1) * M_DK)
        return qkv_ref[j, cc * L:(cc + 1) * L, cols].astype(BF16)

    units = [(j, cc, h) for cc in range(CH) for j in range(G) for h in range(M_HEADS)]
    gates = {}
    for j in range(G):
        m_cur = m_in[j]
        for cc in range(CH):
            gates[j, cc] = _mlstm_gates(gate_ref[j, cc * L:(cc + 1) * L], bias_ref[...], m_cur, L)
            m_cur = gates[j, cc][5]
        m_out[j] = m_cur
    def carried(cc, c_cur):
        return {(j, h): lax.dot_general(qkv(j, cc, h, 0), c_cur[j, h].astype(BF16), NT_DIMS,
                                        preferred_element_type=F32) for j, c, h in units if c == cc}

    c_cur = {(j, h): c_in[j, h] for j in range(G) for h in range(M_HEADS)}
    n_cur = {(j, h): n_in[j, h:h + 1, :] for j in range(G) for h in range(M_HEADS)}
    scores = {u: lax.dot_general(qkv(*u, 0), qkv(*u, 1), NT_DIMS, preferred_element_type=F32) for u in units}
    cqs_first = carried(0, c_cur)
    qks = {(j, cc, h): scores[j, cc, h] * _mlstm_decay(h, L, gates[j, cc]) for j, cc, h in units}
    pvs = {u: jnp.dot(qks[u].astype(BF16), qkv(*u, 2), preferred_element_type=F32) for u in units}
    kws = {(j, cc, h): gates[j, cc][3][:, h:h + 1] * qkv(j, cc, h, 1).astype(F32) for j, cc, h in units}
    for cc in range(CH):
        now = [u for u in units if u[1] == cc]
        cqs = cqs_first if cc == 0 else carried(cc, c_cur)
        for j, _, h in now:
            sl = slice(h * M_DV, (h + 1) * M_DV)
            rows = slice(cc * L, (cc + 1) * L)
            out = _mlstm_output(h, qkv(j, cc, h, 0), og_ref[j, rows, sl].astype(F32), mn_ref[:, sl], gates[j, cc],
                                jnp.sum(qks[j, cc, h], axis=1, keepdims=True), pvs[j, cc, h], cqs[j, h],
                                n_cur[j, h])
            h_ref[j, rows, sl] = out.astype(h_ref.dtype)
        for j, _, h in now:
            dec = gates[j, cc][4][:, h:h + 1]
            c_cur[j, h] = dec * c_cur[j, h] + lax.dot_general(qkv(j, cc, h, 2), kws[j, cc, h].astype(BF16),
                                                              TN_DIMS, preferred_element_type=F32)
            n_cur[j, h] = dec * n_cur[j, h] + jnp.sum(kws[j, cc, h], axis=0, keepdims=True)
    for j in range(G):
        for h in range(M_HEADS):
            c_out[j, h] = c_cur[j, h]
            n_out[j, h:h + 1, :] = n_cur[j, h]

    if carry:
        @pl.when(ci == pl.num_programs(1) - 1)
        def _():
            c_ref[...] = c_in[...]
            n_ref[...] = n_in[...]
            m_ref[...] = m_in[...]


MLSTM_GROUP_CARRY = 2
MLSTM_CHUNKS_CARRY = 1
MLSTM_GROUP_SINGLE = 4


def _mlstm(qkv, gates, og, c0, n0, m0, gate_bias, mnorm, nb, L):
    t = qkv.shape[0]
    nc = t // (nb * L)
    carry = nc > 1
    G = MLSTM_GROUP_CARRY if carry else MLSTM_GROUP_SINGLE
    CH = MLSTM_CHUNKS_CARRY if carry else 1
    qkv, gates, og = (a.reshape(nb, nc * L, a.shape[1]) for a in (qkv, gates, og))
    tokmap = lambda b, c: (b, c, 0)
    const2 = lambda b, c: (0, 0)
    state_specs = [pl.BlockSpec((G, M_HEADS, M_DV, M_DK), lambda b, c: (b, 0, 0, 0)),
                   pl.BlockSpec((G, M_HEADS, M_DK), lambda b, c: (b, 0, 0)),
                   pl.BlockSpec((G, 1, LANES), lambda b, c: (b, 0, 0))]
    scratch = [pltpu.VMEM((G, M_HEADS, M_DV, M_DK), F32),
               pltpu.VMEM((G, M_HEADS, M_DK), F32),
               pltpu.VMEM((G, 1, LANES), F32)] if carry else []
    h, c, n, m = pl.pallas_call(
        functools.partial(_mlstm_kernel, L, G, CH, carry),
        grid=(nb // G, nc // CH),
        in_specs=[pl.BlockSpec((G, CH * L, 3072), tokmap),
                  pl.BlockSpec((G, CH * L, LANES), tokmap),
                  pl.BlockSpec((G, CH * L, 1024), tokmap)]
                 + state_specs
                 + [pl.BlockSpec((1, LANES), const2),
                    pl.BlockSpec((1, 1024), const2)],
        out_specs=[pl.BlockSpec((G, CH * L, 1024), tokmap)] + state_specs,
        out_shape=[jax.ShapeDtypeStruct((nb, nc * L, 1024), qkv.dtype),
                   jax.ShapeDtypeStruct((nb, M_HEADS, M_DV, M_DK), F32),
                   jax.ShapeDtypeStruct((nb, M_HEADS, M_DK), F32),
                   jax.ShapeDtypeStruct((nb, 1, LANES), F32)],
        scratch_shapes=scratch,
        compiler_params=_cparams(("arbitrary", "arbitrary")),
        name="mlstm_L%d" % L,
    )(qkv, gates, og, c0, n0, m0, gate_bias, mnorm)
    return h.reshape(t, 1024), c, n, m


def _swa_bias(m, has_prev):
    nkeys = 2 * A_BLOCK
    t_i = lax.broadcasted_iota(I32, (A_GROUP * m, nkeys), 0) & (m - 1)
    s_i = lax.broadcasted_iota(I32, (A_GROUP * m, nkeys), 1)
    diff = t_i + A_BLOCK - s_i
    mask = (diff >= 0) & (diff <= WINDOW)
    if not has_prev:
        mask = mask & (s_i >= A_BLOCK)
    return jnp.where(mask, 0.0, -jnp.inf).astype(F32)


def _swa_group_ones():
    nkeys = 2 * A_BLOCK
    r = lax.broadcasted_iota(I32, (A_KV_HEADS * nkeys, A_KV), 0) >> (nkeys.bit_length() - 1)
    c = lax.broadcasted_iota(I32, (A_KV_HEADS * nkeys, A_KV), 1) >> (A_HD.bit_length() - 1)
    return (r == c).astype(BF16)


def _swa_core(q, k_prev, v_prev, k_cur, v_cur, bias, group_ones, sink_ref, keys_on_lanes=False):
    m = q.shape[0]
    q = q.astype(BF16)
    qst = jnp.concatenate([q[:, j * A_KV:(j + 1) * A_KV] for j in range(A_GROUP)], axis=0)
    key_axis = 1 if keys_on_lanes else 0
    k_all = jnp.concatenate([k_prev, k_cur], axis=key_axis)
    v_all = jnp.concatenate([v_prev, v_cur], axis=key_axis)
    feat = lax.broadcasted_iota(I32, (A_KV, 1) if keys_on_lanes else (1, A_KV), 1 - key_axis)
    kbd, vbd = [], []
    for g in range(A_KV_HEADS):
        in_g = (feat >= g * A_HD) & (feat < (g + 1) * A_HD)
        kbd.append(jnp.where(in_g, k_all, 0.0).astype(BF16))
        vbd.append(jnp.where(in_g, v_all, 0.0).astype(BF16))
    kbd = jnp.concatenate(kbd, axis=key_axis)
    vbd = jnp.concatenate(vbd, axis=key_axis)
    if keys_on_lanes:
        s = jnp.dot(qst, kbd, preferred_element_type=F32)
    else:
        s = lax.dot_general(qst, kbd, NT_DIMS, preferred_element_type=F32)
    nkeys = 2 * A_BLOCK
    ps, sink_terms = [], []
    for g in range(A_KV_HEADS):
        sg = s[:, g * nkeys:(g + 1) * nkeys] + bias
        sink = jnp.concatenate(
            [jnp.full((m, LANES), sink_ref[A_GROUP * g + j], F32) for j in range(A_GROUP)], axis=0)
        mx = jnp.maximum(jnp.broadcast_to(jnp.max(sg, axis=1, keepdims=True), (A_GROUP * m, LANES)), sink)
        ps += [jnp.exp(sg[:, :LANES] - mx).astype(BF16), jnp.exp(sg[:, LANES:] - mx).astype(BF16)]
        sink_terms.append(jnp.exp(sink - mx))
    p = jnp.concatenate(ps, axis=1)
    if keys_on_lanes:
        o = lax.dot_general(p, vbd, NT_DIMS, preferred_element_type=F32)
    else:
        o = jnp.dot(p, vbd, preferred_element_type=F32)
    den = jnp.dot(p, group_ones, preferred_element_type=F32)
    low = lax.broadcasted_iota(I32, (1, LANES), 1) < A_HD
    den = den + jnp.concatenate([jnp.where(low, sink_terms[0], sink_terms[1]),
                                 jnp.where(low, sink_terms[2], sink_terms[3])], axis=1)
    o = o / den
    return jnp.concatenate([o[j * m:(j + 1) * m] for j in range(A_GROUP)], axis=1)


SWA_STEP_BLOCKS = 4
SWA_STEP = SWA_STEP_BLOCKS * A_BLOCK


def _swa_prompt_kernel(sink_ref, q_ref, kp_ref, kc_ref, vp_ref, vc_ref, o_ref, bias_s, bias0_s, ones_s):
    first = (pl.program_id(0) == 0) & (pl.program_id(1) == 0)

    @pl.when(first)
    def _():
        bias_s[...] = _swa_bias(A_BLOCK, True)
        bias0_s[...] = _swa_bias(A_BLOCK, False)
        ones_s[...] = _swa_group_ones()

    bias = bias_s[...]
    bias_first = jnp.where(pl.program_id(1) > 0, bias, bias0_s[...])
    ones = ones_s[...]
    k_prev, v_prev = kp_ref[...], vp_ref[...]
    for b in range(SWA_STEP_BLOCKS):
        rows = slice(b * A_BLOCK, (b + 1) * A_BLOCK)
        k_cur, v_cur = kc_ref[rows], vc_ref[rows]
        o_ref[rows] = _swa_core(q_ref[rows], k_prev, v_prev, k_cur, v_cur, bias_first if b == 0 else bias,
                                ones, sink_ref).astype(o_ref.dtype)
        k_prev, v_prev = k_cur, v_cur


def _swa_prompt(qa, ka, va, sinks, nb):
    t = qa.shape[0]
    nstep = t // (nb * SWA_STEP)
    cur = lambda b, i, s: (b * nstep + i, 0)
    prev = lambda b, i, s: (SWA_STEP_BLOCKS * (b * nstep + i) - jnp.minimum(i, 1), 0)
    return pl.pallas_call(
        _swa_prompt_kernel,
        grid_spec=pltpu.PrefetchScalarGridSpec(
            num_scalar_prefetch=1,
            grid=(nb, nstep),
            in_specs=[pl.BlockSpec((SWA_STEP, 1024), cur),
                      pl.BlockSpec((A_BLOCK, A_KV), prev),
                      pl.BlockSpec((SWA_STEP, A_KV), cur),
                      pl.BlockSpec((A_BLOCK, A_KV), prev),
                      pl.BlockSpec((SWA_STEP, A_KV), cur)],
            out_specs=pl.BlockSpec((SWA_STEP, 1024), cur),
            scratch_shapes=[pltpu.VMEM((A_GROUP * A_BLOCK, 2 * A_BLOCK), F32),
                            pltpu.VMEM((A_GROUP * A_BLOCK, 2 * A_BLOCK), F32),
                            pltpu.VMEM((A_KV_HEADS * 2 * A_BLOCK, A_KV), BF16)]),
        out_shape=jax.ShapeDtypeStruct((t, 1024), BF16),
        compiler_params=_cparams(("arbitrary", "arbitrary")),
        name="swa_prompt",
    )(sinks, qa, ka, ka, va, va)


SWA_SAMPLE_GROUP = 8


def _swa_sample_kernel(T, sink_ref, q_ref, kn_ref, vn_ref, kb_ref, vb_ref, o_ref, ko_ref, vo_ref):
    pad = jnp.zeros((A_BLOCK - T, A_KV), F32)
    bias = _swa_bias(T, True)
    ones = _swa_group_ones()
    lane = lax.broadcasted_iota(I32, (1, WINDOW), 1)
    for j in range(SWA_SAMPLE_GROUP):
        rows = slice(j * T, (j + 1) * T)
        k_new_t = jnp.concatenate([kn_ref[rows], pad], axis=0).T
        v_new_t = jnp.concatenate([vn_ref[rows], pad], axis=0).T
        k_buf_t = kb_ref[j]
        v_buf_t = vb_ref[j]
        o_ref[rows] = _swa_core(q_ref[rows], k_buf_t, v_buf_t, k_new_t, v_new_t, bias, ones, sink_ref,
                                keys_on_lanes=True).astype(o_ref.dtype)
        keep = lane < WINDOW - T
        ko_ref[j] = jnp.where(keep, pltpu.roll(k_buf_t, WINDOW - T, axis=1), pltpu.roll(k_new_t, WINDOW - T, axis=1))
        vo_ref[j] = jnp.where(keep, pltpu.roll(v_buf_t, WINDOW - T, axis=1), pltpu.roll(v_new_t, WINDOW - T, axis=1))


def _swa_sample(qa, ka, va, k_buf, v_buf, sinks, T):
    nb = k_buf.shape[0]
    g = SWA_SAMPLE_GROUP
    tok = lambda b, s: (b, 0)
    buf = lambda b, s: (b, 0, 0)
    return pl.pallas_call(
        functools.partial(_swa_sample_kernel, T),
        grid_spec=pltpu.PrefetchScalarGridSpec(
            num_scalar_prefetch=1,
            grid=(nb // g,),
            in_specs=[pl.BlockSpec((g * T, 1024), tok),
                      pl.BlockSpec((g * T, A_KV), tok),
                      pl.BlockSpec((g * T, A_KV), tok),
                      pl.BlockSpec((g, A_KV, WINDOW), buf),
                      pl.BlockSpec((g, A_KV, WINDOW), buf)],
            out_specs=[pl.BlockSpec((g * T, 1024), tok),
                       pl.BlockSpec((g, A_KV, WINDOW), buf),
                       pl.BlockSpec((g, A_KV, WINDOW), buf)]),
        out_shape=[jax.ShapeDtypeStruct((nb * T, 1024), qa.dtype),
                   jax.ShapeDtypeStruct((nb, A_KV, WINDOW), F32),
                   jax.ShapeDtypeStruct((nb, A_KV, WINDOW), F32)],
        compiler_params=_cparams(("arbitrary",)),
        name="swa_sample",
    )(sinks, qa, ka, va, k_buf, v_buf)


def _merge_kernel(x_ref, hm_ref, ha_ref, gm_ref, ga_ref, wpm_ref, wpa_ref, wo_ref, g2_ref, wr_ref, br_ref,
                  x1_ref, xn_ref, idx_ref, gate_ref, cnt_ref):
    pm = jnp.dot(hm_ref[...].astype(BF16), wpm_ref[...], preferred_element_type=F32)
    pa = jnp.dot(ha_ref[...].astype(BF16), wpa_ref[...], preferred_element_type=F32)
    mixed = gm_ref[...].astype(F32) * pm + ga_ref[...].astype(F32) * pa
    x1 = x_ref[...] + jnp.dot(mixed.astype(BF16), wo_ref[...], preferred_element_type=F32)
    x1_ref[...] = x1
    xn = _rms(x1, g2_ref[...])
    xn_ref[...] = xn.astype(BF16)
    xn_hi = xn.astype(BF16)
    xn_lo = (xn - xn_hi.astype(F32)).astype(BF16)
    part = jnp.dot(xn_hi, wr_ref[...], preferred_element_type=F32)
    logits = (part[:, :LANES] + part[:, LANES:]
              + jnp.dot(xn_lo, wr_ref[:, :LANES], preferred_element_type=F32) + br_ref[...])
    tm = logits.shape[0]
    cur = logits.T[0:N_EXPERTS]
    eid = lax.broadcasted_iota(I32, (N_EXPERTS, tm), 0).astype(F32)
    vals, idxs = [], []
    onehot = jnp.zeros((N_EXPERTS, tm), F32)
    for _ in range(TOP_K):
        mval = jnp.max(cur, axis=0, keepdims=True)
        sel = jnp.min(jnp.where(cur == mval, eid, float(N_EXPERTS)), axis=0, keepdims=True)
        hit = eid == sel
        onehot = onehot + hit.astype(F32)
        cur = jnp.where(hit, -jnp.inf, cur)
        vals.append(mval)
        idxs.append(sel)
    es = [jnp.exp(v - vals[0]) for v in vals]
    tot = es[0] + es[1] + es[2] + es[3]
    row = lax.broadcasted_iota(I32, (SUBLANES, tm), 0)
    res = jnp.zeros((SUBLANES, tm), F32)
    for k in range(TOP_K):
        res = jnp.where(row == k, idxs[k], res)
        res = jnp.where(row == TOP_K + k, es[k] / tot, res)
    res_t = jnp.concatenate([res, jnp.zeros((LANES - SUBLANES, tm), F32)], axis=0).T
    idx_ref[...] = res_t[:, 0:TOP_K].astype(I32)
    gate_ref[...] = res_t[:, TOP_K:2 * TOP_K]
    for j in range(tm // TOK_TILE):
        cnt_ref[j] = jnp.sum(onehot[:, j * TOK_TILE:(j + 1) * TOK_TILE], axis=1, keepdims=True).astype(I32)


def _merge(x, hm, ha, og, wpm, wpa, wo, g2, wr, br):
    t = x.shape[0]
    tm = MERGE_TILE
    row = lambda n: pl.BlockSpec((tm, n), lambda i: (i, 0))
    const = lambda r, c: pl.BlockSpec((r, c), lambda i: (0, 0))
    return pl.pallas_call(
        _merge_kernel,
        grid=(t // tm,),
        in_specs=[row(1024), row(1024), row(1024),
                  pl.BlockSpec((tm, 1024), lambda i: (i, 1)),
                  pl.BlockSpec((tm, 1024), lambda i: (i, 2)),
                  const(1024, 1024), const(1024, 1024), const(1024, 1024), const(1, 1024),
                  const(1024, 2 * LANES), const(1, LANES)],
        out_specs=[row(1024), row(1024), row(TOP_K), row(TOP_K),
                   pl.BlockSpec((tm // TOK_TILE, N_EXPERTS, 1), lambda i: (i, 0, 0))],
        out_shape=[jax.ShapeDtypeStruct((t, 1024), F32),
                   jax.ShapeDtypeStruct((t, 1024), BF16),
                   jax.ShapeDtypeStruct((t, TOP_K), I32),
                   jax.ShapeDtypeStruct((t, TOP_K), F32),
                   jax.ShapeDtypeStruct((t // TOK_TILE, N_EXPERTS, 1), I32)],
        compiler_params=_cparams(("arbitrary",)),
        name="merge_route",
    )(x, hm, ha, og, og, wpm, wpa, wo, g2, wr, br)


def _rows_to_tiles(ref, lead, x):
    chunks = jnp.stack([x[:, s * LANES:(s + 1) * LANES] for s in range(ROW_TILE[0])], axis=0)
    ref[lead] = jnp.swapaxes(chunks, 0, 1).astype(ref.dtype)


def _tiles_to_rows(ref, lead, rows=slice(None)):
    chunks = jnp.swapaxes(ref[lead + (rows,)].astype(F32), 0, 1)
    return jnp.concatenate([chunks[s] for s in range(ROW_TILE[0])], axis=1)


def _stage_rows(idx, off_row):
    tm = idx.shape[0]
    lane = lax.broadcasted_iota(I32, (tm, LANES), 1)
    hits = [lane == idx[:, k:k + 1] for k in range(TOP_K)]
    onehot = sum(h.astype(F32) for h in hits)
    r_i = lax.broadcasted_iota(I32, (tm, tm), 0)
    c_i = lax.broadcasted_iota(I32, (tm, tm), 1)
    before = (c_i < r_i).astype(BF16)
    rank = jnp.dot(before, onehot.astype(BF16), preferred_element_type=F32)
    pos = rank + off_row
    return [jnp.sum(jnp.where(h, pos, 0.0), axis=1, keepdims=True) for h in hits]


def _dispatch_kernel(ntile_a, cnt_ref, off_ref, base_ref, tail_ref, xa_ref, xb_ref, ia_ref, ib_ref, offv_ref,
                     xs_ref, stage, zeros, sem, zsem):
    i = pl.program_id(0)
    from_a = i < ntile_a
    xn = jnp.where(from_a, xa_ref[...], xb_ref[...])
    idx = jnp.where(from_a, ia_ref[...], ib_ref[...])
    rows = _stage_rows(idx, offv_ref[0])
    tm = idx.shape[0]
    lane = lax.broadcasted_iota(I32, (tm, LANES), 1)
    r4 = jnp.zeros((tm, LANES), F32)
    for k in range(TOP_K):
        r4 = jnp.where(lane == k, rows[k], r4)
    r4t = r4.T.astype(I32)
    r_iota = lax.broadcasted_iota(I32, (STAGE_ROWS, tm), 0)
    sel = r_iota == r4t[0:1, :]
    for k in range(1, TOP_K):
        sel = sel | (r_iota == r4t[k:k + 1, :])
    sel = jnp.where(sel, 1.0, 0.0).astype(BF16)
    staged = jnp.dot(sel, xn, preferred_element_type=F32)

    def runs(tile, slot):
        out = []
        for e in range(N_EXPERTS):
            n = cnt_ref[tile * N_EXPERTS + e]
            off = off_ref[tile * N_EXPERTS + e]
            base = base_ref[tile * N_EXPERTS + e]
            out.append((n, pltpu.make_async_copy(stage.at[slot, pl.ds(off, n)], xs_ref.at[pl.ds(base, n)],
                                                 sem.at[slot])))
        return out

    def wait_runs(slot):
        pltpu.make_async_copy(stage.at[slot], xs_ref.at[pl.ds(0, STAGE_ROWS)], sem.at[slot]).wait()

    for slot in range(2):
        @pl.when((i & 1) == slot)
        def _():
            @pl.when(i >= 2)
            def _():
                wait_runs(slot)
            _rows_to_tiles(stage, (slot,), staged)
            for n, cp in runs(i, slot):
                @pl.when(n > 0)
                def _():
                    cp.start()

    last = pl.num_programs(0) - 1

    @pl.when(i == last)
    def _():
        for slot in range(2):
            @pl.when(((last & 1) == slot) | (last >= 1))
            def _():
                wait_runs(slot)
        zeros[...] = jnp.zeros(zeros.shape, zeros.dtype)
        tails = []
        for e in range(N_EXPERTS):
            start = tail_ref[e]
            n = tail_ref[N_EXPERTS + e]
            tails.append((n, pltpu.make_async_copy(zeros.at[pl.ds(0, n)], xs_ref.at[pl.ds(start, n)], zsem)))
        for n, cp in tails:
            @pl.when(n > 0)
            def _():
                cp.start()
        for n, cp in tails:
            @pl.when(n > 0)
            def _():
                cp.wait()
        first = tail_ref[2 * N_EXPERTS]
        n_unused = tail_ref[2 * N_EXPERTS + 1]

        def unused_block(j):
            return pltpu.make_async_copy(zeros, xs_ref.at[pl.ds((first + j) * MOE_BLOCK, MOE_BLOCK)], zsem)

        @pl.loop(0, n_unused)
        def _(j):
            unused_block(j).start()

        @pl.loop(0, n_unused)
        def _(j):
            unused_block(j).wait()


def _dispatch(xa, xb, ia, ib, cnt, off, base, tails, offv, n_slots):
    tm = TOK_TILE
    ntile_a = xa.shape[0] // tm
    ntile_b = xb.shape[0] // tm
    amap = lambda i, *_: (jnp.minimum(i, ntile_a - 1), 0)
    bmap = lambda i, *_: (jnp.maximum(i - ntile_a, 0), 0)
    return pl.pallas_call(
        functools.partial(_dispatch_kernel, ntile_a),
        grid_spec=pltpu.PrefetchScalarGridSpec(
            num_scalar_prefetch=4,
            grid=(ntile_a + ntile_b,),
            in_specs=[pl.BlockSpec((tm, 1024), amap),
                      pl.BlockSpec((tm, 1024), bmap),
                      pl.BlockSpec((tm, TOP_K), amap),
                      pl.BlockSpec((tm, TOP_K), bmap),
                      pl.BlockSpec((1, 1, LANES), lambda i, *_: (i, 0, 0))],
            out_specs=pl.BlockSpec(memory_space=pl.ANY),
            scratch_shapes=[pltpu.VMEM((2, STAGE_ROWS) + ROW_TILE, ROW_DTYPE),
                            pltpu.VMEM((MOE_BLOCK,) + ROW_TILE, ROW_DTYPE),
                            pltpu.SemaphoreType.DMA((2,)),
                            pltpu.SemaphoreType.DMA]),
        out_shape=jax.ShapeDtypeStruct((n_slots,) + ROW_TILE, ROW_DTYPE),
        compiler_params=_cparams(("arbitrary",)),
        name="moe_dispatch",
    )(cnt, off, base, tails, xa, xb, ia, ib, offv)


MOE_HALF = MOE_BLOCK // 2


def _experts_kernel(be_ref, rows_ref, next_ref, slot_ref, xs_ref, wgu_hbm, bgu_ref, wd_hbm, bd_ref, y_ref,
                    wgu_f, wd_f, wgu_s, wd_s, sem):
    i = pl.program_id(0)
    rows = rows_ref[i]
    live = rows > 0
    run_start = (i == 0) | (be_ref[i] != be_ref[jnp.maximum(i - 1, 0)])

    def fetch(expert, slot):
        return (pltpu.make_async_copy(wgu_hbm.at[expert], wgu_f.at[slot], sem.at[0, slot]),
                pltpu.make_async_copy(wd_hbm.at[expert], wd_f.at[slot], sem.at[1, slot]))

    for slot in range(2):
        @pl.when(live & run_start & (slot_ref[i] == slot))
        def _():
            @pl.when(i == 0)
            def _():
                for cp in fetch(be_ref[i], slot):
                    cp.start()
            for cp in fetch(be_ref[i], slot):
                cp.wait()

            @pl.when(next_ref[i] >= 0)
            def _():
                for cp in fetch(next_ref[i], 1 - slot):
                    cp.start()
            wgu_s[...] = wgu_f[slot].astype(BF16)
            wd_s[...] = wd_f[slot].astype(BF16)

    def ffn(x):
        h = jnp.dot(x.astype(BF16), wgu_s[...], preferred_element_type=F32) + bgu_ref[0]
        gate = jnp.minimum(h[:, :D_FF], SWIGLU_LIMIT)
        up = jnp.clip(h[:, D_FF:], -SWIGLU_LIMIT, SWIGLU_LIMIT)
        act = gate * jax.nn.sigmoid(SWIGLU_ALPHA * gate) * (up + 1.0)
        return jnp.dot(act.astype(BF16), wd_s[...], preferred_element_type=F32) + bd_ref[0]

    @pl.when(rows > MOE_HALF)
    def _():
        _rows_to_tiles(y_ref, (), ffn(_tiles_to_rows(xs_ref, ())))

    @pl.when(live & (rows <= MOE_HALF))
    def _():
        y_half = ffn(_tiles_to_rows(xs_ref, (), slice(0, MOE_HALF)))
        _rows_to_tiles(y_ref, (), jnp.concatenate([y_half, jnp.zeros((MOE_HALF, D_MODEL), F32)], axis=0))

    @pl.when(rows == 0)
    def _():
        y_ref[...] = jnp.zeros(y_ref.shape, y_ref.dtype)


def _experts(block_expert, block_rows, next_expert, slot, xs, wgu, bgu, wd, bd):
    nb_max = xs.shape[0] // MOE_BLOCK
    blk = lambda i, be, rows, *_: (jnp.where(rows[i] > 0, i, 0), 0, 0)
    wmap = lambda i, be, *_: (be[i], 0, 0)
    return pl.pallas_call(
        _experts_kernel,
        grid_spec=pltpu.PrefetchScalarGridSpec(
            num_scalar_prefetch=4,
            grid=(nb_max,),
            in_specs=[pl.BlockSpec((MOE_BLOCK,) + ROW_TILE, blk),
                      pl.BlockSpec(memory_space=pl.ANY),
                      pl.BlockSpec((1, 1, 2 * D_FF), wmap),
                      pl.BlockSpec(memory_space=pl.ANY),
                      pl.BlockSpec((1, 1, D_MODEL), wmap)],
            out_specs=pl.BlockSpec((MOE_BLOCK,) + ROW_TILE, lambda i, *_: (i, 0, 0)),
            scratch_shapes=[pltpu.VMEM((2, D_MODEL, 2 * D_FF), F32),
                            pltpu.VMEM((2, D_FF, D_MODEL), F32),
                            pltpu.VMEM((D_MODEL, 2 * D_FF), BF16),
                            pltpu.VMEM((D_FF, D_MODEL), BF16),
                            pltpu.SemaphoreType.DMA((2, 2))]),
        out_shape=jax.ShapeDtypeStruct(xs.shape, ROW_DTYPE),
        compiler_params=_cparams(("arbitrary",)),
        name="moe_experts",
    )(block_expert, block_rows, next_expert, slot, xs, wgu, bgu, wd, bd)


def _combine_kernel(cnt_ref, off_ref, base_ref, x1_ref, idx_ref, gate_ref, offv_ref, gf_ref, y_ref,
                    o_ref, stage, sem):
    i = pl.program_id(0)
    last = pl.num_programs(0) - 1

    def runs(tile, slot):
        out = []
        for e in range(N_EXPERTS):
            n = cnt_ref[tile * N_EXPERTS + e]
            off = off_ref[tile * N_EXPERTS + e]
            base = base_ref[tile * N_EXPERTS + e]
            out.append((n, pltpu.make_async_copy(y_ref.at[pl.ds(base, n)], stage.at[slot, pl.ds(off, n)],
                                                 sem.at[slot])))
        return out

    def start_runs(tile, slot):
        for n, cp in runs(tile, slot):
            @pl.when(n > 0)
            def _():
                cp.start()

    @pl.when(i == 0)
    def _():
        start_runs(0, 0)

    for slot in range(2):
        @pl.when(((i & 1) != slot) & (i < last))
        def _():
            start_runs(i + 1, slot)

    rows = _stage_rows(idx_ref[...], offv_ref[0])
    tm = idx_ref.shape[0]
    lane = lax.broadcasted_iota(I32, (tm, STAGE_ROWS), 1)
    gates = gate_ref[...]
    gmat = jnp.zeros((tm, STAGE_ROWS), F32)
    for k in range(TOP_K):
        gmat = gmat + jnp.where(lane == rows[k].astype(I32), gates[:, k:k + 1], 0.0)
    gmat = gmat.astype(BF16)
    for slot in range(2):
        @pl.when((i & 1) == slot)
        def _():
            pltpu.make_async_copy(y_ref.at[pl.ds(0, STAGE_ROWS)], stage.at[slot], sem.at[slot]).wait()
            moe = jnp.dot(gmat, _tiles_to_rows(stage, (slot,)).astype(BF16), preferred_element_type=F32)
            o_ref[...] = _rms(x1_ref[...] + moe, gf_ref[...])


def _combine(x1, idx, gates, cnt, off, base, offv, gf, y):
    t = x1.shape[0]
    tm = TOK_TILE
    return pl.pallas_call(
        _combine_kernel,
        grid_spec=pltpu.PrefetchScalarGridSpec(
            num_scalar_prefetch=3,
            grid=(t // tm,),
            in_specs=[pl.BlockSpec((tm, 1024), lambda i, *_: (i, 0)),
                      pl.BlockSpec((tm, TOP_K), lambda i, *_: (i, 0)),
                      pl.BlockSpec((tm, TOP_K), lambda i, *_: (i, 0)),
                      pl.BlockSpec((1, 1, LANES), lambda i, *_: (i, 0, 0)),
                      pl.BlockSpec((1, 1024), lambda i, *_: (0, 0)),
                      pl.BlockSpec(memory_space=pl.ANY)],
            out_specs=pl.BlockSpec((tm, 1024), lambda i, *_: (i, 0)),
            scratch_shapes=[pltpu.VMEM((2, STAGE_ROWS) + ROW_TILE, ROW_DTYPE),
                            pltpu.SemaphoreType.DMA((2,))]),
        out_shape=jax.ShapeDtypeStruct((t, 1024), F32),
        compiler_params=_cparams(("arbitrary",)),
        name="moe_combine",
    )(cnt, off, base, x1, idx, gates, offv, gf, y)


def _prep_w_in(w):
    wt = w.T
    c_if, c_qa, c_kv, c_g = 4096, 4104, 5128, 5640
    wm = wt[:c_if].astype(BF16)
    wif = jnp.pad(wt[c_if:c_qa], ((0, LANES - 2 * M_HEADS), (0, 0))).astype(BF16)
    wqa = (jnp.concatenate([wt[c_qa + (A_GROUP * g + j) * A_HD:c_qa + (A_GROUP * g + j + 1) * A_HD]
                            for j in range(A_GROUP) for g in range(A_KV_HEADS)], axis=0)
           * (A_HD ** -0.5)).astype(BF16)
    wkv = wt[c_kv:c_g].astype(BF16)
    wg = wt[c_g:].astype(BF16)
    return wm, wg, wqa, wkv, wif


def _rope_tables(pos):
    half = ROT_DIM // 2
    inv = ROPE_THETA ** (-(jnp.arange(half, dtype=F32) * 2.0 / ROT_DIM))
    ang = pos.astype(F32)[:, None] * inv[None, :]
    lane = jnp.arange(LANES) % A_HD
    hit = (lane[None, :] % half == jnp.arange(half)[:, None]) & (lane[None, :] < ROT_DIM)
    spread_cos = hit.astype(F32)
    spread_sin = jnp.where(hit, jnp.where(lane < half, -1.0, 1.0)[None, :], 0.0)
    cos_t = jnp.dot(jnp.cos(ang), spread_cos, precision=lax.Precision.HIGHEST) + (lane >= ROT_DIM).astype(F32)[None, :]
    sin_t = jnp.dot(jnp.sin(ang), spread_sin, precision=lax.Precision.HIGHEST)
    return cos_t, sin_t


def _round_up(x, m):
    return (x + m - 1) // m * m


def kernel(x_prompt, x_sample, cache_k, cache_v, state_c, state_n, state_m, norm1, w_in, b_igate, b_fgate,
           mlstm_norm, w_proj_m, w_proj_a, attn_sinks, w_out, norm2, w_router, b_router, w_gate_up, b_gate_up,
           w_down, b_down, norm_f):
    bp, sp, _ = x_prompt.shape
    bs, ts, _ = x_sample.shape
    past_len = 16384
    l = 0
    w1 = _prep_w_in(w_in[l])
    g1 = norm1[l][None, :]
    g2 = norm2[l][None, :]
    gf = norm_f[None, :]
    gate_bias = jnp.pad(jnp.concatenate([b_igate[l], b_fgate[l]]), (0, LANES - 2 * M_HEADS))[None, :]
    mn = mlstm_norm[l][None, :]
    wpm = w_proj_m[l].astype(BF16)
    wpa = jnp.concatenate([w_proj_a[l][(A_GROUP * g + j) * A_HD:(A_GROUP * g + j + 1) * A_HD]
                           for j in range(A_GROUP) for g in range(A_KV_HEADS)], axis=0).astype(BF16)
    wo = w_out[l].astype(BF16)
    wr_f = jnp.pad(w_router[l], ((0, 0), (0, LANES - N_EXPERTS)))
    wr_hi = wr_f.astype(BF16)
    wr = jnp.concatenate([wr_hi, (wr_f - wr_hi.astype(F32)).astype(BF16)], axis=1)
    br = jnp.pad(b_router[l], (0, LANES - N_EXPERTS))[None, :]
    wgu = w_gate_up[l]
    bgu = b_gate_up[l][:, None, :]
    wd = w_down[l]
    bd = b_down[l][:, None, :]
    sinks = attn_sinks[l]

    def mixer(x, nb, seq, pos0, L, c0, n0, m0, k_buf, v_buf, tm):
        t = nb * seq
        xf = x.reshape(t, D_MODEL)
        cos_t, sin_t = _rope_tables(pos0 + jnp.arange(seq))
        if seq < tm:
            cos_t, sin_t = jnp.tile(cos_t, (tm // seq, 1)), jnp.tile(sin_t, (tm // seq, 1))
        act = BF16 if L % 16 == 0 else F32
        qkv, og, qa, ka, va, gates = _inproj(xf, g1, w1, cos_t, sin_t, tm, max(seq // tm, 1), act)
        m0p = jnp.pad(m0, ((0, 0), (0, LANES - M_HEADS)))[:, None, :]
        hm, c, n, m = _mlstm(qkv, gates, og, c0, n0, m0p, gate_bias, mn, nb, L)
        if k_buf is None:
            ha = _swa_prompt(qa, ka, va, sinks, nb)
            k_keep = ka.reshape(nb, seq, A_KV)[:, seq - WINDOW:].reshape(nb, WINDOW, A_KV_HEADS, A_HD)
            v_keep = va.reshape(nb, seq, A_KV)[:, seq - WINDOW:].reshape(nb, WINDOW, A_KV_HEADS, A_HD)
        else:
            to_t = lambda a: a.transpose(0, 2, 3, 1).reshape(nb, A_KV, WINDOW)
            from_t = lambda a: a.reshape(nb, A_KV_HEADS, A_HD, WINDOW).transpose(0, 3, 1, 2)
            ha, k_keep, v_keep = _swa_sample(qa, ka, va, to_t(k_buf), to_t(v_buf), sinks, seq)
            k_keep, v_keep = from_t(k_keep), from_t(v_keep)
        x1, xn, idx, gate, cnt = _merge(xf, hm, ha, og, wpm, wpa, wo, g2, wr, br)
        return (x1, xn, idx, gate, cnt[:, :, 0]), (k_keep, v_keep, c, n, m[:, 0, :M_HEADS])

    zc = jnp.zeros((bp, M_HEADS, M_DV, M_DK), F32)
    zn = jnp.zeros((bp, M_HEADS, M_DK), F32)
    zm = jnp.zeros((bp, M_HEADS), F32)
    rp, sp_out = mixer(x_prompt, bp, sp, 0, M_CHUNK, zc, zn, zm, None, None, 512)
    rs, ss_out = mixer(x_sample, bs, ts, past_len, ts, state_c[l], state_n[l], state_m[l],
                       cache_k[l], cache_v[l], 256)

    cnt = jnp.concatenate([rp[4], rs[4]], axis=0)
    ntile_p = rp[4].shape[0]
    off = jnp.cumsum(cnt, axis=1) - cnt
    per_expert = jnp.sum(cnt, axis=0)
    padded = _round_up(per_expert, MOE_BLOCK)
    padded_end = jnp.cumsum(padded)
    expert_start = padded_end - padded
    base = expert_start[None, :] + jnp.cumsum(cnt, axis=0) - cnt
    n_tok = bp * sp + bs * ts
    n_slots = _round_up(n_tok * TOP_K, MOE_BLOCK) + N_EXPERTS * MOE_BLOCK
    nb_max = n_slots // MOE_BLOCK
    block_row = jnp.arange(nb_max, dtype=I32) * MOE_BLOCK
    block_expert = jnp.minimum(jnp.sum(block_row[:, None] >= padded_end[None, :], axis=1), N_EXPERTS - 1).astype(I32)
    n_blocks = (padded_end[-1] // MOE_BLOCK).astype(I32)[None]
    eid = jnp.arange(N_EXPERTS, dtype=I32)
    later_used = (eid[None, :] > eid[:, None]) & (padded[None, :] > 0)
    next_used = jnp.min(jnp.where(later_used, eid[None, :], N_EXPERTS), axis=1)
    next_used = jnp.where(next_used < N_EXPERTS, next_used, -1).astype(I32)
    run_parity = ((jnp.cumsum(padded > 0) - 1) & 1).astype(I32)
    of_block = lambda table: jnp.sum(jnp.where(block_expert[:, None] == eid[None, :], table[None, :], 0),
                                     axis=1).astype(I32)
    block_rows = jnp.clip(of_block(per_expert) - (block_row - of_block(expert_start)), 0, MOE_BLOCK).astype(I32)
    tails = jnp.concatenate([expert_start + per_expert, padded - per_expert,
                             n_blocks, nb_max - n_blocks]).astype(I32)
    flat = lambda a: a.reshape(-1).astype(I32)
    offv = jnp.pad(off, ((0, 0), (0, LANES - N_EXPERTS))).astype(F32)[:, None, :]

    xs = _dispatch(rp[1], rs[1], rp[2], rs[2], flat(cnt), flat(off), flat(base), tails, offv, n_slots)
    y = _experts(block_expert, block_rows, of_block(next_used), of_block(run_parity), xs, wgu, bgu, wd, bd)
    y_p = _combine(rp[0], rp[2], rp[3], flat(cnt[:ntile_p]), flat(off[:ntile_p]), flat(base[:ntile_p]),
                   offv[:ntile_p], gf, y)
    y_s = _combine(rs[0], rs[2], rs[3], flat(cnt[ntile_p:]), flat(off[ntile_p:]), flat(base[ntile_p:]),
                   offv[ntile_p:], gf, y)

    kp, vp, cp_, np_, mp = sp_out
    ks, vs, cs, ns, ms = ss_out
    return (y_p.reshape(bp, sp, D_MODEL), y_s.reshape(bs, ts, D_MODEL),
            kp[None], vp[None], cp_[None], np_[None], mp[None],
            ks[None], vs[None], cs[None], ns[None], ms[None])
```

```python
import functools

import jax
import jax.numpy as jnp
from jax import lax
from jax.experimental import pallas as pl
from jax.experimental.pallas import tpu as pltpu

F32 = jnp.float32
BF16 = jnp.bfloat16
I32 = jnp.int32

D_MODEL = 1024
M_HEADS = 4
M_DK = 256
M_DV = 256
M_CHUNK = 128
A_Q_HEADS = 16
A_KV_HEADS = 4
A_GROUP = 4
A_HD = 64
A_KV = A_KV_HEADS * A_HD
WINDOW = 128
A_BLOCK = 128
ROT_DIM = 16
ROPE_THETA = 500000.0
N_EXPERTS = 32
TOP_K = 4
D_FF = 1024
SWIGLU_LIMIT = 7.0
SWIGLU_ALPHA = 1.702
NORM_EPS = 1e-6

LANES = 128
SUBLANES = 8
VMEM_LIMIT = 56 * 1024 * 1024

MERGE_TILE = 512
TOK_TILE = 256
STAGE_ROWS = 4 * TOK_TILE
ROW_TILE = (8, 128)
ROW_DTYPE = F32
MOE_BLOCK = 512


def _cparams(sem):
    return pltpu.CompilerParams(dimension_semantics=sem, vmem_limit_bytes=VMEM_LIMIT)


def _rms(x, g):
    return x * lax.rsqrt(jnp.mean(x * x, axis=-1, keepdims=True) + NORM_EPS) * g


def _rope(x, cos_t, sin_t, first_half):
    n = x.shape[1]
    fwd = pltpu.roll(x, n - ROT_DIM // 2, axis=1)
    bwd = pltpu.roll(x, ROT_DIM // 2, axis=1)
    outs = []
    for b in range(n // LANES):
        sl = slice(b * LANES, (b + 1) * LANES)
        partner = jnp.where(first_half, fwd[:, sl], bwd[:, sl])
        outs.append(x[:, sl] * cos_t + partner * sin_t)
    return jnp.concatenate(outs, axis=1)


def _inproj_kernel(x_ref, g1_ref, wm_ref, wg_ref, wqa_ref, wkv_ref, wif_ref, cos_ref, sin_ref,
                   qkv_ref, og_ref, qa_ref, ka_ref, va_ref, gate_ref):
    x = x_ref[...]
    xn = _rms(x, g1_ref[...]).astype(BF16)
    act = qkv_ref.dtype

    def proj(w_ref, c0, n):
        return lax.dot_general(xn, w_ref[c0:c0 + n, :], (((1,), (1,)), ((), ())), preferred_element_type=F32)

    qkv_ref[:, 0:1024] = proj(wm_ref, 0, 1024).astype(act)
    qkv_ref[:, 1024:2048] = (proj(wm_ref, 1024, 1024) * (M_DK ** -0.5)).astype(act)
    qkv_ref[:, 2048:3072] = proj(wm_ref, 2048, 1024).astype(act)
    og_ref[:, 0:1024] = jax.nn.sigmoid(proj(wm_ref, 3072, 1024)).astype(act)
    og_ref[:, 1024:2048] = jax.nn.sigmoid(proj(wg_ref, 0, 1024)).astype(act)
    og_ref[:, 2048:3072] = jax.nn.sigmoid(proj(wg_ref, 1024, 1024)).astype(act)
    cos_t = cos_ref[...]
    sin_t = sin_ref[...]
    lane = lax.broadcasted_iota(I32, (1, LANES), 1)
    first_half = (lane & (ROT_DIM - 1)) < (ROT_DIM // 2)
    qa_ref[...] = _rope(proj(wqa_ref, 0, 1024), cos_t, sin_t, first_half).astype(act)
    ka_ref[...] = _rope(proj(wkv_ref, 0, A_KV), cos_t, sin_t, first_half)
    va_ref[...] = proj(wkv_ref, A_KV, A_KV)
    gate_ref[...] = proj(wif_ref, 0, LANES)


def _inproj(x, g1, weights, cos_t, sin_t, tm, rope_blocks, act):
    t = x.shape[0]
    tok = lambda n, dt: jax.ShapeDtypeStruct((t, n), dt)
    row = lambda n: pl.BlockSpec((tm, n), lambda i: (i, 0))
    resident = lambda w: pl.BlockSpec(w.shape, lambda i: (0, 0), pipeline_mode=pl.Buffered(1))
    return pl.pallas_call(
        _inproj_kernel,
        grid=(t // tm,),
        in_specs=[row(D_MODEL), pl.BlockSpec((1, D_MODEL), lambda i: (0, 0))]
                 + [resident(w) for w in weights]
                 + [pl.BlockSpec((tm, LANES), lambda i: (i % rope_blocks, 0)),
                    pl.BlockSpec((tm, LANES), lambda i: (i % rope_blocks, 0))],
        out_specs=[row(3072), row(3072), row(1024), row(A_KV), row(A_KV), row(LANES)],
        out_shape=[tok(3072, act), tok(3072, act), tok(1024, act), tok(A_KV, F32), tok(A_KV, F32),
                   tok(LANES, F32)],
        compiler_params=_cparams(("arbitrary",)),
        name="inproj",
    )(x, g1, *weights, cos_t, sin_t)


def _mlstm_gates(gc, bias, m_prev, L):
    z = gc + bias
    lf = jnp.minimum(z, 0.0) - jnp.log1p(jnp.exp(-jnp.abs(z)))
    row = lax.broadcasted_iota(I32, (L, LANES), 0)
    fc = lf
    sh = 1
    while sh < L:
        fc = fc + jnp.where(row >= sh, pltpu.roll(fc, sh, axis=0), 0.0)
        sh *= 2
    fcum = pltpu.roll(fc, LANES - M_HEADS, axis=1)
    a = z - fcum
    cmx = a
    sh = 1
    while sh < L:
        cmx = jnp.maximum(cmx, jnp.where(row >= sh, pltpu.roll(cmx, sh, axis=0), -jnp.inf))
        sh *= 2
    mx = jnp.maximum(m_prev, cmx)
    inter = jnp.exp(m_prev - mx)
    f_end = fcum[L - 1:L]
    m_end = f_end + mx[L - 1:L]
    decay = jnp.exp(f_end + m_prev - m_end)
    wsrc = jnp.exp(a + (f_end - m_end))
    return a, mx, inter, wsrc, decay, m_end


NT_DIMS = (((1,), (1,)), ((), ()))
TN_DIMS = (((0,), (0,)), ((), ()))


def _mlstm_decay(h, L, gates):
    a, mx = gates[0], gates[1]
    t_i = lax.broadcasted_iota(I32, (L, L), 0)
    s_i = lax.broadcasted_iota(I32, (L, L), 1)
    a_row = jnp.sum(jnp.where(t_i == s_i, a[:, h:h + 1], 0.0), axis=0, keepdims=True)
    return jnp.exp(jnp.where(s_i <= t_i, a_row - mx[:, h:h + 1], -jnp.inf))


def _mlstm_output(h, q, og, mn, gates, qk_sum, pv, cq, n_h):
    inter_col = gates[2][:, h:h + 1]
    num = inter_col * cq + pv
    den = inter_col * jnp.sum(q.astype(F32) * n_h, axis=1, keepdims=True) + qk_sum
    hh = num / jnp.maximum(jnp.abs(den), 1.0)
    hn = hh * lax.rsqrt(jnp.mean(hh * hh, axis=1, keepdims=True) + NORM_EPS)
    return hn * mn * og


def _mlstm_kernel(L, G, CH, carry, qkv_ref, gate_ref, og_ref, c0_ref, n0_ref, m0_ref, bias_ref, mn_ref,
                  h_ref, c_ref, n_ref, m_ref, *scratch):
    if carry:
        c_in, n_in, m_in = scratch
        ci = pl.program_id(1)

        @pl.when(ci == 0)
        def _():
            c_in[...] = c0_ref[...]
            n_in[...] = n0_ref[...]
            m_in[...] = m0_ref[...]
        c_out, n_out, m_out = scratch
    else:
        c_in, n_in, m_in = c0_ref, n0_ref, m0_ref
        c_out, n_out, m_out = c_ref, n_ref, m_ref

    def qkv(j, cc, h, part):
        cols = slice(part * 1024 + h * M_DK, part * 1024 + (h + 1) * M_DK)
        return qkv_ref[j, cc * L:(cc + 1) * L, cols].astype(BF16)

    units = [(j, cc, h) for cc in range(CH) for j in range(G) for h in range(M_HEADS)]
    gates = {}
    for j in range(G):
        m_cur = m_in[j]
        for cc in range(CH):
            gates[j, cc] = _mlstm_gates(gate_ref[j, cc * L:(cc + 1) * L], bias_ref[...], m_cur, L)
            m_cur = gates[j, cc][5]
        m_out[j] = m_cur
    def carried(cc, c_cur):
        return {(j, h): lax.dot_general(qkv(j, cc, h, 0), c_cur[j, h].astype(BF16), NT_DIMS,
                                        preferred_element_type=F32) for j, c, h in units if c == cc}

    c_cur = {(j, h): c_in[j, h] for j in range(G) for h in range(M_HEADS)}
    n_cur = {(j, h): n_in[j, h:h + 1, :] for j in range(G) for h in range(M_HEADS)}
    scores = {u: lax.dot_general(qkv(*u, 0), qkv(*u, 1), NT_DIMS, preferred_element_type=F32) for u in units}
    cqs_first = carried(0, c_cur)
    qks = {(j, cc, h): scores[j, cc, h] * _mlstm_decay(h, L, gates[j, cc]) for j, cc, h in units}
    pvs = {u: jnp.dot(qks[u].astype(BF16), qkv(*u, 2), preferred_element_type=F32) for u in units}
    kws = {(j, cc, h): gates[j, cc][3][:, h:h + 1] * qkv(j, cc, h, 1).astype(F32) for j, cc, h in units}
    for cc in range(CH):
        now = [u for u in units if u[1] == cc]
        cqs = cqs_first if cc == 0 else carried(cc, c_cur)
        for j, _, h in now:
            sl = slice(h * M_DV, (h + 1) * M_DV)
            rows = slice(cc * L, (cc + 1) * L)
            out = _mlstm_output(h, qkv(j, cc, h, 0), og_ref[j, rows, sl].astype(F32), mn_ref[:, sl], gates[j, cc],
                                jnp.sum(qks[j, cc, h], axis=1, keepdims=True), pvs[j, cc, h], cqs[j, h],
                                n_cur[j, h])
            h_ref[j, rows, sl] = out.astype(h_ref.dtype)
        for j, _, h in now:
            dec = gates[j, cc][4][:, h:h + 1]
            c_cur[j, h] = dec * c_cur[j, h] + lax.dot_general(qkv(j, cc, h, 2), kws[j, cc, h].astype(BF16),
                                                              TN_DIMS, preferred_element_type=F32)
            n_cur[j, h] = dec * n_cur[j, h] + jnp.sum(kws[j, cc, h], axis=0, keepdims=True)
    for j in range(G):
        for h in range(M_HEADS):
            c_out[j, h] = c_cur[j, h]
            n_out[j, h:h + 1, :] = n_cur[j, h]

    if carry:
        @pl.when(ci == pl.num_programs(1) - 1)
        def _():
            c_ref[...] = c_in[...]
            n_ref[...] = n_in[...]
            m_ref[...] = m_in[...]


MLSTM_GROUP_CARRY = 2
MLSTM_CHUNKS_CARRY = 1
MLSTM_GROUP_SINGLE = 4


def _mlstm(qkv, gates, og, c0, n0, m0, gate_bias, mnorm, nb, L):
    t = qkv.shape[0]
    nc = t // (nb * L)
    carry = nc > 1
    G = MLSTM_GROUP_CARRY if carry else MLSTM_GROUP_SINGLE
    CH = MLSTM_CHUNKS_CARRY if carry else 1
    qkv, gates, og = (a.reshape(nb, nc * L, a.shape[1]) for a in (qkv, gates, og))
    tokmap = lambda b, c: (b, c, 0)
    const2 = lambda b, c: (0, 0)
    state_specs = [pl.BlockSpec((G, M_HEADS, M_DV, M_DK), lambda b, c: (b, 0, 0, 0)),
                   pl.BlockSpec((G, M_HEADS, M_DK), lambda b, c: (b, 0, 0)),
                   pl.BlockSpec((G, 1, LANES), lambda b, c: (b, 0, 0))]
    scratch = [pltpu.VMEM((G, M_HEADS, M_DV, M_DK), F32),
               pltpu.VMEM((G, M_HEADS, M_DK), F32),
               pltpu.VMEM((G, 1, LANES), F32)] if carry else []
    h, c, n, m = pl.pallas_call(
        functools.partial(_mlstm_kernel, L, G, CH, carry),
        grid=(nb // G, nc // CH),
        in_specs=[pl.BlockSpec((G, CH * L, 3072), tokmap),
                  pl.BlockSpec((G, CH * L, LANES), tokmap),
                  pl.BlockSpec((G, CH * L, 1024), tokmap)]
                 + state_specs
                 + [pl.BlockSpec((1, LANES), const2),
                    pl.BlockSpec((1, 1024), const2)],
        out_specs=[pl.BlockSpec((G, CH * L, 1024), tokmap)] + state_specs,
        out_shape=[jax.ShapeDtypeStruct((nb, nc * L, 1024), qkv.dtype),
                   jax.ShapeDtypeStruct((nb, M_HEADS, M_DV, M_DK), F32),
                   jax.ShapeDtypeStruct((nb, M_HEADS, M_DK), F32),
                   jax.ShapeDtypeStruct((nb, 1, LANES), F32)],
        scratch_shapes=scratch,
        compiler_params=_cparams(("arbitrary", "arbitrary")),
        name="mlstm_L%d" % L,
    )(qkv, gates, og, c0, n0, m0, gate_bias, mnorm)
    return h.reshape(t, 1024), c, n, m


def _swa_bias(m, has_prev):
    nkeys = 2 * A_BLOCK
    t_i = lax.broadcasted_iota(I32, (A_GROUP * m, nkeys), 0) & (m - 1)
    s_i = lax.broadcasted_iota(I32, (A_GROUP * m, nkeys), 1)
    diff = t_i + A_BLOCK - s_i
    mask = (diff >= 0) & (diff <= WINDOW)
    if not has_prev:
        mask = mask & (s_i >= A_BLOCK)
    return jnp.where(mask, 0.0, -jnp.inf).astype(F32)


def _swa_group_ones():
    nkeys = 2 * A_BLOCK
    r = lax.broadcasted_iota(I32, (A_KV_HEADS * nkeys, A_KV), 0) >> (nkeys.bit_length() - 1)
    c = lax.broadcasted_iota(I32, (A_KV_HEADS * nkeys, A_KV), 1) >> (A_HD.bit_length() - 1)
    return (r == c).astype(BF16)


def _swa_core(q, k_prev, v_prev, k_cur, v_cur, bias, group_ones, sink_ref, keys_on_lanes=False):
    m = q.shape[0]
    q = q.astype(BF16)
    qst = jnp.concatenate([q[:, j * A_KV:(j + 1) * A_KV] for j in range(A_GROUP)], axis=0)
    key_axis = 1 if keys_on_lanes else 0
    k_all = jnp.concatenate([k_prev, k_cur], axis=key_axis)
    v_all = jnp.concatenate([v_prev, v_cur], axis=key_axis)
    feat = lax.broadcasted_iota(I32, (A_KV, 1) if keys_on_lanes else (1, A_KV), 1 - key_axis)
    kbd, vbd = [], []
    for g in range(A_KV_HEADS):
        in_g = (feat >= g * A_HD) & (feat < (g + 1) * A_HD)
        kbd.append(jnp.where(in_g, k_all, 0.0).astype(BF16))
        vbd.append(jnp.where(in_g, v_all, 0.0).astype(BF16))
    kbd = jnp.concatenate(kbd, axis=key_axis)
    vbd = jnp.concatenate(vbd, axis=key_axis)
    if keys_on_lanes:
        s = jnp.dot(qst, kbd, preferred_element_type=F32)
    else:
        s = lax.dot_general(qst, kbd, NT_DIMS, preferred_element_type=F32)
    nkeys = 2 * A_BLOCK
    ps, sink_terms = [], []
    for g in range(A_KV_HEADS):
        sg = s[:, g * nkeys:(g + 1) * nkeys] + bias
        sink = jnp.concatenate(
            [jnp.full((m, LANES), sink_ref[A_GROUP * g + j], F32) for j in range(A_GROUP)], axis=0)
        mx = jnp.maximum(jnp.broadcast_to(jnp.max(sg, axis=1, keepdims=True), (A_GROUP * m, LANES)), sink)
        ps += [jnp.exp(sg[:, :LANES] - mx).astype(BF16), jnp.exp(sg[:, LANES:] - mx).astype(BF16)]
        sink_terms.append(jnp.exp(sink - mx))
    p = jnp.concatenate(ps, axis=1)
    if keys_on_lanes:
        o = lax.dot_general(p, vbd, NT_DIMS, preferred_element_type=F32)
    else:
        o = jnp.dot(p, vbd, preferred_element_type=F32)
    den = jnp.dot(p, group_ones, preferred_element_type=F32)
    low = lax.broadcasted_iota(I32, (1, LANES), 1) < A_HD
    den = den + jnp.concatenate([jnp.where(low, sink_terms[0], sink_terms[1]),
                                 jnp.where(low, sink_terms[2], sink_terms[3])], axis=1)
    o = o / den
    return jnp.concatenate([o[j * m:(j + 1) * m] for j in range(A_GROUP)], axis=1)


SWA_STEP_BLOCKS = 4
SWA_STEP = SWA_STEP_BLOCKS * A_BLOCK


def _swa_prompt_kernel(sink_ref, q_ref, kp_ref, kc_ref, vp_ref, vc_ref, o_ref, bias_s, bias0_s, ones_s):
    first = (pl.program_id(0) == 0) & (pl.program_id(1) == 0)

    @pl.when(first)
    def _():
        bias_s[...] = _swa_bias(A_BLOCK, True)
        bias0_s[...] = _swa_bias(A_BLOCK, False)
        ones_s[...] = _swa_group_ones()

    bias = bias_s[...]
    bias_first = jnp.where(pl.program_id(1) > 0, bias, bias0_s[...])
    ones = ones_s[...]
    k_prev, v_prev = kp_ref[...], vp_ref[...]
    for b in range(SWA_STEP_BLOCKS):
        rows = slice(b * A_BLOCK, (b + 1) * A_BLOCK)
        k_cur, v_cur = kc_ref[rows], vc_ref[rows]
        o_ref[rows] = _swa_core(q_ref[rows], k_prev, v_prev, k_cur, v_cur, bias_first if b == 0 else bias,
                                ones, sink_ref).astype(o_ref.dtype)
        k_prev, v_prev = k_cur, v_cur


def _swa_prompt(qa, ka, va, sinks, nb):
    t = qa.shape[0]
    nstep = t // (nb * SWA_STEP)
    cur = lambda b, i, s: (b * nstep + i, 0)
    prev = lambda b, i, s: (SWA_STEP_BLOCKS * (b * nstep + i) - jnp.minimum(i, 1), 0)
    return pl.pallas_call(
        _swa_prompt_kernel,
        grid_spec=pltpu.PrefetchScalarGridSpec(
            num_scalar_prefetch=1,
            grid=(nb, nstep),
            in_specs=[pl.BlockSpec((SWA_STEP, 1024), cur),
                      pl.BlockSpec((A_BLOCK, A_KV), prev),
                      pl.BlockSpec((SWA_STEP, A_KV), cur),
                      pl.BlockSpec((A_BLOCK, A_KV), prev),
                      pl.BlockSpec((SWA_STEP, A_KV), cur)],
            out_specs=pl.BlockSpec((SWA_STEP, 1024), cur),
            scratch_shapes=[pltpu.VMEM((A_GROUP * A_BLOCK, 2 * A_BLOCK), F32),
                            pltpu.VMEM((A_GROUP * A_BLOCK, 2 * A_BLOCK), F32),
                            pltpu.VMEM((A_KV_HEADS * 2 * A_BLOCK, A_KV), BF16)]),
        out_shape=jax.ShapeDtypeStruct((t, 1024), BF16),
        compiler_params=_cparams(("arbitrary", "arbitrary")),
        name="swa_prompt",
    )(sinks, qa, ka, ka, va, va)


SWA_SAMPLE_GROUP = 8


def _swa_sample_kernel(T, sink_ref, q_ref, kn_ref, vn_ref, kb_ref, vb_ref, o_ref, ko_ref, vo_ref):
    pad = jnp.zeros((A_BLOCK - T, A_KV), F32)
    bias = _swa_bias(T, True)
    ones = _swa_group_ones()
    lane = lax.broadcasted_iota(I32, (1, WINDOW), 1)
    for j in range(SWA_SAMPLE_GROUP):
        rows = slice(j * T, (j + 1) * T)
        k_new_t = jnp.concatenate([kn_ref[rows], pad], axis=0).T
        v_new_t = jnp.concatenate([vn_ref[rows], pad], axis=0).T
        k_buf_t = kb_ref[j]
        v_buf_t = vb_ref[j]
        o_ref[rows] = _swa_core(q_ref[rows], k_buf_t, v_buf_t, k_new_t, v_new_t, bias, ones, sink_ref,
                                keys_on_lanes=True).astype(o_ref.dtype)
        keep = lane < WINDOW - T
        ko_ref[j] = jnp.where(keep, pltpu.roll(k_buf_t, WINDOW - T, axis=1), pltpu.roll(k_new_t, WINDOW - T, axis=1))
        vo_ref[j] = jnp.where(keep, pltpu.roll(v_buf_t, WINDOW - T, axis=1), pltpu.roll(v_new_t, WINDOW - T, axis=1))


def _swa_sample(qa, ka, va, k_buf, v_buf, sinks, T):
    nb = k_buf.shape[0]
    g = SWA_SAMPLE_GROUP
    tok = lambda b, s: (b, 0)
    buf = lambda b, s: (b, 0, 0)
    return pl.pallas_call(
        functools.partial(_swa_sample_kernel, T),
        grid_spec=pltpu.PrefetchScalarGridSpec(
            num_scalar_prefetch=1,
            grid=(nb // g,),
            in_specs=[pl.BlockSpec((g * T, 1024), tok),
                      pl.BlockSpec((g * T, A_KV), tok),
                      pl.BlockSpec((g * T, A_KV), tok),
                      pl.BlockSpec((g, A_KV, WINDOW), buf),
                      pl.BlockSpec((g, A_KV, WINDOW), buf)],
            out_specs=[pl.BlockSpec((g * T, 1024), tok),
                       pl.BlockSpec((g, A_KV, WINDOW), buf),
                       pl.BlockSpec((g, A_KV, WINDOW), buf)]),
        out_shape=[jax.ShapeDtypeStruct((nb * T, 1024), qa.dtype),
                   jax.ShapeDtypeStruct((nb, A_KV, WINDOW), F32),
                   jax.ShapeDtypeStruct((nb, A_KV, WINDOW), F32)],
        compiler_params=_cparams(("arbitrary",)),
        name="swa_sample",
    )(sinks, qa, ka, va, k_buf, v_buf)


def _merge_kernel(x_ref, hm_ref, ha_ref, gm_ref, ga_ref, wpm_ref, wpa_ref, wo_ref, g2_ref, wr_ref, br_ref,
                  x1_ref, xn_ref, idx_ref, gate_ref, cnt_ref):
    pm = jnp.dot(hm_ref[...].astype(BF16), wpm_ref[...], preferred_element_type=F32)
    pa = jnp.dot(ha_ref[...].astype(BF16), wpa_ref[...], preferred_element_type=F32)
    mixed = gm_ref[...].astype(F32) * pm + ga_ref[...].astype(F32) * pa
    x1 = x_ref[...] + jnp.dot(mixed.astype(BF16), wo_ref[...], preferred_element_type=F32)
    x1_ref[...] = x1
    xn = _rms(x1, g2_ref[...])
    xn_ref[...] = xn.astype(BF16)
    xn_hi = xn.astype(BF16)
    xn_lo = (xn - xn_hi.astype(F32)).astype(BF16)
    part = jnp.dot(xn_hi, wr_ref[...], preferred_element_type=F32)
    logits = (part[:, :LANES] + part[:, LANES:]
              + jnp.dot(xn_lo, wr_ref[:, :LANES], preferred_element_type=F32) + br_ref[...])
    tm = logits.shape[0]
    cur = logits.T[0:N_EXPERTS]
    eid = lax.broadcasted_iota(I32, (N_EXPERTS, tm), 0).astype(F32)
    vals, idxs = [], []
    onehot = jnp.zeros((N_EXPERTS, tm), F32)
    for _ in range(TOP_K):
        mval = jnp.max(cur, axis=0, keepdims=True)
        sel = jnp.min(jnp.where(cur == mval, eid, float(N_EXPERTS)), axis=0, keepdims=True)
        hit = eid == sel
        onehot = onehot + hit.astype(F32)
        cur = jnp.where(hit, -jnp.inf, cur)
        vals.append(mval)
        idxs.append(sel)
    es = [jnp.exp(v - vals[0]) for v in vals]
    tot = es[0] + es[1] + es[2] + es[3]
    row = lax.broadcasted_iota(I32, (SUBLANES, tm), 0)
    res = jnp.zeros((SUBLANES, tm), F32)
    for k in range(TOP_K):
        res = jnp.where(row == k, idxs[k], res)
        res = jnp.where(row == TOP_K + k, es[k] / tot, res)
    res_t = jnp.concatenate([res, jnp.zeros((LANES - SUBLANES, tm), F32)], axis=0).T
    idx_ref[...] = res_t[:, 0:TOP_K].astype(I32)
    gate_ref[...] = res_t[:, TOP_K:2 * TOP_K]
    for j in range(tm // TOK_TILE):
        cnt_ref[j] = jnp.sum(onehot[:, j * TOK_TILE:(j + 1) * TOK_TILE], axis=1, keepdims=True).astype(I32)


def _merge(x, hm, ha, og, wpm, wpa, wo, g2, wr, br):
    t = x.shape[0]
    tm = MERGE_TILE
    row = lambda n: pl.BlockSpec((tm, n), lambda i: (i, 0))
    const = lambda r, c: pl.BlockSpec((r, c), lambda i: (0, 0))
    return pl.pallas_call(
        _merge_kernel,
        grid=(t // tm,),
        in_specs=[row(1024), row(1024), row(1024),
                  pl.BlockSpec((tm, 1024), lambda i: (i, 1)),
                  pl.BlockSpec((tm, 1024), lambda i: (i, 2)),
                  const(1024, 1024), const(1024, 1024), const(1024, 1024), const(1, 1024),
                  const(1024, 2 * LANES), const(1, LANES)],
        out_specs=[row(1024), row(1024), row(TOP_K), row(TOP_K),
                   pl.BlockSpec((tm // TOK_TILE, N_EXPERTS, 1), lambda i: (i, 0, 0))],
        out_shape=[jax.ShapeDtypeStruct((t, 1024), F32),
                   jax.ShapeDtypeStruct((t, 1024), BF16),
                   jax.ShapeDtypeStruct((t, TOP_K), I32),
                   jax.ShapeDtypeStruct((t, TOP_K), F32),
                   jax.ShapeDtypeStruct((t // TOK_TILE, N_EXPERTS, 1), I32)],
        compiler_params=_cparams(("arbitrary",)),
        name="merge_route",
    )(x, hm, ha, og, og, wpm, wpa, wo, g2, wr, br)


def _rows_to_tiles(ref, lead, x):
    chunks = jnp.stack([x[:, s * LANES:(s + 1) * LANES] for s in range(ROW_TILE[0])], axis=0)
    ref[lead] = jnp.swapaxes(chunks, 0, 1).astype(ref.dtype)


def _tiles_to_rows(ref, lead, rows=slice(None)):
    chunks = jnp.swapaxes(ref[lead + (rows,)].astype(F32), 0, 1)
    return jnp.concatenate([chunks[s] for s in range(ROW_TILE[0])], axis=1)


def _stage_rows(idx, off_row):
    tm = idx.shape[0]
    lane = lax.broadcasted_iota(I32, (tm, LANES), 1)
    hits = [lane == idx[:, k:k + 1] for k in range(TOP_K)]
    onehot = sum(h.astype(F32) for h in hits)
    r_i = lax.broadcasted_iota(I32, (tm, tm), 0)
    c_i = lax.broadcasted_iota(I32, (tm, tm), 1)
    before = (c_i < r_i).astype(BF16)
    rank = jnp.dot(before, onehot.astype(BF16), preferred_element_type=F32)
    pos = rank + off_row
    return [jnp.sum(jnp.where(h, pos, 0.0), axis=1, keepdims=True) for h in hits]


def _dispatch_kernel(ntile_a, cnt_ref, off_ref, base_ref, tail_ref, xa_ref, xb_ref, ia_ref, ib_ref, offv_ref,
                     xs_ref, stage, zeros, sem, zsem):
    i = pl.program_id(0)
    from_a = 2 * i < ntile_a
    tm = TOK_TILE

    def stage_tile(slot):
        rows_in = slice(slot * tm, (slot + 1) * tm)
        xn = jnp.where(from_a, xa_ref[rows_in], xb_ref[rows_in])
        idx = jnp.where(from_a, ia_ref[rows_in], ib_ref[rows_in])
        rows = _stage_rows(idx, offv_ref[slot])
        lane = lax.broadcasted_iota(I32, (tm, LANES), 1)
        r4 = jnp.zeros((tm, LANES), F32)
        for k in range(TOP_K):
            r4 = jnp.where(lane == k, rows[k], r4)
        r4t = r4.T.astype(I32)
        r_iota = lax.broadcasted_iota(I32, (STAGE_ROWS, tm), 0)
        sel = r_iota == r4t[0:1, :]
        for k in range(1, TOP_K):
            sel = sel | (r_iota == r4t[k:k + 1, :])
        sel = jnp.where(sel, 1.0, 0.0).astype(BF16)
        return jnp.dot(sel, xn, preferred_element_type=F32)

    def runs(tile, slot):
        out = []
        for e in range(N_EXPERTS):
            n = cnt_ref[tile * N_EXPERTS + e]
            off = off_ref[tile * N_EXPERTS + e]
            base = base_ref[tile * N_EXPERTS + e]
            out.append((n, pltpu.make_async_copy(stage.at[slot, pl.ds(off, n)], xs_ref.at[pl.ds(base, n)],
                                                 sem.at[slot])))
        return out

    def wait_runs(slot):
        pltpu.make_async_copy(stage.at[slot], xs_ref.at[pl.ds(0, STAGE_ROWS)], sem.at[slot]).wait()

    for slot in range(2):
        staged = stage_tile(slot)

        @pl.when(i >= 1)
        def _():
            wait_runs(slot)
        _rows_to_tiles(stage, (slot,), staged)
        for n, cp in runs(2 * i + slot, slot):
            @pl.when(n > 0)
            def _():
                cp.start()

    @pl.when(i == pl.num_programs(0) - 1)
    def _():
        for slot in range(2):
            wait_runs(slot)
        zeros[...] = jnp.zeros(zeros.shape, zeros.dtype)
        tails = []
        for e in range(N_EXPERTS):
            start = tail_ref[e]
            n = tail_ref[N_EXPERTS + e]
            tails.append((n, pltpu.make_async_copy(zeros.at[pl.ds(0, n)], xs_ref.at[pl.ds(start, n)], zsem)))
        for n, cp in tails:
            @pl.when(n > 0)
            def _():
                cp.start()
        for n, cp in tails:
            @pl.when(n > 0)
            def _():
                cp.wait()
        first = tail_ref[2 * N_EXPERTS]
        n_unused = tail_ref[2 * N_EXPERTS + 1]

        def unused_block(j):
            return pltpu.make_async_copy(zeros, xs_ref.at[pl.ds((first + j) * MOE_BLOCK, MOE_BLOCK)], zsem)

        @pl.loop(0, n_unused)
        def _(j):
            unused_block(j).start()

        @pl.loop(0, n_unused)
        def _(j):
            unused_block(j).wait()


def _dispatch(xa, xb, ia, ib, cnt, off, base, tails, offv, n_slots):
    tm = 2 * TOK_TILE
    assert xa.shape[0] % tm == 0 and xb.shape[0] % tm == 0
    nstep_a = xa.shape[0] // tm
    nstep_b = xb.shape[0] // tm
    amap = lambda i, *_: (jnp.minimum(i, nstep_a - 1), 0)
    bmap = lambda i, *_: (jnp.maximum(i - nstep_a, 0), 0)
    return pl.pallas_call(
        functools.partial(_dispatch_kernel, 2 * nstep_a),
        grid_spec=pltpu.PrefetchScalarGridSpec(
            num_scalar_prefetch=4,
            grid=(nstep_a + nstep_b,),
            in_specs=[pl.BlockSpec((tm, 1024), amap),
                      pl.BlockSpec((tm, 1024), bmap),
                      pl.BlockSpec((tm, TOP_K), amap),
                      pl.BlockSpec((tm, TOP_K), bmap),
                      pl.BlockSpec((2, 1, LANES), lambda i, *_: (i, 0, 0))],
            out_specs=pl.BlockSpec(memory_space=pl.ANY),
            scratch_shapes=[pltpu.VMEM((2, STAGE_ROWS) + ROW_TILE, ROW_DTYPE),
                            pltpu.VMEM((MOE_BLOCK,) + ROW_TILE, ROW_DTYPE),
                            pltpu.SemaphoreType.DMA((2,)),
                            pltpu.SemaphoreType.DMA]),
        out_shape=jax.ShapeDtypeStruct((n_slots,) + ROW_TILE, ROW_DTYPE),
        compiler_params=_cparams(("arbitrary",)),
        name="moe_dispatch",
    )(cnt, off, base, tails, xa, xb, ia, ib, offv)


MOE_HALF = MOE_BLOCK // 2


def _experts_kernel(be_ref, rows_ref, next_ref, slot_ref, xs_ref, wgu_hbm, bgu_ref, wd_hbm, bd_ref, y_ref,
                    wgu_f, wd_f, wgu_s, wd_s, sem):
    i = pl.program_id(0)
    rows = rows_ref[i]
    live = rows > 0
    run_start = (i == 0) | (be_ref[i] != be_ref[jnp.maximum(i - 1, 0)])

    def fetch(expert, slot):
        return (pltpu.make_async_copy(wgu_hbm.at[expert], wgu_f.at[slot], sem.at[0, slot]),
                pltpu.make_async_copy(wd_hbm.at[expert], wd_f.at[slot], sem.at[1, slot]))

    for slot in range(2):
        @pl.when(live & run_start & (slot_ref[i] == slot))
        def _():
            @pl.when(i == 0)
            def _():
                for cp in fetch(be_ref[i], slot):
                    cp.start()
            for cp in fetch(be_ref[i], slot):
                cp.wait()

            @pl.when(next_ref[i] >= 0)
            def _():
                for cp in fetch(next_ref[i], 1 - slot):
                    cp.start()
            wgu_s[...] = wgu_f[slot].astype(BF16)
            wd_s[...] = wd_f[slot].astype(BF16)

    def ffn(x):
        h = jnp.dot(x.astype(BF16), wgu_s[...], preferred_element_type=F32) + bgu_ref[0]
        gate = jnp.minimum(h[:, :D_FF], SWIGLU_LIMIT)
        up = jnp.clip(h[:, D_FF:], -SWIGLU_LIMIT, SWIGLU_LIMIT)
        act = gate * jax.nn.sigmoid(SWIGLU_ALPHA * gate) * (up + 1.0)
        return jnp.dot(act.astype(BF16), wd_s[...], preferred_element_type=F32) + bd_ref[0]

    @pl.when(rows > MOE_HALF)
    def _():
        _rows_to_tiles(y_ref, (), ffn(_tiles_to_rows(xs_ref, ())))

    @pl.when(live & (rows <= MOE_HALF))
    def _():
        y_half = ffn(_tiles_to_rows(xs_ref, (), slice(0, MOE_HALF)))
        _rows_to_tiles(y_ref, (), jnp.concatenate([y_half, jnp.zeros((MOE_HALF, D_MODEL), F32)], axis=0))

    @pl.when(rows == 0)
    def _():
        y_ref[...] = jnp.zeros(y_ref.shape, y_ref.dtype)


def _experts(block_expert, block_rows, next_expert, slot, xs, wgu, bgu, wd, bd):
    nb_max = xs.shape[0] // MOE_BLOCK
    blk = lambda i, be, rows, *_: (jnp.where(rows[i] > 0, i, 0), 0, 0)
    wmap = lambda i, be, *_: (be[i], 0, 0)
    return pl.pallas_call(
        _experts_kernel,
        grid_spec=pltpu.PrefetchScalarGridSpec(
            num_scalar_prefetch=4,
            grid=(nb_max,),
            in_specs=[pl.BlockSpec((MOE_BLOCK,) + ROW_TILE, blk),
                      pl.BlockSpec(memory_space=pl.ANY),
                      pl.BlockSpec((1, 1, 2 * D_FF), wmap),
                      pl.BlockSpec(memory_space=pl.ANY),
                      pl.BlockSpec((1, 1, D_MODEL), wmap)],
            out_specs=pl.BlockSpec((MOE_BLOCK,) + ROW_TILE, lambda i, *_: (i, 0, 0)),
            scratch_shapes=[pltpu.VMEM((2, D_MODEL, 2 * D_FF), F32),
                            pltpu.VMEM((2, D_FF, D_MODEL), F32),
                            pltpu.VMEM((D_MODEL, 2 * D_FF), BF16),
                            pltpu.VMEM((D_FF, D_MODEL), BF16),
                            pltpu.SemaphoreType.DMA((2, 2))]),
        out_shape=jax.ShapeDtypeStruct(xs.shape, ROW_DTYPE),
        compiler_params=_cparams(("arbitrary",)),
        name="moe_experts",
    )(block_expert, block_rows, next_expert, slot, xs, wgu, bgu, wd, bd)


def _combine_kernel(cnt_ref, off_ref, base_ref, x1_ref, idx_ref, gate_ref, offv_ref, gf_ref, y_ref,
                    o_ref, stage, sem):
    i = pl.program_id(0)
    last = pl.num_programs(0) - 1
    tm = TOK_TILE

    def runs(tile, slot):
        out = []
        for e in range(N_EXPERTS):
            n = cnt_ref[tile * N_EXPERTS + e]
            off = off_ref[tile * N_EXPERTS + e]
            base = base_ref[tile * N_EXPERTS + e]
            out.append((n, pltpu.make_async_copy(y_ref.at[pl.ds(base, n)], stage.at[slot, pl.ds(off, n)],
                                                 sem.at[slot])))
        return out

    def start_runs(tile, slot):
        for n, cp in runs(tile, slot):
            @pl.when(n > 0)
            def _():
                cp.start()

    @pl.when(i == 0)
    def _():
        for half in range(2):
            start_runs(half, half)

    for pair in range(2):
        @pl.when(((i & 1) != pair) & (i < last))
        def _():
            for half in range(2):
                start_runs(2 * (i + 1) + half, 2 * pair + half)

    gmats = []
    for half in range(2):
        rows_in = slice(half * tm, (half + 1) * tm)
        rows = _stage_rows(idx_ref[rows_in], offv_ref[half])
        lane = lax.broadcasted_iota(I32, (tm, STAGE_ROWS), 1)
        gates = gate_ref[rows_in]
        gmat = jnp.zeros((tm, STAGE_ROWS), F32)
        for k in range(TOP_K):
            gmat = gmat + jnp.where(lane == rows[k].astype(I32), gates[:, k:k + 1], 0.0)
        gmats.append(gmat.astype(BF16))
    for pair in range(2):
        @pl.when((i & 1) == pair)
        def _():
            for half in range(2):
                slot = 2 * pair + half
                rows_in = slice(half * tm, (half + 1) * tm)
                pltpu.make_async_copy(y_ref.at[pl.ds(0, STAGE_ROWS)], stage.at[slot], sem.at[slot]).wait()
                moe = jnp.dot(gmats[half], _tiles_to_rows(stage, (slot,)).astype(BF16), preferred_element_type=F32)
                o_ref[rows_in] = _rms(x1_ref[rows_in] + moe, gf_ref[...])


def _combine(x1, idx, gates, cnt, off, base, offv, gf, y):
    t = x1.shape[0]
    tm = 2 * TOK_TILE
    assert t % tm == 0
    return pl.pallas_call(
        _combine_kernel,
        grid_spec=pltpu.PrefetchScalarGridSpec(
            num_scalar_prefetch=3,
            grid=(t // tm,),
            in_specs=[pl.BlockSpec((tm, 1024), lambda i, *_: (i, 0)),
                      pl.BlockSpec((tm, TOP_K), lambda i, *_: (i, 0)),
                      pl.BlockSpec((tm, TOP_K), lambda i, *_: (i, 0)),
                      pl.BlockSpec((2, 1, LANES), lambda i, *_: (i, 0, 0)),
                      pl.BlockSpec((1, 1024), lambda i, *_: (0, 0)),
                      pl.BlockSpec(memory_space=pl.ANY)],
            out_specs=pl.BlockSpec((tm, 1024), lambda i, *_: (i, 0)),
            scratch_shapes=[pltpu.VMEM((4, STAGE_ROWS) + ROW_TILE, ROW_DTYPE),
                            pltpu.SemaphoreType.DMA((4,))]),
        out_shape=jax.ShapeDtypeStruct((t, 1024), F32),
        compiler_params=_cparams(("arbitrary",)),
        name="moe_combine",
    )(cnt, off, base, x1, idx, gates, offv, gf, y)


def _prep_w_in(w):
    wt = w.T
    c_if, c_qa, c_kv, c_g = 4096, 4104, 5128, 5640
    wm = wt[:c_if].astype(BF16)
    wif = jnp.pad(wt[c_if:c_qa], ((0, LANES - 2 * M_HEADS), (0, 0))).astype(BF16)
    wqa = (jnp.concatenate([wt[c_qa + (A_GROUP * g + j) * A_HD:c_qa + (A_GROUP * g + j + 1) * A_HD]
                            for j in range(A_GROUP) for g in range(A_KV_HEADS)], axis=0)
           * (A_HD ** -0.5)).astype(BF16)
    wkv = wt[c_kv:c_g].astype(BF16)
    wg = wt[c_g:].astype(BF16)
    return wm, wg, wqa, wkv, wif


def _rope_tables(pos):
    half = ROT_DIM // 2
    inv = ROPE_THETA ** (-(jnp.arange(half, dtype=F32) * 2.0 / ROT_DIM))
    ang = pos.astype(F32)[:, None] * inv[None, :]
    lane = jnp.arange(LANES) % A_HD
    hit = (lane[None, :] % half == jnp.arange(half)[:, None]) & (lane[None, :] < ROT_DIM)
    spread_cos = hit.astype(F32)
    spread_sin = jnp.where(hit, jnp.where(lane < half, -1.0, 1.0)[None, :], 0.0)
    cos_t = jnp.dot(jnp.cos(ang), spread_cos, precision=lax.Precision.HIGHEST) + (lane >= ROT_DIM).astype(F32)[None, :]
    sin_t = jnp.dot(jnp.sin(ang), spread_sin, precision=lax.Precision.HIGHEST)
    return cos_t, sin_t


def _round_up(x, m):
    return (x + m - 1) // m * m


def kernel(x_prompt, x_sample, cache_k, cache_v, state_c, state_n, state_m, norm1, w_in, b_igate, b_fgate,
           mlstm_norm, w_proj_m, w_proj_a, attn_sinks, w_out, norm2, w_router, b_router, w_gate_up, b_gate_up,
           w_down, b_down, norm_f):
    bp, sp, _ = x_prompt.shape
    bs, ts, _ = x_sample.shape
    past_len = 16384
    l = 0
    w1 = _prep_w_in(w_in[l])
    g1 = norm1[l][None, :]
    g2 = norm2[l][None, :]
    gf = norm_f[None, :]
    gate_bias = jnp.pad(jnp.concatenate([b_igate[l], b_fgate[l]]), (0, LANES - 2 * M_HEADS))[None, :]
    mn = mlstm_norm[l][None, :]
    wpm = w_proj_m[l].astype(BF16)
    wpa = jnp.concatenate([w_proj_a[l][(A_GROUP * g + j) * A_HD:(A_GROUP * g + j + 1) * A_HD]
                           for j in range(A_GROUP) for g in range(A_KV_HEADS)], axis=0).astype(BF16)
    wo = w_out[l].astype(BF16)
    wr_f = jnp.pad(w_router[l], ((0, 0), (0, LANES - N_EXPERTS)))
    wr_hi = wr_f.astype(BF16)
    wr = jnp.concatenate([wr_hi, (wr_f - wr_hi.astype(F32)).astype(BF16)], axis=1)
    br = jnp.pad(b_router[l], (0, LANES - N_EXPERTS))[None, :]
    wgu = w_gate_up[l]
    bgu = b_gate_up[l][:, None, :]
    wd = w_down[l]
    bd = b_down[l][:, None, :]
    sinks = attn_sinks[l]

    def mixer(x, nb, seq, pos0, L, c0, n0, m0, k_buf, v_buf, tm):
        t = nb * seq
        xf = x.reshape(t, D_MODEL)
        cos_t, sin_t = _rope_tables(pos0 + jnp.arange(seq))
        if seq < tm:
            cos_t, sin_t = jnp.tile(cos_t, (tm // seq, 1)), jnp.tile(sin_t, (tm // seq, 1))
        act = BF16 if L % 16 == 0 else F32
        qkv, og, qa, ka, va, gates = _inproj(xf, g1, w1, cos_t, sin_t, tm, max(seq // tm, 1), act)
        m0p = jnp.pad(m0, ((0, 0), (0, LANES - M_HEADS)))[:, None, :]
        hm, c, n, m = _mlstm(qkv, gates, og, c0, n0, m0p, gate_bias, mn, nb, L)
        if k_buf is None:
            ha = _swa_prompt(qa, ka, va, sinks, nb)
            k_keep = ka.reshape(nb, seq, A_KV)[:, seq - WINDOW:].reshape(nb, WINDOW, A_KV_HEADS, A_HD)
            v_keep = va.reshape(nb, seq, A_KV)[:, seq - WINDOW:].reshape(nb, WINDOW, A_KV_HEADS, A_HD)
        else:
            to_t = lambda a: a.transpose(0, 2, 3, 1).reshape(nb, A_KV, WINDOW)
            from_t = lambda a: a.reshape(nb, A_KV_HEADS, A_HD, WINDOW).transpose(0, 3, 1, 2)
            ha, k_keep, v_keep = _swa_sample(qa, ka, va, to_t(k_buf), to_t(v_buf), sinks, seq)
            k_keep, v_keep = from_t(k_keep), from_t(v_keep)
        x1, xn, idx, gate, cnt = _merge(xf, hm, ha, og, wpm, wpa, wo, g2, wr, br)
        return (x1, xn, idx, gate, cnt[:, :, 0]), (k_keep, v_keep, c, n, m[:, 0, :M_HEADS])

    zc = jnp.zeros((bp, M_HEADS, M_DV, M_DK), F32)
    zn = jnp.zeros((bp, M_HEADS, M_DK), F32)
    zm = jnp.zeros((bp, M_HEADS), F32)
    rp, sp_out = mixer(x_prompt, bp, sp, 0, M_CHUNK, zc, zn, zm, None, None, 512)
    rs, ss_out = mixer(x_sample, bs, ts, past_len, ts, state_c[l], state_n[l], state_m[l],
                       cache_k[l], cache_v[l], 256)

    cnt = jnp.concatenate([rp[4], rs[4]], axis=0)
    ntile_p = rp[4].shape[0]
    off = jnp.cumsum(cnt, axis=1) - cnt
    per_expert = jnp.sum(cnt, axis=0)
    padded = _round_up(per_expert, MOE_BLOCK)
    padded_end = jnp.cumsum(padded)
    expert_start = padded_end - padded
    base = expert_start[None, :] + jnp.cumsum(cnt, axis=0) - cnt
    n_tok = bp * sp + bs * ts
    n_slots = _round_up(n_tok * TOP_K, MOE_BLOCK) + N_EXPERTS * MOE_BLOCK
    nb_max = n_slots // MOE_BLOCK
    block_row = jnp.arange(nb_max, dtype=I32) * MOE_BLOCK
    block_expert = jnp.minimum(jnp.sum(block_row[:, None] >= padded_end[None, :], axis=1), N_EXPERTS - 1).astype(I32)
    n_blocks = (padded_end[-1] // MOE_BLOCK).astype(I32)[None]
    eid = jnp.arange(N_EXPERTS, dtype=I32)
    later_used = (eid[None, :] > eid[:, None]) & (padded[None, :] > 0)
    next_used = jnp.min(jnp.where(later_used, eid[None, :], N_EXPERTS), axis=1)
    next_used = jnp.where(next_used < N_EXPERTS, next_used, -1).astype(I32)
    run_parity = ((jnp.cumsum(padded > 0) - 1) & 1).astype(I32)
    of_block = lambda table: jnp.sum(jnp.where(block_expert[:, None] == eid[None, :], table[None, :], 0),
                                     axis=1).astype(I32)
    block_rows = jnp.clip(of_block(per_expert) - (block_row - of_block(expert_start)), 0, MOE_BLOCK).astype(I32)
    tails = jnp.concatenate([expert_start + per_expert, padded - per_expert,
                             n_blocks, nb_max - n_blocks]).astype(I32)
    flat = lambda a: a.reshape(-1).astype(I32)
    offv = jnp.pad(off, ((0, 0), (0, LANES - N_EXPERTS))).astype(F32)[:, None, :]

    xs = _dispatch(rp[1], rs[1], rp[2], rs[2], flat(cnt), flat(off), flat(base), tails, offv, n_slots)
    y = _experts(block_expert, block_rows, of_block(next_used), of_block(run_parity), xs, wgu, bgu, wd, bd)
    y_p = _combine(rp[0], rp[2], rp[3], flat(cnt[:ntile_p]), flat(off[:ntile_p]), flat(base[:ntile_p]),
                   offv[:ntile_p], gf, y)
    y_s = _combine(rs[0], rs[2], rs[3], flat(cnt[ntile_p:]), flat(off[ntile_p:]), flat(base[ntile_p:]),
                   offv[ntile_p:], gf, y)

    kp, vp, cp_, np_, mp = sp_out
    ks, vs, cs, ns, ms = ss_out
    return (y_p.reshape(bp, sp, D_MODEL), y_s.reshape(bs, ts, D_MODEL),
            kp[None], vp[None], cp_[None], np_[None], mp[None],
            ks[None], vs[None], cs[None], ns[None], ms[None])
```

```python
import functools

import jax
import jax.numpy as jnp
from jax import lax
from jax.experimental import pallas as pl
from jax.experimental.pallas import tpu as pltpu

F32 = jnp.float32
BF16 = jnp.bfloat16
I32 = jnp.int32

D_MODEL = 1024
M_HEADS = 4
M_DK = 256
M_DV = 256
M_CHUNK = 128
A_Q_HEADS = 16
A_KV_HEADS = 4
A_GROUP = 4
A_HD = 64
A_KV = A_KV_HEADS * A_HD
WINDOW = 128
A_BLOCK = 128
ROT_DIM = 16
ROPE_THETA = 500000.0
N_EXPERTS = 32
TOP_K = 4
D_FF = 1024
SWIGLU_LIMIT = 7.0
SWIGLU_ALPHA = 1.702
NORM_EPS = 1e-6

LANES = 128
SUBLANES = 8
VMEM_LIMIT = 56 * 1024 * 1024

MERGE_TILE = 512
TOK_TILE = 256
STAGE_ROWS = 4 * TOK_TILE
ROW_TILE = (8, 128)
ROW_DTYPE = F32
MOE_BLOCK = 512


def _cparams(sem):
    return pltpu.CompilerParams(dimension_semantics=sem, vmem_limit_bytes=VMEM_LIMIT)


def _rms(x, g):
    return x * lax.rsqrt(jnp.mean(x * x, axis=-1, keepdims=True) + NORM_EPS) * g


def _rope(x, cos_t, sin_t, first_half):
    n = x.shape[1]
    fwd = pltpu.roll(x, n - ROT_DIM // 2, axis=1)
    bwd = pltpu.roll(x, ROT_DIM // 2, axis=1)
    outs = []
    for b in range(n // LANES):
        sl = slice(b * LANES, (b + 1) * LANES)
        partner = jnp.where(first_half, fwd[:, sl], bwd[:, sl])
        outs.append(x[:, sl] * cos_t + partner * sin_t)
    return jnp.concatenate(outs, axis=1)


def _inproj_kernel(x_ref, g1_ref, wm_ref, wg_ref, wqa_ref, wkv_ref, wif_ref, cos_ref, sin_ref,
                   qkv_ref, og_ref, qa_ref, ka_ref, va_ref, gate_ref):
    x = x_ref[...]
    xn = _rms(x, g1_ref[...]).astype(BF16)
    act = qkv_ref.dtype

    def proj(w_ref, c0, n):
        return lax.dot_general(xn, w_ref[c0:c0 + n, :], (((1,), (1,)), ((), ())), preferred_element_type=F32)

    qkv_ref[:, 0:1024] = proj(wm_ref, 0, 1024).astype(act)
    qkv_ref[:, 1024:2048] = (proj(wm_ref, 1024, 1024) * (M_DK ** -0.5)).astype(act)
    qkv_ref[:, 2048:3072] = proj(wm_ref, 2048, 1024).astype(act)
    og_ref[:, 0:1024] = jax.nn.sigmoid(proj(wm_ref, 3072, 1024)).astype(act)
    og_ref[:, 1024:2048] = jax.nn.sigmoid(proj(wg_ref, 0, 1024)).astype(act)
    og_ref[:, 2048:3072] = jax.nn.sigmoid(proj(wg_ref, 1024, 1024)).astype(act)
    cos_t = cos_ref[...]
    sin_t = sin_ref[...]
    lane = lax.broadcasted_iota(I32, (1, LANES), 1)
    first_half = (lane & (ROT_DIM - 1)) < (ROT_DIM // 2)
    qa_ref[...] = _rope(proj(wqa_ref, 0, 1024), cos_t, sin_t, first_half).astype(act)
    ka_ref[...] = _rope(proj(wkv_ref, 0, A_KV), cos_t, sin_t, first_half)
    va_ref[...] = proj(wkv_ref, A_KV, A_KV)
    gate_ref[...] = proj(wif_ref, 0, LANES)


def _inproj(x, g1, weights, cos_t, sin_t, tm, rope_blocks, act):
    t = x.shape[0]
    tok = lambda n, dt: jax.ShapeDtypeStruct((t, n), dt)
    row = lambda n: pl.BlockSpec((tm, n), lambda i: (i, 0))
    resident = lambda w: pl.BlockSpec(w.shape, lambda i: (0, 0), pipeline_mode=pl.Buffered(1))
    return pl.pallas_call(
        _inproj_kernel,
        grid=(t // tm,),
        in_specs=[row(D_MODEL), pl.BlockSpec((1, D_MODEL), lambda i: (0, 0))]
                 + [resident(w) for w in weights]
                 + [pl.BlockSpec((tm, LANES), lambda i: (i % rope_blocks, 0)),
                    pl.BlockSpec((tm, LANES), lambda i: (i % rope_blocks, 0))],
        out_specs=[row(3072), row(3072), row(1024), row(A_KV), row(A_KV), row(LANES)],
        out_shape=[tok(3072, act), tok(3072, act), tok(1024, act), tok(A_KV, F32), tok(A_KV, F32),
                   tok(LANES, F32)],
        compiler_params=_cparams(("arbitrary",)),
        name="inproj",
    )(x, g1, *weights, cos_t, sin_t)


def _mlstm_gates(gc, bias, m_prev, L):
    z = gc + bias
    lf = jnp.minimum(z, 0.0) - jnp.log1p(jnp.exp(-jnp.abs(z)))
    row = lax.broadcasted_iota(I32, (L, LANES), 0)
    fc = lf
    sh = 1
    while sh < L:
        fc = fc + jnp.where(row >= sh, pltpu.roll(fc, sh, axis=0), 0.0)
        sh *= 2
    fcum = pltpu.roll(fc, LANES - M_HEADS, axis=1)
    a = z - fcum
    cmx = a
    sh = 1
    while sh < L:
        cmx = jnp.maximum(cmx, jnp.where(row >= sh, pltpu.roll(cmx, sh, axis=0), -jnp.inf))
        sh *= 2
    mx = jnp.maximum(m_prev, cmx)
    inter = jnp.exp(m_prev - mx)
    f_end = fcum[L - 1:L]
    m_end = f_end + mx[L - 1:L]
    decay = jnp.exp(f_end + m_prev - m_end)
    wsrc = jnp.exp(a + (f_end - m_end))
    return a, mx, inter, wsrc, decay, m_end


NT_DIMS = (((1,), (1,)), ((), ()))
TN_DIMS = (((0,), (0,)), ((), ()))


def _mlstm_decay(h, L, gates):
    a, mx = gates[0], gates[1]
    t_i = lax.broadcasted_iota(I32, (L, L), 0)
    s_i = lax.broadcasted_iota(I32, (L, L), 1)
    a_row = jnp.sum(jnp.where(t_i == s_i, a[:, h:h + 1], 0.0), axis=0, keepdims=True)
    return jnp.exp(jnp.where(s_i <= t_i, a_row - mx[:, h:h + 1], -jnp.inf))


def _mlstm_output(h, q, og, mn, gates, qk_sum, pv, cq, n_h):
    inter_col = gates[2][:, h:h + 1]
    num = inter_col * cq + pv
    den = inter_col * jnp.sum(q.astype(F32) * n_h, axis=1, keepdims=True) + qk_sum
    hh = num / jnp.maximum(jnp.abs(den), 1.0)
    hn = hh * lax.rsqrt(jnp.mean(hh * hh, axis=1, keepdims=True) + NORM_EPS)
    return hn * mn * og


def _mlstm_kernel(L, G, CH, carry, qkv_ref, gate_ref, og_ref, c0_ref, n0_ref, m0_ref, bias_ref, mn_ref,
                  h_ref, c_ref, n_ref, m_ref, *scratch):
    if carry:
        c_in, n_in, m_in = scratch
        ci = pl.program_id(1)

        @pl.when(ci == 0)
        def _():
            c_in[...] = c0_ref[...]
            n_in[...] = n0_ref[...]
            m_in[...] = m0_ref[...]
        c_out, n_out, m_out = scratch
    else:
        c_in, n_in, m_in = c0_ref, n0_ref, m0_ref
        c_out, n_out, m_out = c_ref, n_ref, m_ref

    def qkv(j, cc, h, part):
        cols = slice(part * 1024 + h * M_DK, part * 1024 + (h + 1) * M_DK)
        return qkv_ref[j, cc * L:(cc + 1) * L, cols].astype(BF16)

    units = [(j, cc, h) for cc in range(CH) for j in range(G) for h in range(M_HEADS)]
    gates = {}
    for j in range(G):
        m_cur = m_in[j]
        for cc in range(CH):
            gates[j, cc] = _mlstm_gates(gate_ref[j, cc * L:(cc + 1) * L], bias_ref[...], m_cur, L)
            m_cur = gates[j, cc][5]
        m_out[j] = m_cur
    def carried(cc, c_cur):
        return {(j, h): lax.dot_general(qkv(j, cc, h, 0), c_cur[j, h].astype(BF16), NT_DIMS,
                                        preferred_element_type=F32) for j, c, h in units if c == cc}

    c_cur = {(j, h): c_in[j, h] for j in range(G) for h in range(M_HEADS)}
    n_cur = {(j, h): n_in[j, h:h + 1, :] for j in range(G) for h in range(M_HEADS)}
    scores = {u: lax.dot_general(qkv(*u, 0), qkv(*u, 1), NT_DIMS, preferred_element_type=F32) for u in units}
    cqs_first = carried(0, c_cur)
    qks = {(j, cc, h): scores[j, cc, h] * _mlstm_decay(h, L, gates[j, cc]) for j, cc, h in units}
    pvs = {u: jnp.dot(qks[u].astype(BF16), qkv(*u, 2), preferred_element_type=F32) for u in units}
    kws = {(j, cc, h): gates[j, cc][3][:, h:h + 1] * qkv(j, cc, h, 1).astype(F32) for j, cc, h in units}
    for cc in range(CH):
        now = [u for u in units if u[1] == cc]
        cqs = cqs_first if cc == 0 else carried(cc, c_cur)
        for j, _, h in now:
            sl = slice(h * M_DV, (h + 1) * M_DV)
            rows = slice(cc * L, (cc + 1) * L)
            out = _mlstm_output(h, qkv(j, cc, h, 0), og_ref[j, rows, sl].astype(F32), mn_ref[:, sl], gates[j, cc],
                                jnp.sum(qks[j, cc, h], axis=1, keepdims=True), pvs[j, cc, h], cqs[j, h],
                                n_cur[j, h])
            h_ref[j, rows, sl] = out.astype(h_ref.dtype)
        for j, _, h in now:
            dec = gates[j, cc][4][:, h:h + 1]
            c_cur[j, h] = dec * c_cur[j, h] + lax.dot_general(qkv(j, cc, h, 2), kws[j, cc, h].astype(BF16),
                                                              TN_DIMS, preferred_element_type=F32)
            n_cur[j, h] = dec * n_cur[j, h] + jnp.sum(kws[j, cc, h], axis=0, keepdims=True)
    for j in range(G):
        for h in range(M_HEADS):
            c_out[j, h] = c_cur[j, h]
            n_out[j, h:h + 1, :] = n_cur[j, h]

    if carry:
        @pl.when(ci == pl.num_programs(1) - 1)
        def _():
            c_ref[...] = c_in[...]
            n_ref[...] = n_in[...]
            m_ref[...] = m_in[...]


MLSTM_GROUP_CARRY = 2
MLSTM_CHUNKS_CARRY = 1
MLSTM_GROUP_SINGLE = 8


def _mlstm(qkv, gates, og, c0, n0, m0, gate_bias, mnorm, nb, L):
    t = qkv.shape[0]
    nc = t // (nb * L)
    carry = nc > 1
    G = MLSTM_GROUP_CARRY if carry else MLSTM_GROUP_SINGLE
    CH = MLSTM_CHUNKS_CARRY if carry else 1
    qkv, gates, og = (a.reshape(nb, nc * L, a.shape[1]) for a in (qkv, gates, og))
    tokmap = lambda b, c: (b, c, 0)
    const2 = lambda b, c: (0, 0)
    state_specs = [pl.BlockSpec((G, M_HEADS, M_DV, M_DK), lambda b, c: (b, 0, 0, 0)),
                   pl.BlockSpec((G, M_HEADS, M_DK), lambda b, c: (b, 0, 0)),
                   pl.BlockSpec((G, 1, LANES), lambda b, c: (b, 0, 0))]
    scratch = [pltpu.VMEM((G, M_HEADS, M_DV, M_DK), F32),
               pltpu.VMEM((G, M_HEADS, M_DK), F32),
               pltpu.VMEM((G, 1, LANES), F32)] if carry else []
    h, c, n, m = pl.pallas_call(
        functools.partial(_mlstm_kernel, L, G, CH, carry),
        grid=(nb // G, nc // CH),
        in_specs=[pl.BlockSpec((G, CH * L, 3072), tokmap),
                  pl.BlockSpec((G, CH * L, LANES), tokmap),
                  pl.BlockSpec((G, CH * L, 1024), tokmap)]
                 + state_specs
                 + [pl.BlockSpec((1, LANES), const2),
                    pl.BlockSpec((1, 1024), const2)],
        out_specs=[pl.BlockSpec((G, CH * L, 1024), tokmap)] + state_specs,
        out_shape=[jax.ShapeDtypeStruct((nb, nc * L, 1024), qkv.dtype),
                   jax.ShapeDtypeStruct((nb, M_HEADS, M_DV, M_DK), F32),
                   jax.ShapeDtypeStruct((nb, M_HEADS, M_DK), F32),
                   jax.ShapeDtypeStruct((nb, 1, LANES), F32)],
        scratch_shapes=scratch,
        compiler_params=_cparams(("arbitrary", "arbitrary")),
        name="mlstm_L%d" % L,
    )(qkv, gates, og, c0, n0, m0, gate_bias, mnorm)
    return h.reshape(t, 1024), c, n, m


def _swa_bias(m, has_prev):
    nkeys = 2 * A_BLOCK
    t_i = lax.broadcasted_iota(I32, (A_GROUP * m, nkeys), 0) & (m - 1)
    s_i = lax.broadcasted_iota(I32, (A_GROUP * m, nkeys), 1)
    diff = t_i + A_BLOCK - s_i
    mask = (diff >= 0) & (diff <= WINDOW)
    if not has_prev:
        mask = mask & (s_i >= A_BLOCK)
    return jnp.where(mask, 0.0, -jnp.inf).astype(F32)


def _swa_group_ones():
    nkeys = 2 * A_BLOCK
    r = lax.broadcasted_iota(I32, (A_KV_HEADS * nkeys, A_KV), 0) >> (nkeys.bit_length() - 1)
    c = lax.broadcasted_iota(I32, (A_KV_HEADS * nkeys, A_KV), 1) >> (A_HD.bit_length() - 1)
    return (r == c).astype(BF16)


def _swa_core(q, k_prev, v_prev, k_cur, v_cur, bias, group_ones, sink_ref, keys_on_lanes=False):
    m = q.shape[0]
    q = q.astype(BF16)
    qst = jnp.concatenate([q[:, j * A_KV:(j + 1) * A_KV] for j in range(A_GROUP)], axis=0)
    key_axis = 1 if keys_on_lanes else 0
    k_all = jnp.concatenate([k_prev, k_cur], axis=key_axis)
    v_all = jnp.concatenate([v_prev, v_cur], axis=key_axis)
    feat = lax.broadcasted_iota(I32, (A_KV, 1) if keys_on_lanes else (1, A_KV), 1 - key_axis)
    kbd, vbd = [], []
    for g in range(A_KV_HEADS):
        in_g = (feat >= g * A_HD) & (feat < (g + 1) * A_HD)
        kbd.append(jnp.where(in_g, k_all, 0.0).astype(BF16))
        vbd.append(jnp.where(in_g, v_all, 0.0).astype(BF16))
    kbd = jnp.concatenate(kbd, axis=key_axis)
    vbd = jnp.concatenate(vbd, axis=key_axis)
    if keys_on_lanes:
        s = jnp.dot(qst, kbd, preferred_element_type=F32)
    else:
        s = lax.dot_general(qst, kbd, NT_DIMS, preferred_element_type=F32)
    nkeys = 2 * A_BLOCK
    ps, sink_terms = [], []
    for g in range(A_KV_HEADS):
        sg = s[:, g * nkeys:(g + 1) * nkeys] + bias
        sink = jnp.concatenate(
            [jnp.full((m, LANES), sink_ref[A_GROUP * g + j], F32) for j in range(A_GROUP)], axis=0)
        mx = jnp.maximum(jnp.broadcast_to(jnp.max(sg, axis=1, keepdims=True), (A_GROUP * m, LANES)), sink)
        ps += [jnp.exp(sg[:, :LANES] - mx).astype(BF16), jnp.exp(sg[:, LANES:] - mx).astype(BF16)]
        sink_terms.append(jnp.exp(sink - mx))
    p = jnp.concatenate(ps, axis=1)
    if keys_on_lanes:
        o = lax.dot_general(p, vbd, NT_DIMS, preferred_element_type=F32)
    else:
        o = jnp.dot(p, vbd, preferred_element_type=F32)
    den = jnp.dot(p, group_ones, preferred_element_type=F32)
    low = lax.broadcasted_iota(I32, (1, LANES), 1) < A_HD
    den = den + jnp.concatenate([jnp.where(low, sink_terms[0], sink_terms[1]),
                                 jnp.where(low, sink_terms[2], sink_terms[3])], axis=1)
    o = o / den
    return jnp.concatenate([o[j * m:(j + 1) * m] for j in range(A_GROUP)], axis=1)


SWA_STEP_BLOCKS = 8
SWA_STEP = SWA_STEP_BLOCKS * A_BLOCK


def _swa_prompt_kernel(sink_ref, q_ref, kp_ref, kc_ref, vp_ref, vc_ref, o_ref, bias_s, bias0_s, ones_s):
    first = (pl.program_id(0) == 0) & (pl.program_id(1) == 0)

    @pl.when(first)
    def _():
        bias_s[...] = _swa_bias(A_BLOCK, True)
        bias0_s[...] = _swa_bias(A_BLOCK, False)
        ones_s[...] = _swa_group_ones()

    bias = bias_s[...]
    bias_first = jnp.where(pl.program_id(1) > 0, bias, bias0_s[...])
    ones = ones_s[...]
    k_prev, v_prev = kp_ref[...], vp_ref[...]
    for b in range(SWA_STEP_BLOCKS):
        rows = slice(b * A_BLOCK, (b + 1) * A_BLOCK)
        k_cur, v_cur = kc_ref[rows], vc_ref[rows]
        o_ref[rows] = _swa_core(q_ref[rows], k_prev, v_prev, k_cur, v_cur, bias_first if b == 0 else bias,
                                ones, sink_ref).astype(o_ref.dtype)
        k_prev, v_prev = k_cur, v_cur


def _swa_prompt(qa, ka, va, sinks, nb):
    t = qa.shape[0]
    nstep = t // (nb * SWA_STEP)
    cur = lambda b, i, s: (b * nstep + i, 0)
    prev = lambda b, i, s: (SWA_STEP_BLOCKS * (b * nstep + i) - jnp.minimum(i, 1), 0)
    return pl.pallas_call(
        _swa_prompt_kernel,
        grid_spec=pltpu.PrefetchScalarGridSpec(
            num_scalar_prefetch=1,
            grid=(nb, nstep),
            in_specs=[pl.BlockSpec((SWA_STEP, 1024), cur),
                      pl.BlockSpec((A_BLOCK, A_KV), prev),
                      pl.BlockSpec((SWA_STEP, A_KV), cur),
                      pl.BlockSpec((A_BLOCK, A_KV), prev),
                      pl.BlockSpec((SWA_STEP, A_KV), cur)],
            out_specs=pl.BlockSpec((SWA_STEP, 1024), cur),
            scratch_shapes=[pltpu.VMEM((A_GROUP * A_BLOCK, 2 * A_BLOCK), F32),
                            pltpu.VMEM((A_GROUP * A_BLOCK, 2 * A_BLOCK), F32),
                            pltpu.VMEM((A_KV_HEADS * 2 * A_BLOCK, A_KV), BF16)]),
        out_shape=jax.ShapeDtypeStruct((t, 1024), BF16),
        compiler_params=_cparams(("arbitrary", "arbitrary")),
        name="swa_prompt",
    )(sinks, qa, ka, ka, va, va)


SWA_SAMPLE_GROUP = 8


def _swa_sample_kernel(T, sink_ref, q_ref, kn_ref, vn_ref, kb_ref, vb_ref, o_ref, ko_ref, vo_ref):
    pad = jnp.zeros((A_BLOCK - T, A_KV), F32)
    bias = _swa_bias(T, True)
    ones = _swa_group_ones()
    lane = lax.broadcasted_iota(I32, (1, WINDOW), 1)
    for j in range(SWA_SAMPLE_GROUP):
        rows = slice(j * T, (j + 1) * T)
        k_new_t = jnp.concatenate([kn_ref[rows], pad], axis=0).T
        v_new_t = jnp.concatenate([vn_ref[rows], pad], axis=0).T
        k_buf_t = kb_ref[j]
        v_buf_t = vb_ref[j]
        o_ref[rows] = _swa_core(q_ref[rows], k_buf_t, v_buf_t, k_new_t, v_new_t, bias, ones, sink_ref,
                                keys_on_lanes=True).astype(o_ref.dtype)
        keep = lane < WINDOW - T
        ko_ref[j] = jnp.where(keep, pltpu.roll(k_buf_t, WINDOW - T, axis=1), pltpu.roll(k_new_t, WINDOW - T, axis=1))
        vo_ref[j] = jnp.where(keep, pltpu.roll(v_buf_t, WINDOW - T, axis=1), pltpu.roll(v_new_t, WINDOW - T, axis=1))


def _swa_sample(qa, ka, va, k_buf, v_buf, sinks, T):
    nb = k_buf.shape[0]
    g = SWA_SAMPLE_GROUP
    tok = lambda b, s: (b, 0)
    buf = lambda b, s: (b, 0, 0)
    return pl.pallas_call(
        functools.partial(_swa_sample_kernel, T),
        grid_spec=pltpu.PrefetchScalarGridSpec(
            num_scalar_prefetch=1,
            grid=(nb // g,),
            in_specs=[pl.BlockSpec((g * T, 1024), tok),
                      pl.BlockSpec((g * T, A_KV), tok),
                      pl.BlockSpec((g * T, A_KV), tok),
                      pl.BlockSpec((g, A_KV, WINDOW), buf),
                      pl.BlockSpec((g, A_KV, WINDOW), buf)],
            out_specs=[pl.BlockSpec((g * T, 1024), tok),
                       pl.BlockSpec((g, A_KV, WINDOW), buf),
                       pl.BlockSpec((g, A_KV, WINDOW), buf)]),
        out_shape=[jax.ShapeDtypeStruct((nb * T, 1024), qa.dtype),
                   jax.ShapeDtypeStruct((nb, A_KV, WINDOW), F32),
                   jax.ShapeDtypeStruct((nb, A_KV, WINDOW), F32)],
        compiler_params=_cparams(("arbitrary",)),
        name="swa_sample",
    )(sinks, qa, ka, va, k_buf, v_buf)


def _merge_kernel(x_ref, hm_ref, ha_ref, gm_ref, ga_ref, wpm_ref, wpa_ref, wo_ref, g2_ref, wr_ref, br_ref,
                  x1_ref, xn_ref, idx_ref, gate_ref, cnt_ref):
    pm = jnp.dot(hm_ref[...].astype(BF16), wpm_ref[...], preferred_element_type=F32)
    pa = jnp.dot(ha_ref[...].astype(BF16), wpa_ref[...], preferred_element_type=F32)
    mixed = gm_ref[...].astype(F32) * pm + ga_ref[...].astype(F32) * pa
    x1 = x_ref[...] + jnp.dot(mixed.astype(BF16), wo_ref[...], preferred_element_type=F32)
    x1_ref[...] = x1
    xn = _rms(x1, g2_ref[...])
    xn_ref[...] = xn.astype(BF16)
    xn_hi = xn.astype(BF16)
    xn_lo = (xn - xn_hi.astype(F32)).astype(BF16)
    part = jnp.dot(xn_hi, wr_ref[...], preferred_element_type=F32)
    logits = (part[:, :LANES] + part[:, LANES:]
              + jnp.dot(xn_lo, wr_ref[:, :LANES], preferred_element_type=F32) + br_ref[...])
    tm = logits.shape[0]
    cur = logits.T[0:N_EXPERTS]
    eid = lax.broadcasted_iota(I32, (N_EXPERTS, tm), 0).astype(F32)
    vals, idxs = [], []
    onehot = jnp.zeros((N_EXPERTS, tm), F32)
    for _ in range(TOP_K):
        mval = jnp.max(cur, axis=0, keepdims=True)
        sel = jnp.min(jnp.where(cur == mval, eid, float(N_EXPERTS)), axis=0, keepdims=True)
        hit = eid == sel
        onehot = onehot + hit.astype(F32)
        cur = jnp.where(hit, -jnp.inf, cur)
        vals.append(mval)
        idxs.append(sel)
    es = [jnp.exp(v - vals[0]) for v in vals]
    tot = es[0] + es[1] + es[2] + es[3]
    row = lax.broadcasted_iota(I32, (SUBLANES, tm), 0)
    res = jnp.zeros((SUBLANES, tm), F32)
    for k in range(TOP_K):
        res = jnp.where(row == k, idxs[k], res)
        res = jnp.where(row == TOP_K + k, es[k] / tot, res)
    res_t = jnp.concatenate([res, jnp.zeros((LANES - SUBLANES, tm), F32)], axis=0).T
    idx_ref[...] = res_t[:, 0:TOP_K].astype(I32)
    gate_ref[...] = res_t[:, TOP_K:2 * TOP_K]
    for j in range(tm // TOK_TILE):
        cnt_ref[j] = jnp.sum(onehot[:, j * TOK_TILE:(j + 1) * TOK_TILE], axis=1, keepdims=True).astype(I32)


def _merge(x, hm, ha, og, wpm, wpa, wo, g2, wr, br):
    t = x.shape[0]
    tm = MERGE_TILE
    row = lambda n: pl.BlockSpec((tm, n), lambda i: (i, 0))
    const = lambda r, c: pl.BlockSpec((r, c), lambda i: (0, 0))
    return pl.pallas_call(
        _merge_kernel,
        grid=(t // tm,),
        in_specs=[row(1024), row(1024), row(1024),
                  pl.BlockSpec((tm, 1024), lambda i: (i, 1)),
                  pl.BlockSpec((tm, 1024), lambda i: (i, 2)),
                  const(1024, 1024), const(1024, 1024), const(1024, 1024), const(1, 1024),
                  const(1024, 2 * LANES), const(1, LANES)],
        out_specs=[row(1024), row(1024), row(TOP_K), row(TOP_K),
                   pl.BlockSpec((tm // TOK_TILE, N_EXPERTS, 1), lambda i: (i, 0, 0))],
        out_shape=[jax.ShapeDtypeStruct((t, 1024), F32),
                   jax.ShapeDtypeStruct((t, 1024), BF16),
                   jax.ShapeDtypeStruct((t, TOP_K), I32),
                   jax.ShapeDtypeStruct((t, TOP_K), F32),
                   jax.ShapeDtypeStruct((t // TOK_TILE, N_EXPERTS, 1), I32)],
        compiler_params=_cparams(("arbitrary",)),
        name="merge_route",
    )(x, hm, ha, og, og, wpm, wpa, wo, g2, wr, br)


def _rows_to_tiles(ref, lead, x):
    chunks = jnp.stack([x[:, s * LANES:(s + 1) * LANES] for s in range(ROW_TILE[0])], axis=0)
    ref[lead] = jnp.swapaxes(chunks, 0, 1).astype(ref.dtype)


def _tiles_to_rows(ref, lead, rows=slice(None)):
    chunks = jnp.swapaxes(ref[lead + (rows,)].astype(F32), 0, 1)
    return jnp.concatenate([chunks[s] for s in range(ROW_TILE[0])], axis=1)


def _stage_rows(idx, off_row):
    tm = idx.shape[0]
    lane = lax.broadcasted_iota(I32, (tm, LANES), 1)
    hits = [lane == idx[:, k:k + 1] for k in range(TOP_K)]
    onehot = sum(h.astype(F32) for h in hits)
    r_i = lax.broadcasted_iota(I32, (tm, tm), 0)
    c_i = lax.broadcasted_iota(I32, (tm, tm), 1)
    before = (c_i < r_i).astype(BF16)
    rank = jnp.dot(before, onehot.astype(BF16), preferred_element_type=F32)
    pos = rank + off_row
    return [jnp.sum(jnp.where(h, pos, 0.0), axis=1, keepdims=True) for h in hits]


def _dispatch_kernel(ntile_a, cnt_ref, off_ref, base_ref, tail_ref, xa_ref, xb_ref, ia_ref, ib_ref, offv_ref,
                     xs_ref, stage, zeros, sem, zsem):
    i = pl.program_id(0)
    from_a = 2 * i < ntile_a
    tm = TOK_TILE

    def stage_tile(slot):
        rows_in = slice(slot * tm, (slot + 1) * tm)
        xn = jnp.where(from_a, xa_ref[rows_in], xb_ref[rows_in])
        idx = jnp.where(from_a, ia_ref[rows_in], ib_ref[rows_in])
        rows = _stage_rows(idx, offv_ref[slot])
        lane = lax.broadcasted_iota(I32, (tm, LANES), 1)
        r4 = jnp.zeros((tm, LANES), F32)
        for k in range(TOP_K):
            r4 = jnp.where(lane == k, rows[k], r4)
        r4t = r4.T.astype(I32)
        r_iota = lax.broadcasted_iota(I32, (STAGE_ROWS, tm), 0)
        sel = r_iota == r4t[0:1, :]
        for k in range(1, TOP_K):
            sel = sel | (r_iota == r4t[k:k + 1, :])
        sel = jnp.where(sel, 1.0, 0.0).astype(BF16)
        return jnp.dot(sel, xn, preferred_element_type=F32)

    def runs(tile, slot):
        out = []
        for e in range(N_EXPERTS):
            n = cnt_ref[tile * N_EXPERTS + e]
            off = off_ref[tile * N_EXPERTS + e]
            base = base_ref[tile * N_EXPERTS + e]
            out.append((n, pltpu.make_async_copy(stage.at[slot, pl.ds(off, n)], xs_ref.at[pl.ds(base, n)],
                                                 sem.at[slot])))
        return out

    def wait_runs(slot):
        pltpu.make_async_copy(stage.at[slot], xs_ref.at[pl.ds(0, STAGE_ROWS)], sem.at[slot]).wait()

    for slot in range(2):
        staged = stage_tile(slot)

        @pl.when(i >= 1)
        def _():
            wait_runs(slot)
        _rows_to_tiles(stage, (slot,), staged)
        for n, cp in runs(2 * i + slot, slot):
            @pl.when(n > 0)
            def _():
                cp.start()

    @pl.when(i == pl.num_programs(0) - 1)
    def _():
        for slot in range(2):
            wait_runs(slot)
        zeros[...] = jnp.zeros(zeros.shape, zeros.dtype)
        tails = []
        for e in range(N_EXPERTS):
            start = tail_ref[e]
            n = tail_ref[N_EXPERTS + e]
            tails.append((n, pltpu.make_async_copy(zeros.at[pl.ds(0, n)], xs_ref.at[pl.ds(start, n)], zsem)))
        for n, cp in tails:
            @pl.when(n > 0)
            def _():
                cp.start()
        for n, cp in tails:
            @pl.when(n > 0)
            def _():
                cp.wait()
        first = tail_ref[2 * N_EXPERTS]
        n_unused = tail_ref[2 * N_EXPERTS + 1]

        def unused_block(j):
            return pltpu.make_async_copy(zeros, xs_ref.at[pl.ds((first + j) * MOE_BLOCK, MOE_BLOCK)], zsem)

        @pl.loop(0, n_unused)
        def _(j):
            unused_block(j).start()

        @pl.loop(0, n_unused)
        def _(j):
            unused_block(j).wait()


def _dispatch(xa, xb, ia, ib, cnt, off, base, tails, offv, n_slots):
    tm = 2 * TOK_TILE
    assert xa.shape[0] % tm == 0 and xb.shape[0] % tm == 0
    nstep_a = xa.shape[0] // tm
    nstep_b = xb.shape[0] // tm
    amap = lambda i, *_: (jnp.minimum(i, nstep_a - 1), 0)
    bmap = lambda i, *_: (jnp.maximum(i - nstep_a, 0), 0)
    return pl.pallas_call(
        functools.partial(_dispatch_kernel, 2 * nstep_a),
        grid_spec=pltpu.PrefetchScalarGridSpec(
            num_scalar_prefetch=4,
            grid=(nstep_a + nstep_b,),
            in_specs=[pl.BlockSpec((tm, 1024), amap),
                      pl.BlockSpec((tm, 1024), bmap),
                      pl.BlockSpec((tm, TOP_K), amap),
                      pl.BlockSpec((tm, TOP_K), bmap),
                      pl.BlockSpec((2, 1, LANES), lambda i, *_: (i, 0, 0))],
            out_specs=pl.BlockSpec(memory_space=pl.ANY),
            scratch_shapes=[pltpu.VMEM((2, STAGE_ROWS) + ROW_TILE, ROW_DTYPE),
                            pltpu.VMEM((MOE_BLOCK,) + ROW_TILE, ROW_DTYPE),
                            pltpu.SemaphoreType.DMA((2,)),
                            pltpu.SemaphoreType.DMA]),
        out_shape=jax.ShapeDtypeStruct((n_slots,) + ROW_TILE, ROW_DTYPE),
        compiler_params=_cparams(("arbitrary",)),
        name="moe_dispatch",
    )(cnt, off, base, tails, xa, xb, ia, ib, offv)


MOE_HALF = MOE_BLOCK // 2


def _experts_kernel(be_ref, rows_ref, next_ref, slot_ref, xs_ref, wgu_hbm, bgu_ref, wd_hbm, bd_ref, y_ref,
                    wgu_f, wd_f, wgu_s, wd_s, sem):
    i = pl.program_id(0)
    rows = rows_ref[i]
    live = rows > 0
    run_start = (i == 0) | (be_ref[i] != be_ref[jnp.maximum(i - 1, 0)])

    def fetch(expert, slot):
        return (pltpu.make_async_copy(wgu_hbm.at[expert], wgu_f.at[slot], sem.at[0, slot]),
                pltpu.make_async_copy(wd_hbm.at[expert], wd_f.at[slot], sem.at[1, slot]))

    for slot in range(2):
        @pl.when(live & run_start & (slot_ref[i] == slot))
        def _():
            @pl.when(i == 0)
            def _():
                for cp in fetch(be_ref[i], slot):
                    cp.start()
            for cp in fetch(be_ref[i], slot):
                cp.wait()

            @pl.when(next_ref[i] >= 0)
            def _():
                for cp in fetch(next_ref[i], 1 - slot):
                    cp.start()
            wgu_s[...] = wgu_f[slot].astype(BF16)
            wd_s[...] = wd_f[slot].astype(BF16)

    def ffn(x):
        h = jnp.dot(x.astype(BF16), wgu_s[...], preferred_element_type=F32) + bgu_ref[0]
        gate = jnp.minimum(h[:, :D_FF], SWIGLU_LIMIT)
        up = jnp.clip(h[:, D_FF:], -SWIGLU_LIMIT, SWIGLU_LIMIT)
        act = gate * jax.nn.sigmoid(SWIGLU_ALPHA * gate) * (up + 1.0)
        return jnp.dot(act.astype(BF16), wd_s[...], preferred_element_type=F32) + bd_ref[0]

    @pl.when(rows > MOE_HALF)
    def _():
        _rows_to_tiles(y_ref, (), ffn(_tiles_to_rows(xs_ref, ())))

    @pl.when(live & (rows <= MOE_HALF))
    def _():
        y_half = ffn(_tiles_to_rows(xs_ref, (), slice(0, MOE_HALF)))
        _rows_to_tiles(y_ref, (), jnp.concatenate([y_half, jnp.zeros((MOE_HALF, D_MODEL), F32)], axis=0))

    @pl.when(rows == 0)
    def _():
        y_ref[...] = jnp.zeros(y_ref.shape, y_ref.dtype)


def _experts(block_expert, block_rows, next_expert, slot, xs, wgu, bgu, wd, bd):
    nb_max = xs.shape[0] // MOE_BLOCK
    blk = lambda i, be, rows, *_: (jnp.where(rows[i] > 0, i, 0), 0, 0)
    wmap = lambda i, be, *_: (be[i], 0, 0)
    return pl.pallas_call(
        _experts_kernel,
        grid_spec=pltpu.PrefetchScalarGridSpec(
            num_scalar_prefetch=4,
            grid=(nb_max,),
            in_specs=[pl.BlockSpec((MOE_BLOCK,) + ROW_TILE, blk),
                      pl.BlockSpec(memory_space=pl.ANY),
                      pl.BlockSpec((1, 1, 2 * D_FF), wmap),
                      pl.BlockSpec(memory_space=pl.ANY),
                      pl.BlockSpec((1, 1, D_MODEL), wmap)],
            out_specs=pl.BlockSpec((MOE_BLOCK,) + ROW_TILE, lambda i, *_: (i, 0, 0)),
            scratch_shapes=[pltpu.VMEM((2, D_MODEL, 2 * D_FF), F32),
                            pltpu.VMEM((2, D_FF, D_MODEL), F32),
                            pltpu.VMEM((D_MODEL, 2 * D_FF), BF16),
                            pltpu.VMEM((D_FF, D_MODEL), BF16),
                            pltpu.SemaphoreType.DMA((2, 2))]),
        out_shape=jax.ShapeDtypeStruct(xs.shape, ROW_DTYPE),
        compiler_params=_cparams(("arbitrary",)),
        name="moe_experts",
    )(block_expert, block_rows, next_expert, slot, xs, wgu, bgu, wd, bd)


def _combine_kernel(cnt_ref, off_ref, base_ref, x1_ref, idx_ref, gate_ref, offv_ref, gf_ref, y_ref,
                    o_ref, stage, sem):
    i = pl.program_id(0)
    last = pl.num_programs(0) - 1
    tm = TOK_TILE

    def runs(tile, slot):
        out = []
        for e in range(N_EXPERTS):
            n = cnt_ref[tile * N_EXPERTS + e]
            off = off_ref[tile * N_EXPERTS + e]
            base = base_ref[tile * N_EXPERTS + e]
            out.append((n, pltpu.make_async_copy(y_ref.at[pl.ds(base, n)], stage.at[slot, pl.ds(off, n)],
                                                 sem.at[slot])))
        return out

    def start_runs(tile, slot):
        for n, cp in runs(tile, slot):
            @pl.when(n > 0)
            def _():
                cp.start()

    @pl.when(i == 0)
    def _():
        for half in range(2):
            start_runs(half, half)

    for pair in range(2):
        @pl.when(((i & 1) != pair) & (i < last))
        def _():
            for half in range(2):
                start_runs(2 * (i + 1) + half, 2 * pair + half)

    gmats = []
    for half in range(2):
        rows_in = slice(half * tm, (half + 1) * tm)
        rows = _stage_rows(idx_ref[rows_in], offv_ref[half])
        lane = lax.broadcasted_iota(I32, (tm, STAGE_ROWS), 1)
        gates = gate_ref[rows_in]
        gmat = jnp.zeros((tm, STAGE_ROWS), F32)
        for k in range(TOP_K):
            gmat = gmat + jnp.where(lane == rows[k].astype(I32), gates[:, k:k + 1], 0.0)
        gmats.append(gmat.astype(BF16))
    for pair in range(2):
        @pl.when((i & 1) == pair)
        def _():
            for half in range(2):
                slot = 2 * pair + half
                rows_in = slice(half * tm, (half + 1) * tm)
                pltpu.make_async_copy(y_ref.at[pl.ds(0, STAGE_ROWS)], stage.at[slot], sem.at[slot]).wait()
                moe = jnp.dot(gmats[half], _tiles_to_rows(stage, (slot,)).astype(BF16), preferred_element_type=F32)
                o_ref[rows_in] = _rms(x1_ref[rows_in] + moe, gf_ref[...])


def _combine(x1, idx, gates, cnt, off, base, offv, gf, y):
    t = x1.shape[0]
    tm = 2 * TOK_TILE
    assert t % tm == 0
    return pl.pallas_call(
        _combine_kernel,
        grid_spec=pltpu.PrefetchScalarGridSpec(
            num_scalar_prefetch=3,
            grid=(t // tm,),
            in_specs=[pl.BlockSpec((tm, 1024), lambda i, *_: (i, 0)),
                      pl.BlockSpec((tm, TOP_K), lambda i, *_: (i, 0)),
                      pl.BlockSpec((tm, TOP_K), lambda i, *_: (i, 0)),
                      pl.BlockSpec((2, 1, LANES), lambda i, *_: (i, 0, 0)),
                      pl.BlockSpec((1, 1024), lambda i, *_: (0, 0)),
                      pl.BlockSpec(memory_space=pl.ANY)],
            out_specs=pl.BlockSpec((tm, 1024), lambda i, *_: (i, 0)),
            scratch_shapes=[pltpu.VMEM((4, STAGE_ROWS) + ROW_TILE, ROW_DTYPE),
                            pltpu.SemaphoreType.DMA((4,))]),
        out_shape=jax.ShapeDtypeStruct((t, 1024), F32),
        compiler_params=_cparams(("arbitrary",)),
        name="moe_combine",
    )(cnt, off, base, x1, idx, gates, offv, gf, y)


def _prep_w_in(w):
    wt = w.T
    c_if, c_qa, c_kv, c_g = 4096, 4104, 5128, 5640
    wm = wt[:c_if].astype(BF16)
    wif = jnp.pad(wt[c_if:c_qa], ((0, LANES - 2 * M_HEADS), (0, 0))).astype(BF16)
    wqa = (jnp.concatenate([wt[c_qa + (A_GROUP * g + j) * A_HD:c_qa + (A_GROUP * g + j + 1) * A_HD]
                            for j in range(A_GROUP) for g in range(A_KV_HEADS)], axis=0)
           * (A_HD ** -0.5)).astype(BF16)
    wkv = wt[c_kv:c_g].astype(BF16)
    wg = wt[c_g:].astype(BF16)
    return wm, wg, wqa, wkv, wif


def _rope_tables(pos):
    half = ROT_DIM // 2
    inv = ROPE_THETA ** (-(jnp.arange(half, dtype=F32) * 2.0 / ROT_DIM))
    ang = pos.astype(F32)[:, None] * inv[None, :]
    lane = jnp.arange(LANES) % A_HD
    hit = (lane[None, :] % half == jnp.arange(half)[:, None]) & (lane[None, :] < ROT_DIM)
    spread_cos = hit.astype(F32)
    spread_sin = jnp.where(hit, jnp.where(lane < half, -1.0, 1.0)[None, :], 0.0)
    cos_t = jnp.dot(jnp.cos(ang), spread_cos, precision=lax.Precision.HIGHEST) + (lane >= ROT_DIM).astype(F32)[None, :]
    sin_t = jnp.dot(jnp.sin(ang), spread_sin, precision=lax.Precision.HIGHEST)
    return cos_t, sin_t


def _round_up(x, m):
    return (x + m - 1) // m * m


def kernel(x_prompt, x_sample, cache_k, cache_v, state_c, state_n, state_m, norm1, w_in, b_igate, b_fgate,
           mlstm_norm, w_proj_m, w_proj_a, attn_sinks, w_out, norm2, w_router, b_router, w_gate_up, b_gate_up,
           w_down, b_down, norm_f):
    bp, sp, _ = x_prompt.shape
    bs, ts, _ = x_sample.shape
    past_len = 16384
    l = 0
    w1 = _prep_w_in(w_in[l])
    g1 = norm1[l][None, :]
    g2 = norm2[l][None, :]
    gf = norm_f[None, :]
    gate_bias = jnp.pad(jnp.concatenate([b_igate[l], b_fgate[l]]), (0, LANES - 2 * M_HEADS))[None, :]
    mn = mlstm_norm[l][None, :]
    wpm = w_proj_m[l].astype(BF16)
    wpa = jnp.concatenate([w_proj_a[l][(A_GROUP * g + j) * A_HD:(A_GROUP * g + j + 1) * A_HD]
                           for j in range(A_GROUP) for g in range(A_KV_HEADS)], axis=0).astype(BF16)
    wo = w_out[l].astype(BF16)
    wr_f = jnp.pad(w_router[l], ((0, 0), (0, LANES - N_EXPERTS)))
    wr_hi = wr_f.astype(BF16)
    wr = jnp.concatenate([wr_hi, (wr_f - wr_hi.astype(F32)).astype(BF16)], axis=1)
    br = jnp.pad(b_router[l], (0, LANES - N_EXPERTS))[None, :]
    wgu = w_gate_up[l]
    bgu = b_gate_up[l][:, None, :]
    wd = w_down[l]
    bd = b_down[l][:, None, :]
    sinks = attn_sinks[l]

    def mixer(x, nb, seq, pos0, L, c0, n0, m0, k_buf, v_buf, tm):
        t = nb * seq
        xf = x.reshape(t, D_MODEL)
        cos_t, sin_t = _rope_tables(pos0 + jnp.arange(seq))
        if seq < tm:
            cos_t, sin_t = jnp.tile(cos_t, (tm // seq, 1)), jnp.tile(sin_t, (tm // seq, 1))
        act = BF16 if L % 16 == 0 else F32
        qkv, og, qa, ka, va, gates = _inproj(xf, g1, w1, cos_t, sin_t, tm, max(seq // tm, 1), act)
        m0p = jnp.pad(m0, ((0, 0), (0, LANES - M_HEADS)))[:, None, :]
        hm, c, n, m = _mlstm(qkv, gates, og, c0, n0, m0p, gate_bias, mn, nb, L)
        if k_buf is None:
            ha = _swa_prompt(qa, ka, va, sinks, nb)
            k_keep = ka.reshape(nb, seq, A_KV)[:, seq - WINDOW:].reshape(nb, WINDOW, A_KV_HEADS, A_HD)
            v_keep = va.reshape(nb, seq, A_KV)[:, seq - WINDOW:].reshape(nb, WINDOW, A_KV_HEADS, A_HD)
        else:
            to_t = lambda a: a.transpose(0, 2, 3, 1).reshape(nb, A_KV, WINDOW)
            from_t = lambda a: a.reshape(nb, A_KV_HEADS, A_HD, WINDOW).transpose(0, 3, 1, 2)
            ha, k_keep, v_keep = _swa_sample(qa, ka, va, to_t(k_buf), to_t(v_buf), sinks, seq)
            k_keep, v_keep = from_t(k_keep), from_t(v_keep)
        x1, xn, idx, gate, cnt = _merge(xf, hm, ha, og, wpm, wpa, wo, g2, wr, br)
        return (x1, xn, idx, gate, cnt[:, :, 0]), (k_keep, v_keep, c, n, m[:, 0, :M_HEADS])

    zc = jnp.zeros((bp, M_HEADS, M_DV, M_DK), F32)
    zn = jnp.zeros((bp, M_HEADS, M_DK), F32)
    zm = jnp.zeros((bp, M_HEADS), F32)
    rp, sp_out = mixer(x_prompt, bp, sp, 0, M_CHUNK, zc, zn, zm, None, None, 512)
    rs, ss_out = mixer(x_sample, bs, ts, past_len, ts, state_c[l], state_n[l], state_m[l],
                       cache_k[l], cache_v[l], 256)

    cnt = jnp.concatenate([rp[4], rs[4]], axis=0)
    ntile_p = rp[4].shape[0]
    off = jnp.cumsum(cnt, axis=1) - cnt
    per_expert = jnp.sum(cnt, axis=0)
    padded = _round_up(per_expert, MOE_BLOCK)
    padded_end = jnp.cumsum(padded)
    expert_start = padded_end - padded
    base = expert_start[None, :] + jnp.cumsum(cnt, axis=0) - cnt
    n_tok = bp * sp + bs * ts
    n_slots = _round_up(n_tok * TOP_K, MOE_BLOCK) + N_EXPERTS * MOE_BLOCK
    nb_max = n_slots // MOE_BLOCK
    block_row = jnp.arange(nb_max, dtype=I32) * MOE_BLOCK
    block_expert = jnp.minimum(jnp.sum(block_row[:, None] >= padded_end[None, :], axis=1), N_EXPERTS - 1).astype(I32)
    n_blocks = (padded_end[-1] // MOE_BLOCK).astype(I32)[None]
    eid = jnp.arange(N_EXPERTS, dtype=I32)
    later_used = (eid[None, :] > eid[:, None]) & (padded[None, :] > 0)
    next_used = jnp.min(jnp.where(later_used, eid[None, :], N_EXPERTS), axis=1)
    next_used = jnp.where(next_used < N_EXPERTS, next_used, -1).astype(I32)
    run_parity = ((jnp.cumsum(padded > 0) - 1) & 1).astype(I32)
    of_block = lambda table: jnp.sum(jnp.where(block_expert[:, None] == eid[None, :], table[None, :], 0),
                                     axis=1).astype(I32)
    block_rows = jnp.clip(of_block(per_expert) - (block_row - of_block(expert_start)), 0, MOE_BLOCK).astype(I32)
    tails = jnp.concatenate([expert_start + per_expert, padded - per_expert,
                             n_blocks, nb_max - n_blocks]).astype(I32)
    flat = lambda a: a.reshape(-1).astype(I32)
    offv = jnp.pad(off, ((0, 0), (0, LANES - N_EXPERTS))).astype(F32)[:, None, :]

    xs = _dispatch(rp[1], rs[1], rp[2], rs[2], flat(cnt), flat(off), flat(base), tails, offv, n_slots)
    y = _experts(block_expert, block_rows, of_block(next_used), of_block(run_parity), xs, wgu, bgu, wd, bd)
    y_p = _combine(rp[0], rp[2], rp[3], flat(cnt[:ntile_p]), flat(off[:ntile_p]), flat(base[:ntile_p]),
                   offv[:ntile_p], gf, y)
    y_s = _combine(rs[0], rs[2], rs[3], flat(cnt[ntile_p:]), flat(off[ntile_p:]), flat(base[ntile_p:]),
                   offv[ntile_p:], gf, y)

    kp, vp, cp_, np_, mp = sp_out
    ks, vs, cs, ns, ms = ss_out
    return (y_p.reshape(bp, sp, D_MODEL), y_s.reshape(bs, ts, D_MODEL),
            kp[None], vp[None], cp_[None], np_[None], mp[None],
            ks[None], vs[None], cs[None], ns[None], ms[None])
```

```python
import functools

import jax
import jax.numpy as jnp
from jax import lax
from jax.experimental import pallas as pl
from jax.experimental.pallas import tpu as pltpu

F32 = jnp.float32
BF16 = jnp.bfloat16
I32 = jnp.int32

D_MODEL = 1024
M_HEADS = 4
M_DK = 256
M_DV = 256
M_CHUNK = 128
A_Q_HEADS = 16
A_KV_HEADS = 4
A_GROUP = 4
A_HD = 64
A_KV = A_KV_HEADS * A_HD
WINDOW = 128
A_BLOCK = 128
ROT_DIM = 16
ROPE_THETA = 500000.0
N_EXPERTS = 32
TOP_K = 4
D_FF = 1024
SWIGLU_LIMIT = 7.0
SWIGLU_ALPHA = 1.702
NORM_EPS = 1e-6

LANES = 128
SUBLANES = 8
VMEM_LIMIT = 56 * 1024 * 1024

MERGE_TILE = 512
TOK_TILE = 256
STAGE_ROWS = 4 * TOK_TILE
ROW_TILE = (8, 128)
ROW_DTYPE = F32
MOE_BLOCK = 512


def _cparams(sem):
    return pltpu.CompilerParams(dimension_semantics=sem, vmem_limit_bytes=VMEM_LIMIT)


def _rms(x, g):
    return x * lax.rsqrt(jnp.mean(x * x, axis=-1, keepdims=True) + NORM_EPS) * g


def _rope(x, cos_t, sin_t, first_half):
    n = x.shape[1]
    fwd = pltpu.roll(x, n - ROT_DIM // 2, axis=1)
    bwd = pltpu.roll(x, ROT_DIM // 2, axis=1)
    outs = []
    for b in range(n // LANES):
        sl = slice(b * LANES, (b + 1) * LANES)
        partner = jnp.where(first_half, fwd[:, sl], bwd[:, sl])
        outs.append(x[:, sl] * cos_t + partner * sin_t)
    return jnp.concatenate(outs, axis=1)


def _inproj_kernel(x_ref, g1_ref, wm_ref, wg_ref, wqa_ref, wkv_ref, wif_ref, cos_ref, sin_ref,
                   qkv_ref, og_ref, qa_ref, ka_ref, va_ref, gate_ref):
    x = x_ref[...]
    xn = _rms(x, g1_ref[...]).astype(BF16)
    act = qkv_ref.dtype

    def proj(w_ref, c0, n):
        return lax.dot_general(xn, w_ref[c0:c0 + n, :], (((1,), (1,)), ((), ())), preferred_element_type=F32)

    qkv_ref[:, 0:1024] = proj(wm_ref, 0, 1024).astype(act)
    qkv_ref[:, 1024:2048] = (proj(wm_ref, 1024, 1024) * (M_DK ** -0.5)).astype(act)
    qkv_ref[:, 2048:3072] = proj(wm_ref, 2048, 1024).astype(act)
    og_ref[:, 0:1024] = jax.nn.sigmoid(proj(wm_ref, 3072, 1024)).astype(act)
    og_ref[:, 1024:2048] = jax.nn.sigmoid(proj(wg_ref, 0, 1024)).astype(act)
    og_ref[:, 2048:3072] = jax.nn.sigmoid(proj(wg_ref, 1024, 1024)).astype(act)
    cos_t = cos_ref[...]
    sin_t = sin_ref[...]
    lane = lax.broadcasted_iota(I32, (1, LANES), 1)
    first_half = (lane & (ROT_DIM - 1)) < (ROT_DIM // 2)
    qa_ref[...] = _rope(proj(wqa_ref, 0, 1024), cos_t, sin_t, first_half).astype(act)
    ka_ref[...] = _rope(proj(wkv_ref, 0, A_KV), cos_t, sin_t, first_half)
    va_ref[...] = proj(wkv_ref, A_KV, A_KV)
    gate_ref[...] = proj(wif_ref, 0, LANES)


def _inproj(x, g1, weights, cos_t, sin_t, tm, rope_blocks, act):
    t = x.shape[0]
    tok = lambda n, dt: jax.ShapeDtypeStruct((t, n), dt)
    row = lambda n: pl.BlockSpec((tm, n), lambda i: (i, 0))
    resident = lambda w: pl.BlockSpec(w.shape, lambda i: (0, 0), pipeline_mode=pl.Buffered(1))
    return pl.pallas_call(
        _inproj_kernel,
        grid=(t // tm,),
        in_specs=[row(D_MODEL), pl.BlockSpec((1, D_MODEL), lambda i: (0, 0))]
                 + [resident(w) for w in weights]
                 + [pl.BlockSpec((tm, LANES), lambda i: (i % rope_blocks, 0)),
                    pl.BlockSpec((tm, LANES), lambda i: (i % rope_blocks, 0))],
        out_specs=[row(3072), row(3072), row(1024), row(A_KV), row(A_KV), row(LANES)],
        out_shape=[tok(3072, act), tok(3072, act), tok(1024, act), tok(A_KV, F32), tok(A_KV, F32),
                   tok(LANES, F32)],
        compiler_params=_cparams(("arbitrary",)),
        name="inproj",
    )(x, g1, *weights, cos_t, sin_t)


def _mlstm_gates(gc, bias, m_prev, L):
    z = gc + bias
    lf = jnp.minimum(z, 0.0) - jnp.log1p(jnp.exp(-jnp.abs(z)))
    row = lax.broadcasted_iota(I32, (L, LANES), 0)
    fc = lf
    sh = 1
    while sh < L:
        fc = fc + jnp.where(row >= sh, pltpu.roll(fc, sh, axis=0), 0.0)
        sh *= 2
    fcum = pltpu.roll(fc, LANES - M_HEADS, axis=1)
    a = z - fcum
    cmx = a
    sh = 1
    while sh < L:
        cmx = jnp.maximum(cmx, jnp.where(row >= sh, pltpu.roll(cmx, sh, axis=0), -jnp.inf))
        sh *= 2
    mx = jnp.maximum(m_prev, cmx)
    inter = jnp.exp(m_prev - mx)
    f_end = fcum[L - 1:L]
    m_end = f_end + mx[L - 1:L]
    decay = jnp.exp(f_end + m_prev - m_end)
    wsrc = jnp.exp(a + (f_end - m_end))
    return a, mx, inter, wsrc, decay, m_end


NT_DIMS = (((1,), (1,)), ((), ()))
TN_DIMS = (((0,), (0,)), ((), ()))


def _mlstm_decay(h, L, gates):
    a, mx = gates[0], gates[1]
    t_i = lax.broadcasted_iota(I32, (L, L), 0)
    s_i = lax.broadcasted_iota(I32, (L, L), 1)
    a_row = jnp.sum(jnp.where(t_i == s_i, a[:, h:h + 1], 0.0), axis=0, keepdims=True)
    return jnp.exp(jnp.where(s_i <= t_i, a_row - mx[:, h:h + 1], -jnp.inf))


def _mlstm_output(h, q, og, mn, gates, qk_sum, pv, cq, n_h):
    inter_col = gates[2][:, h:h + 1]
    num = inter_col * cq + pv
    den = inter_col * jnp.sum(q.astype(F32) * n_h, axis=1, keepdims=True) + qk_sum
    hh = num / jnp.maximum(jnp.abs(den), 1.0)
    hn = hh * lax.rsqrt(jnp.mean(hh * hh, axis=1, keepdims=True) + NORM_EPS)
    return hn * mn * og


def _mlstm_kernel(L, G, CH, carry, qkv_ref, gate_ref, og_ref, c0_ref, n0_ref, m0_ref, bias_ref, mn_ref,
                  h_ref, c_ref, n_ref, m_ref, *scratch):
    if carry:
        c_in, n_in, m_in = scratch
        ci = pl.program_id(1)

        @pl.when(ci == 0)
        def _():
            c_in[...] = c0_ref[...]
            n_in[...] = n0_ref[...]
            m_in[...] = m0_ref[...]
        c_out, n_out, m_out = scratch
    else:
        c_in, n_in, m_in = c0_ref, n0_ref, m0_ref
        c_out, n_out, m_out = c_ref, n_ref, m_ref

    def qkv(j, cc, h, part):
        cols = slice(part * 1024 + h * M_DK, part * 1024 + (h + 1) * M_DK)
        return qkv_ref[j, cc * L:(cc + 1) * L, cols].astype(BF16)

    units = [(j, cc, h) for cc in range(CH) for j in range(G) for h in range(M_HEADS)]
    gates = {}
    for j in range(G):
        m_cur = m_in[j]
        for cc in range(CH):
            gates[j, cc] = _mlstm_gates(gate_ref[j, cc * L:(cc + 1) * L], bias_ref[...], m_cur, L)
            m_cur = gates[j, cc][5]
        m_out[j] = m_cur
    def carried(cc, c_cur):
        return {(j, h): lax.dot_general(qkv(j, cc, h, 0), c_cur[j, h].astype(BF16), NT_DIMS,
                                        preferred_element_type=F32) for j, c, h in units if c == cc}

    c_cur = {(j, h): c_in[j, h] for j in range(G) for h in range(M_HEADS)}
    n_cur = {(j, h): n_in[j, h:h + 1, :] for j in range(G) for h in range(M_HEADS)}
    scores = {u: lax.dot_general(qkv(*u, 0), qkv(*u, 1), NT_DIMS, preferred_element_type=F32) for u in units}
    cqs_first = carried(0, c_cur)
    qks = {(j, cc, h): scores[j, cc, h] * _mlstm_decay(h, L, gates[j, cc]) for j, cc, h in units}
    pvs = {u: jnp.dot(qks[u].astype(BF16), qkv(*u, 2), preferred_element_type=F32) for u in units}
    kws = {(j, cc, h): gates[j, cc][3][:, h:h + 1] * qkv(j, cc, h, 1).astype(F32) for j, cc, h in units}
    for cc in range(CH):
        now = [u for u in units if u[1] == cc]
        cqs = cqs_first if cc == 0 else carried(cc, c_cur)
        n_old = dict(n_cur)

        def update_states():
            for j, _, h in now:
                dec = gates[j, cc][4][:, h:h + 1]
                c_cur[j, h] = dec * c_cur[j, h] + lax.dot_general(qkv(j, cc, h, 2), kws[j, cc, h].astype(BF16),
                                                                  TN_DIMS, preferred_element_type=F32)
                n_cur[j, h] = dec * n_cur[j, h] + jnp.sum(kws[j, cc, h], axis=0, keepdims=True)

        def write_outputs():
            for j, _, h in now:
                sl = slice(h * M_DV, (h + 1) * M_DV)
                rows = slice(cc * L, (cc + 1) * L)
                out = _mlstm_output(h, qkv(j, cc, h, 0), og_ref[j, rows, sl].astype(F32), mn_ref[:, sl],
                                    gates[j, cc], jnp.sum(qks[j, cc, h], axis=1, keepdims=True), pvs[j, cc, h],
                                    cqs[j, h], n_old[j, h])
                h_ref[j, rows, sl] = out.astype(h_ref.dtype)

        for phase in ((update_states, write_outputs) if carry else (write_outputs, update_states)):
            phase()
    for j in range(G):
        for h in range(M_HEADS):
            c_out[j, h] = c_cur[j, h]
            n_out[j, h:h + 1, :] = n_cur[j, h]

    if carry:
        @pl.when(ci == pl.num_programs(1) - 1)
        def _():
            c_ref[...] = c_in[...]
            n_ref[...] = n_in[...]
            m_ref[...] = m_in[...]


MLSTM_GROUP_CARRY = 2
MLSTM_CHUNKS_CARRY = 1
MLSTM_GROUP_SINGLE = 8


def _mlstm(qkv, gates, og, c0, n0, m0, gate_bias, mnorm, nb, L):
    t = qkv.shape[0]
    nc = t // (nb * L)
    carry = nc > 1
    G = MLSTM_GROUP_CARRY if carry else MLSTM_GROUP_SINGLE
    CH = MLSTM_CHUNKS_CARRY if carry else 1
    qkv, gates, og = (a.reshape(nb, nc * L, a.shape[1]) for a in (qkv, gates, og))
    tokmap = lambda b, c: (b, c, 0)
    const2 = lambda b, c: (0, 0)
    state_specs = [pl.BlockSpec((G, M_HEADS, M_DV, M_DK), lambda b, c: (b, 0, 0, 0)),
                   pl.BlockSpec((G, M_HEADS, M_DK), lambda b, c: (b, 0, 0)),
                   pl.BlockSpec((G, 1, LANES), lambda b, c: (b, 0, 0))]
    scratch = [pltpu.VMEM((G, M_HEADS, M_DV, M_DK), F32),
               pltpu.VMEM((G, M_HEADS, M_DK), F32),
               pltpu.VMEM((G, 1, LANES), F32)] if carry else []
    h, c, n, m = pl.pallas_call(
        functools.partial(_mlstm_kernel, L, G, CH, carry),
        grid=(nb // G, nc // CH),
        in_specs=[pl.BlockSpec((G, CH * L, 3072), tokmap),
                  pl.BlockSpec((G, CH * L, LANES), tokmap),
                  pl.BlockSpec((G, CH * L, 1024), tokmap)]
                 + state_specs
                 + [pl.BlockSpec((1, LANES), const2),
                    pl.BlockSpec((1, 1024), const2)],
        out_specs=[pl.BlockSpec((G, CH * L, 1024), tokmap)] + state_specs,
        out_shape=[jax.ShapeDtypeStruct((nb, nc * L, 1024), qkv.dtype),
                   jax.ShapeDtypeStruct((nb, M_HEADS, M_DV, M_DK), F32),
                   jax.ShapeDtypeStruct((nb, M_HEADS, M_DK), F32),
                   jax.ShapeDtypeStruct((nb, 1, LANES), F32)],
        scratch_shapes=scratch,
        compiler_params=_cparams(("arbitrary", "arbitrary")),
        name="mlstm_L%d" % L,
    )(qkv, gates, og, c0, n0, m0, gate_bias, mnorm)
    return h.reshape(t, 1024), c, n, m


def _swa_bias(m, has_prev):
    nkeys = 2 * A_BLOCK
    t_i = lax.broadcasted_iota(I32, (A_GROUP * m, nkeys), 0) & (m - 1)
    s_i = lax.broadcasted_iota(I32, (A_GROUP * m, nkeys), 1)
    diff = t_i + A_BLOCK - s_i
    mask = (diff >= 0) & (diff <= WINDOW)
    if not has_prev:
        mask = mask & (s_i >= A_BLOCK)
    return jnp.where(mask, 0.0, -jnp.inf).astype(F32)


def _swa_group_ones():
    nkeys = 2 * A_BLOCK
    r = lax.broadcasted_iota(I32, (A_KV_HEADS * nkeys, A_KV), 0) >> (nkeys.bit_length() - 1)
    c = lax.broadcasted_iota(I32, (A_KV_HEADS * nkeys, A_KV), 1) >> (A_HD.bit_length() - 1)
    return (r == c).astype(BF16)


def _swa_core(q, k_prev, v_prev, k_cur, v_cur, bias, group_ones, sink_ref, keys_on_lanes=False):
    m = q.shape[0]
    q = q.astype(BF16)
    qst = jnp.concatenate([q[:, j * A_KV:(j + 1) * A_KV] for j in range(A_GROUP)], axis=0)
    key_axis = 1 if keys_on_lanes else 0
    k_all = jnp.concatenate([k_prev, k_cur], axis=key_axis)
    v_all = jnp.concatenate([v_prev, v_cur], axis=key_axis)
    feat = lax.broadcasted_iota(I32, (A_KV, 1) if keys_on_lanes else (1, A_KV), 1 - key_axis)
    kbd, vbd = [], []
    for g in range(A_KV_HEADS):
        in_g = (feat >= g * A_HD) & (feat < (g + 1) * A_HD)
        kbd.append(jnp.where(in_g, k_all, 0.0).astype(BF16))
        vbd.append(jnp.where(in_g, v_all, 0.0).astype(BF16))
    kbd = jnp.concatenate(kbd, axis=key_axis)
    vbd = jnp.concatenate(vbd, axis=key_axis)
    if keys_on_lanes:
        s = jnp.dot(qst, kbd, preferred_element_type=F32)
    else:
        s = lax.dot_general(qst, kbd, NT_DIMS, preferred_element_type=F32)
    nkeys = 2 * A_BLOCK
    ps, sink_terms = [], []
    for g in range(A_KV_HEADS):
        sg = s[:, g * nkeys:(g + 1) * nkeys] + bias
        sink = jnp.concatenate(
            [jnp.full((m, LANES), sink_ref[A_GROUP * g + j], F32) for j in range(A_GROUP)], axis=0)
        mx = jnp.maximum(jnp.broadcast_to(jnp.max(sg, axis=1, keepdims=True), (A_GROUP * m, LANES)), sink)
        ps += [jnp.exp(sg[:, :LANES] - mx).astype(BF16), jnp.exp(sg[:, LANES:] - mx).astype(BF16)]
        sink_terms.append(jnp.exp(sink - mx))
    p = jnp.concatenate(ps, axis=1)
    if keys_on_lanes:
        o = lax.dot_general(p, vbd, NT_DIMS, preferred_element_type=F32)
    else:
        o = jnp.dot(p, vbd, preferred_element_type=F32)
    den = jnp.dot(p, group_ones, preferred_element_type=F32)
    low = lax.broadcasted_iota(I32, (1, LANES), 1) < A_HD
    den = den + jnp.concatenate([jnp.where(low, sink_terms[0], sink_terms[1]),
                                 jnp.where(low, sink_terms[2], sink_terms[3])], axis=1)
    o = o / den
    return jnp.concatenate([o[j * m:(j + 1) * m] for j in range(A_GROUP)], axis=1)


SWA_STEP_BLOCKS = 8
SWA_STEP = SWA_STEP_BLOCKS * A_BLOCK


def _swa_prompt_kernel(sink_ref, q_ref, kp_ref, kc_ref, vp_ref, vc_ref, o_ref, bias_s, bias0_s, ones_s):
    first = (pl.program_id(0) == 0) & (pl.program_id(1) == 0)

    @pl.when(first)
    def _():
        bias_s[...] = _swa_bias(A_BLOCK, True)
        bias0_s[...] = _swa_bias(A_BLOCK, False)
        ones_s[...] = _swa_group_ones()

    bias = bias_s[...]
    bias_first = jnp.where(pl.program_id(1) > 0, bias, bias0_s[...])
    ones = ones_s[...]
    k_prev, v_prev = kp_ref[...], vp_ref[...]
    for b in range(SWA_STEP_BLOCKS):
        rows = slice(b * A_BLOCK, (b + 1) * A_BLOCK)
        k_cur, v_cur = kc_ref[rows], vc_ref[rows]
        o_ref[rows] = _swa_core(q_ref[rows], k_prev, v_prev, k_cur, v_cur, bias_first if b == 0 else bias,
                                ones, sink_ref).astype(o_ref.dtype)
        k_prev, v_prev = k_cur, v_cur


def _swa_prompt(qa, ka, va, sinks, nb):
    t = qa.shape[0]
    nstep = t // (nb * SWA_STEP)
    cur = lambda b, i, s: (b * nstep + i, 0)
    prev = lambda b, i, s: (SWA_STEP_BLOCKS * (b * nstep + i) - jnp.minimum(i, 1), 0)
    return pl.pallas_call(
        _swa_prompt_kernel,
        grid_spec=pltpu.PrefetchScalarGridSpec(
            num_scalar_prefetch=1,
            grid=(nb, nstep),
            in_specs=[pl.BlockSpec((SWA_STEP, 1024), cur),
                      pl.BlockSpec((A_BLOCK, A_KV), prev),
                      pl.BlockSpec((SWA_STEP, A_KV), cur),
                      pl.BlockSpec((A_BLOCK, A_KV), prev),
                      pl.BlockSpec((SWA_STEP, A_KV), cur)],
            out_specs=pl.BlockSpec((SWA_STEP, 1024), cur),
            scratch_shapes=[pltpu.VMEM((A_GROUP * A_BLOCK, 2 * A_BLOCK), F32),
                            pltpu.VMEM((A_GROUP * A_BLOCK, 2 * A_BLOCK), F32),
                            pltpu.VMEM((A_KV_HEADS * 2 * A_BLOCK, A_KV), BF16)]),
        out_shape=jax.ShapeDtypeStruct((t, 1024), BF16),
        compiler_params=_cparams(("arbitrary", "arbitrary")),
        name="swa_prompt",
    )(sinks, qa, ka, ka, va, va)


SWA_SAMPLE_GROUP = 16


def _swa_sample_kernel(T, sink_ref, q_ref, kn_ref, vn_ref, kb_ref, vb_ref, o_ref, ko_ref, vo_ref):
    pad = jnp.zeros((A_BLOCK - T, A_KV), F32)
    bias = _swa_bias(T, True)
    ones = _swa_group_ones()
    lane = lax.broadcasted_iota(I32, (1, WINDOW), 1)
    for j in range(SWA_SAMPLE_GROUP):
        rows = slice(j * T, (j + 1) * T)
        k_new_t = jnp.concatenate([kn_ref[rows], pad], axis=0).T
        v_new_t = jnp.concatenate([vn_ref[rows], pad], axis=0).T
        k_buf_t = kb_ref[j]
        v_buf_t = vb_ref[j]
        o_ref[rows] = _swa_core(q_ref[rows], k_buf_t, v_buf_t, k_new_t, v_new_t, bias, ones, sink_ref,
                                keys_on_lanes=True).astype(o_ref.dtype)
        keep = lane < WINDOW - T
        ko_ref[j] = jnp.where(keep, pltpu.roll(k_buf_t, WINDOW - T, axis=1), pltpu.roll(k_new_t, WINDOW - T, axis=1))
        vo_ref[j] = jnp.where(keep, pltpu.roll(v_buf_t, WINDOW - T, axis=1), pltpu.roll(v_new_t, WINDOW - T, axis=1))


def _swa_sample(qa, ka, va, k_buf, v_buf, sinks, T):
    nb = k_buf.shape[0]
    g = SWA_SAMPLE_GROUP
    tok = lambda b, s: (b, 0)
    buf = lambda b, s: (b, 0, 0)
    return pl.pallas_call(
        functools.partial(_swa_sample_kernel, T),
        grid_spec=pltpu.PrefetchScalarGridSpec(
            num_scalar_prefetch=1,
            grid=(nb // g,),
            in_specs=[pl.BlockSpec((g * T, 1024), tok),
                      pl.BlockSpec((g * T, A_KV), tok),
                      pl.BlockSpec((g * T, A_KV), tok),
                      pl.BlockSpec((g, A_KV, WINDOW), buf),
                      pl.BlockSpec((g, A_KV, WINDOW), buf)],
            out_specs=[pl.BlockSpec((g * T, 1024), tok),
                       pl.BlockSpec((g, A_KV, WINDOW), buf),
                       pl.BlockSpec((g, A_KV, WINDOW), buf)]),
        out_shape=[jax.ShapeDtypeStruct((nb * T, 1024), qa.dtype),
                   jax.ShapeDtypeStruct((nb, A_KV, WINDOW), F32),
                   jax.ShapeDtypeStruct((nb, A_KV, WINDOW), F32)],
        compiler_params=_cparams(("arbitrary",)),
        name="swa_sample",
    )(sinks, qa, ka, va, k_buf, v_buf)


def _merge_kernel(x_ref, hm_ref, ha_ref, gm_ref, ga_ref, wpm_ref, wpa_ref, wo_ref, g2_ref, wr_ref, br_ref,
                  x1_ref, xn_ref, idx_ref, gate_ref, cnt_ref):
    pm = jnp.dot(hm_ref[...].astype(BF16), wpm_ref[...], preferred_element_type=F32)
    pa = jnp.dot(ha_ref[...].astype(BF16), wpa_ref[...], preferred_element_type=F32)
    mixed = gm_ref[...].astype(F32) * pm + ga_ref[...].astype(F32) * pa
    x1 = x_ref[...] + jnp.dot(mixed.astype(BF16), wo_ref[...], preferred_element_type=F32)
    x1_ref[...] = x1
    xn = _rms(x1, g2_ref[...])
    xn_ref[...] = xn.astype(BF16)
    xn_hi = xn.astype(BF16)
    xn_lo = (xn - xn_hi.astype(F32)).astype(BF16)
    part = jnp.dot(xn_hi, wr_ref[...], preferred_element_type=F32)
    logits = (part[:, :LANES] + part[:, LANES:]
              + jnp.dot(xn_lo, wr_ref[:, :LANES], preferred_element_type=F32) + br_ref[...])
    tm = logits.shape[0]
    cur = logits.T[0:N_EXPERTS]
    eid = lax.broadcasted_iota(I32, (N_EXPERTS, tm), 0).astype(F32)
    vals, idxs = [], []
    onehot = jnp.zeros((N_EXPERTS, tm), F32)
    for _ in range(TOP_K):
        mval = jnp.max(cur, axis=0, keepdims=True)
        sel = jnp.min(jnp.where(cur == mval, eid, float(N_EXPERTS)), axis=0, keepdims=True)
        hit = eid == sel
        onehot = onehot + hit.astype(F32)
        cur = jnp.where(hit, -jnp.inf, cur)
        vals.append(mval)
        idxs.append(sel)
    es = [jnp.exp(v - vals[0]) for v in vals]
    tot = es[0] + es[1] + es[2] + es[3]
    row = lax.broadcasted_iota(I32, (SUBLANES, tm), 0)
    res = jnp.zeros((SUBLANES, tm), F32)
    for k in range(TOP_K):
        res = jnp.where(row == k, idxs[k], res)
        res = jnp.where(row == TOP_K + k, es[k] / tot, res)
    res_t = jnp.concatenate([res, jnp.zeros((LANES - SUBLANES, tm), F32)], axis=0).T
    idx_ref[...] = res_t[:, 0:TOP_K].astype(I32)
    gate_ref[...] = res_t[:, TOP_K:2 * TOP_K]
    for j in range(tm // TOK_TILE):
        cnt_ref[j] = jnp.sum(onehot[:, j * TOK_TILE:(j + 1) * TOK_TILE], axis=1, keepdims=True).astype(I32)


def _merge(x, hm, ha, og, wpm, wpa, wo, g2, wr, br):
    t = x.shape[0]
    tm = MERGE_TILE
    row = lambda n: pl.BlockSpec((tm, n), lambda i: (i, 0))
    const = lambda r, c: pl.BlockSpec((r, c), lambda i: (0, 0))
    return pl.pallas_call(
        _merge_kernel,
        grid=(t // tm,),
        in_specs=[row(1024), row(1024), row(1024),
                  pl.BlockSpec((tm, 1024), lambda i: (i, 1)),
                  pl.BlockSpec((tm, 1024), lambda i: (i, 2)),
                  const(1024, 1024), const(1024, 1024), const(1024, 1024), const(1, 1024),
                  const(1024, 2 * LANES), const(1, LANES)],
        out_specs=[row(1024), row(1024), row(TOP_K), row(TOP_K),
                   pl.BlockSpec((tm // TOK_TILE, N_EXPERTS, 1), lambda i: (i, 0, 0))],
        out_shape=[jax.ShapeDtypeStruct((t, 1024), F32),
                   jax.ShapeDtypeStruct((t, 1024), BF16),
                   jax.ShapeDtypeStruct((t, TOP_K), I32),
                   jax.ShapeDtypeStruct((t, TOP_K), F32),
                   jax.ShapeDtypeStruct((t // TOK_TILE, N_EXPERTS, 1), I32)],
        compiler_params=_cparams(("arbitrary",)),
        name="merge_route",
    )(x, hm, ha, og, og, wpm, wpa, wo, g2, wr, br)


def _rows_to_tiles(ref, lead, x):
    chunks = jnp.stack([x[:, s * LANES:(s + 1) * LANES] for s in range(ROW_TILE[0])], axis=0)
    ref[lead] = jnp.swapaxes(chunks, 0, 1).astype(ref.dtype)


def _tiles_to_rows(ref, lead, rows=slice(None)):
    chunks = jnp.swapaxes(ref[lead + (rows,)].astype(F32), 0, 1)
    return jnp.concatenate([chunks[s] for s in range(ROW_TILE[0])], axis=1)


def _stage_rows(idx, off_row):
    tm = idx.shape[0]
    lane = lax.broadcasted_iota(I32, (tm, LANES), 1)
    hits = [lane == idx[:, k:k + 1] for k in range(TOP_K)]
    onehot = sum(h.astype(F32) for h in hits)
    r_i = lax.broadcasted_iota(I32, (tm, tm), 0)
    c_i = lax.broadcasted_iota(I32, (tm, tm), 1)
    before = (c_i < r_i).astype(BF16)
    rank = jnp.dot(before, onehot.astype(BF16), preferred_element_type=F32)
    pos = rank + off_row
    return [jnp.sum(jnp.where(h, pos, 0.0), axis=1, keepdims=True) for h in hits]


def _dispatch_kernel(ntile_a, cnt_ref, off_ref, base_ref, tail_ref, xa_ref, xb_ref, ia_ref, ib_ref, offv_ref,
                     xs_ref, stage, zeros, sem, zsem):
    i = pl.program_id(0)
    from_a = 2 * i < ntile_a
    tm = TOK_TILE

    def stage_tile(slot):
        rows_in = slice(slot * tm, (slot + 1) * tm)
        xn = jnp.where(from_a, xa_ref[rows_in], xb_ref[rows_in])
        idx = jnp.where(from_a, ia_ref[rows_in], ib_ref[rows_in])
        rows = _stage_rows(idx, offv_ref[slot])
        lane = lax.broadcasted_iota(I32, (tm, LANES), 1)
        r4 = jnp.zeros((tm, LANES), F32)
        for k in range(TOP_K):
            r4 = jnp.where(lane == k, rows[k], r4)
        r4t = r4.T.astype(I32)
        r_iota = lax.broadcasted_iota(I32, (STAGE_ROWS, tm), 0)
        sel = r_iota == r4t[0:1, :]
        for k in range(1, TOP_K):
            sel = sel | (r_iota == r4t[k:k + 1, :])
        sel = jnp.where(sel, 1.0, 0.0).astype(BF16)
        return jnp.dot(sel, xn, preferred_element_type=F32)

    def runs(tile, slot):
        out = []
        for e in range(N_EXPERTS):
            n = cnt_ref[tile * N_EXPERTS + e]
            off = off_ref[tile * N_EXPERTS + e]
            base = base_ref[tile * N_EXPERTS + e]
            out.append((n, pltpu.make_async_copy(stage.at[slot, pl.ds(off, n)], xs_ref.at[pl.ds(base, n)],
                                                 sem.at[slot])))
        return out

    def wait_runs(slot):
        pltpu.make_async_copy(stage.at[slot], xs_ref.at[pl.ds(0, STAGE_ROWS)], sem.at[slot]).wait()

    for slot in range(2):
        staged = stage_tile(slot)

        @pl.when(i >= 1)
        def _():
            wait_runs(slot)
        _rows_to_tiles(stage, (slot,), staged)
        for n, cp in runs(2 * i + slot, slot):
            @pl.when(n > 0)
            def _():
                cp.start()

    @pl.when(i == pl.num_programs(0) - 1)
    def _():
        for slot in range(2):
            wait_runs(slot)
        zeros[...] = jnp.zeros(zeros.shape, zeros.dtype)
        tails = []
        for e in range(N_EXPERTS):
            start = tail_ref[e]
            n = tail_ref[N_EXPERTS + e]
            tails.append((n, pltpu.make_async_copy(zeros.at[pl.ds(0, n)], xs_ref.at[pl.ds(start, n)], zsem)))
        for n, cp in tails:
            @pl.when(n > 0)
            def _():
                cp.start()
        for n, cp in tails:
            @pl.when(n > 0)
            def _():
                cp.wait()
        first = tail_ref[2 * N_EXPERTS]
        n_unused = tail_ref[2 * N_EXPERTS + 1]

        def unused_block(j):
            return pltpu.make_async_copy(zeros, xs_ref.at[pl.ds((first + j) * MOE_BLOCK, MOE_BLOCK)], zsem)

        @pl.loop(0, n_unused)
        def _(j):
            unused_block(j).start()

        @pl.loop(0, n_unused)
        def _(j):
            unused_block(j).wait()


def _dispatch(xa, xb, ia, ib, cnt, off, base, tails, offv, n_slots):
    tm = 2 * TOK_TILE
    assert xa.shape[0] % tm == 0 and xb.shape[0] % tm == 0
    nstep_a = xa.shape[0] // tm
    nstep_b = xb.shape[0] // tm
    amap = lambda i, *_: (jnp.minimum(i, nstep_a - 1), 0)
    bmap = lambda i, *_: (jnp.maximum(i - nstep_a, 0), 0)
    return pl.pallas_call(
        functools.partial(_dispatch_kernel, 2 * nstep_a),
        grid_spec=pltpu.PrefetchScalarGridSpec(
            num_scalar_prefetch=4,
            grid=(nstep_a + nstep_b,),
            in_specs=[pl.BlockSpec((tm, 1024), amap),
                      pl.BlockSpec((tm, 1024), bmap),
                      pl.BlockSpec((tm, TOP_K), amap),
                      pl.BlockSpec((tm, TOP_K), bmap),
                      pl.BlockSpec((2, 1, LANES), lambda i, *_: (i, 0, 0))],
            out_specs=pl.BlockSpec(memory_space=pl.ANY),
            scratch_shapes=[pltpu.VMEM((2, STAGE_ROWS) + ROW_TILE, ROW_DTYPE),
                            pltpu.VMEM((MOE_BLOCK,) + ROW_TILE, ROW_DTYPE),
                            pltpu.SemaphoreType.DMA((2,)),
                            pltpu.SemaphoreType.DMA]),
        out_shape=jax.ShapeDtypeStruct((n_slots,) + ROW_TILE, ROW_DTYPE),
        compiler_params=_cparams(("arbitrary",)),
        name="moe_dispatch",
    )(cnt, off, base, tails, xa, xb, ia, ib, offv)


MOE_HALF = MOE_BLOCK // 2


def _experts_kernel(be_ref, rows_ref, next_ref, slot_ref, xs_ref, wgu_hbm, bgu_ref, wd_hbm, bd_ref, y_ref,
                    wgu_f, wd_f, wgu_s, wd_s, sem):
    i = pl.program_id(0)
    rows = rows_ref[i]
    live = rows > 0
    run_start = (i == 0) | (be_ref[i] != be_ref[jnp.maximum(i - 1, 0)])

    def fetch(expert, slot):
        return (pltpu.make_async_copy(wgu_hbm.at[expert], wgu_f.at[slot], sem.at[0, slot]),
                pltpu.make_async_copy(wd_hbm.at[expert], wd_f.at[slot], sem.at[1, slot]))

    for slot in range(2):
        @pl.when(live & run_start & (slot_ref[i] == slot))
        def _():
            @pl.when(i == 0)
            def _():
                for cp in fetch(be_ref[i], slot):
                    cp.start()
            for cp in fetch(be_ref[i], slot):
                cp.wait()

            @pl.when(next_ref[i] >= 0)
            def _():
                for cp in fetch(next_ref[i], 1 - slot):
                    cp.start()
            wgu_s[...] = wgu_f[slot].astype(BF16)
            wd_s[...] = wd_f[slot].astype(BF16)

    def ffn(x):
        h = jnp.dot(x.astype(BF16), wgu_s[...], preferred_element_type=F32) + bgu_ref[0]
        gate = jnp.minimum(h[:, :D_FF], SWIGLU_LIMIT)
        up = jnp.clip(h[:, D_FF:], -SWIGLU_LIMIT, SWIGLU_LIMIT)
        act = gate * jax.nn.sigmoid(SWIGLU_ALPHA * gate) * (up + 1.0)
        return jnp.dot(act.astype(BF16), wd_s[...], preferred_element_type=F32) + bd_ref[0]

    @pl.when(rows > MOE_HALF)
    def _():
        _rows_to_tiles(y_ref, (), ffn(_tiles_to_rows(xs_ref, ())))

    @pl.when(live & (rows <= MOE_HALF))
    def _():
        y_half = ffn(_tiles_to_rows(xs_ref, (), slice(0, MOE_HALF)))
        _rows_to_tiles(y_ref, (), jnp.concatenate([y_half, jnp.zeros((MOE_HALF, D_MODEL), F32)], axis=0))

    @pl.when(rows == 0)
    def _():
        y_ref[...] = jnp.zeros(y_ref.shape, y_ref.dtype)


def _experts(block_expert, block_rows, next_expert, slot, xs, wgu, bgu, wd, bd):
    nb_max = xs.shape[0] // MOE_BLOCK
    blk = lambda i, be, rows, *_: (jnp.where(rows[i] > 0, i, 0), 0, 0)
    wmap = lambda i, be, *_: (be[i], 0, 0)
    return pl.pallas_call(
        _experts_kernel,
        grid_spec=pltpu.PrefetchScalarGridSpec(
            num_scalar_prefetch=4,
            grid=(nb_max,),
            in_specs=[pl.BlockSpec((MOE_BLOCK,) + ROW_TILE, blk),
                      pl.BlockSpec(memory_space=pl.ANY),
                      pl.BlockSpec((1, 1, 2 * D_FF), wmap),
                      pl.BlockSpec(memory_space=pl.ANY),
                      pl.BlockSpec((1, 1, D_MODEL), wmap)],
            out_specs=pl.BlockSpec((MOE_BLOCK,) + ROW_TILE, lambda i, *_: (i, 0, 0)),
            scratch_shapes=[pltpu.VMEM((2, D_MODEL, 2 * D_FF), F32),
                            pltpu.VMEM((2, D_FF, D_MODEL), F32),
                            pltpu.VMEM((D_MODEL, 2 * D_FF), BF16),
                            pltpu.VMEM((D_FF, D_MODEL), BF16),
                            pltpu.SemaphoreType.DMA((2, 2))]),
        out_shape=jax.ShapeDtypeStruct(xs.shape, ROW_DTYPE),
        compiler_params=_cparams(("arbitrary",)),
        name="moe_experts",
    )(block_expert, block_rows, next_expert, slot, xs, wgu, bgu, wd, bd)


def _combine_kernel(cnt_ref, off_ref, base_ref, x1_ref, idx_ref, gate_ref, offv_ref, gf_ref, y_ref,
                    o_ref, stage, sem):
    i = pl.program_id(0)
    last = pl.num_programs(0) - 1
    tm = TOK_TILE

    def runs(tile, slot):
        out = []
        for e in range(N_EXPERTS):
            n = cnt_ref[tile * N_EXPERTS + e]
            off = off_ref[tile * N_EXPERTS + e]
            base = base_ref[tile * N_EXPERTS + e]
            out.append((n, pltpu.make_async_copy(y_ref.at[pl.ds(base, n)], stage.at[slot, pl.ds(off, n)],
                                                 sem.at[slot])))
        return out

    def start_runs(tile, slot):
        for n, cp in runs(tile, slot):
            @pl.when(n > 0)
            def _():
                cp.start()

    @pl.when(i == 0)
    def _():
        for half in range(2):
            start_runs(half, half)

    for pair in range(2):
        @pl.when(((i & 1) != pair) & (i < last))
        def _():
            for half in range(2):
                start_runs(2 * (i + 1) + half, 2 * pair + half)

    gmats = []
    for half in range(2):
        rows_in = slice(half * tm, (half + 1) * tm)
        rows = _stage_rows(idx_ref[rows_in], offv_ref[half])
        lane = lax.broadcasted_iota(I32, (tm, STAGE_ROWS), 1)
        gates = gate_ref[rows_in]
        gmat = jnp.zeros((tm, STAGE_ROWS), F32)
        for k in range(TOP_K):
            gmat = gmat + jnp.where(lane == rows[k].astype(I32), gates[:, k:k + 1], 0.0)
        gmats.append(gmat.astype(BF16))
    for pair in range(2):
        @pl.when((i & 1) == pair)
        def _():
            for half in range(2):
                slot = 2 * pair + half
                rows_in = slice(half * tm, (half + 1) * tm)
                pltpu.make_async_copy(y_ref.at[pl.ds(0, STAGE_ROWS)], stage.at[slot], sem.at[slot]).wait()
                moe = jnp.dot(gmats[half], _tiles_to_rows(stage, (slot,)).astype(BF16), preferred_element_type=F32)
                o_ref[rows_in] = _rms(x1_ref[rows_in] + moe, gf_ref[...])


def _combine(x1, idx, gates, cnt, off, base, offv, gf, y):
    t = x1.shape[0]
    tm = 2 * TOK_TILE
    assert t % tm == 0
    return pl.pallas_call(
        _combine_kernel,
        grid_spec=pltpu.PrefetchScalarGridSpec(
            num_scalar_prefetch=3,
            grid=(t // tm,),
            in_specs=[pl.BlockSpec((tm, 1024), lambda i, *_: (i, 0)),
                      pl.BlockSpec((tm, TOP_K), lambda i, *_: (i, 0)),
                      pl.BlockSpec((tm, TOP_K), lambda i, *_: (i, 0)),
                      pl.BlockSpec((2, 1, LANES), lambda i, *_: (i, 0, 0)),
                      pl.BlockSpec((1, 1024), lambda i, *_: (0, 0)),
                      pl.BlockSpec(memory_space=pl.ANY)],
            out_specs=pl.BlockSpec((tm, 1024), lambda i, *_: (i, 0)),
            scratch_shapes=[pltpu.VMEM((4, STAGE_ROWS) + ROW_TILE, ROW_DTYPE),
                            pltpu.SemaphoreType.DMA((4,))]),
        out_shape=jax.ShapeDtypeStruct((t, 1024), F32),
        compiler_params=_cparams(("arbitrary",)),
        name="moe_combine",
    )(cnt, off, base, x1, idx, gates, offv, gf, y)


def _prep_w_in(w):
    wt = w.T
    c_if, c_qa, c_kv, c_g = 4096, 4104, 5128, 5640
    wm = wt[:c_if].astype(BF16)
    wif = jnp.pad(wt[c_if:c_qa], ((0, LANES - 2 * M_HEADS), (0, 0))).astype(BF16)
    wqa = (jnp.concatenate([wt[c_qa + (A_GROUP * g + j) * A_HD:c_qa + (A_GROUP * g + j + 1) * A_HD]
                            for j in range(A_GROUP) for g in range(A_KV_HEADS)], axis=0)
           * (A_HD ** -0.5)).astype(BF16)
    wkv = wt[c_kv:c_g].astype(BF16)
    wg = wt[c_g:].astype(BF16)
    return wm, wg, wqa, wkv, wif


def _rope_tables(pos):
    half = ROT_DIM // 2
    inv = ROPE_THETA ** (-(jnp.arange(half, dtype=F32) * 2.0 / ROT_DIM))
    ang = pos.astype(F32)[:, None] * inv[None, :]
    lane = jnp.arange(LANES) % A_HD
    hit = (lane[None, :] % half == jnp.arange(half)[:, None]) & (lane[None, :] < ROT_DIM)
    spread_cos = hit.astype(F32)
    spread_sin = jnp.where(hit, jnp.where(lane < half, -1.0, 1.0)[None, :], 0.0)
    cos_t = jnp.dot(jnp.cos(ang), spread_cos, precision=lax.Precision.HIGHEST) + (lane >= ROT_DIM).astype(F32)[None, :]
    sin_t = jnp.dot(jnp.sin(ang), spread_sin, precision=lax.Precision.HIGHEST)
    return cos_t, sin_t


def _round_up(x, m):
    return (x + m - 1) // m * m


def kernel(x_prompt, x_sample, cache_k, cache_v, state_c, state_n, state_m, norm1, w_in, b_igate, b_fgate,
           mlstm_norm, w_proj_m, w_proj_a, attn_sinks, w_out, norm2, w_router, b_router, w_gate_up, b_gate_up,
           w_down, b_down, norm_f):
    bp, sp, _ = x_prompt.shape
    bs, ts, _ = x_sample.shape
    past_len = 16384
    l = 0
    w1 = _prep_w_in(w_in[l])
    g1 = norm1[l][None, :]
    g2 = norm2[l][None, :]
    gf = norm_f[None, :]
    gate_bias = jnp.pad(jnp.concatenate([b_igate[l], b_fgate[l]]), (0, LANES - 2 * M_HEADS))[None, :]
    mn = mlstm_norm[l][None, :]
    wpm = w_proj_m[l].astype(BF16)
    wpa = jnp.concatenate([w_proj_a[l][(A_GROUP * g + j) * A_HD:(A_GROUP * g + j + 1) * A_HD]
                           for j in range(A_GROUP) for g in range(A_KV_HEADS)], axis=0).astype(BF16)
    wo = w_out[l].astype(BF16)
    wr_f = jnp.pad(w_router[l], ((0, 0), (0, LANES - N_EXPERTS)))
    wr_hi = wr_f.astype(BF16)
    wr = jnp.concatenate([wr_hi, (wr_f - wr_hi.astype(F32)).astype(BF16)], axis=1)
    br = jnp.pad(b_router[l], (0, LANES - N_EXPERTS))[None, :]
    wgu = w_gate_up[l]
    bgu = b_gate_up[l][:, None, :]
    wd = w_down[l]
    bd = b_down[l][:, None, :]
    sinks = attn_sinks[l]

    def mixer(x, nb, seq, pos0, L, c0, n0, m0, k_buf, v_buf, tm):
        t = nb * seq
        xf = x.reshape(t, D_MODEL)
        cos_t, sin_t = _rope_tables(pos0 + jnp.arange(seq))
        if seq < tm:
            cos_t, sin_t = jnp.tile(cos_t, (tm // seq, 1)), jnp.tile(sin_t, (tm // seq, 1))
        act = BF16 if L % 16 == 0 else F32
        qkv, og, qa, ka, va, gates = _inproj(xf, g1, w1, cos_t, sin_t, tm, max(seq // tm, 1), act)
        m0p = jnp.pad(m0, ((0, 0), (0, LANES - M_HEADS)))[:, None, :]
        hm, c, n, m = _mlstm(qkv, gates, og, c0, n0, m0p, gate_bias, mn, nb, L)
        if k_buf is None:
            ha = _swa_prompt(qa, ka, va, sinks, nb)
            k_keep = ka.reshape(nb, seq, A_KV)[:, seq - WINDOW:].reshape(nb, WINDOW, A_KV_HEADS, A_HD)
            v_keep = va.reshape(nb, seq, A_KV)[:, seq - WINDOW:].reshape(nb, WINDOW, A_KV_HEADS, A_HD)
        else:
            to_t = lambda a: a.transpose(0, 2, 3, 1).reshape(nb, A_KV, WINDOW)
            from_t = lambda a: a.reshape(nb, A_KV_HEADS, A_HD, WINDOW).transpose(0, 3, 1, 2)
            ha, k_keep, v_keep = _swa_sample(qa, ka, va, to_t(k_buf), to_t(v_buf), sinks, seq)
            k_keep, v_keep = from_t(k_keep), from_t(v_keep)
        x1, xn, idx, gate, cnt = _merge(xf, hm, ha, og, wpm, wpa, wo, g2, wr, br)
        return (x1, xn, idx, gate, cnt[:, :, 0]), (k_keep, v_keep, c, n, m[:, 0, :M_HEADS])

    zc = jnp.zeros((bp, M_HEADS, M_DV, M_DK), F32)
    zn = jnp.zeros((bp, M_HEADS, M_DK), F32)
    zm = jnp.zeros((bp, M_HEADS), F32)
    rp, sp_out = mixer(x_prompt, bp, sp, 0, M_CHUNK, zc, zn, zm, None, None, 512)
    rs, ss_out = mixer(x_sample, bs, ts, past_len, ts, state_c[l], state_n[l], state_m[l],
                       cache_k[l], cache_v[l], 256)

    cnt = jnp.concatenate([rp[4], rs[4]], axis=0)
    ntile_p = rp[4].shape[0]
    off = jnp.cumsum(cnt, axis=1) - cnt
    per_expert = jnp.sum(cnt, axis=0)
    padded = _round_up(per_expert, MOE_BLOCK)
    padded_end = jnp.cumsum(padded)
    expert_start = padded_end - padded
    base = expert_start[None, :] + jnp.cumsum(cnt, axis=0) - cnt
    n_tok = bp * sp + bs * ts
    n_slots = _round_up(n_tok * TOP_K, MOE_BLOCK) + N_EXPERTS * MOE_BLOCK
    nb_max = n_slots // MOE_BLOCK
    block_row = jnp.arange(nb_max, dtype=I32) * MOE_BLOCK
    block_expert = jnp.minimum(jnp.sum(block_row[:, None] >= padded_end[None, :], axis=1), N_EXPERTS - 1).astype(I32)
    n_blocks = (padded_end[-1] // MOE_BLOCK).astype(I32)[None]
    eid = jnp.arange(N_EXPERTS, dtype=I32)
    later_used = (eid[None, :] > eid[:, None]) & (padded[None, :] > 0)
    next_used = jnp.min(jnp.where(later_used, eid[None, :], N_EXPERTS), axis=1)
    next_used = jnp.where(next_used < N_EXPERTS, next_used, -1).astype(I32)
    run_parity = ((jnp.cumsum(padded > 0) - 1) & 1).astype(I32)
    of_block = lambda table: jnp.sum(jnp.where(block_expert[:, None] == eid[None, :], table[None, :], 0),
                                     axis=1).astype(I32)
    block_rows = jnp.clip(of_block(per_expert) - (block_row - of_block(expert_start)), 0, MOE_BLOCK).astype(I32)
    tails = jnp.concatenate([expert_start + per_expert, padded - per_expert,
                             n_blocks, nb_max - n_blocks]).astype(I32)
    flat = lambda a: a.reshape(-1).astype(I32)
    offv = jnp.pad(off, ((0, 0), (0, LANES - N_EXPERTS))).astype(F32)[:, None, :]

    xs = _dispatch(rp[1], rs[1], rp[2], rs[2], flat(cnt), flat(off), flat(base), tails, offv, n_slots)
    y = _experts(block_expert, block_rows, of_block(next_used), of_block(run_parity), xs, wgu, bgu, wd, bd)
    y_p = _combine(rp[0], rp[2], rp[3], flat(cnt[:ntile_p]), flat(off[:ntile_p]), flat(base[:ntile_p]),
                   offv[:ntile_p], gf, y)
    y_s = _combine(rs[0], rs[2], rs[3], flat(cnt[ntile_p:]), flat(off[ntile_p:]), flat(base[ntile_p:]),
                   offv[ntile_p:], gf, y)

    kp, vp, cp_, np_, mp = sp_out
    ks, vs, cs, ns, ms = ss_out
    return (y_p.reshape(bp, sp, D_MODEL), y_s.reshape(bs, ts, D_MODEL),
            kp[None], vp[None], cp_[None], np_[None], mp[None],
            ks[None], vs[None], cs[None], ns[None], ms[None])
```

```python
import functools

import jax
import jax.numpy as jnp
from jax import lax
from jax.experimental import pallas as pl
from jax.experimental.pallas import tpu as pltpu

F32 = jnp.float32
BF16 = jnp.bfloat16
I32 = jnp.int32

D_MODEL = 1024
M_HEADS = 4
M_DK = 256
M_DV = 256
M_CHUNK = 128
A_Q_HEADS = 16
A_KV_HEADS = 4
A_GROUP = 4
A_HD = 64
A_KV = A_KV_HEADS * A_HD
WINDOW = 128
A_BLOCK = 128
ROT_DIM = 16
ROPE_THETA = 500000.0
N_EXPERTS = 32
TOP_K = 4
D_FF = 1024
SWIGLU_LIMIT = 7.0
SWIGLU_ALPHA = 1.702
NORM_EPS = 1e-6

LANES = 128
SUBLANES = 8
VMEM_LIMIT = 56 * 1024 * 1024

MERGE_TILE = 512
TOK_TILE = 256
STAGE_ROWS = 4 * TOK_TILE
ROW_TILE = (8, 128)
ROW_DTYPE = F32
MOE_BLOCK = 512


def _cparams(sem):
    return pltpu.CompilerParams(dimension_semantics=sem, vmem_limit_bytes=VMEM_LIMIT)


def _rms(x, g):
    return x * lax.rsqrt(jnp.mean(x * x, axis=-1, keepdims=True) + NORM_EPS) * g


def _rope(x, cos_t, sin_t, first_half):
    n = x.shape[1]
    fwd = pltpu.roll(x, n - ROT_DIM // 2, axis=1)
    bwd = pltpu.roll(x, ROT_DIM // 2, axis=1)
    outs = []
    for b in range(n // LANES):
        sl = slice(b * LANES, (b + 1) * LANES)
        partner = jnp.where(first_half, fwd[:, sl], bwd[:, sl])
        outs.append(x[:, sl] * cos_t + partner * sin_t)
    return jnp.concatenate(outs, axis=1)


def _inproj_kernel(x_ref, g1_ref, wm_ref, wg_ref, wqa_ref, wkv_ref, wif_ref, cos_ref, sin_ref,
                   qkv_ref, og_ref, qa_ref, ka_ref, va_ref, gate_ref):
    x = x_ref[...]
    xn = _rms(x, g1_ref[...]).astype(BF16)
    act = qkv_ref.dtype

    def proj(w_ref, c0, n):
        return lax.dot_general(xn, w_ref[c0:c0 + n, :], (((1,), (1,)), ((), ())), preferred_element_type=F32)

    qkv_ref[:, 0:1024] = proj(wm_ref, 0, 1024).astype(act)
    qkv_ref[:, 1024:2048] = (proj(wm_ref, 1024, 1024) * (M_DK ** -0.5)).astype(act)
    qkv_ref[:, 2048:3072] = proj(wm_ref, 2048, 1024).astype(act)
    og_ref[:, 0:1024] = jax.nn.sigmoid(proj(wm_ref, 3072, 1024)).astype(act)
    og_ref[:, 1024:2048] = jax.nn.sigmoid(proj(wg_ref, 0, 1024)).astype(act)
    og_ref[:, 2048:3072] = jax.nn.sigmoid(proj(wg_ref, 1024, 1024)).astype(act)
    cos_t = cos_ref[...]
    sin_t = sin_ref[...]
    lane = lax.broadcasted_iota(I32, (1, LANES), 1)
    first_half = (lane & (ROT_DIM - 1)) < (ROT_DIM // 2)
    qa_ref[...] = _rope(proj(wqa_ref, 0, 1024), cos_t, sin_t, first_half).astype(act)
    ka_ref[...] = _rope(proj(wkv_ref, 0, A_KV), cos_t, sin_t, first_half)
    va_ref[...] = proj(wkv_ref, A_KV, A_KV)
    gate_ref[...] = proj(wif_ref, 0, LANES)


def _inproj(x, g1, weights, cos_t, sin_t, tm, rope_blocks, act):
    t = x.shape[0]
    tok = lambda n, dt: jax.ShapeDtypeStruct((t, n), dt)
    row = lambda n: pl.BlockSpec((tm, n), lambda i: (i, 0))
    resident = lambda w: pl.BlockSpec(w.shape, lambda i: (0, 0), pipeline_mode=pl.Buffered(1))
    return pl.pallas_call(
        _inproj_kernel,
        grid=(t // tm,),
        in_specs=[row(D_MODEL), pl.BlockSpec((1, D_MODEL), lambda i: (0, 0))]
                 + [resident(w) for w in weights]
                 + [pl.BlockSpec((tm, LANES), lambda i: (i % rope_blocks, 0)),
                    pl.BlockSpec((tm, LANES), lambda i: (i % rope_blocks, 0))],
        out_specs=[row(3072), row(3072), row(1024), row(A_KV), row(A_KV), row(LANES)],
        out_shape=[tok(3072, act), tok(3072, act), tok(1024, act), tok(A_KV, F32), tok(A_KV, F32),
                   tok(LANES, F32)],
        compiler_params=_cparams(("arbitrary",)),
        name="inproj",
    )(x, g1, *weights, cos_t, sin_t)


def _mlstm_gates(gc, bias, m_prev, L):
    z = gc + bias
    lf = jnp.minimum(z, 0.0) - jnp.log1p(jnp.exp(-jnp.abs(z)))
    row = lax.broadcasted_iota(I32, (L, LANES), 0)
    fc = lf
    sh = 1
    while sh < L:
        fc = fc + jnp.where(row >= sh, pltpu.roll(fc, sh, axis=0), 0.0)
        sh *= 2
    fcum = pltpu.roll(fc, LANES - M_HEADS, axis=1)
    a = z - fcum
    cmx = a
    sh = 1
    while sh < L:
        cmx = jnp.maximum(cmx, jnp.where(row >= sh, pltpu.roll(cmx, sh, axis=0), -jnp.inf))
        sh *= 2
    mx = jnp.maximum(m_prev, cmx)
    inter = jnp.exp(m_prev - mx)
    f_end = fcum[L - 1:L]
    m_end = f_end + mx[L - 1:L]
    decay = jnp.exp(f_end + m_prev - m_end)
    wsrc = jnp.exp(a + (f_end - m_end))
    return a, mx, inter, wsrc, decay, m_end


NT_DIMS = (((1,), (1,)), ((), ()))
TN_DIMS = (((0,), (0,)), ((), ()))


def _mlstm_decay(h, L, gates):
    a, mx = gates[0], gates[1]
    t_i = lax.broadcasted_iota(I32, (L, L), 0)
    s_i = lax.broadcasted_iota(I32, (L, L), 1)
    a_row = jnp.sum(jnp.where(t_i == s_i, a[:, h:h + 1], 0.0), axis=0, keepdims=True)
    return jnp.exp(jnp.where(s_i <= t_i, a_row - mx[:, h:h + 1], -jnp.inf))


def _mlstm_output(h, q, og, mn, gates, qk_sum, pv, cq, n_h):
    inter_col = gates[2][:, h:h + 1]
    num = inter_col * cq + pv
    den = inter_col * jnp.sum(q.astype(F32) * n_h, axis=1, keepdims=True) + qk_sum
    hh = num / jnp.maximum(jnp.abs(den), 1.0)
    hn = hh * lax.rsqrt(jnp.mean(hh * hh, axis=1, keepdims=True) + NORM_EPS)
    return hn * mn * og


def _mlstm_kernel(L, G, CH, carry, qkv_ref, gate_ref, og_ref, c0_ref, n0_ref, m0_ref, bias_ref, mn_ref,
                  h_ref, c_ref, n_ref, m_ref, *scratch):
    if carry:
        c_in, n_in, m_in = scratch
        ci = pl.program_id(1)

        @pl.when(ci == 0)
        def _():
            c_in[...] = c0_ref[...]
            n_in[...] = n0_ref[...]
            m_in[...] = m0_ref[...]
        c_out, n_out, m_out = scratch
    else:
        c_in, n_in, m_in = c0_ref, n0_ref, m0_ref
        c_out, n_out, m_out = c_ref, n_ref, m_ref

    def qkv(j, cc, h, part):
        cols = slice(part * 1024 + h * M_DK, part * 1024 + (h + 1) * M_DK)
        return qkv_ref[j, cc * L:(cc + 1) * L, cols].astype(BF16)

    units = [(j, cc, h) for cc in range(CH) for j in range(G) for h in range(M_HEADS)]
    gates = {}
    for j in range(G):
        m_cur = m_in[j]
        for cc in range(CH):
            gates[j, cc] = _mlstm_gates(gate_ref[j, cc * L:(cc + 1) * L], bias_ref[...], m_cur, L)
            m_cur = gates[j, cc][5]
        m_out[j] = m_cur
    def carried(cc, c_cur):
        return {(j, h): lax.dot_general(qkv(j, cc, h, 0), c_cur[j, h].astype(BF16), NT_DIMS,
                                        preferred_element_type=F32) for j, c, h in units if c == cc}

    c_cur = {(j, h): c_in[j, h] for j in range(G) for h in range(M_HEADS)}
    n_cur = {(j, h): n_in[j, h:h + 1, :] for j in range(G) for h in range(M_HEADS)}
    scores = {u: lax.dot_general(qkv(*u, 0), qkv(*u, 1), NT_DIMS, preferred_element_type=F32) for u in units}
    cqs_first = carried(0, c_cur)
    qks = {(j, cc, h): scores[j, cc, h] * _mlstm_decay(h, L, gates[j, cc]) for j, cc, h in units}
    pvs = {u: jnp.dot(qks[u].astype(BF16), qkv(*u, 2), preferred_element_type=F32) for u in units}
    kws = {(j, cc, h): gates[j, cc][3][:, h:h + 1] * qkv(j, cc, h, 1).astype(F32) for j, cc, h in units}
    for cc in range(CH):
        now = [u for u in units if u[1] == cc]
        cqs = cqs_first if cc == 0 else carried(cc, c_cur)
        n_old = dict(n_cur)

        def update_states():
            for j, _, h in now:
                dec = gates[j, cc][4][:, h:h + 1]
                c_cur[j, h] = dec * c_cur[j, h] + lax.dot_general(qkv(j, cc, h, 2), kws[j, cc, h].astype(BF16),
                                                                  TN_DIMS, preferred_element_type=F32)
                n_cur[j, h] = dec * n_cur[j, h] + jnp.sum(kws[j, cc, h], axis=0, keepdims=True)

        def write_outputs():
            for j, _, h in now:
                sl = slice(h * M_DV, (h + 1) * M_DV)
                rows = slice(cc * L, (cc + 1) * L)
                out = _mlstm_output(h, qkv(j, cc, h, 0), og_ref[j, rows, sl].astype(F32), mn_ref[:, sl],
                                    gates[j, cc], jnp.sum(qks[j, cc, h], axis=1, keepdims=True), pvs[j, cc, h],
                                    cqs[j, h], n_old[j, h])
                h_ref[j, rows, sl] = out.astype(h_ref.dtype)

        for phase in ((update_states, write_outputs) if carry else (write_outputs, update_states)):
            phase()
    for j in range(G):
        for h in range(M_HEADS):
            c_out[j, h] = c_cur[j, h]
            n_out[j, h:h + 1, :] = n_cur[j, h]

    if carry:
        @pl.when(ci == pl.num_programs(1) - 1)
        def _():
            c_ref[...] = c_in[...]
            n_ref[...] = n_in[...]
            m_ref[...] = m_in[...]


MLSTM_GROUP_CARRY = 2
MLSTM_CHUNKS_CARRY = 1
MLSTM_GROUP_SINGLE = 8


def _mlstm(qkv, gates, og, c0, n0, m0, gate_bias, mnorm, nb, L):
    t = qkv.shape[0]
    nc = t // (nb * L)
    carry = nc > 1
    G = MLSTM_GROUP_CARRY if carry else MLSTM_GROUP_SINGLE
    CH = MLSTM_CHUNKS_CARRY if carry else 1
    qkv, gates, og = (a.reshape(nb, nc * L, a.shape[1]) for a in (qkv, gates, og))
    tokmap = lambda b, c: (b, c, 0)
    const2 = lambda b, c: (0, 0)
    state_specs = [pl.BlockSpec((G, M_HEADS, M_DV, M_DK), lambda b, c: (b, 0, 0, 0)),
                   pl.BlockSpec((G, M_HEADS, M_DK), lambda b, c: (b, 0, 0)),
                   pl.BlockSpec((G, 1, LANES), lambda b, c: (b, 0, 0))]
    scratch = [pltpu.VMEM((G, M_HEADS, M_DV, M_DK), F32),
               pltpu.VMEM((G, M_HEADS, M_DK), F32),
               pltpu.VMEM((G, 1, LANES), F32)] if carry else []
    h, c, n, m = pl.pallas_call(
        functools.partial(_mlstm_kernel, L, G, CH, carry),
        grid=(nb // G, nc // CH),
        in_specs=[pl.BlockSpec((G, CH * L, 3072), tokmap),
                  pl.BlockSpec((G, CH * L, LANES), tokmap),
                  pl.BlockSpec((G, CH * L, 1024), tokmap)]
                 + state_specs
                 + [pl.BlockSpec((1, LANES), const2),
                    pl.BlockSpec((1, 1024), const2)],
        out_specs=[pl.BlockSpec((G, CH * L, 1024), tokmap)] + state_specs,
        out_shape=[jax.ShapeDtypeStruct((nb, nc * L, 1024), qkv.dtype),
                   jax.ShapeDtypeStruct((nb, M_HEADS, M_DV, M_DK), F32),
                   jax.ShapeDtypeStruct((nb, M_HEADS, M_DK), F32),
                   jax.ShapeDtypeStruct((nb, 1, LANES), F32)],
        scratch_shapes=scratch,
        compiler_params=_cparams(("arbitrary", "arbitrary")),
        name="mlstm_L%d" % L,
    )(qkv, gates, og, c0, n0, m0, gate_bias, mnorm)
    return h.reshape(t, 1024), c, n, m


def _swa_bias(m, has_prev):
    nkeys = 2 * A_BLOCK
    t_i = lax.broadcasted_iota(I32, (A_GROUP * m, nkeys), 0) & (m - 1)
    s_i = lax.broadcasted_iota(I32, (A_GROUP * m, nkeys), 1)
    diff = t_i + A_BLOCK - s_i
    mask = (diff >= 0) & (diff <= WINDOW)
    if not has_prev:
        mask = mask & (s_i >= A_BLOCK)
    return jnp.where(mask, 0.0, -jnp.inf).astype(F32)


def _swa_group_ones():
    nkeys = 2 * A_BLOCK
    r = lax.broadcasted_iota(I32, (A_KV_HEADS * nkeys, A_KV), 0) >> (nkeys.bit_length() - 1)
    c = lax.broadcasted_iota(I32, (A_KV_HEADS * nkeys, A_KV), 1) >> (A_HD.bit_length() - 1)
    return (r == c).astype(BF16)


def _swa_core(q, k_prev, v_prev, k_cur, v_cur, bias, group_ones, sink_ref, keys_on_lanes=False):
    m = q.shape[0]
    q = q.astype(BF16)
    qst = jnp.concatenate([q[:, j * A_KV:(j + 1) * A_KV] for j in range(A_GROUP)], axis=0)
    key_axis = 1 if keys_on_lanes else 0
    k_all = jnp.concatenate([k_prev, k_cur], axis=key_axis)
    v_all = jnp.concatenate([v_prev, v_cur], axis=key_axis)
    feat = lax.broadcasted_iota(I32, (A_KV, 1) if keys_on_lanes else (1, A_KV), 1 - key_axis)
    kbd, vbd = [], []
    for g in range(A_KV_HEADS):
        in_g = (feat >= g * A_HD) & (feat < (g + 1) * A_HD)
        kbd.append(jnp.where(in_g, k_all, 0.0).astype(BF16))
        vbd.append(jnp.where(in_g, v_all, 0.0).astype(BF16))
    kbd = jnp.concatenate(kbd, axis=key_axis)
    vbd = jnp.concatenate(vbd, axis=key_axis)
    if keys_on_lanes:
        s = jnp.dot(qst, kbd, preferred_element_type=F32)
    else:
        s = lax.dot_general(qst, kbd, NT_DIMS, preferred_element_type=F32)
    nkeys = 2 * A_BLOCK
    ps, sink_terms = [], []
    for g in range(A_KV_HEADS):
        sg = s[:, g * nkeys:(g + 1) * nkeys] + bias
        sink = jnp.concatenate(
            [jnp.full((m, LANES), sink_ref[A_GROUP * g + j], F32) for j in range(A_GROUP)], axis=0)
        mx = jnp.maximum(jnp.broadcast_to(jnp.max(sg, axis=1, keepdims=True), (A_GROUP * m, LANES)), sink)
        ps += [jnp.exp(sg[:, :LANES] - mx).astype(BF16), jnp.exp(sg[:, LANES:] - mx).astype(BF16)]
        sink_terms.append(jnp.exp(sink - mx))
    p = jnp.concatenate(ps, axis=1)
    if keys_on_lanes:
        o = lax.dot_general(p, vbd, NT_DIMS, preferred_element_type=F32)
    else:
        o = jnp.dot(p, vbd, preferred_element_type=F32)
    den = jnp.dot(p, group_ones, preferred_element_type=F32)
    low = lax.broadcasted_iota(I32, (1, LANES), 1) < A_HD
    den = den + jnp.concatenate([jnp.where(low, sink_terms[0], sink_terms[1]),
                                 jnp.where(low, sink_terms[2], sink_terms[3])], axis=1)
    o = o / den
    return jnp.concatenate([o[j * m:(j + 1) * m] for j in range(A_GROUP)], axis=1)


SWA_STEP_BLOCKS = 8
SWA_STEP = SWA_STEP_BLOCKS * A_BLOCK


def _swa_prompt_kernel(sink_ref, q_ref, kp_ref, kc_ref, vp_ref, vc_ref, o_ref, bias_s, bias0_s, ones_s):
    first = (pl.program_id(0) == 0) & (pl.program_id(1) == 0)

    @pl.when(first)
    def _():
        bias_s[...] = _swa_bias(A_BLOCK, True)
        bias0_s[...] = _swa_bias(A_BLOCK, False)
        ones_s[...] = _swa_group_ones()

    bias = bias_s[...]
    bias_first = jnp.where(pl.program_id(1) > 0, bias, bias0_s[...])
    ones = ones_s[...]
    k_prev, v_prev = kp_ref[...], vp_ref[...]
    for b in range(SWA_STEP_BLOCKS):
        rows = slice(b * A_BLOCK, (b + 1) * A_BLOCK)
        k_cur, v_cur = kc_ref[rows], vc_ref[rows]
        o_ref[rows] = _swa_core(q_ref[rows], k_prev, v_prev, k_cur, v_cur, bias_first if b == 0 else bias,
                                ones, sink_ref).astype(o_ref.dtype)
        k_prev, v_prev = k_cur, v_cur


def _swa_prompt(qa, ka, va, sinks, nb):
    t = qa.shape[0]
    nstep = t // (nb * SWA_STEP)
    cur = lambda b, i, s: (b * nstep + i, 0)
    prev = lambda b, i, s: (SWA_STEP_BLOCKS * (b * nstep + i) - jnp.minimum(i, 1), 0)
    return pl.pallas_call(
        _swa_prompt_kernel,
        grid_spec=pltpu.PrefetchScalarGridSpec(
            num_scalar_prefetch=1,
            grid=(nb, nstep),
            in_specs=[pl.BlockSpec((SWA_STEP, 1024), cur),
                      pl.BlockSpec((A_BLOCK, A_KV), prev),
                      pl.BlockSpec((SWA_STEP, A_KV), cur),
                      pl.BlockSpec((A_BLOCK, A_KV), prev),
                      pl.BlockSpec((SWA_STEP, A_KV), cur)],
            out_specs=pl.BlockSpec((SWA_STEP, 1024), cur),
            scratch_shapes=[pltpu.VMEM((A_GROUP * A_BLOCK, 2 * A_BLOCK), F32),
                            pltpu.VMEM((A_GROUP * A_BLOCK, 2 * A_BLOCK), F32),
                            pltpu.VMEM((A_KV_HEADS * 2 * A_BLOCK, A_KV), BF16)]),
        out_shape=jax.ShapeDtypeStruct((t, 1024), BF16),
        compiler_params=_cparams(("arbitrary", "arbitrary")),
        name="swa_prompt",
    )(sinks, qa, ka, ka, va, va)


SWA_SAMPLE_GROUP = 16


def _swa_sample_kernel(T, sink_ref, q_ref, kn_ref, vn_ref, kb_ref, vb_ref, o_ref, ko_ref, vo_ref):
    pad = jnp.zeros((A_BLOCK - T, A_KV), F32)
    bias = _swa_bias(T, True)
    ones = _swa_group_ones()
    lane = lax.broadcasted_iota(I32, (1, WINDOW), 1)
    for j in range(SWA_SAMPLE_GROUP):
        rows = slice(j * T, (j + 1) * T)
        k_new_t = jnp.concatenate([kn_ref[rows], pad], axis=0).T
        v_new_t = jnp.concatenate([vn_ref[rows], pad], axis=0).T
        k_buf_t = kb_ref[j]
        v_buf_t = vb_ref[j]
        o_ref[rows] = _swa_core(q_ref[rows], k_buf_t, v_buf_t, k_new_t, v_new_t, bias, ones, sink_ref,
                                keys_on_lanes=True).astype(o_ref.dtype)
        keep = lane < WINDOW - T
        ko_ref[j] = jnp.where(keep, pltpu.roll(k_buf_t, WINDOW - T, axis=1), pltpu.roll(k_new_t, WINDOW - T, axis=1))
        vo_ref[j] = jnp.where(keep, pltpu.roll(v_buf_t, WINDOW - T, axis=1), pltpu.roll(v_new_t, WINDOW - T, axis=1))


def _swa_sample(qa, ka, va, k_buf, v_buf, sinks, T):
    nb = k_buf.shape[0]
    g = SWA_SAMPLE_GROUP
    tok = lambda b, s: (b, 0)
    buf = lambda b, s: (b, 0, 0)
    return pl.pallas_call(
        functools.partial(_swa_sample_kernel, T),
        grid_spec=pltpu.PrefetchScalarGridSpec(
            num_scalar_prefetch=1,
            grid=(nb // g,),
            in_specs=[pl.BlockSpec((g * T, 1024), tok),
                      pl.BlockSpec((g * T, A_KV), tok),
                      pl.BlockSpec((g * T, A_KV), tok),
                      pl.BlockSpec((g, A_KV, WINDOW), buf),
                      pl.BlockSpec((g, A_KV, WINDOW), buf)],
            out_specs=[pl.BlockSpec((g * T, 1024), tok),
                       pl.BlockSpec((g, A_KV, WINDOW), buf),
                       pl.BlockSpec((g, A_KV, WINDOW), buf)]),
        out_shape=[jax.ShapeDtypeStruct((nb * T, 1024), qa.dtype),
                   jax.ShapeDtypeStruct((nb, A_KV, WINDOW), F32),
                   jax.ShapeDtypeStruct((nb, A_KV, WINDOW), F32)],
        compiler_params=_cparams(("arbitrary",)),
        name="swa_sample",
    )(sinks, qa, ka, va, k_buf, v_buf)


def _merge_kernel(x_ref, hm_ref, ha_ref, gm_ref, ga_ref, wpm_ref, wpa_ref, wo_ref, g2_ref, wr_ref, br_ref,
                  x1_ref, xn_ref, idx_ref, gate_ref, cnt_ref):
    pm = jnp.dot(hm_ref[...].astype(BF16), wpm_ref[...], preferred_element_type=F32)
    pa = jnp.dot(ha_ref[...].astype(BF16), wpa_ref[...], preferred_element_type=F32)
    mixed = gm_ref[...].astype(F32) * pm + ga_ref[...].astype(F32) * pa
    x1 = x_ref[...] + jnp.dot(mixed.astype(BF16), wo_ref[...], preferred_element_type=F32)
    x1_ref[...] = x1
    xn = _rms(x1, g2_ref[...])
    xn_ref[...] = xn.astype(BF16)
    xn_hi = xn.astype(BF16)
    xn_lo = (xn - xn_hi.astype(F32)).astype(BF16)
    part = jnp.dot(xn_hi, wr_ref[...], preferred_element_type=F32)
    logits = (part[:, :LANES] + part[:, LANES:]
              + jnp.dot(xn_lo, wr_ref[:, :LANES], preferred_element_type=F32) + br_ref[...])
    tm = logits.shape[0]
    cur = logits.T[0:N_EXPERTS]
    eid = lax.broadcasted_iota(I32, (N_EXPERTS, tm), 0).astype(F32)
    vals, idxs = [], []
    onehot = jnp.zeros((N_EXPERTS, tm), F32)
    for _ in range(TOP_K):
        mval = jnp.max(cur, axis=0, keepdims=True)
        sel = jnp.min(jnp.where(cur == mval, eid, float(N_EXPERTS)), axis=0, keepdims=True)
        hit = eid == sel
        onehot = onehot + hit.astype(F32)
        cur = jnp.where(hit, -jnp.inf, cur)
        vals.append(mval)
        idxs.append(sel)
    es = [jnp.exp(v - vals[0]) for v in vals]
    tot = es[0] + es[1] + es[2] + es[3]
    row = lax.broadcasted_iota(I32, (SUBLANES, tm), 0)
    res = jnp.zeros((SUBLANES, tm), F32)
    for k in range(TOP_K):
        res = jnp.where(row == k, idxs[k], res)
        res = jnp.where(row == TOP_K + k, es[k] / tot, res)
    res_t = jnp.concatenate([res, jnp.zeros((LANES - SUBLANES, tm), F32)], axis=0).T
    idx_ref[...] = res_t[:, 0:TOP_K].astype(I32)
    gate_ref[...] = res_t[:, TOP_K:2 * TOP_K]
    for j in range(tm // TOK_TILE):
        cnt_ref[j] = jnp.sum(onehot[:, j * TOK_TILE:(j + 1) * TOK_TILE], axis=1, keepdims=True).astype(I32)


def _merge(x, hm, ha, og, wpm, wpa, wo, g2, wr, br):
    t = x.shape[0]
    tm = MERGE_TILE
    row = lambda n: pl.BlockSpec((tm, n), lambda i: (i, 0))
    const = lambda r, c: pl.BlockSpec((r, c), lambda i: (0, 0))
    return pl.pallas_call(
        _merge_kernel,
        grid=(t // tm,),
        in_specs=[row(1024), row(1024), row(1024),
                  pl.BlockSpec((tm, 1024), lambda i: (i, 1)),
                  pl.BlockSpec((tm, 1024), lambda i: (i, 2)),
                  const(1024, 1024), const(1024, 1024), const(1024, 1024), const(1, 1024),
                  const(1024, 2 * LANES), const(1, LANES)],
        out_specs=[row(1024), row(1024), row(TOP_K), row(TOP_K),
                   pl.BlockSpec((tm // TOK_TILE, N_EXPERTS, 1), lambda i: (i, 0, 0))],
        out_shape=[jax.ShapeDtypeStruct((t, 1024), F32),
                   jax.ShapeDtypeStruct((t, 1024), BF16),
                   jax.ShapeDtypeStruct((t, TOP_K), I32),
                   jax.ShapeDtypeStruct((t, TOP_K), F32),
                   jax.ShapeDtypeStruct((t // TOK_TILE, N_EXPERTS, 1), I32)],
        compiler_params=_cparams(("arbitrary",)),
        name="merge_route",
    )(x, hm, ha, og, og, wpm, wpa, wo, g2, wr, br)


def _rows_to_tiles(ref, lead, x):
    chunks = jnp.stack([x[:, s * LANES:(s + 1) * LANES] for s in range(ROW_TILE[0])], axis=0)
    ref[lead] = jnp.swapaxes(chunks, 0, 1).astype(ref.dtype)


def _tiles_to_rows(ref, lead, rows=slice(None)):
    chunks = jnp.swapaxes(ref[lead + (rows,)].astype(F32), 0, 1)
    return jnp.concatenate([chunks[s] for s in range(ROW_TILE[0])], axis=1)


def _stage_rows(idx, off_row):
    tm = idx.shape[0]
    lane = lax.broadcasted_iota(I32, (tm, LANES), 1)
    hits = [lane == idx[:, k:k + 1] for k in range(TOP_K)]
    onehot = sum(h.astype(F32) for h in hits)
    r_i = lax.broadcasted_iota(I32, (tm, tm), 0)
    c_i = lax.broadcasted_iota(I32, (tm, tm), 1)
    before = (c_i < r_i).astype(BF16)
    rank = jnp.dot(before, onehot.astype(BF16), preferred_element_type=F32)
    pos = rank + off_row
    return [jnp.sum(jnp.where(h, pos, 0.0), axis=1, keepdims=True) for h in hits]


def _dispatch_kernel(ntile_a, cnt_ref, off_ref, base_ref, tail_ref, xa_ref, xb_ref, ia_ref, ib_ref, offv_ref,
                     xs_ref, stage, zeros, sem, zsem):
    i = pl.program_id(0)
    from_a = 2 * i < ntile_a
    tm = TOK_TILE

    def stage_tile(slot):
        rows_in = slice(slot * tm, (slot + 1) * tm)
        xn = jnp.where(from_a, xa_ref[rows_in], xb_ref[rows_in])
        idx = jnp.where(from_a, ia_ref[rows_in], ib_ref[rows_in])
        rows = _stage_rows(idx, offv_ref[slot])
        lane = lax.broadcasted_iota(I32, (tm, LANES), 1)
        r4 = jnp.zeros((tm, LANES), F32)
        for k in range(TOP_K):
            r4 = jnp.where(lane == k, rows[k], r4)
        r4t = r4.T.astype(I32)
        r_iota = lax.broadcasted_iota(I32, (STAGE_ROWS, tm), 0)
        sel = r_iota == r4t[0:1, :]
        for k in range(1, TOP_K):
            sel = sel | (r_iota == r4t[k:k + 1, :])
        sel = jnp.where(sel, 1.0, 0.0).astype(BF16)
        return jnp.dot(sel, xn, preferred_element_type=F32)

    def runs(tile, slot):
        out = []
        for e in range(N_EXPERTS):
            n = cnt_ref[tile * N_EXPERTS + e]
            off = off_ref[tile * N_EXPERTS + e]
            base = base_ref[tile * N_EXPERTS + e]
            out.append((n, pltpu.make_async_copy(stage.at[slot, pl.ds(off, n)], xs_ref.at[pl.ds(base, n)],
                                                 sem.at[slot])))
        return out

    def wait_runs(slot):
        pltpu.make_async_copy(stage.at[slot], xs_ref.at[pl.ds(0, STAGE_ROWS)], sem.at[slot]).wait()

    staged_tiles = [stage_tile(slot) for slot in range(2)]

    for slot in range(2):
        @pl.when(i >= 1)
        def _():
            wait_runs(slot)
        _rows_to_tiles(stage, (slot,), staged_tiles[slot])
        for n, cp in runs(2 * i + slot, slot):
            @pl.when(n > 0)
            def _():
                cp.start()

    @pl.when(i == pl.num_programs(0) - 1)
    def _():
        for slot in range(2):
            wait_runs(slot)
        zeros[...] = jnp.zeros(zeros.shape, zeros.dtype)
        tails = []
        for e in range(N_EXPERTS):
            start = tail_ref[e]
            n = tail_ref[N_EXPERTS + e]
            tails.append((n, pltpu.make_async_copy(zeros.at[pl.ds(0, n)], xs_ref.at[pl.ds(start, n)], zsem)))
        for n, cp in tails:
            @pl.when(n > 0)
            def _():
                cp.start()
        for n, cp in tails:
            @pl.when(n > 0)
            def _():
                cp.wait()
        first = tail_ref[2 * N_EXPERTS]
        n_unused = tail_ref[2 * N_EXPERTS + 1]

        def unused_block(j):
            return pltpu.make_async_copy(zeros, xs_ref.at[pl.ds((first + j) * MOE_BLOCK, MOE_BLOCK)], zsem)

        @pl.loop(0, n_unused)
        def _(j):
            unused_block(j).start()

        @pl.loop(0, n_unused)
        def _(j):
            unused_block(j).wait()


def _dispatch(xa, xb, ia, ib, cnt, off, base, tails, offv, n_slots):
    tm = 2 * TOK_TILE
    assert xa.shape[0] % tm == 0 and xb.shape[0] % tm == 0
    nstep_a = xa.shape[0] // tm
    nstep_b = xb.shape[0] // tm
    amap = lambda i, *_: (jnp.minimum(i, nstep_a - 1), 0)
    bmap = lambda i, *_: (jnp.maximum(i - nstep_a, 0), 0)
    return pl.pallas_call(
        functools.partial(_dispatch_kernel, 2 * nstep_a),
        grid_spec=pltpu.PrefetchScalarGridSpec(
            num_scalar_prefetch=4,
            grid=(nstep_a + nstep_b,),
            in_specs=[pl.BlockSpec((tm, 1024), amap),
                      pl.BlockSpec((tm, 1024), bmap),
                      pl.BlockSpec((tm, TOP_K), amap),
                      pl.BlockSpec((tm, TOP_K), bmap),
                      pl.BlockSpec((2, 1, LANES), lambda i, *_: (i, 0, 0))],
            out_specs=pl.BlockSpec(memory_space=pl.ANY),
            scratch_shapes=[pltpu.VMEM((2, STAGE_ROWS) + ROW_TILE, ROW_DTYPE),
                            pltpu.VMEM((MOE_BLOCK,) + ROW_TILE, ROW_DTYPE),
                            pltpu.SemaphoreType.DMA((2,)),
                            pltpu.SemaphoreType.DMA]),
        out_shape=jax.ShapeDtypeStruct((n_slots,) + ROW_TILE, ROW_DTYPE),
        compiler_params=_cparams(("arbitrary",)),
        name="moe_dispatch",
    )(cnt, off, base, tails, xa, xb, ia, ib, offv)


MOE_HALF = MOE_BLOCK // 2


def _experts_kernel(be_ref, rows_ref, next_ref, slot_ref, xs_ref, wgu_hbm, bgu_ref, wd_hbm, bd_ref, y_ref,
                    wgu_f, wd_f, wgu_s, wd_s, sem):
    i = pl.program_id(0)
    rows = rows_ref[i]
    live = rows > 0
    run_start = (i == 0) | (be_ref[i] != be_ref[jnp.maximum(i - 1, 0)])

    def fetch(expert, slot):
        return (pltpu.make_async_copy(wgu_hbm.at[expert], wgu_f.at[slot], sem.at[0, slot]),
                pltpu.make_async_copy(wd_hbm.at[expert], wd_f.at[slot], sem.at[1, slot]))

    for slot in range(2):
        @pl.when(live & run_start & (slot_ref[i] == slot))
        def _():
            @pl.when(i == 0)
            def _():
                for cp in fetch(be_ref[i], slot):
                    cp.start()
            for cp in fetch(be_ref[i], slot):
                cp.wait()

            @pl.when(next_ref[i] >= 0)
            def _():
                for cp in fetch(next_ref[i], 1 - slot):
                    cp.start()
            wgu_s[...] = wgu_f[slot].astype(BF16)
            wd_s[...] = wd_f[slot].astype(BF16)

    def ffn(x):
        h = jnp.dot(x.astype(BF16), wgu_s[...], preferred_element_type=F32) + bgu_ref[0]
        gate = jnp.minimum(h[:, :D_FF], SWIGLU_LIMIT)
        up = jnp.clip(h[:, D_FF:], -SWIGLU_LIMIT, SWIGLU_LIMIT)
        act = gate * jax.nn.sigmoid(SWIGLU_ALPHA * gate) * (up + 1.0)
        return jnp.dot(act.astype(BF16), wd_s[...], preferred_element_type=F32) + bd_ref[0]

    @pl.when(rows > MOE_HALF)
    def _():
        _rows_to_tiles(y_ref, (), ffn(_tiles_to_rows(xs_ref, ())))

    @pl.when(live & (rows <= MOE_HALF))
    def _():
        y_half = ffn(_tiles_to_rows(xs_ref, (), slice(0, MOE_HALF)))
        _rows_to_tiles(y_ref, (), jnp.concatenate([y_half, jnp.zeros((MOE_HALF, D_MODEL), F32)], axis=0))

    @pl.when(rows == 0)
    def _():
        y_ref[...] = jnp.zeros(y_ref.shape, y_ref.dtype)


def _experts(block_expert, block_rows, next_expert, slot, xs, wgu, bgu, wd, bd):
    nb_max = xs.shape[0] // MOE_BLOCK
    blk = lambda i, be, rows, *_: (jnp.where(rows[i] > 0, i, 0), 0, 0)
    wmap = lambda i, be, *_: (be[i], 0, 0)
    return pl.pallas_call(
        _experts_kernel,
        grid_spec=pltpu.PrefetchScalarGridSpec(
            num_scalar_prefetch=4,
            grid=(nb_max,),
            in_specs=[pl.BlockSpec((MOE_BLOCK,) + ROW_TILE, blk),
                      pl.BlockSpec(memory_space=pl.ANY),
                      pl.BlockSpec((1, 1, 2 * D_FF), wmap),
                      pl.BlockSpec(memory_space=pl.ANY),
                      pl.BlockSpec((1, 1, D_MODEL), wmap)],
            out_specs=pl.BlockSpec((MOE_BLOCK,) + ROW_TILE, lambda i, *_: (i, 0, 0)),
            scratch_shapes=[pltpu.VMEM((2, D_MODEL, 2 * D_FF), F32),
                            pltpu.VMEM((2, D_FF, D_MODEL), F32),
                            pltpu.VMEM((D_MODEL, 2 * D_FF), BF16),
                            pltpu.VMEM((D_FF, D_MODEL), BF16),
                            pltpu.SemaphoreType.DMA((2, 2))]),
        out_shape=jax.ShapeDtypeStruct(xs.shape, ROW_DTYPE),
        compiler_params=_cparams(("arbitrary",)),
        name="moe_experts",
    )(block_expert, block_rows, next_expert, slot, xs, wgu, bgu, wd, bd)


def _combine_kernel(cnt_ref, off_ref, base_ref, x1_ref, idx_ref, gate_ref, offv_ref, gf_ref, y_ref,
                    o_ref, stage, sem):
    i = pl.program_id(0)
    last = pl.num_programs(0) - 1
    tm = TOK_TILE

    def runs(tile, slot):
        out = []
        for e in range(N_EXPERTS):
            n = cnt_ref[tile * N_EXPERTS + e]
            off = off_ref[tile * N_EXPERTS + e]
            base = base_ref[tile * N_EXPERTS + e]
            out.append((n, pltpu.make_async_copy(y_ref.at[pl.ds(base, n)], stage.at[slot, pl.ds(off, n)],
                                                 sem.at[slot])))
        return out

    def start_runs(tile, slot):
        for n, cp in runs(tile, slot):
            @pl.when(n > 0)
            def _():
                cp.start()

    @pl.when(i == 0)
    def _():
        for half in range(2):
            start_runs(half, half)

    for pair in range(2):
        @pl.when(((i & 1) != pair) & (i < last))
        def _():
            for half in range(2):
                start_runs(2 * (i + 1) + half, 2 * pair + half)

    gmats = []
    for half in range(2):
        rows_in = slice(half * tm, (half + 1) * tm)
        rows = _stage_rows(idx_ref[rows_in], offv_ref[half])
        lane = lax.broadcasted_iota(I32, (tm, STAGE_ROWS), 1)
        gates = gate_ref[rows_in]
        gmat = jnp.zeros((tm, STAGE_ROWS), F32)
        for k in range(TOP_K):
            gmat = gmat + jnp.where(lane == rows[k].astype(I32), gates[:, k:k + 1], 0.0)
        gmats.append(gmat.astype(BF16))
    for pair in range(2):
        @pl.when((i & 1) == pair)
        def _():
            for half in range(2):
                slot = 2 * pair + half
                rows_in = slice(half * tm, (half + 1) * tm)
                pltpu.make_async_copy(y_ref.at[pl.ds(0, STAGE_ROWS)], stage.at[slot], sem.at[slot]).wait()
                moe = jnp.dot(gmats[half], _tiles_to_rows(stage, (slot,)).astype(BF16), preferred_element_type=F32)
                o_ref[rows_in] = _rms(x1_ref[rows_in] + moe, gf_ref[...])


def _combine(x1, idx, gates, cnt, off, base, offv, gf, y):
    t = x1.shape[0]
    tm = 2 * TOK_TILE
    assert t % tm == 0
    return pl.pallas_call(
        _combine_kernel,
        grid_spec=pltpu.PrefetchScalarGridSpec(
            num_scalar_prefetch=3,
            grid=(t // tm,),
            in_specs=[pl.BlockSpec((tm, 1024), lambda i, *_: (i, 0)),
                      pl.BlockSpec((tm, TOP_K), lambda i, *_: (i, 0)),
                      pl.BlockSpec((tm, TOP_K), lambda i, *_: (i, 0)),
                      pl.BlockSpec((2, 1, LANES), lambda i, *_: (i, 0, 0)),
                      pl.BlockSpec((1, 1024), lambda i, *_: (0, 0)),
                      pl.BlockSpec(memory_space=pl.ANY)],
            out_specs=pl.BlockSpec((tm, 1024), lambda i, *_: (i, 0)),
            scratch_shapes=[pltpu.VMEM((4, STAGE_ROWS) + ROW_TILE, ROW_DTYPE),
                            pltpu.SemaphoreType.DMA((4,))]),
        out_shape=jax.ShapeDtypeStruct((t, 1024), F32),
        compiler_params=_cparams(("arbitrary",)),
        name="moe_combine",
    )(cnt, off, base, x1, idx, gates, offv, gf, y)


def _prep_w_in(w):
    wt = w.T
    c_if, c_qa, c_kv, c_g = 4096, 4104, 5128, 5640
    wm = wt[:c_if].astype(BF16)
    wif = jnp.pad(wt[c_if:c_qa], ((0, LANES - 2 * M_HEADS), (0, 0))).astype(BF16)
    wqa = (jnp.concatenate([wt[c_qa + (A_GROUP * g + j) * A_HD:c_qa + (A_GROUP * g + j + 1) * A_HD]
                            for j in range(A_GROUP) for g in range(A_KV_HEADS)], axis=0)
           * (A_HD ** -0.5)).astype(BF16)
    wkv = wt[c_kv:c_g].astype(BF16)
    wg = wt[c_g:].astype(BF16)
    return wm, wg, wqa, wkv, wif


def _rope_tables(pos):
    half = ROT_DIM // 2
    inv = ROPE_THETA ** (-(jnp.arange(half, dtype=F32) * 2.0 / ROT_DIM))
    ang = pos.astype(F32)[:, None] * inv[None, :]
    lane = jnp.arange(LANES) % A_HD
    hit = (lane[None, :] % half == jnp.arange(half)[:, None]) & (lane[None, :] < ROT_DIM)
    spread_cos = hit.astype(F32)
    spread_sin = jnp.where(hit, jnp.where(lane < half, -1.0, 1.0)[None, :], 0.0)
    cos_t = jnp.dot(jnp.cos(ang), spread_cos, precision=lax.Precision.HIGHEST) + (lane >= ROT_DIM).astype(F32)[None, :]
    sin_t = jnp.dot(jnp.sin(ang), spread_sin, precision=lax.Precision.HIGHEST)
    return cos_t, sin_t


def _round_up(x, m):
    return (x + m - 1) // m * m


def kernel(x_prompt, x_sample, cache_k, cache_v, state_c, state_n, state_m, norm1, w_in, b_igate, b_fgate,
           mlstm_norm, w_proj_m, w_proj_a, attn_sinks, w_out, norm2, w_router, b_router, w_gate_up, b_gate_up,
           w_down, b_down, norm_f):
    bp, sp, _ = x_prompt.shape
    bs, ts, _ = x_sample.shape
    past_len = 16384
    l = 0
    w1 = _prep_w_in(w_in[l])
    g1 = norm1[l][None, :]
    g2 = norm2[l][None, :]
    gf = norm_f[None, :]
    gate_bias = jnp.pad(jnp.concatenate([b_igate[l], b_fgate[l]]), (0, LANES - 2 * M_HEADS))[None, :]
    mn = mlstm_norm[l][None, :]
    wpm = w_proj_m[l].astype(BF16)
    wpa = jnp.concatenate([w_proj_a[l][(A_GROUP * g + j) * A_HD:(A_GROUP * g + j + 1) * A_HD]
                           for j in range(A_GROUP) for g in range(A_KV_HEADS)], axis=0).astype(BF16)
    wo = w_out[l].astype(BF16)
    wr_f = jnp.pad(w_router[l], ((0, 0), (0, LANES - N_EXPERTS)))
    wr_hi = wr_f.astype(BF16)
    wr = jnp.concatenate([wr_hi, (wr_f - wr_hi.astype(F32)).astype(BF16)], axis=1)
    br = jnp.pad(b_router[l], (0, LANES - N_EXPERTS))[None, :]
    wgu = w_gate_up[l]
    bgu = b_gate_up[l][:, None, :]
    wd = w_down[l]
    bd = b_down[l][:, None, :]
    sinks = attn_sinks[l]

    def mixer(x, nb, seq, pos0, L, c0, n0, m0, k_buf, v_buf, tm):
        t = nb * seq
        xf = x.reshape(t, D_MODEL)
        cos_t, sin_t = _rope_tables(pos0 + jnp.arange(seq))
        if seq < tm:
            cos_t, sin_t = jnp.tile(cos_t, (tm // seq, 1)), jnp.tile(sin_t, (tm // seq, 1))
        act = BF16 if L % 16 == 0 else F32
        qkv, og, qa, ka, va, gates = _inproj(xf, g1, w1, cos_t, sin_t, tm, max(seq // tm, 1), act)
        m0p = jnp.pad(m0, ((0, 0), (0, LANES - M_HEADS)))[:, None, :]
        hm, c, n, m = _mlstm(qkv, gates, og, c0, n0, m0p, gate_bias, mn, nb, L)
        if k_buf is None:
            ha = _swa_prompt(qa, ka, va, sinks, nb)
            k_keep = ka.reshape(nb, seq, A_KV)[:, seq - WINDOW:].reshape(nb, WINDOW, A_KV_HEADS, A_HD)
            v_keep = va.reshape(nb, seq, A_KV)[:, seq - WINDOW:].reshape(nb, WINDOW, A_KV_HEADS, A_HD)
        else:
            to_t = lambda a: a.transpose(0, 2, 3, 1).reshape(nb, A_KV, WINDOW)
            from_t = lambda a: a.reshape(nb, A_KV_HEADS, A_HD, WINDOW).transpose(0, 3, 1, 2)
            ha, k_keep, v_keep = _swa_sample(qa, ka, va, to_t(k_buf), to_t(v_buf), sinks, seq)
            k_keep, v_keep = from_t(k_keep), from_t(v_keep)
        x1, xn, idx, gate, cnt = _merge(xf, hm, ha, og, wpm, wpa, wo, g2, wr, br)
        return (x1, xn, idx, gate, cnt[:, :, 0]), (k_keep, v_keep, c, n, m[:, 0, :M_HEADS])

    zc = jnp.zeros((bp, M_HEADS, M_DV, M_DK), F32)
    zn = jnp.zeros((bp, M_HEADS, M_DK), F32)
    zm = jnp.zeros((bp, M_HEADS), F32)
    rp, sp_out = mixer(x_prompt, bp, sp, 0, M_CHUNK, zc, zn, zm, None, None, 512)
    rs, ss_out = mixer(x_sample, bs, ts, past_len, ts, state_c[l], state_n[l], state_m[l],
                       cache_k[l], cache_v[l], 256)

    cnt = jnp.concatenate([rp[4], rs[4]], axis=0)
    ntile_p = rp[4].shape[0]
    off = jnp.cumsum(cnt, axis=1) - cnt
    per_expert = jnp.sum(cnt, axis=0)
    padded = _round_up(per_expert, MOE_BLOCK)
    padded_end = jnp.cumsum(padded)
    expert_start = padded_end - padded
    base = expert_start[None, :] + jnp.cumsum(cnt, axis=0) - cnt
    n_tok = bp * sp + bs * ts
    n_slots = _round_up(n_tok * TOP_K, MOE_BLOCK) + N_EXPERTS * MOE_BLOCK
    nb_max = n_slots // MOE_BLOCK
    block_row = jnp.arange(nb_max, dtype=I32) * MOE_BLOCK
    block_expert = jnp.minimum(jnp.sum(block_row[:, None] >= padded_end[None, :], axis=1), N_EXPERTS - 1).astype(I32)
    n_blocks = (padded_end[-1] // MOE_BLOCK).astype(I32)[None]
    eid = jnp.arange(N_EXPERTS, dtype=I32)
    later_used = (eid[None, :] > eid[:, None]) & (padded[None, :] > 0)
    next_used = jnp.min(jnp.where(later_used, eid[None, :], N_EXPERTS), axis=1)
    next_used = jnp.where(next_used < N_EXPERTS, next_used, -1).astype(I32)
    run_parity = ((jnp.cumsum(padded > 0) - 1) & 1).astype(I32)
    of_block = lambda table: jnp.sum(jnp.where(block_expert[:, None] == eid[None, :], table[None, :], 0),
                                     axis=1).astype(I32)
    block_rows = jnp.clip(of_block(per_expert) - (block_row - of_block(expert_start)), 0, MOE_BLOCK).astype(I32)
    tails = jnp.concatenate([expert_start + per_expert, padded - per_expert,
                             n_blocks, nb_max - n_blocks]).astype(I32)
    flat = lambda a: a.reshape(-1).astype(I32)
    offv = jnp.pad(off, ((0, 0), (0, LANES - N_EXPERTS))).astype(F32)[:, None, :]

    xs = _dispatch(rp[1], rs[1], rp[2], rs[2], flat(cnt), flat(off), flat(base), tails, offv, n_slots)
    y = _experts(block_expert, block_rows, of_block(next_used), of_block(run_parity), xs, wgu, bgu, wd, bd)
    y_p = _combine(rp[0], rp[2], rp[3], flat(cnt[:ntile_p]), flat(off[:ntile_p]), flat(base[:ntile_p]),
                   offv[:ntile_p], gf, y)
    y_s = _combine(rs[0], rs[2], rs[3], flat(cnt[ntile_p:]), flat(off[ntile_p:]), flat(base[ntile_p:]),
                   offv[ntile_p:], gf, y)

    kp, vp, cp_, np_, mp = sp_out
    ks, vs, cs, ns, ms = ss_out
    return (y_p.reshape(bp, sp, D_MODEL), y_s.reshape(bs, ts, D_MODEL),
            kp[None], vp[None], cp_[None], np_[None], mp[None],
            ks[None], vs[None], cs[None], ns[None], ms[None])
```

```python
import functools

import jax
import jax.numpy as jnp
from jax import lax
from jax.experimental import pallas as pl
from jax.experimental.pallas import tpu as pltpu

F32 = jnp.float32
BF16 = jnp.bfloat16
I32 = jnp.int32

D_MODEL = 1024
M_HEADS = 4
M_DK = 256
M_DV = 256
M_CHUNK = 128
A_Q_HEADS = 16
A_KV_HEADS = 4
A_GROUP = 4
A_HD = 64
A_KV = A_KV_HEADS * A_HD
WINDOW = 128
A_BLOCK = 128
ROT_DIM = 16
ROPE_THETA = 500000.0
N_EXPERTS = 32
TOP_K = 4
D_FF = 1024
SWIGLU_LIMIT = 7.0
SWIGLU_ALPHA = 1.702
NORM_EPS = 1e-6

LANES = 128
SUBLANES = 8
VMEM_LIMIT = 56 * 1024 * 1024

MERGE_TILE = 512
TOK_TILE = 256
STAGE_ROWS = 4 * TOK_TILE
ROW_TILE = (8, 128)
ROW_DTYPE = F32
MOE_BLOCK = 512
DISPATCH_TILES = 4


def _cparams(sem):
    return pltpu.CompilerParams(dimension_semantics=sem, vmem_limit_bytes=VMEM_LIMIT)


def _rms(x, g):
    return x * lax.rsqrt(jnp.mean(x * x, axis=-1, keepdims=True) + NORM_EPS) * g


def _rope(x, cos_t, sin_t, first_half):
    n = x.shape[1]
    fwd = pltpu.roll(x, n - ROT_DIM // 2, axis=1)
    bwd = pltpu.roll(x, ROT_DIM // 2, axis=1)
    outs = []
    for b in range(n // LANES):
        sl = slice(b * LANES, (b + 1) * LANES)
        partner = jnp.where(first_half, fwd[:, sl], bwd[:, sl])
        outs.append(x[:, sl] * cos_t + partner * sin_t)
    return jnp.concatenate(outs, axis=1)


def _inproj_kernel(x_ref, g1_ref, wm_ref, wg_ref, wqa_ref, wkv_ref, wif_ref, cos_ref, sin_ref,
                   qkv_ref, og_ref, qa_ref, ka_ref, va_ref, gate_ref):
    x = x_ref[...]
    xn = _rms(x, g1_ref[...]).astype(BF16)
    act = qkv_ref.dtype

    def proj(w_ref, c0, n):
        return lax.dot_general(xn, w_ref[c0:c0 + n, :], (((1,), (1,)), ((), ())), preferred_element_type=F32)

    qkv_ref[:, 0:1024] = proj(wm_ref, 0, 1024).astype(act)
    qkv_ref[:, 1024:2048] = (proj(wm_ref, 1024, 1024) * (M_DK ** -0.5)).astype(act)
    qkv_ref[:, 2048:3072] = proj(wm_ref, 2048, 1024).astype(act)
    og_ref[:, 0:1024] = jax.nn.sigmoid(proj(wm_ref, 3072, 1024)).astype(act)
    og_ref[:, 1024:2048] = jax.nn.sigmoid(proj(wg_ref, 0, 1024)).astype(act)
    og_ref[:, 2048:3072] = jax.nn.sigmoid(proj(wg_ref, 1024, 1024)).astype(act)
    cos_t = cos_ref[...]
    sin_t = sin_ref[...]
    lane = lax.broadcasted_iota(I32, (1, LANES), 1)
    first_half = (lane & (ROT_DIM - 1)) < (ROT_DIM // 2)
    qa_ref[...] = _rope(proj(wqa_ref, 0, 1024), cos_t, sin_t, first_half).astype(act)
    ka_ref[...] = _rope(proj(wkv_ref, 0, A_KV), cos_t, sin_t, first_half)
    va_ref[...] = proj(wkv_ref, A_KV, A_KV)
    gate_ref[...] = proj(wif_ref, 0, LANES)


def _inproj(x, g1, weights, cos_t, sin_t, tm, rope_blocks, act):
    t = x.shape[0]
    tok = lambda n, dt: jax.ShapeDtypeStruct((t, n), dt)
    row = lambda n: pl.BlockSpec((tm, n), lambda i: (i, 0))
    resident = lambda w: pl.BlockSpec(w.shape, lambda i: (0, 0), pipeline_mode=pl.Buffered(1))
    return pl.pallas_call(
        _inproj_kernel,
        grid=(t // tm,),
        in_specs=[row(D_MODEL), pl.BlockSpec((1, D_MODEL), lambda i: (0, 0))]
                 + [resident(w) for w in weights]
                 + [pl.BlockSpec((tm, LANES), lambda i: (i % rope_blocks, 0)),
                    pl.BlockSpec((tm, LANES), lambda i: (i % rope_blocks, 0))],
        out_specs=[row(3072), row(3072), row(1024), row(A_KV), row(A_KV), row(LANES)],
        out_shape=[tok(3072, act), tok(3072, act), tok(1024, act), tok(A_KV, F32), tok(A_KV, F32),
                   tok(LANES, F32)],
        compiler_params=_cparams(("arbitrary",)),
        name="inproj",
    )(x, g1, *weights, cos_t, sin_t)


def _mlstm_gates(gc, bias, m_prev, L):
    z = gc + bias
    lf = jnp.minimum(z, 0.0) - jnp.log1p(jnp.exp(-jnp.abs(z)))
    row = lax.broadcasted_iota(I32, (L, LANES), 0)
    fc = lf
    sh = 1
    while sh < L:
        fc = fc + jnp.where(row >= sh, pltpu.roll(fc, sh, axis=0), 0.0)
        sh *= 2
    fcum = pltpu.roll(fc, LANES - M_HEADS, axis=1)
    a = z - fcum
    cmx = a
    sh = 1
    while sh < L:
        cmx = jnp.maximum(cmx, jnp.where(row >= sh, pltpu.roll(cmx, sh, axis=0), -jnp.inf))
        sh *= 2
    mx = jnp.maximum(m_prev, cmx)
    inter = jnp.exp(m_prev - mx)
    f_end = fcum[L - 1:L]
    m_end = f_end + mx[L - 1:L]
    decay = jnp.exp(f_end + m_prev - m_end)
    wsrc = jnp.exp(a + (f_end - m_end))
    return a, mx, inter, wsrc, decay, m_end


NT_DIMS = (((1,), (1,)), ((), ()))
TN_DIMS = (((0,), (0,)), ((), ()))


def _mlstm_decay(h, L, gates):
    a, mx = gates[0], gates[1]
    t_i = lax.broadcasted_iota(I32, (L, L), 0)
    s_i = lax.broadcasted_iota(I32, (L, L), 1)
    a_row = jnp.sum(jnp.where(t_i == s_i, a[:, h:h + 1], 0.0), axis=0, keepdims=True)
    return jnp.exp(jnp.where(s_i <= t_i, a_row - mx[:, h:h + 1], -jnp.inf))


def _mlstm_output(h, q, og, mn, gates, qk_sum, pv, cq, n_h):
    inter_col = gates[2][:, h:h + 1]
    num = inter_col * cq + pv
    den = inter_col * jnp.sum(q.astype(F32) * n_h, axis=1, keepdims=True) + qk_sum
    hh = num / jnp.maximum(jnp.abs(den), 1.0)
    hn = hh * lax.rsqrt(jnp.mean(hh * hh, axis=1, keepdims=True) + NORM_EPS)
    return hn * mn * og


def _mlstm_kernel(L, G, CH, carry, qkv_ref, gate_ref, og_ref, c0_ref, n0_ref, m0_ref, bias_ref, mn_ref,
                  h_ref, c_ref, n_ref, m_ref, *scratch):
    if carry:
        c_in, n_in, m_in = scratch
        ci = pl.program_id(1)

        @pl.when(ci == 0)
        def _():
            c_in[...] = c0_ref[...]
            n_in[...] = n0_ref[...]
            m_in[...] = m0_ref[...]
        c_out, n_out, m_out = scratch
    else:
        c_in, n_in, m_in = c0_ref, n0_ref, m0_ref
        c_out, n_out, m_out = c_ref, n_ref, m_ref

    def qkv(j, cc, h, part):
        cols = slice(part * 1024 + h * M_DK, part * 1024 + (h + 1) * M_DK)
        return qkv_ref[j, cc * L:(cc + 1) * L, cols].astype(BF16)

    units = [(j, cc, h) for cc in range(CH) for j in range(G) for h in range(M_HEADS)]
    gates = {}
    for j in range(G):
        m_cur = m_in[j]
        for cc in range(CH):
            gates[j, cc] = _mlstm_gates(gate_ref[j, cc * L:(cc + 1) * L], bias_ref[...], m_cur, L)
            m_cur = gates[j, cc][5]
        m_out[j] = m_cur
    def carried(cc, c_cur):
        return {(j, h): lax.dot_general(qkv(j, cc, h, 0), c_cur[j, h].astype(BF16), NT_DIMS,
                                        preferred_element_type=F32) for j, c, h in units if c == cc}

    c_cur = {(j, h): c_in[j, h] for j in range(G) for h in range(M_HEADS)}
    n_cur = {(j, h): n_in[j, h:h + 1, :] for j in range(G) for h in range(M_HEADS)}
    scores = {u: lax.dot_general(qkv(*u, 0), qkv(*u, 1), NT_DIMS, preferred_element_type=F32) for u in units}
    cqs_first = carried(0, c_cur)
    qks = {(j, cc, h): scores[j, cc, h] * _mlstm_decay(h, L, gates[j, cc]) for j, cc, h in units}
    pvs = {u: jnp.dot(qks[u].astype(BF16), qkv(*u, 2), preferred_element_type=F32) for u in units}
    kws = {(j, cc, h): gates[j, cc][3][:, h:h + 1] * qkv(j, cc, h, 1).astype(F32) for j, cc, h in units}
    for cc in range(CH):
        now = [u for u in units if u[1] == cc]
        cqs = cqs_first if cc == 0 else carried(cc, c_cur)
        n_old = dict(n_cur)

        def update_states():
            for j, _, h in now:
                dec = gates[j, cc][4][:, h:h + 1]
                c_cur[j, h] = dec * c_cur[j, h] + lax.dot_general(qkv(j, cc, h, 2), kws[j, cc, h].astype(BF16),
                                                                  TN_DIMS, preferred_element_type=F32)
                n_cur[j, h] = dec * n_cur[j, h] + jnp.sum(kws[j, cc, h], axis=0, keepdims=True)

        def write_outputs():
            for j, _, h in now:
                sl = slice(h * M_DV, (h + 1) * M_DV)
                rows = slice(cc * L, (cc + 1) * L)
                out = _mlstm_output(h, qkv(j, cc, h, 0), og_ref[j, rows, sl].astype(F32), mn_ref[:, sl],
                                    gates[j, cc], jnp.sum(qks[j, cc, h], axis=1, keepdims=True), pvs[j, cc, h],
                                    cqs[j, h], n_old[j, h])
                h_ref[j, rows, sl] = out.astype(h_ref.dtype)

        for phase in ((update_states, write_outputs) if carry else (write_outputs, update_states)):
            phase()
    for j in range(G):
        for h in range(M_HEADS):
            c_out[j, h] = c_cur[j, h]
            n_out[j, h:h + 1, :] = n_cur[j, h]

    if carry:
        @pl.when(ci == pl.num_programs(1) - 1)
        def _():
            c_ref[...] = c_in[...]
            n_ref[...] = n_in[...]
            m_ref[...] = m_in[...]


MLSTM_GROUP_CARRY = 2
MLSTM_CHUNKS_CARRY = 1
MLSTM_GROUP_SINGLE = 8


def _mlstm(qkv, gates, og, c0, n0, m0, gate_bias, mnorm, nb, L):
    t = qkv.shape[0]
    nc = t // (nb * L)
    carry = nc > 1
    G = MLSTM_GROUP_CARRY if carry else MLSTM_GROUP_SINGLE
    CH = MLSTM_CHUNKS_CARRY if carry else 1
    qkv, gates, og = (a.reshape(nb, nc * L, a.shape[1]) for a in (qkv, gates, og))
    tokmap = lambda b, c: (b, c, 0)
    const2 = lambda b, c: (0, 0)
    state_specs = [pl.BlockSpec((G, M_HEADS, M_DV, M_DK), lambda b, c: (b, 0, 0, 0)),
                   pl.BlockSpec((G, M_HEADS, M_DK), lambda b, c: (b, 0, 0)),
                   pl.BlockSpec((G, 1, LANES), lambda b, c: (b, 0, 0))]
    scratch = [pltpu.VMEM((G, M_HEADS, M_DV, M_DK), F32),
               pltpu.VMEM((G, M_HEADS, M_DK), F32),
               pltpu.VMEM((G, 1, LANES), F32)] if carry else []
    h, c, n, m = pl.pallas_call(
        functools.partial(_mlstm_kernel, L, G, CH, carry),
        grid=(nb // G, nc // CH),
        in_specs=[pl.BlockSpec((G, CH * L, 3072), tokmap),
                  pl.BlockSpec((G, CH * L, LANES), tokmap),
                  pl.BlockSpec((G, CH * L, 1024), tokmap)]
                 + state_specs
                 + [pl.BlockSpec((1, LANES), const2),
                    pl.BlockSpec((1, 1024), const2)],
        out_specs=[pl.BlockSpec((G, CH * L, 1024), tokmap)] + state_specs,
        out_shape=[jax.ShapeDtypeStruct((nb, nc * L, 1024), qkv.dtype),
                   jax.ShapeDtypeStruct((nb, M_HEADS, M_DV, M_DK), F32),
                   jax.ShapeDtypeStruct((nb, M_HEADS, M_DK), F32),
                   jax.ShapeDtypeStruct((nb, 1, LANES), F32)],
        scratch_shapes=scratch,
        compiler_params=_cparams(("arbitrary", "arbitrary")),
        name="mlstm_L%d" % L,
    )(qkv, gates, og, c0, n0, m0, gate_bias, mnorm)
    return h.reshape(t, 1024), c, n, m


def _swa_bias(m, has_prev):
    nkeys = 2 * A_BLOCK
    t_i = lax.broadcasted_iota(I32, (A_GROUP * m, nkeys), 0) & (m - 1)
    s_i = lax.broadcasted_iota(I32, (A_GROUP * m, nkeys), 1)
    diff = t_i + A_BLOCK - s_i
    mask = (diff >= 0) & (diff <= WINDOW)
    if not has_prev:
        mask = mask & (s_i >= A_BLOCK)
    return jnp.where(mask, 0.0, -jnp.inf).astype(F32)


def _swa_group_ones():
    nkeys = 2 * A_BLOCK
    r = lax.broadcasted_iota(I32, (A_KV_HEADS * nkeys, A_KV), 0) >> (nkeys.bit_length() - 1)
    c = lax.broadcasted_iota(I32, (A_KV_HEADS * nkeys, A_KV), 1) >> (A_HD.bit_length() - 1)
    return (r == c).astype(BF16)


def _swa_core(q, k_prev, v_prev, k_cur, v_cur, bias, group_ones, sink_ref, keys_on_lanes=False):
    m = q.shape[0]
    q = q.astype(BF16)
    qst = jnp.concatenate([q[:, j * A_KV:(j + 1) * A_KV] for j in range(A_GROUP)], axis=0)
    key_axis = 1 if keys_on_lanes else 0
    k_all = jnp.concatenate([k_prev, k_cur], axis=key_axis)
    v_all = jnp.concatenate([v_prev, v_cur], axis=key_axis)
    feat = lax.broadcasted_iota(I32, (A_KV, 1) if keys_on_lanes else (1, A_KV), 1 - key_axis)
    kbd, vbd = [], []
    for g in range(A_KV_HEADS):
        in_g = (feat >= g * A_HD) & (feat < (g + 1) * A_HD)
        kbd.append(jnp.where(in_g, k_all, 0.0).astype(BF16))
        vbd.append(jnp.where(in_g, v_all, 0.0).astype(BF16))
    kbd = jnp.concatenate(kbd, axis=key_axis)
    vbd = jnp.concatenate(vbd, axis=key_axis)
    if keys_on_lanes:
        s = jnp.dot(qst, kbd, preferred_element_type=F32)
    else:
        s = lax.dot_general(qst, kbd, NT_DIMS, preferred_element_type=F32)
    nkeys = 2 * A_BLOCK
    ps, sink_terms = [], []
    for g in range(A_KV_HEADS):
        sg = s[:, g * nkeys:(g + 1) * nkeys] + bias
        sink = jnp.concatenate(
            [jnp.full((m, LANES), sink_ref[A_GROUP * g + j], F32) for j in range(A_GROUP)], axis=0)
        mx = jnp.maximum(jnp.broadcast_to(jnp.max(sg, axis=1, keepdims=True), (A_GROUP * m, LANES)), sink)
        ps += [jnp.exp(sg[:, :LANES] - mx).astype(BF16), jnp.exp(sg[:, LANES:] - mx).astype(BF16)]
        sink_terms.append(jnp.exp(sink - mx))
    p = jnp.concatenate(ps, axis=1)
    if keys_on_lanes:
        o = lax.dot_general(p, vbd, NT_DIMS, preferred_element_type=F32)
    else:
        o = jnp.dot(p, vbd, preferred_element_type=F32)
    den = jnp.dot(p, group_ones, preferred_element_type=F32)
    low = lax.broadcasted_iota(I32, (1, LANES), 1) < A_HD
    den = den + jnp.concatenate([jnp.where(low, sink_terms[0], sink_terms[1]),
                                 jnp.where(low, sink_terms[2], sink_terms[3])], axis=1)
    o = o / den
    return jnp.concatenate([o[j * m:(j + 1) * m] for j in range(A_GROUP)], axis=1)


SWA_STEP_BLOCKS = 8
SWA_STEP = SWA_STEP_BLOCKS * A_BLOCK


def _swa_prompt_kernel(sink_ref, q_ref, kp_ref, kc_ref, vp_ref, vc_ref, o_ref, bias_s, bias0_s, ones_s):
    first = (pl.program_id(0) == 0) & (pl.program_id(1) == 0)

    @pl.when(first)
    def _():
        bias_s[...] = _swa_bias(A_BLOCK, True)
        bias0_s[...] = _swa_bias(A_BLOCK, False)
        ones_s[...] = _swa_group_ones()

    bias = bias_s[...]
    bias_first = jnp.where(pl.program_id(1) > 0, bias, bias0_s[...])
    ones = ones_s[...]
    k_prev, v_prev = kp_ref[...], vp_ref[...]
    for b in range(SWA_STEP_BLOCKS):
        rows = slice(b * A_BLOCK, (b + 1) * A_BLOCK)
        k_cur, v_cur = kc_ref[rows], vc_ref[rows]
        o_ref[rows] = _swa_core(q_ref[rows], k_prev, v_prev, k_cur, v_cur, bias_first if b == 0 else bias,
                                ones, sink_ref).astype(o_ref.dtype)
        k_prev, v_prev = k_cur, v_cur


def _swa_prompt(qa, ka, va, sinks, nb):
    t = qa.shape[0]
    nstep = t // (nb * SWA_STEP)
    cur = lambda b, i, s: (b * nstep + i, 0)
    prev = lambda b, i, s: (SWA_STEP_BLOCKS * (b * nstep + i) - jnp.minimum(i, 1), 0)
    return pl.pallas_call(
        _swa_prompt_kernel,
        grid_spec=pltpu.PrefetchScalarGridSpec(
            num_scalar_prefetch=1,
            grid=(nb, nstep),
            in_specs=[pl.BlockSpec((SWA_STEP, 1024), cur),
                      pl.BlockSpec((A_BLOCK, A_KV), prev),
                      pl.BlockSpec((SWA_STEP, A_KV), cur),
                      pl.BlockSpec((A_BLOCK, A_KV), prev),
                      pl.BlockSpec((SWA_STEP, A_KV), cur)],
            out_specs=pl.BlockSpec((SWA_STEP, 1024), cur),
            scratch_shapes=[pltpu.VMEM((A_GROUP * A_BLOCK, 2 * A_BLOCK), F32),
                            pltpu.VMEM((A_GROUP * A_BLOCK, 2 * A_BLOCK), F32),
                            pltpu.VMEM((A_KV_HEADS * 2 * A_BLOCK, A_KV), BF16)]),
        out_shape=jax.ShapeDtypeStruct((t, 1024), BF16),
        compiler_params=_cparams(("arbitrary", "arbitrary")),
        name="swa_prompt",
    )(sinks, qa, ka, ka, va, va)


SWA_SAMPLE_GROUP = 16


def _swa_sample_kernel(T, sink_ref, q_ref, kn_ref, vn_ref, kb_ref, vb_ref, o_ref, ko_ref, vo_ref):
    pad = jnp.zeros((A_BLOCK - T, A_KV), F32)
    bias = _swa_bias(T, True)
    ones = _swa_group_ones()
    lane = lax.broadcasted_iota(I32, (1, WINDOW), 1)
    for j in range(SWA_SAMPLE_GROUP):
        rows = slice(j * T, (j + 1) * T)
        k_new_t = jnp.concatenate([kn_ref[rows], pad], axis=0).T
        v_new_t = jnp.concatenate([vn_ref[rows], pad], axis=0).T
        k_buf_t = kb_ref[j]
        v_buf_t = vb_ref[j]
        o_ref[rows] = _swa_core(q_ref[rows], k_buf_t, v_buf_t, k_new_t, v_new_t, bias, ones, sink_ref,
                                keys_on_lanes=True).astype(o_ref.dtype)
        keep = lane < WINDOW - T
        ko_ref[j] = jnp.where(keep, pltpu.roll(k_buf_t, WINDOW - T, axis=1), pltpu.roll(k_new_t, WINDOW - T, axis=1))
        vo_ref[j] = jnp.where(keep, pltpu.roll(v_buf_t, WINDOW - T, axis=1), pltpu.roll(v_new_t, WINDOW - T, axis=1))


def _swa_sample(qa, ka, va, k_buf, v_buf, sinks, T):
    nb = k_buf.shape[0]
    g = SWA_SAMPLE_GROUP
    tok = lambda b, s: (b, 0)
    buf = lambda b, s: (b, 0, 0)
    return pl.pallas_call(
        functools.partial(_swa_sample_kernel, T),
        grid_spec=pltpu.PrefetchScalarGridSpec(
            num_scalar_prefetch=1,
            grid=(nb // g,),
            in_specs=[pl.BlockSpec((g * T, 1024), tok),
                      pl.BlockSpec((g * T, A_KV), tok),
                      pl.BlockSpec((g * T, A_KV), tok),
                      pl.BlockSpec((g, A_KV, WINDOW), buf),
                      pl.BlockSpec((g, A_KV, WINDOW), buf)],
            out_specs=[pl.BlockSpec((g * T, 1024), tok),
                       pl.BlockSpec((g, A_KV, WINDOW), buf),
                       pl.BlockSpec((g, A_KV, WINDOW), buf)]),
        out_shape=[jax.ShapeDtypeStruct((nb * T, 1024), qa.dtype),
                   jax.ShapeDtypeStruct((nb, A_KV, WINDOW), F32),
                   jax.ShapeDtypeStruct((nb, A_KV, WINDOW), F32)],
        compiler_params=_cparams(("arbitrary",)),
        name="swa_sample",
    )(sinks, qa, ka, va, k_buf, v_buf)


def _merge_kernel(x_ref, hm_ref, ha_ref, gm_ref, ga_ref, wpm_ref, wpa_ref, wo_ref, g2_ref, wr_ref, br_ref,
                  x1_ref, xn_ref, idx_ref, gate_ref, cnt_ref):
    pm = jnp.dot(hm_ref[...].astype(BF16), wpm_ref[...], preferred_element_type=F32)
    pa = jnp.dot(ha_ref[...].astype(BF16), wpa_ref[...], preferred_element_type=F32)
    mixed = gm_ref[...].astype(F32) * pm + ga_ref[...].astype(F32) * pa
    x1 = x_ref[...] + jnp.dot(mixed.astype(BF16), wo_ref[...], preferred_element_type=F32)
    x1_ref[...] = x1
    xn = _rms(x1, g2_ref[...])
    xn_ref[...] = xn.astype(BF16)
    xn_hi = xn.astype(BF16)
    xn_lo = (xn - xn_hi.astype(F32)).astype(BF16)
    part = jnp.dot(xn_hi, wr_ref[...], preferred_element_type=F32)
    logits = (part[:, :LANES] + part[:, LANES:]
              + jnp.dot(xn_lo, wr_ref[:, :LANES], preferred_element_type=F32) + br_ref[...])
    tm = logits.shape[0]
    cur = logits.T[0:N_EXPERTS]
    eid = lax.broadcasted_iota(I32, (N_EXPERTS, tm), 0).astype(F32)
    vals, idxs = [], []
    onehot = jnp.zeros((N_EXPERTS, tm), F32)
    for _ in range(TOP_K):
        mval = jnp.max(cur, axis=0, keepdims=True)
        sel = jnp.min(jnp.where(cur == mval, eid, float(N_EXPERTS)), axis=0, keepdims=True)
        hit = eid == sel
        onehot = onehot + hit.astype(F32)
        cur = jnp.where(hit, -jnp.inf, cur)
        vals.append(mval)
        idxs.append(sel)
    es = [jnp.exp(v - vals[0]) for v in vals]
    tot = es[0] + es[1] + es[2] + es[3]
    row = lax.broadcasted_iota(I32, (SUBLANES, tm), 0)
    res = jnp.zeros((SUBLANES, tm), F32)
    for k in range(TOP_K):
        res = jnp.where(row == k, idxs[k], res)
        res = jnp.where(row == TOP_K + k, es[k] / tot, res)
    res_t = jnp.concatenate([res, jnp.zeros((LANES - SUBLANES, tm), F32)], axis=0).T
    idx_ref[...] = res_t[:, 0:TOP_K].astype(I32)
    gate_ref[...] = res_t[:, TOP_K:2 * TOP_K]
    for j in range(tm // TOK_TILE):
        cnt_ref[j] = jnp.sum(onehot[:, j * TOK_TILE:(j + 1) * TOK_TILE], axis=1, keepdims=True).astype(I32)


def _merge(x, hm, ha, og, wpm, wpa, wo, g2, wr, br):
    t = x.shape[0]
    tm = MERGE_TILE
    row = lambda n: pl.BlockSpec((tm, n), lambda i: (i, 0))
    const = lambda r, c: pl.BlockSpec((r, c), lambda i: (0, 0))
    return pl.pallas_call(
        _merge_kernel,
        grid=(t // tm,),
        in_specs=[row(1024), row(1024), row(1024),
                  pl.BlockSpec((tm, 1024), lambda i: (i, 1)),
                  pl.BlockSpec((tm, 1024), lambda i: (i, 2)),
                  const(1024, 1024), const(1024, 1024), const(1024, 1024), const(1, 1024),
                  const(1024, 2 * LANES), const(1, LANES)],
        out_specs=[row(1024), row(1024), row(TOP_K), row(TOP_K),
                   pl.BlockSpec((tm // TOK_TILE, N_EXPERTS, 1), lambda i: (i, 0, 0))],
        out_shape=[jax.ShapeDtypeStruct((t, 1024), F32),
                   jax.ShapeDtypeStruct((t, 1024), BF16),
                   jax.ShapeDtypeStruct((t, TOP_K), I32),
                   jax.ShapeDtypeStruct((t, TOP_K), F32),
                   jax.ShapeDtypeStruct((t // TOK_TILE, N_EXPERTS, 1), I32)],
        compiler_params=_cparams(("arbitrary",)),
        name="merge_route",
    )(x, hm, ha, og, og, wpm, wpa, wo, g2, wr, br)


def _rows_to_tiles(ref, lead, x):
    chunks = jnp.stack([x[:, s * LANES:(s + 1) * LANES] for s in range(ROW_TILE[0])], axis=0)
    ref[lead] = jnp.swapaxes(chunks, 0, 1).astype(ref.dtype)


def _tiles_to_rows(ref, lead, rows=slice(None)):
    chunks = jnp.swapaxes(ref[lead + (rows,)].astype(F32), 0, 1)
    return jnp.concatenate([chunks[s] for s in range(ROW_TILE[0])], axis=1)


def _stage_rows(idx, off_row):
    tm = idx.shape[0]
    lane = lax.broadcasted_iota(I32, (tm, LANES), 1)
    hits = [lane == idx[:, k:k + 1] for k in range(TOP_K)]
    onehot = sum(h.astype(F32) for h in hits)
    r_i = lax.broadcasted_iota(I32, (tm, tm), 0)
    c_i = lax.broadcasted_iota(I32, (tm, tm), 1)
    before = (c_i < r_i).astype(BF16)
    rank = jnp.dot(before, onehot.astype(BF16), preferred_element_type=F32)
    pos = rank + off_row
    return [jnp.sum(jnp.where(h, pos, 0.0), axis=1, keepdims=True) for h in hits]


def _dispatch_kernel(ntile_a, cnt_ref, off_ref, base_ref, tail_ref, xa_ref, xb_ref, ia_ref, ib_ref, offv_ref,
                     xs_ref, stage, zeros, sem, zsem):
    i = pl.program_id(0)
    from_a = DISPATCH_TILES * i < ntile_a
    tm = TOK_TILE

    def stage_tile(slot):
        rows_in = slice(slot * tm, (slot + 1) * tm)
        xn = jnp.where(from_a, xa_ref[rows_in], xb_ref[rows_in])
        idx = jnp.where(from_a, ia_ref[rows_in], ib_ref[rows_in])
        rows = _stage_rows(idx, offv_ref[slot])
        lane = lax.broadcasted_iota(I32, (tm, LANES), 1)
        r4 = jnp.zeros((tm, LANES), F32)
        for k in range(TOP_K):
            r4 = jnp.where(lane == k, rows[k], r4)
        r4t = r4.T.astype(I32)
        r_iota = lax.broadcasted_iota(I32, (STAGE_ROWS, tm), 0)
        sel = r_iota == r4t[0:1, :]
        for k in range(1, TOP_K):
            sel = sel | (r_iota == r4t[k:k + 1, :])
        sel = jnp.where(sel, 1.0, 0.0).astype(BF16)
        return jnp.dot(sel, xn, preferred_element_type=F32)

    def runs(tile, slot):
        out = []
        for e in range(N_EXPERTS):
            n = cnt_ref[tile * N_EXPERTS + e]
            off = off_ref[tile * N_EXPERTS + e]
            base = base_ref[tile * N_EXPERTS + e]
            out.append((n, pltpu.make_async_copy(stage.at[slot, pl.ds(off, n)], xs_ref.at[pl.ds(base, n)],
                                                 sem.at[slot])))
        return out

    def wait_runs(slot):
        pltpu.make_async_copy(stage.at[slot], xs_ref.at[pl.ds(0, STAGE_ROWS)], sem.at[slot]).wait()

    staged_tiles = [stage_tile(slot) for slot in range(DISPATCH_TILES)]

    for slot in range(DISPATCH_TILES):
        @pl.when(i >= 1)
        def _():
            wait_runs(slot)
        _rows_to_tiles(stage, (slot,), staged_tiles[slot])
        for n, cp in runs(DISPATCH_TILES * i + slot, slot):
            @pl.when(n > 0)
            def _():
                cp.start()

    @pl.when(i == pl.num_programs(0) - 1)
    def _():
        for slot in range(DISPATCH_TILES):
            wait_runs(slot)
        zeros[...] = jnp.zeros(zeros.shape, zeros.dtype)
        tails = []
        for e in range(N_EXPERTS):
            start = tail_ref[e]
            n = tail_ref[N_EXPERTS + e]
            tails.append((n, pltpu.make_async_copy(zeros.at[pl.ds(0, n)], xs_ref.at[pl.ds(start, n)], zsem)))
        for n, cp in tails:
            @pl.when(n > 0)
            def _():
                cp.start()
        for n, cp in tails:
            @pl.when(n > 0)
            def _():
                cp.wait()
        first = tail_ref[2 * N_EXPERTS]
        n_unused = tail_ref[2 * N_EXPERTS + 1]

        def unused_block(j):
            return pltpu.make_async_copy(zeros, xs_ref.at[pl.ds((first + j) * MOE_BLOCK, MOE_BLOCK)], zsem)

        @pl.loop(0, n_unused)
        def _(j):
            unused_block(j).start()

        @pl.loop(0, n_unused)
        def _(j):
            unused_block(j).wait()


def _dispatch(xa, xb, ia, ib, cnt, off, base, tails, offv, n_slots):
    tm = DISPATCH_TILES * TOK_TILE
    assert xa.shape[0] % tm == 0 and xb.shape[0] % tm == 0
    nstep_a = xa.shape[0] // tm
    nstep_b = xb.shape[0] // tm
    amap = lambda i, *_: (jnp.minimum(i, nstep_a - 1), 0)
    bmap = lambda i, *_: (jnp.maximum(i - nstep_a, 0), 0)
    return pl.pallas_call(
        functools.partial(_dispatch_kernel, DISPATCH_TILES * nstep_a),
        grid_spec=pltpu.PrefetchScalarGridSpec(
            num_scalar_prefetch=4,
            grid=(nstep_a + nstep_b,),
            in_specs=[pl.BlockSpec((tm, 1024), amap),
                      pl.BlockSpec((tm, 1024), bmap),
                      pl.BlockSpec((tm, TOP_K), amap),
                      pl.BlockSpec((tm, TOP_K), bmap),
                      pl.BlockSpec((DISPATCH_TILES, 1, LANES), lambda i, *_: (i, 0, 0))],
            out_specs=pl.BlockSpec(memory_space=pl.ANY),
            scratch_shapes=[pltpu.VMEM((DISPATCH_TILES, STAGE_ROWS) + ROW_TILE, ROW_DTYPE),
                            pltpu.VMEM((MOE_BLOCK,) + ROW_TILE, ROW_DTYPE),
                            pltpu.SemaphoreType.DMA((DISPATCH_TILES,)),
                            pltpu.SemaphoreType.DMA]),
        out_shape=jax.ShapeDtypeStruct((n_slots,) + ROW_TILE, ROW_DTYPE),
        compiler_params=_cparams(("arbitrary",)),
        name="moe_dispatch",
    )(cnt, off, base, tails, xa, xb, ia, ib, offv)


MOE_HALF = MOE_BLOCK // 2


def _experts_kernel(be_ref, rows_ref, next_ref, slot_ref, xs_ref, wgu_hbm, bgu_ref, wd_hbm, bd_ref, y_ref,
                    wgu_f, wd_f, wgu_s, wd_s, sem):
    i = pl.program_id(0)
    rows = rows_ref[i]
    live = rows > 0
    run_start = (i == 0) | (be_ref[i] != be_ref[jnp.maximum(i - 1, 0)])

    def fetch(expert, slot):
        return (pltpu.make_async_copy(wgu_hbm.at[expert], wgu_f.at[slot], sem.at[0, slot]),
                pltpu.make_async_copy(wd_hbm.at[expert], wd_f.at[slot], sem.at[1, slot]))

    for slot in range(2):
        @pl.when(live & run_start & (slot_ref[i] == slot))
        def _():
            @pl.when(i == 0)
            def _():
                for cp in fetch(be_ref[i], slot):
                    cp.start()
            for cp in fetch(be_ref[i], slot):
                cp.wait()

            @pl.when(next_ref[i] >= 0)
            def _():
                for cp in fetch(next_ref[i], 1 - slot):
                    cp.start()
            wgu_s[...] = wgu_f[slot].astype(BF16)
            wd_s[...] = wd_f[slot].astype(BF16)

    def ffn(x):
        h = jnp.dot(x.astype(BF16), wgu_s[...], preferred_element_type=F32) + bgu_ref[0]
        gate = jnp.minimum(h[:, :D_FF], SWIGLU_LIMIT)
        up = jnp.clip(h[:, D_FF:], -SWIGLU_LIMIT, SWIGLU_LIMIT)
        act = gate * jax.nn.sigmoid(SWIGLU_ALPHA * gate) * (up + 1.0)
        return jnp.dot(act.astype(BF16), wd_s[...], preferred_element_type=F32) + bd_ref[0]

    @pl.when(rows > MOE_HALF)
    def _():
        _rows_to_tiles(y_ref, (), ffn(_tiles_to_rows(xs_ref, ())))

    @pl.when(live & (rows <= MOE_HALF))
    def _():
        y_half = ffn(_tiles_to_rows(xs_ref, (), slice(0, MOE_HALF)))
        _rows_to_tiles(y_ref, (), jnp.concatenate([y_half, jnp.zeros((MOE_HALF, D_MODEL), F32)], axis=0))

    @pl.when(rows == 0)
    def _():
        y_ref[...] = jnp.zeros(y_ref.shape, y_ref.dtype)


def _experts(block_expert, block_rows, next_expert, slot, xs, wgu, bgu, wd, bd):
    nb_max = xs.shape[0] // MOE_BLOCK
    blk = lambda i, be, rows, *_: (jnp.where(rows[i] > 0, i, 0), 0, 0)
    wmap = lambda i, be, *_: (be[i], 0, 0)
    return pl.pallas_call(
        _experts_kernel,
        grid_spec=pltpu.PrefetchScalarGridSpec(
            num_scalar_prefetch=4,
            grid=(nb_max,),
            in_specs=[pl.BlockSpec((MOE_BLOCK,) + ROW_TILE, blk),
                      pl.BlockSpec(memory_space=pl.ANY),
                      pl.BlockSpec((1, 1, 2 * D_FF), wmap),
                      pl.BlockSpec(memory_space=pl.ANY),
                      pl.BlockSpec((1, 1, D_MODEL), wmap)],
            out_specs=pl.BlockSpec((MOE_BLOCK,) + ROW_TILE, lambda i, *_: (i, 0, 0)),
            scratch_shapes=[pltpu.VMEM((2, D_MODEL, 2 * D_FF), F32),
                            pltpu.VMEM((2, D_FF, D_MODEL), F32),
                            pltpu.VMEM((D_MODEL, 2 * D_FF), BF16),
                            pltpu.VMEM((D_FF, D_MODEL), BF16),
                            pltpu.SemaphoreType.DMA((2, 2))]),
        out_shape=jax.ShapeDtypeStruct(xs.shape, ROW_DTYPE),
        compiler_params=_cparams(("arbitrary",)),
        name="moe_experts",
    )(block_expert, block_rows, next_expert, slot, xs, wgu, bgu, wd, bd)


def _combine_kernel(cnt_ref, off_ref, base_ref, x1_ref, idx_ref, gate_ref, offv_ref, gf_ref, y_ref,
                    o_ref, stage, sem):
    i = pl.program_id(0)
    last = pl.num_programs(0) - 1
    tm = TOK_TILE

    def runs(tile, slot):
        out = []
        for e in range(N_EXPERTS):
            n = cnt_ref[tile * N_EXPERTS + e]
            off = off_ref[tile * N_EXPERTS + e]
            base = base_ref[tile * N_EXPERTS + e]
            out.append((n, pltpu.make_async_copy(y_ref.at[pl.ds(base, n)], stage.at[slot, pl.ds(off, n)],
                                                 sem.at[slot])))
        return out

    def start_runs(tile, slot):
        for n, cp in runs(tile, slot):
            @pl.when(n > 0)
            def _():
                cp.start()

    @pl.when(i == 0)
    def _():
        for half in range(2):
            start_runs(half, half)

    for pair in range(2):
        @pl.when(((i & 1) != pair) & (i < last))
        def _():
            for half in range(2):
                start_runs(2 * (i + 1) + half, 2 * pair + half)

    gmats = []
    for half in range(2):
        rows_in = slice(half * tm, (half + 1) * tm)
        rows = _stage_rows(idx_ref[rows_in], offv_ref[half])
        lane = lax.broadcasted_iota(I32, (tm, STAGE_ROWS), 1)
        gates = gate_ref[rows_in]
        gmat = jnp.zeros((tm, STAGE_ROWS), F32)
        for k in range(TOP_K):
            gmat = gmat + jnp.where(lane == rows[k].astype(I32), gates[:, k:k + 1], 0.0)
        gmats.append(gmat.astype(BF16))
    for pair in range(2):
        @pl.when((i & 1) == pair)
        def _():
            for half in range(2):
                slot = 2 * pair + half
                rows_in = slice(half * tm, (half + 1) * tm)
                pltpu.make_async_copy(y_ref.at[pl.ds(0, STAGE_ROWS)], stage.at[slot], sem.at[slot]).wait()
                moe = jnp.dot(gmats[half], _tiles_to_rows(stage, (slot,)).astype(BF16), preferred_element_type=F32)
                o_ref[rows_in] = _rms(x1_ref[rows_in] + moe, gf_ref[...])


def _combine(x1, idx, gates, cnt, off, base, offv, gf, y):
    t = x1.shape[0]
    tm = 2 * TOK_TILE
    assert t % tm == 0
    return pl.pallas_call(
        _combine_kernel,
        grid_spec=pltpu.PrefetchScalarGridSpec(
            num_scalar_prefetch=3,
            grid=(t // tm,),
            in_specs=[pl.BlockSpec((tm, 1024), lambda i, *_: (i, 0)),
                      pl.BlockSpec((tm, TOP_K), lambda i, *_: (i, 0)),
                      pl.BlockSpec((tm, TOP_K), lambda i, *_: (i, 0)),
                      pl.BlockSpec((2, 1, LANES), lambda i, *_: (i, 0, 0)),
                      pl.BlockSpec((1, 1024), lambda i, *_: (0, 0)),
                      pl.BlockSpec(memory_space=pl.ANY)],
            out_specs=pl.BlockSpec((tm, 1024), lambda i, *_: (i, 0)),
            scratch_shapes=[pltpu.VMEM((4, STAGE_ROWS) + ROW_TILE, ROW_DTYPE),
                            pltpu.SemaphoreType.DMA((4,))]),
        out_shape=jax.ShapeDtypeStruct((t, 1024), F32),
        compiler_params=_cparams(("arbitrary",)),
        name="moe_combine",
    )(cnt, off, base, x1, idx, gates, offv, gf, y)


def _prep_w_in(w):
    wt = w.T
    c_if, c_qa, c_kv, c_g = 4096, 4104, 5128, 5640
    wm = wt[:c_if].astype(BF16)
    wif = jnp.pad(wt[c_if:c_qa], ((0, LANES - 2 * M_HEADS), (0, 0))).astype(BF16)
    wqa = (jnp.concatenate([wt[c_qa + (A_GROUP * g + j) * A_HD:c_qa + (A_GROUP * g + j + 1) * A_HD]
                            for j in range(A_GROUP) for g in range(A_KV_HEADS)], axis=0)
           * (A_HD ** -0.5)).astype(BF16)
    wkv = wt[c_kv:c_g].astype(BF16)
    wg = wt[c_g:].astype(BF16)
    return wm, wg, wqa, wkv, wif


def _rope_tables(pos):
    half = ROT_DIM // 2
    inv = ROPE_THETA ** (-(jnp.arange(half, dtype=F32) * 2.0 / ROT_DIM))
    ang = pos.astype(F32)[:, None] * inv[None, :]
    lane = jnp.arange(LANES) % A_HD
    hit = (lane[None, :] % half == jnp.arange(half)[:, None]) & (lane[None, :] < ROT_DIM)
    spread_cos = hit.astype(F32)
    spread_sin = jnp.where(hit, jnp.where(lane < half, -1.0, 1.0)[None, :], 0.0)
    cos_t = jnp.dot(jnp.cos(ang), spread_cos, precision=lax.Precision.HIGHEST) + (lane >= ROT_DIM).astype(F32)[None, :]
    sin_t = jnp.dot(jnp.sin(ang), spread_sin, precision=lax.Precision.HIGHEST)
    return cos_t, sin_t


def _round_up(x, m):
    return (x + m - 1) // m * m


def kernel(x_prompt, x_sample, cache_k, cache_v, state_c, state_n, state_m, norm1, w_in, b_igate, b_fgate,
           mlstm_norm, w_proj_m, w_proj_a, attn_sinks, w_out, norm2, w_router, b_router, w_gate_up, b_gate_up,
           w_down, b_down, norm_f):
    bp, sp, _ = x_prompt.shape
    bs, ts, _ = x_sample.shape
    past_len = 16384
    l = 0
    w1 = _prep_w_in(w_in[l])
    g1 = norm1[l][None, :]
    g2 = norm2[l][None, :]
    gf = norm_f[None, :]
    gate_bias = jnp.pad(jnp.concatenate([b_igate[l], b_fgate[l]]), (0, LANES - 2 * M_HEADS))[None, :]
    mn = mlstm_norm[l][None, :]
    wpm = w_proj_m[l].astype(BF16)
    wpa = jnp.concatenate([w_proj_a[l][(A_GROUP * g + j) * A_HD:(A_GROUP * g + j + 1) * A_HD]
                           for j in range(A_GROUP) for g in range(A_KV_HEADS)], axis=0).astype(BF16)
    wo = w_out[l].astype(BF16)
    wr_f = jnp.pad(w_router[l], ((0, 0), (0, LANES - N_EXPERTS)))
    wr_hi = wr_f.astype(BF16)
    wr = jnp.concatenate([wr_hi, (wr_f - wr_hi.astype(F32)).astype(BF16)], axis=1)
    br = jnp.pad(b_router[l], (0, LANES - N_EXPERTS))[None, :]
    wgu = w_gate_up[l]
    bgu = b_gate_up[l][:, None, :]
    wd = w_down[l]
    bd = b_down[l][:, None, :]
    sinks = attn_sinks[l]

    def mixer(x, nb, seq, pos0, L, c0, n0, m0, k_buf, v_buf, tm):
        t = nb * seq
        xf = x.reshape(t, D_MODEL)
        cos_t, sin_t = _rope_tables(pos0 + jnp.arange(seq))
        if seq < tm:
            cos_t, sin_t = jnp.tile(cos_t, (tm // seq, 1)), jnp.tile(sin_t, (tm // seq, 1))
        act = BF16 if L % 16 == 0 else F32
        qkv, og, qa, ka, va, gates = _inproj(xf, g1, w1, cos_t, sin_t, tm, max(seq // tm, 1), act)
        m0p = jnp.pad(m0, ((0, 0), (0, LANES - M_HEADS)))[:, None, :]
        hm, c, n, m = _mlstm(qkv, gates, og, c0, n0, m0p, gate_bias, mn, nb, L)
        if k_buf is None:
            ha = _swa_prompt(qa, ka, va, sinks, nb)
            k_keep = ka.reshape(nb, seq, A_KV)[:, seq - WINDOW:].reshape(nb, WINDOW, A_KV_HEADS, A_HD)
            v_keep = va.reshape(nb, seq, A_KV)[:, seq - WINDOW:].reshape(nb, WINDOW, A_KV_HEADS, A_HD)
        else:
            to_t = lambda a: a.transpose(0, 2, 3, 1).reshape(nb, A_KV, WINDOW)
            from_t = lambda a: a.reshape(nb, A_KV_HEADS, A_HD, WINDOW).transpose(0, 3, 1, 2)
            ha, k_keep, v_keep = _swa_sample(qa, ka, va, to_t(k_buf), to_t(v_buf), sinks, seq)
            k_keep, v_keep = from_t(k_keep), from_t(v_keep)
        x1, xn, idx, gate, cnt = _merge(xf, hm, ha, og, wpm, wpa, wo, g2, wr, br)
        return (x1, xn, idx, gate, cnt[:, :, 0]), (k_keep, v_keep, c, n, m[:, 0, :M_HEADS])

    zc = jnp.zeros((bp, M_HEADS, M_DV, M_DK), F32)
    zn = jnp.zeros((bp, M_HEADS, M_DK), F32)
    zm = jnp.zeros((bp, M_HEADS), F32)
    rp, sp_out = mixer(x_prompt, bp, sp, 0, M_CHUNK, zc, zn, zm, None, None, 512)
    rs, ss_out = mixer(x_sample, bs, ts, past_len, ts, state_c[l], state_n[l], state_m[l],
                       cache_k[l], cache_v[l], 256)

    cnt = jnp.concatenate([rp[4], rs[4]], axis=0)
    ntile_p = rp[4].shape[0]
    off = jnp.cumsum(cnt, axis=1) - cnt
    per_expert = jnp.sum(cnt, axis=0)
    padded = _round_up(per_expert, MOE_BLOCK)
    padded_end = jnp.cumsum(padded)
    expert_start = padded_end - padded
    base = expert_start[None, :] + jnp.cumsum(cnt, axis=0) - cnt
    n_tok = bp * sp + bs * ts
    n_slots = _round_up(n_tok * TOP_K, MOE_BLOCK) + N_EXPERTS * MOE_BLOCK
    nb_max = n_slots // MOE_BLOCK
    block_row = jnp.arange(nb_max, dtype=I32) * MOE_BLOCK
    block_expert = jnp.minimum(jnp.sum(block_row[:, None] >= padded_end[None, :], axis=1), N_EXPERTS - 1).astype(I32)
    n_blocks = (padded_end[-1] // MOE_BLOCK).astype(I32)[None]
    eid = jnp.arange(N_EXPERTS, dtype=I32)
    later_used = (eid[None, :] > eid[:, None]) & (padded[None, :] > 0)
    next_used = jnp.min(jnp.where(later_used, eid[None, :], N_EXPERTS), axis=1)
    next_used = jnp.where(next_used < N_EXPERTS, next_used, -1).astype(I32)
    run_parity = ((jnp.cumsum(padded > 0) - 1) & 1).astype(I32)
    of_block = lambda table: jnp.sum(jnp.where(block_expert[:, None] == eid[None, :], table[None, :], 0),
                                     axis=1).astype(I32)
    block_rows = jnp.clip(of_block(per_expert) - (block_row - of_block(expert_start)), 0, MOE_BLOCK).astype(I32)
    tails = jnp.concatenate([expert_start + per_expert, padded - per_expert,
                             n_blocks, nb_max - n_blocks]).astype(I32)
    flat = lambda a: a.reshape(-1).astype(I32)
    offv = jnp.pad(off, ((0, 0), (0, LANES - N_EXPERTS))).astype(F32)[:, None, :]

    xs = _dispatch(rp[1], rs[1], rp[2], rs[2], flat(cnt), flat(off), flat(base), tails, offv, n_slots)
    y = _experts(block_expert, block_rows, of_block(next_used), of_block(run_parity), xs, wgu, bgu, wd, bd)
    y_p = _combine(rp[0], rp[2], rp[3], flat(cnt[:ntile_p]), flat(off[:ntile_p]), flat(base[:ntile_p]),
                   offv[:ntile_p], gf, y)
    y_s = _combine(rs[0], rs[2], rs[3], flat(cnt[ntile_p:]), flat(off[ntile_p:]), flat(base[ntile_p:]),
                   offv[ntile_p:], gf, y)

    kp, vp, cp_, np_, mp = sp_out
    ks, vs, cs, ns, ms = ss_out
    return (y_p.reshape(bp, sp, D_MODEL), y_s.reshape(bs, ts, D_MODEL),
            kp[None], vp[None], cp_[None], np_[None], mp[None],
            ks[None], vs[None], cs[None], ns[None], ms[None])
```

```python
import functools

import jax
import jax.numpy as jnp
from jax import lax
from jax.experimental import pallas as pl
from jax.experimental.pallas import tpu as pltpu

F32 = jnp.float32
BF16 = jnp.bfloat16
I32 = jnp.int32

D_MODEL = 1024
M_HEADS = 4
M_DK = 256
M_DV = 256
M_CHUNK = 128
A_Q_HEADS = 16
A_KV_HEADS = 4
A_GROUP = 4
A_HD = 64
A_KV = A_KV_HEADS * A_HD
WINDOW = 128
A_BLOCK = 128
ROT_DIM = 16
ROPE_THETA = 500000.0
N_EXPERTS = 32
TOP_K = 4
D_FF = 1024
SWIGLU_LIMIT = 7.0
SWIGLU_ALPHA = 1.702
NORM_EPS = 1e-6

LANES = 128
SUBLANES = 8
VMEM_LIMIT = 56 * 1024 * 1024

MERGE_TILE = 512
TOK_TILE = 256
STAGE_ROWS = 4 * TOK_TILE
ROW_TILE = (8, 128)
ROW_DTYPE = F32
MOE_BLOCK = 512
DISPATCH_TILES = 4


def _cparams(sem):
    return pltpu.CompilerParams(dimension_semantics=sem, vmem_limit_bytes=VMEM_LIMIT)


def _rms(x, g):
    return x * lax.rsqrt(jnp.mean(x * x, axis=-1, keepdims=True) + NORM_EPS) * g


def _rope(x, cos_t, sin_t, first_half):
    n = x.shape[1]
    fwd = pltpu.roll(x, n - ROT_DIM // 2, axis=1)
    bwd = pltpu.roll(x, ROT_DIM // 2, axis=1)
    outs = []
    for b in range(n // LANES):
        sl = slice(b * LANES, (b + 1) * LANES)
        partner = jnp.where(first_half, fwd[:, sl], bwd[:, sl])
        outs.append(x[:, sl] * cos_t + partner * sin_t)
    return jnp.concatenate(outs, axis=1)


def _inproj_kernel(x_ref, g1_ref, wm_ref, wg_ref, wqa_ref, wkv_ref, wif_ref, cos_ref, sin_ref,
                   qkv_ref, og_ref, qa_ref, ka_ref, va_ref, gate_ref):
    x = x_ref[...]
    xn = _rms(x, g1_ref[...]).astype(BF16)
    act = qkv_ref.dtype

    def proj(w_ref, c0, n):
        return lax.dot_general(xn, w_ref[c0:c0 + n, :], (((1,), (1,)), ((), ())), preferred_element_type=F32)

    qkv_ref[:, 0:1024] = proj(wm_ref, 0, 1024).astype(act)
    qkv_ref[:, 1024:2048] = (proj(wm_ref, 1024, 1024) * (M_DK ** -0.5)).astype(act)
    qkv_ref[:, 2048:3072] = proj(wm_ref, 2048, 1024).astype(act)
    og_ref[:, 0:1024] = jax.nn.sigmoid(proj(wm_ref, 3072, 1024)).astype(act)
    og_ref[:, 1024:2048] = jax.nn.sigmoid(proj(wg_ref, 0, 1024)).astype(act)
    og_ref[:, 2048:3072] = jax.nn.sigmoid(proj(wg_ref, 1024, 1024)).astype(act)
    cos_t = cos_ref[...]
    sin_t = sin_ref[...]
    lane = lax.broadcasted_iota(I32, (1, LANES), 1)
    first_half = (lane & (ROT_DIM - 1)) < (ROT_DIM // 2)
    qa_ref[...] = _rope(proj(wqa_ref, 0, 1024), cos_t, sin_t, first_half).astype(act)
    ka_ref[...] = _rope(proj(wkv_ref, 0, A_KV), cos_t, sin_t, first_half)
    va_ref[...] = proj(wkv_ref, A_KV, A_KV)
    gate_ref[...] = proj(wif_ref, 0, LANES)


def _inproj(x, g1, weights, cos_t, sin_t, tm, rope_blocks, act):
    t = x.shape[0]
    tok = lambda n, dt: jax.ShapeDtypeStruct((t, n), dt)
    row = lambda n: pl.BlockSpec((tm, n), lambda i: (i, 0))
    resident = lambda w: pl.BlockSpec(w.shape, lambda i: (0, 0), pipeline_mode=pl.Buffered(1))
    return pl.pallas_call(
        _inproj_kernel,
        grid=(t // tm,),
        in_specs=[row(D_MODEL), pl.BlockSpec((1, D_MODEL), lambda i: (0, 0))]
                 + [resident(w) for w in weights]
                 + [pl.BlockSpec((tm, LANES), lambda i: (i % rope_blocks, 0)),
                    pl.BlockSpec((tm, LANES), lambda i: (i % rope_blocks, 0))],
        out_specs=[row(3072), row(3072), row(1024), row(A_KV), row(A_KV), row(LANES)],
        out_shape=[tok(3072, act), tok(3072, act), tok(1024, act), tok(A_KV, F32), tok(A_KV, F32),
                   tok(LANES, F32)],
        compiler_params=_cparams(("arbitrary",)),
        name="inproj",
    )(x, g1, *weights, cos_t, sin_t)


def _mlstm_gates(gc, bias, m_prev, L):
    z = gc + bias
    lf = jnp.minimum(z, 0.0) - jnp.log1p(jnp.exp(-jnp.abs(z)))
    row = lax.broadcasted_iota(I32, (L, LANES), 0)
    fc = lf
    sh = 1
    while sh < L:
        fc = fc + jnp.where(row >= sh, pltpu.roll(fc, sh, axis=0), 0.0)
        sh *= 2
    fcum = pltpu.roll(fc, LANES - M_HEADS, axis=1)
    a = z - fcum
    cmx = a
    sh = 1
    while sh < L:
        cmx = jnp.maximum(cmx, jnp.where(row >= sh, pltpu.roll(cmx, sh, axis=0), -jnp.inf))
        sh *= 2
    mx = jnp.maximum(m_prev, cmx)
    inter = jnp.exp(m_prev - mx)
    f_end = fcum[L - 1:L]
    m_end = f_end + mx[L - 1:L]
    decay = jnp.exp(f_end + m_prev - m_end)
    wsrc = jnp.exp(a + (f_end - m_end))
    return a, mx, inter, wsrc, decay, m_end


NT_DIMS = (((1,), (1,)), ((), ()))
TN_DIMS = (((0,), (0,)), ((), ()))


def _mlstm_decay(h, L, gates):
    a, mx = gates[0], gates[1]
    t_i = lax.broadcasted_iota(I32, (L, L), 0)
    s_i = lax.broadcasted_iota(I32, (L, L), 1)
    a_row = jnp.sum(jnp.where(t_i == s_i, a[:, h:h + 1], 0.0), axis=0, keepdims=True)
    return jnp.exp(jnp.where(s_i <= t_i, a_row - mx[:, h:h + 1], -jnp.inf))


def _mlstm_output(h, q, og, mn, gates, qk_sum, pv, cq, n_h):
    inter_col = gates[2][:, h:h + 1]
    num = inter_col * cq + pv
    den = inter_col * jnp.sum(q.astype(F32) * n_h, axis=1, keepdims=True) + qk_sum
    hh = num / jnp.maximum(jnp.abs(den), 1.0)
    hn = hh * lax.rsqrt(jnp.mean(hh * hh, axis=1, keepdims=True) + NORM_EPS)
    return hn * mn * og


def _mlstm_kernel(L, G, CH, carry, qkv_ref, gate_ref, og_ref, c0_ref, n0_ref, m0_ref, bias_ref, mn_ref,
                  h_ref, c_ref, n_ref, m_ref, *scratch):
    if carry:
        c_in, n_in, m_in = scratch
        ci = pl.program_id(1)

        @pl.when(ci == 0)
        def _():
            c_in[...] = c0_ref[...]
            n_in[...] = n0_ref[...]
            m_in[...] = m0_ref[...]
        c_out, n_out, m_out = scratch
    else:
        c_in, n_in, m_in = c0_ref, n0_ref, m0_ref
        c_out, n_out, m_out = c_ref, n_ref, m_ref

    def qkv(j, cc, h, part):
        cols = slice(part * 1024 + h * M_DK, part * 1024 + (h + 1) * M_DK)
        return qkv_ref[j, cc * L:(cc + 1) * L, cols].astype(BF16)

    units = [(j, cc, h) for cc in range(CH) for j in range(G) for h in range(M_HEADS)]
    gates = {}
    for j in range(G):
        m_cur = m_in[j]
        for cc in range(CH):
            gates[j, cc] = _mlstm_gates(gate_ref[j, cc * L:(cc + 1) * L], bias_ref[...], m_cur, L)
            m_cur = gates[j, cc][5]
        m_out[j] = m_cur
    def carried(cc, c_cur):
        return {(j, h): lax.dot_general(qkv(j, cc, h, 0), c_cur[j, h].astype(BF16), NT_DIMS,
                                        preferred_element_type=F32) for j, c, h in units if c == cc}

    c_cur = {(j, h): c_in[j, h] for j in range(G) for h in range(M_HEADS)}
    n_cur = {(j, h): n_in[j, h:h + 1, :] for j in range(G) for h in range(M_HEADS)}
    scores = {u: lax.dot_general(qkv(*u, 0), qkv(*u, 1), NT_DIMS, preferred_element_type=F32) for u in units}
    cqs_first = carried(0, c_cur)
    qks = {(j, cc, h): scores[j, cc, h] * _mlstm_decay(h, L, gates[j, cc]) for j, cc, h in units}
    pvs = {u: jnp.dot(qks[u].astype(BF16), qkv(*u, 2), preferred_element_type=F32) for u in units}
    kws = {(j, cc, h): gates[j, cc][3][:, h:h + 1] * qkv(j, cc, h, 1).astype(F32) for j, cc, h in units}
    for cc in range(CH):
        now = [u for u in units if u[1] == cc]
        cqs = cqs_first if cc == 0 else carried(cc, c_cur)
        n_old = dict(n_cur)

        def update_states():
            for j, _, h in now:
                dec = gates[j, cc][4][:, h:h + 1]
                c_cur[j, h] = dec * c_cur[j, h] + lax.dot_general(qkv(j, cc, h, 2), kws[j, cc, h].astype(BF16),
                                                                  TN_DIMS, preferred_element_type=F32)
                n_cur[j, h] = dec * n_cur[j, h] + jnp.sum(kws[j, cc, h], axis=0, keepdims=True)

        def write_outputs():
            for j, _, h in now:
                sl = slice(h * M_DV, (h + 1) * M_DV)
                rows = slice(cc * L, (cc + 1) * L)
                out = _mlstm_output(h, qkv(j, cc, h, 0), og_ref[j, rows, sl].astype(F32), mn_ref[:, sl],
                                    gates[j, cc], jnp.sum(qks[j, cc, h], axis=1, keepdims=True), pvs[j, cc, h],
                                    cqs[j, h], n_old[j, h])
                h_ref[j, rows, sl] = out.astype(h_ref.dtype)

        for phase in ((update_states, write_outputs) if carry else (write_outputs, update_states)):
            phase()
    for j in range(G):
        for h in range(M_HEADS):
            c_out[j, h] = c_cur[j, h]
            n_out[j, h:h + 1, :] = n_cur[j, h]

    if carry:
        @pl.when(ci == pl.num_programs(1) - 1)
        def _():
            c_ref[...] = c_in[...]
            n_ref[...] = n_in[...]
            m_ref[...] = m_in[...]


MLSTM_GROUP_CARRY = 2
MLSTM_CHUNKS_CARRY = 1
MLSTM_GROUP_SINGLE = 8


def _mlstm(qkv, gates, og, c0, n0, m0, gate_bias, mnorm, nb, L):
    t = qkv.shape[0]
    nc = t // (nb * L)
    carry = nc > 1
    G = MLSTM_GROUP_CARRY if carry else MLSTM_GROUP_SINGLE
    CH = MLSTM_CHUNKS_CARRY if carry else 1
    qkv, gates, og = (a.reshape(nb, nc * L, a.shape[1]) for a in (qkv, gates, og))
    tokmap = lambda b, c: (b, c, 0)
    const2 = lambda b, c: (0, 0)
    state_specs = [pl.BlockSpec((G, M_HEADS, M_DV, M_DK), lambda b, c: (b, 0, 0, 0)),
                   pl.BlockSpec((G, M_HEADS, M_DK), lambda b, c: (b, 0, 0)),
                   pl.BlockSpec((G, 1, LANES), lambda b, c: (b, 0, 0))]
    scratch = [pltpu.VMEM((G, M_HEADS, M_DV, M_DK), F32),
               pltpu.VMEM((G, M_HEADS, M_DK), F32),
               pltpu.VMEM((G, 1, LANES), F32)] if carry else []
    h, c, n, m = pl.pallas_call(
        functools.partial(_mlstm_kernel, L, G, CH, carry),
        grid=(nb // G, nc // CH),
        in_specs=[pl.BlockSpec((G, CH * L, 3072), tokmap),
                  pl.BlockSpec((G, CH * L, LANES), tokmap),
                  pl.BlockSpec((G, CH * L, 1024), tokmap)]
                 + state_specs
                 + [pl.BlockSpec((1, LANES), const2),
                    pl.BlockSpec((1, 1024), const2)],
        out_specs=[pl.BlockSpec((G, CH * L, 1024), tokmap)] + state_specs,
        out_shape=[jax.ShapeDtypeStruct((nb, nc * L, 1024), qkv.dtype),
                   jax.ShapeDtypeStruct((nb, M_HEADS, M_DV, M_DK), F32),
                   jax.ShapeDtypeStruct((nb, M_HEADS, M_DK), F32),
                   jax.ShapeDtypeStruct((nb, 1, LANES), F32)],
        scratch_shapes=scratch,
        compiler_params=_cparams(("arbitrary", "arbitrary")),
        name="mlstm_L%d" % L,
    )(qkv, gates, og, c0, n0, m0, gate_bias, mnorm)
    return h.reshape(t, 1024), c, n, m


def _swa_bias(m, has_prev):
    nkeys = 2 * A_BLOCK
    t_i = lax.broadcasted_iota(I32, (A_GROUP * m, nkeys), 0) & (m - 1)
    s_i = lax.broadcasted_iota(I32, (A_GROUP * m, nkeys), 1)
    diff = t_i + A_BLOCK - s_i
    mask = (diff >= 0) & (diff <= WINDOW)
    if not has_prev:
        mask = mask & (s_i >= A_BLOCK)
    return jnp.where(mask, 0.0, -jnp.inf).astype(F32)


def _swa_group_ones():
    nkeys = 2 * A_BLOCK
    r = lax.broadcasted_iota(I32, (A_KV_HEADS * nkeys, A_KV), 0) >> (nkeys.bit_length() - 1)
    c = lax.broadcasted_iota(I32, (A_KV_HEADS * nkeys, A_KV), 1) >> (A_HD.bit_length() - 1)
    return (r == c).astype(BF16)


def _swa_core(q, k_prev, v_prev, k_cur, v_cur, bias, group_ones, sink_ref, keys_on_lanes=False):
    m = q.shape[0]
    q = q.astype(BF16)
    qst = jnp.concatenate([q[:, j * A_KV:(j + 1) * A_KV] for j in range(A_GROUP)], axis=0)
    key_axis = 1 if keys_on_lanes else 0
    k_all = jnp.concatenate([k_prev, k_cur], axis=key_axis)
    v_all = jnp.concatenate([v_prev, v_cur], axis=key_axis)
    feat = lax.broadcasted_iota(I32, (A_KV, 1) if keys_on_lanes else (1, A_KV), 1 - key_axis)
    kbd, vbd = [], []
    for g in range(A_KV_HEADS):
        in_g = (feat >= g * A_HD) & (feat < (g + 1) * A_HD)
        kbd.append(jnp.where(in_g, k_all, 0.0).astype(BF16))
        vbd.append(jnp.where(in_g, v_all, 0.0).astype(BF16))
    kbd = jnp.concatenate(kbd, axis=key_axis)
    vbd = jnp.concatenate(vbd, axis=key_axis)
    if keys_on_lanes:
        s = jnp.dot(qst, kbd, preferred_element_type=F32)
    else:
        s = lax.dot_general(qst, kbd, NT_DIMS, preferred_element_type=F32)
    nkeys = 2 * A_BLOCK
    ps, sink_terms = [], []
    for g in range(A_KV_HEADS):
        sg = s[:, g * nkeys:(g + 1) * nkeys] + bias
        sink = jnp.concatenate(
            [jnp.full((m, LANES), sink_ref[A_GROUP * g + j], F32) for j in range(A_GROUP)], axis=0)
        mx = jnp.maximum(jnp.broadcast_to(jnp.max(sg, axis=1, keepdims=True), (A_GROUP * m, LANES)), sink)
        ps += [jnp.exp(sg[:, :LANES] - mx).astype(BF16), jnp.exp(sg[:, LANES:] - mx).astype(BF16)]
        sink_terms.append(jnp.exp(sink - mx))
    p = jnp.concatenate(ps, axis=1)
    if keys_on_lanes:
        o = lax.dot_general(p, vbd, NT_DIMS, preferred_element_type=F32)
    else:
        o = jnp.dot(p, vbd, preferred_element_type=F32)
    den = jnp.dot(p, group_ones, preferred_element_type=F32)
    low = lax.broadcasted_iota(I32, (1, LANES), 1) < A_HD
    den = den + jnp.concatenate([jnp.where(low, sink_terms[0], sink_terms[1]),
                                 jnp.where(low, sink_terms[2], sink_terms[3])], axis=1)
    o = o / den
    return jnp.concatenate([o[j * m:(j + 1) * m] for j in range(A_GROUP)], axis=1)


SWA_STEP_BLOCKS = 8
SWA_STEP = SWA_STEP_BLOCKS * A_BLOCK


def _swa_prompt_kernel(sink_ref, q_ref, kp_ref, kc_ref, vp_ref, vc_ref, o_ref, bias_s, bias0_s, ones_s):
    first = (pl.program_id(0) == 0) & (pl.program_id(1) == 0)

    @pl.when(first)
    def _():
        bias_s[...] = _swa_bias(A_BLOCK, True)
        bias0_s[...] = _swa_bias(A_BLOCK, False)
        ones_s[...] = _swa_group_ones()

    bias = bias_s[...]
    bias_first = jnp.where(pl.program_id(1) > 0, bias, bias0_s[...])
    ones = ones_s[...]
    k_prev, v_prev = kp_ref[...], vp_ref[...]
    for b in range(SWA_STEP_BLOCKS):
        rows = slice(b * A_BLOCK, (b + 1) * A_BLOCK)
        k_cur, v_cur = kc_ref[rows], vc_ref[rows]
        o_ref[rows] = _swa_core(q_ref[rows], k_prev, v_prev, k_cur, v_cur, bias_first if b == 0 else bias,
                                ones, sink_ref).astype(o_ref.dtype)
        k_prev, v_prev = k_cur, v_cur


def _swa_prompt(qa, ka, va, sinks, nb):
    t = qa.shape[0]
    nstep = t // (nb * SWA_STEP)
    cur = lambda b, i, s: (b * nstep + i, 0)
    prev = lambda b, i, s: (SWA_STEP_BLOCKS * (b * nstep + i) - jnp.minimum(i, 1), 0)
    return pl.pallas_call(
        _swa_prompt_kernel,
        grid_spec=pltpu.PrefetchScalarGridSpec(
            num_scalar_prefetch=1,
            grid=(nb, nstep),
            in_specs=[pl.BlockSpec((SWA_STEP, 1024), cur),
                      pl.BlockSpec((A_BLOCK, A_KV), prev),
                      pl.BlockSpec((SWA_STEP, A_KV), cur),
                      pl.BlockSpec((A_BLOCK, A_KV), prev),
                      pl.BlockSpec((SWA_STEP, A_KV), cur)],
            out_specs=pl.BlockSpec((SWA_STEP, 1024), cur),
            scratch_shapes=[pltpu.VMEM((A_GROUP * A_BLOCK, 2 * A_BLOCK), F32),
                            pltpu.VMEM((A_GROUP * A_BLOCK, 2 * A_BLOCK), F32),
                            pltpu.VMEM((A_KV_HEADS * 2 * A_BLOCK, A_KV), BF16)]),
        out_shape=jax.ShapeDtypeStruct((t, 1024), BF16),
        compiler_params=_cparams(("arbitrary", "arbitrary")),
        name="swa_prompt",
    )(sinks, qa, ka, ka, va, va)


SWA_SAMPLE_GROUP = 16


def _swa_sample_kernel(T, sink_ref, q_ref, kn_ref, vn_ref, kb_ref, vb_ref, o_ref, ko_ref, vo_ref):
    pad = jnp.zeros((A_BLOCK - T, A_KV), F32)
    bias = _swa_bias(T, True)
    ones = _swa_group_ones()
    lane = lax.broadcasted_iota(I32, (1, WINDOW), 1)
    for j in range(SWA_SAMPLE_GROUP):
        rows = slice(j * T, (j + 1) * T)
        k_new_t = jnp.concatenate([kn_ref[rows], pad], axis=0).T
        v_new_t = jnp.concatenate([vn_ref[rows], pad], axis=0).T
        k_buf_t = kb_ref[j]
        v_buf_t = vb_ref[j]
        o_ref[rows] = _swa_core(q_ref[rows], k_buf_t, v_buf_t, k_new_t, v_new_t, bias, ones, sink_ref,
                                keys_on_lanes=True).astype(o_ref.dtype)
        keep = lane < WINDOW - T
        ko_ref[j] = jnp.where(keep, pltpu.roll(k_buf_t, WINDOW - T, axis=1), pltpu.roll(k_new_t, WINDOW - T, axis=1))
        vo_ref[j] = jnp.where(keep, pltpu.roll(v_buf_t, WINDOW - T, axis=1), pltpu.roll(v_new_t, WINDOW - T, axis=1))


def _swa_sample(qa, ka, va, k_buf, v_buf, sinks, T):
    nb = k_buf.shape[0]
    g = SWA_SAMPLE_GROUP
    tok = lambda b, s: (b, 0)
    buf = lambda b, s: (b, 0, 0)
    return pl.pallas_call(
        functools.partial(_swa_sample_kernel, T),
        grid_spec=pltpu.PrefetchScalarGridSpec(
            num_scalar_prefetch=1,
            grid=(nb // g,),
            in_specs=[pl.BlockSpec((g * T, 1024), tok),
                      pl.BlockSpec((g * T, A_KV), tok),
                      pl.BlockSpec((g * T, A_KV), tok),
                      pl.BlockSpec((g, A_KV, WINDOW), buf),
                      pl.BlockSpec((g, A_KV, WINDOW), buf)],
            out_specs=[pl.BlockSpec((g * T, 1024), tok),
                       pl.BlockSpec((g, A_KV, WINDOW), buf),
                       pl.BlockSpec((g, A_KV, WINDOW), buf)]),
        out_shape=[jax.ShapeDtypeStruct((nb * T, 1024), qa.dtype),
                   jax.ShapeDtypeStruct((nb, A_KV, WINDOW), F32),
                   jax.ShapeDtypeStruct((nb, A_KV, WINDOW), F32)],
        compiler_params=_cparams(("arbitrary",)),
        name="swa_sample",
    )(sinks, qa, ka, va, k_buf, v_buf)


def _merge_kernel(x_ref, hm_ref, ha_ref, gm_ref, ga_ref, wpm_ref, wpa_ref, wo_ref, g2_ref, wr_ref, br_ref,
                  x1_ref, xn_ref, idx_ref, gate_ref, cnt_ref):
    pm = jnp.dot(hm_ref[...].astype(BF16), wpm_ref[...], preferred_element_type=F32)
    pa = jnp.dot(ha_ref[...].astype(BF16), wpa_ref[...], preferred_element_type=F32)
    mixed = gm_ref[...].astype(F32) * pm + ga_ref[...].astype(F32) * pa
    x1 = x_ref[...] + jnp.dot(mixed.astype(BF16), wo_ref[...], preferred_element_type=F32)
    x1_ref[...] = x1
    xn = _rms(x1, g2_ref[...])
    xn_ref[...] = xn.astype(BF16)
    xn_hi = xn.astype(BF16)
    xn_lo = (xn - xn_hi.astype(F32)).astype(BF16)
    part = jnp.dot(xn_hi, wr_ref[...], preferred_element_type=F32)
    logits = (part[:, :LANES] + part[:, LANES:]
              + jnp.dot(xn_lo, wr_ref[:, :LANES], preferred_element_type=F32) + br_ref[...])
    tm = logits.shape[0]
    cur = logits.T[0:N_EXPERTS]
    eid = lax.broadcasted_iota(I32, (N_EXPERTS, tm), 0).astype(F32)
    vals, idxs = [], []
    onehot = jnp.zeros((N_EXPERTS, tm), F32)
    for _ in range(TOP_K):
        mval = jnp.max(cur, axis=0, keepdims=True)
        sel = jnp.min(jnp.where(cur == mval, eid, float(N_EXPERTS)), axis=0, keepdims=True)
        hit = eid == sel
        onehot = onehot + hit.astype(F32)
        cur = jnp.where(hit, -jnp.inf, cur)
        vals.append(mval)
        idxs.append(sel)
    es = [jnp.exp(v - vals[0]) for v in vals]
    tot = es[0] + es[1] + es[2] + es[3]
    row = lax.broadcasted_iota(I32, (SUBLANES, tm), 0)
    res = jnp.zeros((SUBLANES, tm), F32)
    for k in range(TOP_K):
        res = jnp.where(row == k, idxs[k], res)
        res = jnp.where(row == TOP_K + k, es[k] / tot, res)
    res_t = jnp.concatenate([res, jnp.zeros((LANES - SUBLANES, tm), F32)], axis=0).T
    idx_ref[...] = res_t[:, 0:TOP_K].astype(I32)
    gate_ref[...] = res_t[:, TOP_K:2 * TOP_K]
    for j in range(tm // TOK_TILE):
        cnt_ref[j] = jnp.sum(onehot[:, j * TOK_TILE:(j + 1) * TOK_TILE], axis=1, keepdims=True).astype(I32)


def _merge(x, hm, ha, og, wpm, wpa, wo, g2, wr, br):
    t = x.shape[0]
    tm = MERGE_TILE
    row = lambda n: pl.BlockSpec((tm, n), lambda i: (i, 0))
    const = lambda r, c: pl.BlockSpec((r, c), lambda i: (0, 0), pipeline_mode=pl.Buffered(1))
    return pl.pallas_call(
        _merge_kernel,
        grid=(t // tm,),
        in_specs=[row(1024), row(1024), row(1024),
                  pl.BlockSpec((tm, 1024), lambda i: (i, 1)),
                  pl.BlockSpec((tm, 1024), lambda i: (i, 2)),
                  const(1024, 1024), const(1024, 1024), const(1024, 1024), const(1, 1024),
                  const(1024, 2 * LANES), const(1, LANES)],
        out_specs=[row(1024), row(1024), row(TOP_K), row(TOP_K),
                   pl.BlockSpec((tm // TOK_TILE, N_EXPERTS, 1), lambda i: (i, 0, 0))],
        out_shape=[jax.ShapeDtypeStruct((t, 1024), F32),
                   jax.ShapeDtypeStruct((t, 1024), BF16),
                   jax.ShapeDtypeStruct((t, TOP_K), I32),
                   jax.ShapeDtypeStruct((t, TOP_K), F32),
                   jax.ShapeDtypeStruct((t // TOK_TILE, N_EXPERTS, 1), I32)],
        compiler_params=_cparams(("arbitrary",)),
        name="merge_route",
    )(x, hm, ha, og, og, wpm, wpa, wo, g2, wr, br)


def _rows_to_tiles(ref, lead, x):
    chunks = jnp.stack([x[:, s * LANES:(s + 1) * LANES] for s in range(ROW_TILE[0])], axis=0)
    ref[lead] = jnp.swapaxes(chunks, 0, 1).astype(ref.dtype)


def _tiles_to_rows(ref, lead, rows=slice(None)):
    chunks = jnp.swapaxes(ref[lead + (rows,)].astype(F32), 0, 1)
    return jnp.concatenate([chunks[s] for s in range(ROW_TILE[0])], axis=1)


def _stage_rows(idx, off_row):
    tm = idx.shape[0]
    lane = lax.broadcasted_iota(I32, (tm, LANES), 1)
    hits = [lane == idx[:, k:k + 1] for k in range(TOP_K)]
    onehot = sum(h.astype(F32) for h in hits)
    r_i = lax.broadcasted_iota(I32, (tm, tm), 0)
    c_i = lax.broadcasted_iota(I32, (tm, tm), 1)
    before = (c_i < r_i).astype(BF16)
    rank = jnp.dot(before, onehot.astype(BF16), preferred_element_type=F32)
    pos = rank + off_row
    return [jnp.sum(jnp.where(h, pos, 0.0), axis=1, keepdims=True) for h in hits]


def _dispatch_kernel(ntile_a, cnt_ref, off_ref, base_ref, tail_ref, xa_ref, xb_ref, ia_ref, ib_ref, offv_ref,
                     xs_ref, stage, zeros, sem, zsem):
    i = pl.program_id(0)
    from_a = DISPATCH_TILES * i < ntile_a
    tm = TOK_TILE

    def stage_tile(slot):
        rows_in = slice(slot * tm, (slot + 1) * tm)
        xn = jnp.where(from_a, xa_ref[rows_in], xb_ref[rows_in])
        idx = jnp.where(from_a, ia_ref[rows_in], ib_ref[rows_in])
        rows = _stage_rows(idx, offv_ref[slot])
        lane = lax.broadcasted_iota(I32, (tm, LANES), 1)
        r4 = jnp.zeros((tm, LANES), F32)
        for k in range(TOP_K):
            r4 = jnp.where(lane == k, rows[k], r4)
        r4t = r4.T.astype(I32)
        r_iota = lax.broadcasted_iota(I32, (STAGE_ROWS, tm), 0)
        sel = r_iota == r4t[0:1, :]
        for k in range(1, TOP_K):
            sel = sel | (r_iota == r4t[k:k + 1, :])
        sel = jnp.where(sel, 1.0, 0.0).astype(BF16)
        return jnp.dot(sel, xn, preferred_element_type=F32)

    def runs(tile, slot):
        out = []
        for e in range(N_EXPERTS):
            n = cnt_ref[tile * N_EXPERTS + e]
            off = off_ref[tile * N_EXPERTS + e]
            base = base_ref[tile * N_EXPERTS + e]
            out.append((n, pltpu.make_async_copy(stage.at[slot, pl.ds(off, n)], xs_ref.at[pl.ds(base, n)],
                                                 sem.at[slot])))
        return out

    def wait_runs(slot):
        pltpu.make_async_copy(stage.at[slot], xs_ref.at[pl.ds(0, STAGE_ROWS)], sem.at[slot]).wait()

    staged_tiles = [stage_tile(slot) for slot in range(DISPATCH_TILES)]

    for slot in range(DISPATCH_TILES):
        @pl.when(i >= 1)
        def _():
            wait_runs(slot)
        _rows_to_tiles(stage, (slot,), staged_tiles[slot])
        for n, cp in runs(DISPATCH_TILES * i + slot, slot):
            @pl.when(n > 0)
            def _():
                cp.start()

    @pl.when(i == pl.num_programs(0) - 1)
    def _():
        for slot in range(DISPATCH_TILES):
            wait_runs(slot)
        zeros[...] = jnp.zeros(zeros.shape, zeros.dtype)
        tails = []
        for e in range(N_EXPERTS):
            start = tail_ref[e]
            n = tail_ref[N_EXPERTS + e]
            tails.append((n, pltpu.make_async_copy(zeros.at[pl.ds(0, n)], xs_ref.at[pl.ds(start, n)], zsem)))
        for n, cp in tails:
            @pl.when(n > 0)
            def _():
                cp.start()
        for n, cp in tails:
            @pl.when(n > 0)
            def _():
                cp.wait()
        first = tail_ref[2 * N_EXPERTS]
        n_unused = tail_ref[2 * N_EXPERTS + 1]

        def unused_block(j):
            return pltpu.make_async_copy(zeros, xs_ref.at[pl.ds((first + j) * MOE_BLOCK, MOE_BLOCK)], zsem)

        @pl.loop(0, n_unused)
        def _(j):
            unused_block(j).start()

        @pl.loop(0, n_unused)
        def _(j):
            unused_block(j).wait()


def _dispatch(xa, xb, ia, ib, cnt, off, base, tails, offv, n_slots):
    tm = DISPATCH_TILES * TOK_TILE
    assert xa.shape[0] % tm == 0 and xb.shape[0] % tm == 0
    nstep_a = xa.shape[0] // tm
    nstep_b = xb.shape[0] // tm
    amap = lambda i, *_: (jnp.minimum(i, nstep_a - 1), 0)
    bmap = lambda i, *_: (jnp.maximum(i - nstep_a, 0), 0)
    return pl.pallas_call(
        functools.partial(_dispatch_kernel, DISPATCH_TILES * nstep_a),
        grid_spec=pltpu.PrefetchScalarGridSpec(
            num_scalar_prefetch=4,
            grid=(nstep_a + nstep_b,),
            in_specs=[pl.BlockSpec((tm, 1024), amap),
                      pl.BlockSpec((tm, 1024), bmap),
                      pl.BlockSpec((tm, TOP_K), amap),
                      pl.BlockSpec((tm, TOP_K), bmap),
                      pl.BlockSpec((DISPATCH_TILES, 1, LANES), lambda i, *_: (i, 0, 0))],
            out_specs=pl.BlockSpec(memory_space=pl.ANY),
            scratch_shapes=[pltpu.VMEM((DISPATCH_TILES, STAGE_ROWS) + ROW_TILE, ROW_DTYPE),
                            pltpu.VMEM((MOE_BLOCK,) + ROW_TILE, ROW_DTYPE),
                            pltpu.SemaphoreType.DMA((DISPATCH_TILES,)),
                            pltpu.SemaphoreType.DMA]),
        out_shape=jax.ShapeDtypeStruct((n_slots,) + ROW_TILE, ROW_DTYPE),
        compiler_params=_cparams(("arbitrary",)),
        name="moe_dispatch",
    )(cnt, off, base, tails, xa, xb, ia, ib, offv)


MOE_HALF = MOE_BLOCK // 2


def _experts_kernel(be_ref, rows_ref, next_ref, slot_ref, xs_ref, wgu_hbm, bgu_ref, wd_hbm, bd_ref, y_ref,
                    wgu_f, wd_f, wgu_s, wd_s, sem):
    i = pl.program_id(0)
    rows = rows_ref[i]
    live = rows > 0
    run_start = (i == 0) | (be_ref[i] != be_ref[jnp.maximum(i - 1, 0)])

    def fetch(expert, slot):
        return (pltpu.make_async_copy(wgu_hbm.at[expert], wgu_f.at[slot], sem.at[0, slot]),
                pltpu.make_async_copy(wd_hbm.at[expert], wd_f.at[slot], sem.at[1, slot]))

    for slot in range(2):
        @pl.when(live & run_start & (slot_ref[i] == slot))
        def _():
            @pl.when(i == 0)
            def _():
                for cp in fetch(be_ref[i], slot):
                    cp.start()
            for cp in fetch(be_ref[i], slot):
                cp.wait()

            @pl.when(next_ref[i] >= 0)
            def _():
                for cp in fetch(next_ref[i], 1 - slot):
                    cp.start()
            wgu_s[...] = wgu_f[slot].astype(BF16)
            wd_s[...] = wd_f[slot].astype(BF16)

    def ffn(x):
        h = jnp.dot(x.astype(BF16), wgu_s[...], preferred_element_type=F32) + bgu_ref[0]
        gate = jnp.minimum(h[:, :D_FF], SWIGLU_LIMIT)
        up = jnp.clip(h[:, D_FF:], -SWIGLU_LIMIT, SWIGLU_LIMIT)
        act = gate * jax.nn.sigmoid(SWIGLU_ALPHA * gate) * (up + 1.0)
        return jnp.dot(act.astype(BF16), wd_s[...], preferred_element_type=F32) + bd_ref[0]

    @pl.when(rows > MOE_HALF)
    def _():
        _rows_to_tiles(y_ref, (), ffn(_tiles_to_rows(xs_ref, ())))

    @pl.when(live & (rows <= MOE_HALF))
    def _():
        y_half = ffn(_tiles_to_rows(xs_ref, (), slice(0, MOE_HALF)))
        _rows_to_tiles(y_ref, (), jnp.concatenate([y_half, jnp.zeros((MOE_HALF, D_MODEL), F32)], axis=0))

    @pl.when(rows == 0)
    def _():
        y_ref[...] = jnp.zeros(y_ref.shape, y_ref.dtype)


def _experts(block_expert, block_rows, next_expert, slot, xs, wgu, bgu, wd, bd):
    nb_max = xs.shape[0] // MOE_BLOCK
    blk = lambda i, be, rows, *_: (jnp.where(rows[i] > 0, i, 0), 0, 0)
    wmap = lambda i, be, *_: (be[i], 0, 0)
    return pl.pallas_call(
        _experts_kernel,
        grid_spec=pltpu.PrefetchScalarGridSpec(
            num_scalar_prefetch=4,
            grid=(nb_max,),
            in_specs=[pl.BlockSpec((MOE_BLOCK,) + ROW_TILE, blk),
                      pl.BlockSpec(memory_space=pl.ANY),
                      pl.BlockSpec((1, 1, 2 * D_FF), wmap),
                      pl.BlockSpec(memory_space=pl.ANY),
                      pl.BlockSpec((1, 1, D_MODEL), wmap)],
            out_specs=pl.BlockSpec((MOE_BLOCK,) + ROW_TILE, lambda i, *_: (i, 0, 0)),
            scratch_shapes=[pltpu.VMEM((2, D_MODEL, 2 * D_FF), F32),
                            pltpu.VMEM((2, D_FF, D_MODEL), F32),
                            pltpu.VMEM((D_MODEL, 2 * D_FF), BF16),
                            pltpu.VMEM((D_FF, D_MODEL), BF16),
                            pltpu.SemaphoreType.DMA((2, 2))]),
        out_shape=jax.ShapeDtypeStruct(xs.shape, ROW_DTYPE),
        compiler_params=_cparams(("arbitrary",)),
        name="moe_experts",
    )(block_expert, block_rows, next_expert, slot, xs, wgu, bgu, wd, bd)


def _combine_kernel(cnt_ref, off_ref, base_ref, x1_ref, idx_ref, gate_ref, offv_ref, gf_ref, y_ref,
                    o_ref, stage, sem):
    i = pl.program_id(0)
    last = pl.num_programs(0) - 1
    tm = TOK_TILE

    def runs(tile, slot):
        out = []
        for e in range(N_EXPERTS):
            n = cnt_ref[tile * N_EXPERTS + e]
            off = off_ref[tile * N_EXPERTS + e]
            base = base_ref[tile * N_EXPERTS + e]
            out.append((n, pltpu.make_async_copy(y_ref.at[pl.ds(base, n)], stage.at[slot, pl.ds(off, n)],
                                                 sem.at[slot])))
        return out

    def start_runs(tile, slot):
        for n, cp in runs(tile, slot):
            @pl.when(n > 0)
            def _():
                cp.start()

    @pl.when(i == 0)
    def _():
        for half in range(2):
            start_runs(half, half)

    for pair in range(2):
        @pl.when(((i & 1) != pair) & (i < last))
        def _():
            for half in range(2):
                start_runs(2 * (i + 1) + half, 2 * pair + half)

    gmats = []
    for half in range(2):
        rows_in = slice(half * tm, (half + 1) * tm)
        rows = _stage_rows(idx_ref[rows_in], offv_ref[half])
        lane = lax.broadcasted_iota(I32, (tm, STAGE_ROWS), 1)
        gates = gate_ref[rows_in]
        gmat = jnp.zeros((tm, STAGE_ROWS), F32)
        for k in range(TOP_K):
            gmat = gmat + jnp.where(lane == rows[k].astype(I32), gates[:, k:k + 1], 0.0)
        gmats.append(gmat.astype(BF16))
    for pair in range(2):
        @pl.when((i & 1) == pair)
        def _():
            for half in range(2):
                slot = 2 * pair + half
                rows_in = slice(half * tm, (half + 1) * tm)
                pltpu.make_async_copy(y_ref.at[pl.ds(0, STAGE_ROWS)], stage.at[slot], sem.at[slot]).wait()
                moe = jnp.dot(gmats[half], _tiles_to_rows(stage, (slot,)).astype(BF16), preferred_element_type=F32)
                o_ref[rows_in] = _rms(x1_ref[rows_in] + moe, gf_ref[...])


def _combine(x1, idx, gates, cnt, off, base, offv, gf, y):
    t = x1.shape[0]
    tm = 2 * TOK_TILE
    assert t % tm == 0
    return pl.pallas_call(
        _combine_kernel,
        grid_spec=pltpu.PrefetchScalarGridSpec(
            num_scalar_prefetch=3,
            grid=(t // tm,),
            in_specs=[pl.BlockSpec((tm, 1024), lambda i, *_: (i, 0)),
                      pl.BlockSpec((tm, TOP_K), lambda i, *_: (i, 0)),
                      pl.BlockSpec((tm, TOP_K), lambda i, *_: (i, 0)),
                      pl.BlockSpec((2, 1, LANES), lambda i, *_: (i, 0, 0)),
                      pl.BlockSpec((1, 1024), lambda i, *_: (0, 0)),
                      pl.BlockSpec(memory_space=pl.ANY)],
            out_specs=pl.BlockSpec((tm, 1024), lambda i, *_: (i, 0)),
            scratch_shapes=[pltpu.VMEM((4, STAGE_ROWS) + ROW_TILE, ROW_DTYPE),
                            pltpu.SemaphoreType.DMA((4,))]),
        out_shape=jax.ShapeDtypeStruct((t, 1024), F32),
        compiler_params=_cparams(("arbitrary",)),
        name="moe_combine",
    )(cnt, off, base, x1, idx, gates, offv, gf, y)


def _prep_w_in(w):
    wt = w.T
    c_if, c_qa, c_kv, c_g = 4096, 4104, 5128, 5640
    wm = wt[:c_if].astype(BF16)
    wif = jnp.pad(wt[c_if:c_qa], ((0, LANES - 2 * M_HEADS), (0, 0))).astype(BF16)
    wqa = (jnp.concatenate([wt[c_qa + (A_GROUP * g + j) * A_HD:c_qa + (A_GROUP * g + j + 1) * A_HD]
                            for j in range(A_GROUP) for g in range(A_KV_HEADS)], axis=0)
           * (A_HD ** -0.5)).astype(BF16)
    wkv = wt[c_kv:c_g].astype(BF16)
    wg = wt[c_g:].astype(BF16)
    return wm, wg, wqa, wkv, wif


def _rope_tables(pos):
    half = ROT_DIM // 2
    inv = ROPE_THETA ** (-(jnp.arange(half, dtype=F32) * 2.0 / ROT_DIM))
    ang = pos.astype(F32)[:, None] * inv[None, :]
    lane = jnp.arange(LANES) % A_HD
    hit = (lane[None, :] % half == jnp.arange(half)[:, None]) & (lane[None, :] < ROT_DIM)
    spread_cos = hit.astype(F32)
    spread_sin = jnp.where(hit, jnp.where(lane < half, -1.0, 1.0)[None, :], 0.0)
    cos_t = jnp.dot(jnp.cos(ang), spread_cos, precision=lax.Precision.HIGHEST) + (lane >= ROT_DIM).astype(F32)[None, :]
    sin_t = jnp.dot(jnp.sin(ang), spread_sin, precision=lax.Precision.HIGHEST)
    return cos_t, sin_t


def _round_up(x, m):
    return (x + m - 1) // m * m


def kernel(x_prompt, x_sample, cache_k, cache_v, state_c, state_n, state_m, norm1, w_in, b_igate, b_fgate,
           mlstm_norm, w_proj_m, w_proj_a, attn_sinks, w_out, norm2, w_router, b_router, w_gate_up, b_gate_up,
           w_down, b_down, norm_f):
    bp, sp, _ = x_prompt.shape
    bs, ts, _ = x_sample.shape
    past_len = 16384
    l = 0
    w1 = _prep_w_in(w_in[l])
    g1 = norm1[l][None, :]
    g2 = norm2[l][None, :]
    gf = norm_f[None, :]
    gate_bias = jnp.pad(jnp.concatenate([b_igate[l], b_fgate[l]]), (0, LANES - 2 * M_HEADS))[None, :]
    mn = mlstm_norm[l][None, :]
    wpm = w_proj_m[l].astype(BF16)
    wpa = jnp.concatenate([w_proj_a[l][(A_GROUP * g + j) * A_HD:(A_GROUP * g + j + 1) * A_HD]
                           for j in range(A_GROUP) for g in range(A_KV_HEADS)], axis=0).astype(BF16)
    wo = w_out[l].astype(BF16)
    wr_f = jnp.pad(w_router[l], ((0, 0), (0, LANES - N_EXPERTS)))
    wr_hi = wr_f.astype(BF16)
    wr = jnp.concatenate([wr_hi, (wr_f - wr_hi.astype(F32)).astype(BF16)], axis=1)
    br = jnp.pad(b_router[l], (0, LANES - N_EXPERTS))[None, :]
    wgu = w_gate_up[l]
    bgu = b_gate_up[l][:, None, :]
    wd = w_down[l]
    bd = b_down[l][:, None, :]
    sinks = attn_sinks[l]

    def mixer(x, nb, seq, pos0, L, c0, n0, m0, k_buf, v_buf, tm):
        t = nb * seq
        xf = x.reshape(t, D_MODEL)
        cos_t, sin_t = _rope_tables(pos0 + jnp.arange(seq))
        if seq < tm:
            cos_t, sin_t = jnp.tile(cos_t, (tm // seq, 1)), jnp.tile(sin_t, (tm // seq, 1))
        act = BF16 if L % 16 == 0 else F32
        qkv, og, qa, ka, va, gates = _inproj(xf, g1, w1, cos_t, sin_t, tm, max(seq // tm, 1), act)
        m0p = jnp.pad(m0, ((0, 0), (0, LANES - M_HEADS)))[:, None, :]
        hm, c, n, m = _mlstm(qkv, gates, og, c0, n0, m0p, gate_bias, mn, nb, L)
        if k_buf is None:
            ha = _swa_prompt(qa, ka, va, sinks, nb)
            k_keep = ka.reshape(nb, seq, A_KV)[:, seq - WINDOW:].reshape(nb, WINDOW, A_KV_HEADS, A_HD)
            v_keep = va.reshape(nb, seq, A_KV)[:, seq - WINDOW:].reshape(nb, WINDOW, A_KV_HEADS, A_HD)
        else:
            to_t = lambda a: a.transpose(0, 2, 3, 1).reshape(nb, A_KV, WINDOW)
            from_t = lambda a: a.reshape(nb, A_KV_HEADS, A_HD, WINDOW).transpose(0, 3, 1, 2)
            ha, k_keep, v_keep = _swa_sample(qa, ka, va, to_t(k_buf), to_t(v_buf), sinks, seq)
            k_keep, v_keep = from_t(k_keep), from_t(v_keep)
        x1, xn, idx, gate, cnt = _merge(xf, hm, ha, og, wpm, wpa, wo, g2, wr, br)
        return (x1, xn, idx, gate, cnt[:, :, 0]), (k_keep, v_keep, c, n, m[:, 0, :M_HEADS])

    zc = jnp.zeros((bp, M_HEADS, M_DV, M_DK), F32)
    zn = jnp.zeros((bp, M_HEADS, M_DK), F32)
    zm = jnp.zeros((bp, M_HEADS), F32)
    rp, sp_out = mixer(x_prompt, bp, sp, 0, M_CHUNK, zc, zn, zm, None, None, 512)
    rs, ss_out = mixer(x_sample, bs, ts, past_len, ts, state_c[l], state_n[l], state_m[l],
                       cache_k[l], cache_v[l], 256)

    cnt = jnp.concatenate([rp[4], rs[4]], axis=0)
    ntile_p = rp[4].shape[0]
    off = jnp.cumsum(cnt, axis=1) - cnt
    per_expert = jnp.sum(cnt, axis=0)
    padded = _round_up(per_expert, MOE_BLOCK)
    padded_end = jnp.cumsum(padded)
    expert_start = padded_end - padded
    base = expert_start[None, :] + jnp.cumsum(cnt, axis=0) - cnt
    n_tok = bp * sp + bs * ts
    n_slots = _round_up(n_tok * TOP_K, MOE_BLOCK) + N_EXPERTS * MOE_BLOCK
    nb_max = n_slots // MOE_BLOCK
    block_row = jnp.arange(nb_max, dtype=I32) * MOE_BLOCK
    block_expert = jnp.minimum(jnp.sum(block_row[:, None] >= padded_end[None, :], axis=1), N_EXPERTS - 1).astype(I32)
    n_blocks = (padded_end[-1] // MOE_BLOCK).astype(I32)[None]
    eid = jnp.arange(N_EXPERTS, dtype=I32)
    later_used = (eid[None, :] > eid[:, None]) & (padded[None, :] > 0)
    next_used = jnp.min(jnp.where(later_used, eid[None, :], N_EXPERTS), axis=1)
    next_used = jnp.where(next_used < N_EXPERTS, next_used, -1).astype(I32)
    run_parity = ((jnp.cumsum(padded > 0) - 1) & 1).astype(I32)
    of_block = lambda table: jnp.sum(jnp.where(block_expert[:, None] == eid[None, :], table[None, :], 0),
                                     axis=1).astype(I32)
    block_rows = jnp.clip(of_block(per_expert) - (block_row - of_block(expert_start)), 0, MOE_BLOCK).astype(I32)
    tails = jnp.concatenate([expert_start + per_expert, padded - per_expert,
                             n_blocks, nb_max - n_blocks]).astype(I32)
    flat = lambda a: a.reshape(-1).astype(I32)
    offv = jnp.pad(off, ((0, 0), (0, LANES - N_EXPERTS))).astype(F32)[:, None, :]

    xs = _dispatch(rp[1], rs[1], rp[2], rs[2], flat(cnt), flat(off), flat(base), tails, offv, n_slots)
    y = _experts(block_expert, block_rows, of_block(next_used), of_block(run_parity), xs, wgu, bgu, wd, bd)
    y_p = _combine(rp[0], rp[2], rp[3], flat(cnt[:ntile_p]), flat(off[:ntile_p]), flat(base[:ntile_p]),
                   offv[:ntile_p], gf, y)
    y_s = _combine(rs[0], rs[2], rs[3], flat(cnt[ntile_p:]), flat(off[ntile_p:]), flat(base[ntile_p:]),
                   offv[ntile_p:], gf, y)

    kp, vp, cp_, np_, mp = sp_out
    ks, vs, cs, ns, ms = ss_out
    return (y_p.reshape(bp, sp, D_MODEL), y_s.reshape(bs, ts, D_MODEL),
            kp[None], vp[None], cp_[None], np_[None], mp[None],
            ks[None], vs[None], cs[None], ns[None], ms[None])
```
